```python
import math
import jax, jax.numpy as jnp
from jax import lax
import numpy as np

D_MODEL = 2048
BATCH = 2
SEQ = 4096
DEPTH = 2

CHUNK = 64
QBLOCK = 128
N_EVEN = (DEPTH + 1) // 2
N_ODD = DEPTH // 2
EPS = 1e-6

H_A = 8
DH_A = 128
H_B = 4
DK_B = 128
DV_B = 2 * DK_B
NUM_BUCKETS = 32
MAX_DISTANCE = 128
H_C = 4
DK_C = 128
DV_C = 2 * DK_C
ROPE_BASE = 10000.0
D_D = 1024
G_D = 4
SGU_LEN = 128
D_FF = 5632
CONV_WIDTH = 3

AB_IN = 3 * H_A * DH_A + 2 * H_B * 2 * DK_B + H_B * DV_B
AB_WIDTH = H_A * DH_A + H_B * DV_B
CD_IN = 2 * H_C * DK_C + 2 * H_C * DV_C + 2 * D_D
CD_WIDTH = H_C * DV_C + D_D

kernel_name = "hybrid_stickbreak_diffattn_retention_sgu_convffn"


def _rmsnorm(x, g):
    x32 = x.astype(jnp.float32)
    y = x32 * lax.rsqrt(jnp.mean(x32 * x32, axis=-1, keepdims=True) + EPS)
    return (y * g.astype(jnp.float32)).astype(x.dtype)


def _layernorm(x, g, b):
    x32 = x.astype(jnp.float32)
    mu = jnp.mean(x32, axis=-1, keepdims=True)
    var = jnp.mean(jnp.square(x32 - mu), axis=-1, keepdims=True)
    y = (x32 - mu) * lax.rsqrt(var + EPS)
    return (y * g.astype(jnp.float32) + b.astype(jnp.float32)).astype(x.dtype)


def _heads(t, n_heads, dh):
    b, s, _ = t.shape
    return t.reshape(b, s, n_heads, dh).transpose(0, 2, 1, 3)


def _merge_heads(t):
    b, h, s, d = t.shape
    return t.transpose(0, 2, 1, 3).reshape(b, s, h * d)


def _rel_bucket(rel):
    nb = NUM_BUCKETS // 2
    max_exact = nb // 2
    ret = jnp.where(rel > 0, nb, 0)
    n = jnp.abs(rel)
    n_f = jnp.maximum(n, 1).astype(jnp.float32)
    large = max_exact + (jnp.log(n_f / max_exact) / math.log(MAX_DISTANCE / max_exact)
                         * (nb - max_exact)).astype(jnp.int32)
    large = jnp.minimum(large, nb - 1)
    return ret + jnp.where(n < max_exact, n, large)


def _stick_breaking(q, k, v):
    b, h, s, dh = q.shape
    scale = dh ** -0.5
    kpos = jnp.arange(s)

    def block(i):
        start = i * QBLOCK
        qpos = start + jnp.arange(QBLOCK)
        q_blk = lax.dynamic_slice_in_dim(q, start, QBLOCK, axis=2)
        z = jnp.einsum('bhqd,bhkd->bhqk', q_blk, k).astype(jnp.float32) * scale
        earlier = kpos[None, :] < qpos[:, None]
        log_1m = jnp.where(earlier, jax.nn.log_sigmoid(-z), 0.0)
        after = lax.cumsum(log_1m, axis=3, reverse=True) - log_1m
        w = jnp.where(earlier, jnp.exp(jax.nn.log_sigmoid(z) + after), 0.0)
        return jnp.einsum('bhqk,bhkd->bhqd', w.astype(v.dtype), v)

    out = lax.map(block, jnp.arange(s // QBLOCK))
    return out.transpose(1, 2, 0, 3, 4).reshape(b, h, s, dh)


def _diff_attention(q1, q2, k1, k2, v, rel_bias, lam):
    b, h, s, dk = q1.shape
    dv = v.shape[-1]
    scale = dk ** -0.5
    kpos = jnp.arange(s)

    def block(i):
        start = i * QBLOCK
        qpos = start + jnp.arange(QBLOCK)
        bucket = _rel_bucket(kpos[None, :] - qpos[:, None])
        bias = jnp.transpose(rel_bias[bucket], (2, 0, 1)).astype(jnp.float32)
        allowed = (kpos[None, :] // CHUNK) <= (qpos[:, None] // CHUNK)

        def probs(qq, kk):
            q_blk = lax.dynamic_slice_in_dim(qq, start, QBLOCK, axis=2)
            logits = jnp.einsum('bhqd,bhkd->bhqk', q_blk, kk).astype(jnp.float32) * scale + bias
            return jax.nn.softmax(jnp.where(allowed, logits, -jnp.inf), axis=-1)

        a = probs(q1, k1) - lam * probs(q2, k2)
        return jnp.einsum('bhqk,bhkd->bhqd', a.astype(v.dtype), v)

    out = lax.map(block, jnp.arange(s // QBLOCK))
    return out.transpose(1, 2, 0, 3, 4).reshape(b, h, s, dv)


def _rotary(t):
    s, d = t.shape[2], t.shape[3]
    inv_freq = ROPE_BASE ** (-jnp.arange(0, d, 2, dtype=jnp.float32) / d)
    ang = jnp.arange(s, dtype=jnp.float32)[:, None] * inv_freq[None, :]
    cos = jnp.concatenate([jnp.cos(ang), jnp.cos(ang)], axis=-1)
    sin = jnp.concatenate([jnp.sin(ang), jnp.sin(ang)], axis=-1)
    t1, t2 = jnp.split(t, 2, axis=-1)
    return t * cos + jnp.concatenate([-t2, t1], axis=-1) * sin


def _retention(q, k, v):
    b, h, s, dk = q.shape
    dv = v.shape[-1]
    n = s // CHUNK
    log_g = jnp.log(1.0 - 2.0 ** (-5.0 - jnp.arange(h, dtype=jnp.float32)))
    idx = jnp.arange(CHUNK, dtype=jnp.float32)
    intra_decay = jnp.exp(log_g[:, None, None] * jnp.abs(idx[:, None] - idx[None, :]))
    q_decay = jnp.exp(log_g[:, None] * (idx + 1.0))
    k_decay = jnp.exp(log_g[:, None] * (CHUNK - 1.0 - idx))
    chunk_decay = jnp.exp(log_g * CHUNK)

    qc = q.reshape(b, h, n, CHUNK, dk)
    kc = k.reshape(b, h, n, CHUNK, dk)
    vc = v.reshape(b, h, n, CHUNK, dv)
    scores = jnp.einsum('bhncd,bhnmd->bhncm', qc, kc) * intra_decay[None, :, None]
    intra = jnp.einsum('bhncm,bhnme->bhnce', scores, vc)
    kv = jnp.einsum('bhncd,bhnce->bhnde', kc * k_decay[None, :, None, :, None], vc)

    def step(state, kv_n):
        return chunk_decay[None, :, None, None] * state + kv_n, state

    _, prev = lax.scan(step, jnp.zeros((b, h, dk, dv), jnp.float32),
                       kv.transpose(2, 0, 1, 3, 4))
    cross = jnp.einsum('bhncd,nbhde->bhnce', qc, prev) * q_decay[None, :, None, :, None]
    return (intra + cross).reshape(b, h, s, dv)


def _spatial_gate(v, w_s, b_s):
    b, s, _ = v.shape
    vg = v.reshape(b, s // SGU_LEN, SGU_LEN, G_D, D_D // G_D)
    pos = jnp.arange(SGU_LEN)
    mask = (pos[None, :] // CHUNK) <= (pos[:, None] // CHUNK)
    w = jnp.where(mask[None], w_s, 0.0)
    out = jnp.einsum('gij,bnjgc->bnigc', w, vg) + b_s.T[None, None, :, :, None]
    return out.reshape(b, s, D_D)


def _mixer_ab(h, w_in, w_out, rel_bias, lam_vecs, subln_g, lam_init):
    b, s, _ = h.shape
    p = h @ w_in
    sa = H_A * DH_A
    sbq = H_B * 2 * DK_B
    qa, ka, va, qb, kb, vb = jnp.split(
        p, np.cumsum([sa, sa, sa, sbq, sbq])[:5].tolist(), axis=-1)
    o_a = _stick_breaking(_heads(qa, H_A, DH_A), _heads(ka, H_A, DH_A), _heads(va, H_A, DH_A))
    qb = qb.reshape(b, s, H_B, 2, DK_B).transpose(0, 2, 3, 1, 4)
    kb = kb.reshape(b, s, H_B, 2, DK_B).transpose(0, 2, 3, 1, 4)
    lv = lam_vecs.astype(jnp.float32)
    lam = jnp.exp(jnp.sum(lv[0] * lv[1])) - jnp.exp(jnp.sum(lv[2] * lv[3])) + lam_init
    o_b = _diff_attention(qb[:, :, 0], qb[:, :, 1], kb[:, :, 0], kb[:, :, 1],
                          _heads(vb, H_B, DV_B), rel_bias, lam)
    o_b = _rmsnorm(o_b, subln_g) * (1.0 - lam_init)
    return jnp.concatenate([_merge_heads(o_a), _merge_heads(o_b)], axis=-1) @ w_out


def _mixer_cd(h, w_in, w_out, ret_norm_g, ln_g, ln_b, sgu_w, sgu_b):
    sq = H_C * DK_C
    sv = H_C * DV_C
    qc, kc, vc, gc, zd = jnp.split(
        h @ w_in, np.cumsum([sq, sq, sv, sv])[:4].tolist(), axis=-1)
    q = _rotary(_heads(qc, H_C, DK_C).astype(jnp.float32)) * (DK_C ** -0.5)
    k = _rotary(_heads(kc, H_C, DK_C).astype(jnp.float32))
    y = _retention(q, k, _heads(vc, H_C, DV_C).astype(jnp.float32))
    y = _rmsnorm(y, ret_norm_g).astype(h.dtype)
    o_c = jax.nn.silu(gc) * _merge_heads(y)
    u, v = jnp.split(jax.nn.gelu(zd), 2, axis=-1)
    o_d = u * _spatial_gate(_layernorm(v, ln_g, ln_b), sgu_w, sgu_b)
    return jnp.concatenate([o_c, o_d], axis=-1) @ w_out


def _conv_ffn(h, w_up, conv_w, conv_b, w_down):
    s = h.shape[1]
    up = h @ w_up
    padded = jnp.pad(up, ((0, 0), (CONV_WIDTH - 1, 0), (0, 0)))
    c = conv_b + sum(conv_w[j] * padded[:, j:j + s] for j in range(CONV_WIDTH))
    a, g = jnp.split(c, 2, axis=-1)
    return (jax.nn.silu(g) * a) @ w_down


def setup_inputs(seed: int = 0) -> dict:
    key = jax.random.key(seed)
    ks = jax.random.split(key, 20)
    nrm = jax.random.normal
    f32 = jnp.float32
    return {
        "x": nrm(ks[0], (BATCH, SEQ, D_MODEL), f32),
        "norm_mix_g": 1.0 + 0.02 * nrm(ks[1], (DEPTH, D_MODEL), f32),
        "norm_ffn_g": 1.0 + 0.02 * nrm(ks[2], (DEPTH, D_MODEL), f32),
        "final_norm_g": 1.0 + 0.02 * nrm(ks[3], (D_MODEL,), f32),
        "rel_bias": 0.2 * nrm(ks[4], (NUM_BUCKETS, H_B), f32),
        "ab_w_in": nrm(ks[5], (N_EVEN, D_MODEL, AB_IN), f32) * D_MODEL ** -0.5,
        "ab_w_out": nrm(ks[6], (N_EVEN, AB_WIDTH, D_MODEL), f32) * AB_WIDTH ** -0.5,
        "diff_lambda": 0.1 * nrm(ks[7], (N_EVEN, 4, DK_B), f32),
        "diff_subln_g": 1.0 + 0.02 * nrm(ks[8], (N_EVEN, DV_B), f32),
        "cd_w_in": nrm(ks[9], (N_ODD, D_MODEL, CD_IN), f32) * D_MODEL ** -0.5,
        "cd_w_out": nrm(ks[10], (N_ODD, CD_WIDTH, D_MODEL), f32) * CD_WIDTH ** -0.5,
        "ret_norm_g": 1.0 + 0.02 * nrm(ks[11], (N_ODD, DV_C), f32),
        "sgu_ln_g": 1.0 + 0.02 * nrm(ks[12], (N_ODD, D_D), f32),
        "sgu_ln_b": 0.02 * nrm(ks[13], (N_ODD, D_D), f32),
        "sgu_w": nrm(ks[14], (N_ODD, G_D, SGU_LEN, SGU_LEN), f32) * SGU_LEN ** -0.5,
        "sgu_b": 1.0 + 0.02 * nrm(ks[15], (N_ODD, G_D, SGU_LEN), f32),
        "ffn_w_up": nrm(ks[16], (DEPTH, D_MODEL, 2 * D_FF), f32) * D_MODEL ** -0.5,
        "ffn_conv_w": nrm(ks[17], (DEPTH, CONV_WIDTH, 2 * D_FF), f32) * CONV_WIDTH ** -0.5,
        "ffn_conv_b": 0.02 * nrm(ks[18], (DEPTH, 2 * D_FF), f32),
        "ffn_w_down": nrm(ks[19], (DEPTH, D_FF, D_MODEL), f32) * D_FF ** -0.5,
    }


def reference(x, norm_mix_g, norm_ffn_g, final_norm_g, rel_bias, ab_w_in, ab_w_out,
              diff_lambda, diff_subln_g, cd_w_in, cd_w_out, ret_norm_g, sgu_ln_g,
              sgu_ln_b, sgu_w, sgu_b, ffn_w_up, ffn_conv_w, ffn_conv_b, ffn_w_down):
    for layer in range(DEPTH):
        h = _rmsnorm(x, norm_mix_g[layer])
        j = layer // 2
        if layer % 2 == 0:
            lam_init = 0.8 - 0.6 * math.exp(-0.3 * layer)
            x = x + _mixer_ab(h, ab_w_in[j], ab_w_out[j], rel_bias, diff_lambda[j],
                              diff_subln_g[j], lam_init)
        else:
            x = x + _mixer_cd(h, cd_w_in[j], cd_w_out[j], ret_norm_g[j], sgu_ln_g[j],
                              sgu_ln_b[j], sgu_w[j], sgu_b[j])
        h = _rmsnorm(x, norm_ffn_g[layer])
        x = x + _conv_ffn(h, ffn_w_up[layer], ffn_conv_w[layer], ffn_conv_b[layer],
                          ffn_w_down[layer])
    return _rmsnorm(x, final_norm_g)
```

```python
import functools
import math

import numpy as np
import jax
import jax.numpy as jnp
from jax import lax
from jax.experimental import pallas as pl
from jax.experimental.pallas import tpu as pltpu

F32 = jnp.float32
BF16 = jnp.bfloat16

EPS = 1e-6
CHUNK = 64
H_A = 8
DH_A = 128
H_B = 4
DK_B = 128
DV_B = 2 * DK_B
NUM_BUCKETS = 32
MAX_DISTANCE = 128
H_C = 4
DK_C = 128
DV_C = 2 * DK_C
ROPE_BASE = 10000.0
D_D = 1024
G_D = 4
SGU_LEN = 128
CONV_WIDTH = 3

LANES = 128
ATT_BLOCK = 128
VMEM_LIMIT = 56 * 1024 * 1024


def _params(*semantics):
    return pltpu.CompilerParams(dimension_semantics=semantics,
                                vmem_limit_bytes=VMEM_LIMIT)


def _dot(a, b):
    return jnp.dot(a, b, preferred_element_type=F32)


def _dot_nt(a, b):
    return lax.dot_general(a, b, (((1,), (1,)), ((), ())), preferred_element_type=F32)


def _rms_rows(x, g):
    return x * lax.rsqrt(jnp.mean(x * x, axis=-1, keepdims=True) + EPS) * g


def _norm_matmul_kernel(x_ref, g_ref, w_ref, o_ref, h_ref):
    @pl.when(pl.program_id(1) == 0)
    def _():
        h_ref[...] = _rms_rows(x_ref[...], g_ref[...]).astype(BF16)

    o_ref[...] = _dot(h_ref[...], w_ref[...]).astype(o_ref.dtype)


def _norm_matmul(x, g, w, out_dtype, tm=512, tn=1024):
    t, d = x.shape
    n = w.shape[1]
    return pl.pallas_call(
        _norm_matmul_kernel,
        grid=(t // tm, n // tn),
        in_specs=[pl.BlockSpec((tm, d), lambda i, j: (i, 0)),
                  pl.BlockSpec((1, d), lambda i, j: (0, 0)),
                  pl.BlockSpec((d, tn), lambda i, j: (0, j))],
        out_specs=pl.BlockSpec((tm, tn), lambda i, j: (i, j)),
        out_shape=jax.ShapeDtypeStruct((t, n), out_dtype),
        scratch_shapes=[pltpu.VMEM((tm, d), BF16)],
        compiler_params=_params("parallel", "arbitrary"),
        name="norm_matmul",
    )(x, g.reshape(1, d), w)


def _out_proj_kernel(a1_ref, a2_ref, w_ref, x_ref, o_ref):
    half = a1_ref.shape[1]
    acc = _dot(a1_ref[...], w_ref[0:half, :])
    acc += _dot(a2_ref[...], w_ref[half:2 * half, :])
    o_ref[...] = x_ref[...] + acc


def _out_proj(a1, a2, w, x, tm=512):
    t, half = a1.shape
    d = w.shape[1]
    return pl.pallas_call(
        _out_proj_kernel,
        grid=(t // tm,),
        in_specs=[pl.BlockSpec((tm, half), lambda i: (i, 0)),
                  pl.BlockSpec((tm, half), lambda i: (i, 0)),
                  pl.BlockSpec((2 * half, d), lambda i: (0, 0)),
                  pl.BlockSpec((tm, d), lambda i: (i, 0))],
        out_specs=pl.BlockSpec((tm, d), lambda i: (i, 0)),
        out_shape=jax.ShapeDtypeStruct((t, d), F32),
        compiler_params=_params("parallel"),
        name="out_proj",
    )(a1, a2, w, x)


def _ffn_kernel(x_ref, g_ref, wa_ref, wg_ref, cwa_ref, cwg_ref, cba_ref, cbg_ref,
                wd_ref, fg_ref, o_ref, h_ref, acc_ref, act_ref, carry_ref,
                *, seq_tiles, final_norm):
    i = pl.program_id(0)
    j = pl.program_id(1)
    tm = x_ref.shape[0]

    @pl.when(j == 0)
    def _():
        h_ref[...] = _rms_rows(x_ref[...], g_ref[...]).astype(BF16)

    @pl.when((i % seq_tiles) == 0)
    def _():
        carry_ref[j] = jnp.zeros(carry_ref.shape[1:], F32)

    h = h_ref[...]

    def conv(w_ref, cw_ref, cb_ref, slot):
        up = _dot(h, w_ref[...])
        w0 = cw_ref[0:1, :]
        w1 = cw_ref[1:2, :]
        w2 = cw_ref[2:3, :]
        b = cb_ref[...]
        c = b + w0 * pltpu.roll(up, 2, 0) + w1 * pltpu.roll(up, 1, 0) + w2 * up
        prev = carry_ref[j, slot]
        both = jnp.concatenate([prev, up[0:8, :]], axis=0)
        top = (b + w0 * pltpu.roll(both, 2, 0)[8:16, :]
               + w1 * pltpu.roll(both, 1, 0)[8:16, :] + w2 * both[8:16, :])
        carry_ref[j, slot] = up[tm - 8:tm, :]
        return c, top

    ca, ta = conv(wa_ref, cwa_ref, cba_ref, 0)
    cg, tg = conv(wg_ref, cwg_ref, cbg_ref, 1)
    act_ref[...] = (cg * (1.0 / (1.0 + jnp.exp(-cg))) * ca).astype(BF16)
    act_ref[0:8, :] = (tg * (1.0 / (1.0 + jnp.exp(-tg))) * ta).astype(BF16)
    part = _dot(act_ref[...], wd_ref[...])

    @pl.when(j == 0)
    def _():
        acc_ref[...] = part

    @pl.when(j > 0)
    def _():
        acc_ref[...] += part

    @pl.when(j == pl.num_programs(1) - 1)
    def _():
        y = x_ref[...] + acc_ref[...]
        if final_norm:
            y = _rms_rows(y, fg_ref[...])
        o_ref[...] = y


def _ffn(x, g, w_up, conv_w, conv_b, w_down, final_g, *, seq, final_norm, tm=512, tf=512):
    t, d = x.shape
    f = w_down.shape[0]
    nj = f // tf
    kern = functools.partial(_ffn_kernel, seq_tiles=seq // tm, final_norm=final_norm)
    return pl.pallas_call(
        kern,
        grid=(t // tm, nj),
        in_specs=[pl.BlockSpec((tm, d), lambda i, j: (i, 0)),
                  pl.BlockSpec((1, d), lambda i, j: (0, 0)),
                  pl.BlockSpec((d, tf), lambda i, j: (0, j)),
                  pl.BlockSpec((d, tf), lambda i, j: (0, j + nj)),
                  pl.BlockSpec((CONV_WIDTH, tf), lambda i, j: (0, j)),
                  pl.BlockSpec((CONV_WIDTH, tf), lambda i, j: (0, j + nj)),
                  pl.BlockSpec((1, tf), lambda i, j: (0, j)),
                  pl.BlockSpec((1, tf), lambda i, j: (0, j + nj)),
                  pl.BlockSpec((tf, d), lambda i, j: (j, 0)),
                  pl.BlockSpec((1, d), lambda i, j: (0, 0))],
        out_specs=pl.BlockSpec((tm, d), lambda i, j: (i, 0)),
        out_shape=jax.ShapeDtypeStruct((t, d), F32),
        scratch_shapes=[pltpu.VMEM((tm, d), BF16),
                        pltpu.VMEM((tm, d), F32),
                        pltpu.VMEM((tm, tf), BF16),
                        pltpu.VMEM((nj, 2, 8, tf), F32)],
        compiler_params=_params("arbitrary", "arbitrary"),
        name="conv_ffn",
    )(x, g.reshape(1, d), w_up, w_up, conv_w, conv_w, conv_b.reshape(1, 2 * f),
      conv_b.reshape(1, 2 * f), w_down, final_g.reshape(1, d))


def _suffix_sum_matrix():
    blk = ATT_BLOCK
    j = np.arange(blk)[:, None]
    s = np.arange(blk)[None, :]
    m = np.concatenate([(j > s).astype(np.float32), np.ones((blk, blk), np.float32)], axis=1)
    return jnp.asarray(np.concatenate([m, m], axis=0), dtype=BF16)


def _stick_kernel(q_ref, k_ref, v_ref, m_ref, o_ref):
    blk = ATT_BLOCK
    i = pl.program_id(2)
    q = q_ref[0]
    scale = DH_A ** -0.5
    suffix = m_ref[...]
    row = lax.broadcasted_iota(jnp.int32, (blk, blk), 0)
    col = lax.broadcasted_iota(jnp.int32, (blk, blk), 1)
    earlier = col < row

    def block(j, carry, acc, diagonal):
        start = pl.multiple_of(j * blk, blk)
        k = k_ref[0, pl.ds(start, blk), :]
        v = v_ref[0, pl.ds(start, blk), :]
        z = _dot_nt(q, k) * scale
        log_sig = jnp.minimum(z, 0.0) - jnp.log(1.0 + jnp.exp(-jnp.abs(z)))
        log_1m = log_sig - z
        if diagonal:
            log_1m = jnp.where(earlier, log_1m, 0.0)
        hi = log_1m.astype(BF16)
        lo = (log_1m - hi.astype(F32)).astype(BF16)
        sums = _dot(jnp.concatenate([hi, lo], axis=1), suffix)
        w = jnp.exp(log_sig + sums[:, :blk] + carry)
        if diagonal:
            w = jnp.where(earlier, w, 0.0)
        acc = acc + _dot(w.astype(BF16), v)
        return carry + sums[:, blk:], acc

    zeros = jnp.zeros((blk, blk), F32)
    carry, acc = block(i, zeros, zeros, True)

    def body(t, state):
        return block(i - 1 - t, state[0], state[1], False)

    carry, acc = lax.fori_loop(0, i, body, (carry, acc))
    o_ref[0] = acc.astype(o_ref.dtype)


def _stick_breaking(p, batch, seq):
    blk = ATT_BLOCK
    return pl.pallas_call(
        _stick_kernel,
        grid=(batch, H_A, seq // blk),
        in_specs=[pl.BlockSpec((1, blk, DH_A), lambda b, h, i: (b, i, h)),
                  pl.BlockSpec((1, seq, DH_A), lambda b, h, i: (b, 0, H_A + h)),
                  pl.BlockSpec((1, seq, DH_A), lambda b, h, i: (b, 0, 2 * H_A + h)),
                  pl.BlockSpec((2 * blk, 2 * blk), lambda b, h, i: (0, 0))],
        out_specs=pl.BlockSpec((1, blk, DH_A), lambda b, h, i: (b, i, h)),
        out_shape=jax.ShapeDtypeStruct((batch, seq, H_A * DH_A), BF16),
        compiler_params=_params("parallel", "parallel", "arbitrary"),
        name="stick_breaking",
    )(p, p, p, _suffix_sum_matrix())


def _rel_bucket(rel):
    nb = NUM_BUCKETS // 2
    max_exact = nb // 2
    ret = jnp.where(rel > 0, nb, 0)
    n = jnp.abs(rel)
    n_f = jnp.maximum(n, 1).astype(F32)
    large = max_exact + (jnp.log(n_f / max_exact) / math.log(MAX_DISTANCE / max_exact)
                         * (nb - max_exact)).astype(jnp.int32)
    large = jnp.minimum(large, nb - 1)
    return ret + jnp.where(n < max_exact, n, large)


def _near_bias(rel_bias):
    blk = ATT_BLOCK
    qpos = blk + jnp.arange(blk)
    kpos = jnp.arange(2 * blk)
    bucket = _rel_bucket(kpos[None, :] - qpos[:, None])
    bias = jnp.transpose(rel_bias[bucket], (2, 0, 1)).astype(F32)
    allowed = (kpos[None, :] // CHUNK) <= (qpos[:, None] // CHUNK)
    return jnp.where(allowed[None], bias, -jnp.inf)


def _diff_kernel(q_ref, k_ref, v_ref, bias_ref, far_ref, lam_ref, g_ref, o_ref, *, lam_init):
    blk = ATT_BLOCK
    i = pl.program_id(2)
    scale = DK_B ** -0.5
    far_bias = far_ref[0]
    qs = (q_ref[0, :, 0:DK_B], q_ref[0, :, DK_B:2 * DK_B])

    def block(j, state, bias):
        start = pl.multiple_of(j * blk, blk)
        kk = k_ref[0, pl.ds(start, blk), :]
        v = v_ref[0, pl.ds(start, blk), :]
        new = []
        for c in range(2):
            m, l, acc = state[c]
            s = _dot_nt(qs[c], kk[:, c * DK_B:(c + 1) * DK_B]) * scale + bias
            m_new = jnp.maximum(m, jnp.max(s, axis=-1, keepdims=True))
            alpha = jnp.exp(m - m_new)
            p = jnp.exp(s - m_new)
            l = alpha * l + jnp.sum(p, axis=-1, keepdims=True)
            acc = alpha * acc + _dot(p.astype(BF16), v)
            new.append((m_new, l, acc))
        return tuple(new)

    init = (jnp.full((blk, 1), -jnp.inf, F32), jnp.zeros((blk, 1), F32),
            jnp.zeros((blk, DV_B), F32))
    state = block(i, (init, init), bias_ref[0, :, blk:2 * blk])

    state = lax.cond(i > 0,
                     lambda st: block(i - 1, st, bias_ref[0, :, 0:blk]),
                     lambda st: st, state)

    def body(t, st):
        return block(t, st, far_bias)

    state = lax.fori_loop(0, jnp.maximum(i - 1, 0), body, state)

    lv = lam_ref[...]
    lam = (jnp.exp(jnp.sum(lv[0:1] * lv[1:2], axis=-1, keepdims=True))
           - jnp.exp(jnp.sum(lv[2:3] * lv[3:4], axis=-1, keepdims=True)) + lam_init)
    (_, l1, a1), (_, l2, a2) = state
    o = a1 / l1 - lam * (a2 / l2)
    o_ref[0] = (_rms_rows(o, g_ref[...]) * (1.0 - lam_init)).astype(o_ref.dtype)


def _diff_attention(p, rel_bias, lam_vecs, subln_g, lam_init, batch, seq):
    blk = ATT_BLOCK
    q_col = 3 * H_A * DH_A // DV_B
    k_col = q_col + H_B
    v_col = k_col + H_B
    far_bucket = _rel_bucket(jnp.asarray(-2 * blk, jnp.int32))
    far = jnp.broadcast_to(rel_bias[far_bucket].astype(F32)[:, None, None], (H_B, 1, blk))
    kern = functools.partial(_diff_kernel, lam_init=lam_init)
    return pl.pallas_call(
        kern,
        grid=(batch, H_B, seq // blk),
        in_specs=[pl.BlockSpec((1, blk, DV_B), lambda b, h, i: (b, i, q_col + h)),
                  pl.BlockSpec((1, seq, DV_B), lambda b, h, i: (b, 0, k_col + h)),
                  pl.BlockSpec((1, seq, DV_B), lambda b, h, i: (b, 0, v_col + h)),
                  pl.BlockSpec((1, blk, 2 * blk), lambda b, h, i: (h, 0, 0)),
                  pl.BlockSpec((1, 1, blk), lambda b, h, i: (h, 0, 0)),
                  pl.BlockSpec((4, DK_B), lambda b, h, i: (0, 0)),
                  pl.BlockSpec((1, DV_B), lambda b, h, i: (0, 0))],
        out_specs=pl.BlockSpec((1, blk, DV_B), lambda b, h, i: (b, i, h)),
        out_shape=jax.ShapeDtypeStruct((batch, seq, H_B * DV_B), BF16),
        compiler_params=_params("parallel", "parallel", "arbitrary"),
        name="diff_attention",
    )(p, p, p, _near_bias(rel_bias), far, lam_vecs.astype(F32), subln_g.reshape(1, DV_B))


def _rotary_tables(seq):
    inv_freq = ROPE_BASE ** (-jnp.arange(0, DK_C, 2, dtype=F32) / DK_C)
    ang = jnp.arange(seq, dtype=F32)[:, None] * inv_freq[None, :]
    cos = jnp.concatenate([jnp.cos(ang), jnp.cos(ang)], axis=-1)
    sin = jnp.concatenate([-jnp.sin(ang), jnp.sin(ang)], axis=-1)
    return cos, sin


def _decay_tables():
    log_g = jnp.log(1.0 - 2.0 ** (-5.0 - jnp.arange(H_C, dtype=F32)))
    idx = jnp.arange(CHUNK, dtype=F32)
    intra = jnp.exp(log_g[:, None, None] * jnp.abs(idx[:, None] - idx[None, :]))
    q_decay = jnp.exp(log_g[:, None] * (idx + 1.0))
    k_decay = jnp.exp(log_g[:, None] * (CHUNK - 1.0 - idx))
    chunk_decay = jnp.exp(log_g * CHUNK)
    return (intra,
            jnp.broadcast_to(q_decay[:, :, None], (H_C, CHUNK, DV_C)),
            jnp.broadcast_to(k_decay[:, :, None], (H_C, CHUNK, DK_C)),
            jnp.broadcast_to(chunk_decay[:, None, None], (H_C, 1, DV_C)))


def _retention_kernel(q_ref, k_ref, v_ref, gate_ref, cos_ref, sin_ref, intra_ref, qd_ref,
                      kd_ref, cd_ref, g_ref, o_ref, state_ref):
    rows = q_ref.shape[1]

    @pl.when(pl.program_id(2) == 0)
    def _():
        state_ref[...] = jnp.zeros_like(state_ref)

    cos = cos_ref[...]
    sin = sin_ref[...]

    def rotary(t):
        return t * cos + pltpu.roll(t, DK_C // 2, 1) * sin

    q = (rotary(q_ref[0]) * (DK_C ** -0.5)).astype(BF16)
    k = rotary(k_ref[0])
    v = v_ref[0].astype(BF16)
    intra_decay = intra_ref[0]
    q_decay = qd_ref[0]
    k_decay = kd_ref[0]
    chunk_decay = cd_ref[0]

    state = state_ref[...]
    for c in range(rows // CHUNK):
        sl = slice(c * CHUNK, (c + 1) * CHUNK)
        qc = q[sl]
        kc = k[sl]
        vc = v[sl]
        scores = _dot_nt(qc, kc.astype(BF16)) * intra_decay
        y = _dot(scores.astype(BF16), vc) + _dot(qc, state.astype(BF16)) * q_decay
        kv = _dot((kc * k_decay).T.astype(BF16), vc)
        state = chunk_decay * state + kv
        gate = gate_ref[0, sl, :]
        y = _rms_rows(y, g_ref[...])
        o_ref[0, sl, :] = (gate * (1.0 / (1.0 + jnp.exp(-gate))) * y).astype(o_ref.dtype)
    state_ref[...] = state


def _retention(p, ret_norm_g, batch, seq, rows=512):
    k_col = H_C
    v_col = 2 * H_C * DK_C // DV_C
    gate_col = v_col + H_C
    cos, sin = _rotary_tables(seq)
    intra, q_decay, k_decay, chunk_decay = _decay_tables()
    return pl.pallas_call(
        _retention_kernel,
        grid=(batch, H_C, seq // rows),
        in_specs=[pl.BlockSpec((1, rows, DK_C), lambda b, h, i: (b, i, h)),
                  pl.BlockSpec((1, rows, DK_C), lambda b, h, i: (b, i, k_col + h)),
                  pl.BlockSpec((1, rows, DV_C), lambda b, h, i: (b, i, v_col + h)),
                  pl.BlockSpec((1, rows, DV_C), lambda b, h, i: (b, i, gate_col + h)),
                  pl.BlockSpec((rows, DK_C), lambda b, h, i: (i, 0)),
                  pl.BlockSpec((rows, DK_C), lambda b, h, i: (i, 0)),
                  pl.BlockSpec((1, CHUNK, CHUNK), lambda b, h, i: (h, 0, 0)),
                  pl.BlockSpec((1, CHUNK, DV_C), lambda b, h, i: (h, 0, 0)),
                  pl.BlockSpec((1, CHUNK, DK_C), lambda b, h, i: (h, 0, 0)),
                  pl.BlockSpec((1, 1, DV_C), lambda b, h, i: (h, 0, 0)),
                  pl.BlockSpec((1, DV_C), lambda b, h, i: (0, 0))],
        out_specs=pl.BlockSpec((1, rows, DV_C), lambda b, h, i: (b, i, h)),
        out_shape=jax.ShapeDtypeStruct((batch, seq, H_C * DV_C), BF16),
        scratch_shapes=[pltpu.VMEM((DK_C, DV_C), F32)],
        compiler_params=_params("parallel", "parallel", "arbitrary"),
        name="retention",
    )(p, p, p, p, cos, sin, intra, q_decay, k_decay, chunk_decay, ret_norm_g.reshape(1, DV_C))


def _gelu_tanh(x):
    return 0.5 * x * (1.0 + jnp.tanh(math.sqrt(2.0 / math.pi) * (x + 0.044715 * (x * x * x))))


def _sgu_kernel(zu_ref, zv_ref, lg_ref, lb_ref, w_ref, b_ref, o_ref):
    rows = zu_ref.shape[0]
    width = D_D // G_D
    v = _gelu_tanh(zv_ref[...])
    mu = jnp.mean(v, axis=-1, keepdims=True)
    var = jnp.mean(jnp.square(v - mu), axis=-1, keepdims=True)
    vn = ((v - mu) * lax.rsqrt(var + EPS) * lg_ref[...] + lb_ref[...]).astype(BF16)
    pos_i = lax.broadcasted_iota(jnp.int32, (SGU_LEN, SGU_LEN), 0)
    pos_j = lax.broadcasted_iota(jnp.int32, (SGU_LEN, SGU_LEN), 1)
    mask = (pos_j // CHUNK) <= (pos_i // CHUNK)
    for g in range(G_D):
        w = jnp.where(mask, w_ref[g], 0.0).astype(BF16)
        bias = b_ref[g]
        cols = slice(g * width, (g + 1) * width)
        for r in range(rows // SGU_LEN):
            sl = slice(r * SGU_LEN, (r + 1) * SGU_LEN)
            gate = _dot(w, vn[sl, cols]) + bias
            o_ref[sl, cols] = (_gelu_tanh(zu_ref[sl, cols]) * gate).astype(o_ref.dtype)


def _sgu(p, ln_g, ln_b, w_s, b_s, rows=512):
    t = p.shape[0]
    u_col = (2 * H_C * DK_C + 2 * H_C * DV_C) // D_D
    return pl.pallas_call(
        _sgu_kernel,
        grid=(t // rows,),
        in_specs=[pl.BlockSpec((rows, D_D), lambda i: (i, u_col)),
                  pl.BlockSpec((rows, D_D), lambda i: (i, u_col + 1)),
                  pl.BlockSpec((1, D_D), lambda i: (0, 0)),
                  pl.BlockSpec((1, D_D), lambda i: (0, 0)),
                  pl.BlockSpec((G_D, SGU_LEN, SGU_LEN), lambda i: (0, 0, 0)),
                  pl.BlockSpec((G_D, SGU_LEN, 1), lambda i: (0, 0, 0))],
        out_specs=pl.BlockSpec((rows, D_D), lambda i: (i, 0)),
        out_shape=jax.ShapeDtypeStruct((t, D_D), BF16),
        compiler_params=_params("parallel"),
        name="spatial_gate",
    )(p, p, ln_g.reshape(1, D_D), ln_b.reshape(1, D_D), w_s, b_s.reshape(G_D, SGU_LEN, 1))


def kernel(x, norm_mix_g, norm_ffn_g, final_norm_g, rel_bias, ab_w_in, ab_w_out, diff_lambda,
           diff_subln_g, cd_w_in, cd_w_out, ret_norm_g, sgu_ln_g, sgu_ln_b, sgu_w, sgu_b,
           ffn_w_up, ffn_conv_w, ffn_conv_b, ffn_w_down):
    batch, seq, d = x.shape
    t = batch * seq
    depth = norm_mix_g.shape[0]
    xt = x.reshape(t, d)
    for layer in range(depth):
        j = layer // 2
        if layer % 2 == 0:
            lam_init = 0.8 - 0.6 * math.exp(-0.3 * layer)
            p = _norm_matmul(xt, norm_mix_g[layer], ab_w_in[j].astype(BF16), BF16)
            p3 = p.reshape(batch, seq, p.shape[1])
            o_a = _stick_breaking(p3, batch, seq)
            o_b = _diff_attention(p3, rel_bias, diff_lambda[j], diff_subln_g[j], lam_init,
                                  batch, seq)
            xt = _out_proj(o_a.reshape(t, -1), o_b.reshape(t, -1), ab_w_out[j].astype(BF16), xt)
        else:
            p = _norm_matmul(xt, norm_mix_g[layer], cd_w_in[j].astype(BF16), F32)
            o_c = _retention(p.reshape(batch, seq, p.shape[1]), ret_norm_g[j], batch, seq)
            o_d = _sgu(p, sgu_ln_g[j], sgu_ln_b[j], sgu_w[j], sgu_b[j])
            xt = _out_proj(o_c.reshape(t, -1), o_d, cd_w_out[j].astype(BF16), xt)
        xt = _ffn(xt, norm_ffn_g[layer], ffn_w_up[layer].astype(BF16), ffn_conv_w[layer],
                  ffn_conv_b[layer], ffn_w_down[layer].astype(BF16), final_norm_g,
                  seq=seq, final_norm=(layer == depth - 1))
    return xt.reshape(batch, seq, d)
```

```python
import functools
import math

import numpy as np
import jax
import jax.numpy as jnp
from jax import lax
from jax.experimental import pallas as pl
from jax.experimental.pallas import tpu as pltpu

F32 = jnp.float32
BF16 = jnp.bfloat16

EPS = 1e-6
CHUNK = 64
H_A = 8
DH_A = 128
H_B = 4
DK_B = 128
DV_B = 2 * DK_B
NUM_BUCKETS = 32
MAX_DISTANCE = 128
H_C = 4
DK_C = 128
DV_C = 2 * DK_C
ROPE_BASE = 10000.0
D_D = 1024
G_D = 4
SGU_LEN = 128
CONV_WIDTH = 3

LANES = 128
ATT_BLOCK = 128
ATT_TILE = 512
VMEM_LIMIT = 56 * 1024 * 1024
LOG2E = math.log2(math.e)


def _params(*semantics):
    return pltpu.CompilerParams(dimension_semantics=semantics,
                                vmem_limit_bytes=VMEM_LIMIT)


def _dot(a, b):
    return jnp.dot(a, b, preferred_element_type=F32)


def _dot_nt(a, b):
    return lax.dot_general(a, b, (((1,), (1,)), ((), ())), preferred_element_type=F32)


def _rms_rows(x, g):
    return x * lax.rsqrt(jnp.mean(x * x, axis=-1, keepdims=True) + EPS) * g


def _norm_matmul_kernel(x_ref, g_ref, w_ref, o_ref, h_ref):
    @pl.when(pl.program_id(1) == 0)
    def _():
        h_ref[...] = _rms_rows(x_ref[...], g_ref[...]).astype(BF16)

    o_ref[...] = _dot(h_ref[...], w_ref[...]).astype(o_ref.dtype)


def _norm_matmul(x, g, w, out_dtype, tm=512, tn=1024):
    t, d = x.shape
    n = w.shape[1]
    return pl.pallas_call(
        _norm_matmul_kernel,
        grid=(t // tm, n // tn),
        in_specs=[pl.BlockSpec((tm, d), lambda i, j: (i, 0)),
                  pl.BlockSpec((1, d), lambda i, j: (0, 0)),
                  pl.BlockSpec((d, tn), lambda i, j: (0, j))],
        out_specs=pl.BlockSpec((tm, tn), lambda i, j: (i, j)),
        out_shape=jax.ShapeDtypeStruct((t, n), out_dtype),
        scratch_shapes=[pltpu.VMEM((tm, d), BF16)],
        compiler_params=_params("parallel", "arbitrary"),
        name="norm_matmul",
    )(x, g.reshape(1, d), w)


def _out_proj_kernel(a1_ref, a2_ref, w_ref, x_ref, o_ref):
    half = a1_ref.shape[1]
    acc = _dot(a1_ref[...], w_ref[0:half, :])
    acc += _dot(a2_ref[...], w_ref[half:2 * half, :])
    o_ref[...] = x_ref[...] + acc


def _out_proj(a1, a2, w, x, tm=512):
    t, half = a1.shape
    d = w.shape[1]
    return pl.pallas_call(
        _out_proj_kernel,
        grid=(t // tm,),
        in_specs=[pl.BlockSpec((tm, half), lambda i: (i, 0)),
                  pl.BlockSpec((tm, half), lambda i: (i, 0)),
                  pl.BlockSpec((2 * half, d), lambda i: (0, 0)),
                  pl.BlockSpec((tm, d), lambda i: (i, 0))],
        out_specs=pl.BlockSpec((tm, d), lambda i: (i, 0)),
        out_shape=jax.ShapeDtypeStruct((t, d), F32),
        compiler_params=_params("parallel"),
        name="out_proj",
    )(a1, a2, w, x)


def _ffn_kernel(x_ref, g_ref, wa_ref, wg_ref, cwa_ref, cwg_ref, cba_ref, cbg_ref,
                wd_ref, fg_ref, o_ref, h_ref, acc_ref, act_ref, carry_ref,
                *, seq_tiles, final_norm):
    i = pl.program_id(0)
    j = pl.program_id(1)
    tm = x_ref.shape[0]

    @pl.when(j == 0)
    def _():
        h_ref[...] = _rms_rows(x_ref[...], g_ref[...]).astype(BF16)

    @pl.when((i % seq_tiles) == 0)
    def _():
        carry_ref[j] = jnp.zeros(carry_ref.shape[1:], F32)

    h = h_ref[...]

    def conv(w_ref, cw_ref, cb_ref, slot):
        up = _dot(h, w_ref[...])
        w0 = cw_ref[0:1, :]
        w1 = cw_ref[1:2, :]
        w2 = cw_ref[2:3, :]
        b = cb_ref[...]
        c = b + w0 * pltpu.roll(up, 2, 0) + w1 * pltpu.roll(up, 1, 0) + w2 * up
        prev = carry_ref[j, slot]
        both = jnp.concatenate([prev, up[0:8, :]], axis=0)
        top = (b + w0 * pltpu.roll(both, 2, 0)[8:16, :]
               + w1 * pltpu.roll(both, 1, 0)[8:16, :] + w2 * both[8:16, :])
        carry_ref[j, slot] = up[tm - 8:tm, :]
        return c, top

    ca, ta = conv(wa_ref, cwa_ref, cba_ref, 0)
    cg, tg = conv(wg_ref, cwg_ref, cbg_ref, 1)
    act_ref[...] = (cg * (1.0 / (1.0 + jnp.exp(-cg))) * ca).astype(BF16)
    act_ref[0:8, :] = (tg * (1.0 / (1.0 + jnp.exp(-tg))) * ta).astype(BF16)
    part = _dot(act_ref[...], wd_ref[...])

    @pl.when(j == 0)
    def _():
        acc_ref[...] = part

    @pl.when(j > 0)
    def _():
        acc_ref[...] += part

    @pl.when(j == pl.num_programs(1) - 1)
    def _():
        y = x_ref[...] + acc_ref[...]
        if final_norm:
            y = _rms_rows(y, fg_ref[...])
        o_ref[...] = y


def _ffn(x, g, w_up, conv_w, conv_b, w_down, final_g, *, seq, final_norm, tm=512, tf=512):
    t, d = x.shape
    f = w_down.shape[0]
    nj = f // tf
    kern = functools.partial(_ffn_kernel, seq_tiles=seq // tm, final_norm=final_norm)
    return pl.pallas_call(
        kern,
        grid=(t // tm, nj),
        in_specs=[pl.BlockSpec((tm, d), lambda i, j: (i, 0)),
                  pl.BlockSpec((1, d), lambda i, j: (0, 0)),
                  pl.BlockSpec((d, tf), lambda i, j: (0, j)),
                  pl.BlockSpec((d, tf), lambda i, j: (0, j + nj)),
                  pl.BlockSpec((CONV_WIDTH, tf), lambda i, j: (0, j)),
                  pl.BlockSpec((CONV_WIDTH, tf), lambda i, j: (0, j + nj)),
                  pl.BlockSpec((1, tf), lambda i, j: (0, j)),
                  pl.BlockSpec((1, tf), lambda i, j: (0, j + nj)),
                  pl.BlockSpec((tf, d), lambda i, j: (j, 0)),
                  pl.BlockSpec((1, d), lambda i, j: (0, 0))],
        out_specs=pl.BlockSpec((tm, d), lambda i, j: (i, 0)),
        out_shape=jax.ShapeDtypeStruct((t, d), F32),
        scratch_shapes=[pltpu.VMEM((tm, d), BF16),
                        pltpu.VMEM((tm, d), F32),
                        pltpu.VMEM((tm, tf), BF16),
                        pltpu.VMEM((nj, 2, 8, tf), F32)],
        compiler_params=_params("arbitrary", "arbitrary"),
        name="conv_ffn",
    )(x, g.reshape(1, d), w_up, w_up, conv_w, conv_w, conv_b.reshape(1, 2 * f),
      conv_b.reshape(1, 2 * f), w_down, final_g.reshape(1, d))


def _suffix_sum_matrix():
    blk = ATT_BLOCK
    j = np.arange(blk)[:, None]
    s = np.arange(blk)[None, :]
    m = np.concatenate([(j > s).astype(np.float32), np.ones((blk, blk), np.float32)], axis=1)
    return jnp.asarray(np.concatenate([m, m], axis=0), dtype=BF16)


def _stick_kernel(q_ref, k_ref, v_ref, m_ref, o_ref, acc_ref, carry_ref):
    blk = ATT_BLOCK
    tile = q_ref.shape[1]
    nsub = tile // blk
    i = pl.program_id(2)
    q = q_ref[0]
    scale2 = DH_A ** -0.5 * LOG2E
    suffix = m_ref[...]

    acc_ref[...] = jnp.zeros_like(acc_ref)
    carry_ref[...] = jnp.zeros_like(carry_ref)

    def step(j, diagonal):
        start = pl.multiple_of(j * tile, tile)
        k = k_ref[0, pl.ds(start, tile), :]
        v = v_ref[0, pl.ds(start, tile), :]
        z_all = _dot_nt(q, k) * scale2
        carry = carry_ref[...]
        ws = [None] * nsub
        for c in reversed(range(nsub)):
            z = z_all[:, c * blk:(c + 1) * blk]
            if diagonal:
                row = lax.broadcasted_iota(jnp.int32, (tile, blk), 0)
                col = lax.broadcasted_iota(jnp.int32, (tile, blk), 1) + c * blk
                z = jnp.where(col < row, z, -1e4)
            log_sig = jnp.minimum(z, 0.0) - jnp.log2(1.0 + jnp.exp2(-jnp.abs(z)))
            log_1m = log_sig - z
            hi = log_1m.astype(BF16)
            lo = (log_1m - hi.astype(F32)).astype(BF16)
            sums = _dot(jnp.concatenate([hi, lo], axis=1), suffix)
            ws[c] = jnp.exp2(log_sig + sums[:, :blk] + carry).astype(BF16)
            carry = carry + sums[:, blk:]
        carry_ref[...] = carry
        acc_ref[...] += _dot(jnp.concatenate(ws, axis=1), v)

    step(i, True)

    def body(t, _):
        step(i - 1 - t, False)
        return 0

    lax.fori_loop(0, i, body, 0)
    o_ref[0] = acc_ref[...].astype(o_ref.dtype)


def _stick_breaking(p, batch, seq):
    blk = ATT_BLOCK
    tile = ATT_TILE
    return pl.pallas_call(
        _stick_kernel,
        grid=(batch, H_A, seq // tile),
        in_specs=[pl.BlockSpec((1, tile, DH_A), lambda b, h, i: (b, i, h)),
                  pl.BlockSpec((1, seq, DH_A), lambda b, h, i: (b, 0, H_A + h)),
                  pl.BlockSpec((1, seq, DH_A), lambda b, h, i: (b, 0, 2 * H_A + h)),
                  pl.BlockSpec((2 * blk, 2 * blk), lambda b, h, i: (0, 0))],
        out_specs=pl.BlockSpec((1, tile, DH_A), lambda b, h, i: (b, i, h)),
        out_shape=jax.ShapeDtypeStruct((batch, seq, H_A * DH_A), BF16),
        scratch_shapes=[pltpu.VMEM((tile, DH_A), F32),
                        pltpu.VMEM((tile, blk), F32)],
        compiler_params=_params("arbitrary", "arbitrary", "arbitrary"),
        name="stick_breaking",
    )(p, p, p, _suffix_sum_matrix())


def _rel_bucket(rel):
    nb = NUM_BUCKETS // 2
    max_exact = nb // 2
    ret = jnp.where(rel > 0, nb, 0)
    n = jnp.abs(rel)
    n_f = jnp.maximum(n, 1).astype(F32)
    large = max_exact + (jnp.log(n_f / max_exact) / math.log(MAX_DISTANCE / max_exact)
                         * (nb - max_exact)).astype(jnp.int32)
    large = jnp.minimum(large, nb - 1)
    return ret + jnp.where(n < max_exact, n, large)


def _near_buckets():
    blk = ATT_BLOCK
    qpos = blk + jnp.arange(blk)
    kpos = jnp.arange(2 * blk)
    near = _rel_bucket(kpos[None, :] - qpos[:, None]).astype(jnp.int32)
    far = _rel_bucket(jnp.full((1,), -2 * blk, jnp.int32)).astype(jnp.int32)
    return near, far


def _diff_kernel(far_ref, relb_ref, q_ref, k_ref, v_ref, bucket_ref, lam_ref, g_ref, o_ref,
                 bias_ref, m_ref, l_ref, acc_ref, *, lam_init):
    blk = ATT_BLOCK
    tile = q_ref.shape[1]
    nsub = tile // blk
    h = pl.program_id(1)
    i = pl.program_id(2)
    scale2 = DK_B ** -0.5 * LOG2E

    @pl.when(i == 0)
    def _():
        bucket = bucket_ref[...]
        far = relb_ref[far_ref[0], h]
        near = jnp.zeros((blk, 2 * blk), F32)
        for b in range(NUM_BUCKETS):
            near = jnp.where(bucket == b, relb_ref[b, h], near)
        near = (near - far) * LOG2E
        qpos = blk + lax.broadcasted_iota(jnp.int32, (blk, 2 * blk), 0)
        kpos = lax.broadcasted_iota(jnp.int32, (blk, 2 * blk), 1)
        near = jnp.where((kpos // CHUNK) <= (qpos // CHUNK), near, -jnp.inf)
        bias_ref[...] = jnp.zeros_like(bias_ref)
        for a in range(nsub):
            rows = slice(a * blk, (a + 1) * blk)
            bias_ref[0, rows, a * blk:(a + 1) * blk] = near[:, blk:]
            if a >= 1:
                bias_ref[0, rows, (a - 1) * blk:a * blk] = near[:, :blk]
            if a + 1 < nsub:
                bias_ref[0, rows, (a + 1) * blk:] = jnp.full((blk, tile - (a + 1) * blk),
                                                             -jnp.inf, F32)
        bias_ref[1, 0:blk, (nsub - 1) * blk:] = near[:, :blk]

    m_ref[...] = jnp.full_like(m_ref, -jnp.inf)
    l_ref[...] = jnp.zeros_like(l_ref)
    acc_ref[...] = jnp.zeros_like(acc_ref)

    def step(j, bias):
        start = pl.multiple_of(j * tile, tile)
        kk = k_ref[0, pl.ds(start, tile), :]
        v = v_ref[0, pl.ds(start, tile), :]
        for c in range(2):
            s = _dot_nt(q_ref[0, :, c * DK_B:(c + 1) * DK_B],
                        kk[:, c * DK_B:(c + 1) * DK_B]) * scale2
            if bias is not None:
                s = s + bias
            subs = [s[:, u * blk:(u + 1) * blk] for u in range(nsub)]
            m_prev = m_ref[c]
            m_cur = functools.reduce(jnp.maximum, subs)
            m_new = jnp.maximum(m_prev, jnp.max(m_cur, axis=-1, keepdims=True))
            alpha = jnp.exp2(m_prev - m_new)
            ps = [jnp.exp2(u - m_new) for u in subs]
            l_ref[c] = alpha * l_ref[c] + functools.reduce(jnp.add, ps)
            pv = _dot(jnp.concatenate([u.astype(BF16) for u in ps], axis=1), v)
            acc_ref[c] = jnp.concatenate([alpha] * (DV_B // blk), axis=1) * acc_ref[c] + pv
            m_ref[c] = m_new

    step(i, bias_ref[0])

    @pl.when(i > 0)
    def _():
        step(i - 1, bias_ref[1])

    def body(t, _):
        step(t, None)
        return 0

    lax.fori_loop(0, jnp.maximum(i - 1, 0), body, 0)

    lv = lam_ref[...]
    lam = (jnp.exp(jnp.sum(lv[0:1] * lv[1:2], axis=-1, keepdims=True))
           - jnp.exp(jnp.sum(lv[2:3] * lv[3:4], axis=-1, keepdims=True)) + lam_init)
    l1 = jnp.sum(l_ref[0], axis=-1, keepdims=True)
    l2 = jnp.sum(l_ref[1], axis=-1, keepdims=True)
    o = acc_ref[0] / l1 - lam * (acc_ref[1] / l2)
    o_ref[0] = (_rms_rows(o, g_ref[...]) * (1.0 - lam_init)).astype(o_ref.dtype)


def _diff_attention(p, rel_bias, lam_vecs, subln_g, lam_init, batch, seq):
    blk = ATT_BLOCK
    tile = ATT_TILE
    q_col = 3 * H_A * DH_A // DV_B
    k_col = q_col + H_B
    v_col = k_col + H_B
    near, far = _near_buckets()
    kern = functools.partial(_diff_kernel, lam_init=lam_init)
    grid_spec = pltpu.PrefetchScalarGridSpec(
        num_scalar_prefetch=2,
        grid=(batch, H_B, seq // tile),
        in_specs=[pl.BlockSpec((1, tile, DV_B), lambda b, h, i, *_: (b, i, q_col + h)),
                  pl.BlockSpec((1, seq, DV_B), lambda b, h, i, *_: (b, 0, k_col + h)),
                  pl.BlockSpec((1, seq, DV_B), lambda b, h, i, *_: (b, 0, v_col + h)),
                  pl.BlockSpec((blk, 2 * blk), lambda b, h, i, *_: (0, 0)),
                  pl.BlockSpec((4, DK_B), lambda b, h, i, *_: (0, 0)),
                  pl.BlockSpec((1, DV_B), lambda b, h, i, *_: (0, 0))],
        out_specs=pl.BlockSpec((1, tile, DV_B), lambda b, h, i, *_: (b, i, h)),
        scratch_shapes=[pltpu.VMEM((2, tile, tile), F32),
                        pltpu.VMEM((2, tile, blk), F32),
                        pltpu.VMEM((2, tile, blk), F32),
                        pltpu.VMEM((2, tile, DV_B), F32)])
    return pl.pallas_call(
        kern,
        grid_spec=grid_spec,
        out_shape=jax.ShapeDtypeStruct((batch, seq, H_B * DV_B), BF16),
        compiler_params=_params("arbitrary", "arbitrary", "arbitrary"),
        name="diff_attention",
    )(far, rel_bias.astype(F32), p, p, p, near, lam_vecs.astype(F32), subln_g.reshape(1, DV_B))


def _rotary_tables(seq):
    inv_freq = ROPE_BASE ** (-jnp.arange(0, DK_C, 2, dtype=F32) / DK_C)
    ang = jnp.arange(seq, dtype=F32)[:, None] * inv_freq[None, :]
    cos = jnp.concatenate([jnp.cos(ang), jnp.cos(ang)], axis=-1)
    sin = jnp.concatenate([-jnp.sin(ang), jnp.sin(ang)], axis=-1)
    return cos, sin


def _decay_tables():
    log_g = jnp.log(1.0 - 2.0 ** (-5.0 - jnp.arange(H_C, dtype=F32)))
    idx = jnp.arange(CHUNK, dtype=F32)
    intra = jnp.exp(log_g[:, None, None] * jnp.abs(idx[:, None] - idx[None, :]))
    q_decay = jnp.exp(log_g[:, None] * (idx + 1.0))
    k_decay = jnp.exp(log_g[:, None] * (CHUNK - 1.0 - idx))
    chunk_decay = jnp.exp(log_g * CHUNK)
    return (intra,
            jnp.broadcast_to(q_decay[:, :, None], (H_C, CHUNK, DV_C)),
            jnp.broadcast_to(k_decay[:, :, None], (H_C, CHUNK, DK_C)),
            jnp.broadcast_to(chunk_decay[:, None, None], (H_C, 1, DV_C)))


def _retention_kernel(q_ref, k_ref, v_ref, gate_ref, cos_ref, sin_ref, intra_ref, qd_ref,
                      kd_ref, cd_ref, g_ref, o_ref, state_ref):
    rows = q_ref.shape[1]

    @pl.when(pl.program_id(2) == 0)
    def _():
        state_ref[...] = jnp.zeros_like(state_ref)

    cos = cos_ref[...]
    sin = sin_ref[...]

    def rotary(t):
        return t * cos + pltpu.roll(t, DK_C // 2, 1) * sin

    q = (rotary(q_ref[0]) * (DK_C ** -0.5)).astype(BF16)
    k = rotary(k_ref[0])
    v = v_ref[0].astype(BF16)
    intra_decay = intra_ref[0]
    q_decay = qd_ref[0]
    k_decay = kd_ref[0]
    chunk_decay = cd_ref[0]

    state = state_ref[...]
    for c in range(rows // CHUNK):
        sl = slice(c * CHUNK, (c + 1) * CHUNK)
        qc = q[sl]
        kc = k[sl]
        vc = v[sl]
        scores = _dot_nt(qc, kc.astype(BF16)) * intra_decay
        y = _dot(scores.astype(BF16), vc) + _dot(qc, state.astype(BF16)) * q_decay
        kv = _dot((kc * k_decay).T.astype(BF16), vc)
        state = chunk_decay * state + kv
        gate = gate_ref[0, sl, :]
        y = _rms_rows(y, g_ref[...])
        o_ref[0, sl, :] = (gate * (1.0 / (1.0 + jnp.exp(-gate))) * y).astype(o_ref.dtype)
    state_ref[...] = state


def _retention(p, ret_norm_g, batch, seq, rows=512):
    k_col = H_C
    v_col = 2 * H_C * DK_C // DV_C
    gate_col = v_col + H_C
    cos, sin = _rotary_tables(seq)
    intra, q_decay, k_decay, chunk_decay = _decay_tables()
    return pl.pallas_call(
        _retention_kernel,
        grid=(batch, H_C, seq // rows),
        in_specs=[pl.BlockSpec((1, rows, DK_C), lambda b, h, i: (b, i, h)),
                  pl.BlockSpec((1, rows, DK_C), lambda b, h, i: (b, i, k_col + h)),
                  pl.BlockSpec((1, rows, DV_C), lambda b, h, i: (b, i, v_col + h)),
                  pl.BlockSpec((1, rows, DV_C), lambda b, h, i: (b, i, gate_col + h)),
                  pl.BlockSpec((rows, DK_C), lambda b, h, i: (i, 0)),
                  pl.BlockSpec((rows, DK_C), lambda b, h, i: (i, 0)),
                  pl.BlockSpec((1, CHUNK, CHUNK), lambda b, h, i: (h, 0, 0)),
                  pl.BlockSpec((1, CHUNK, DV_C), lambda b, h, i: (h, 0, 0)),
                  pl.BlockSpec((1, CHUNK, DK_C), lambda b, h, i: (h, 0, 0)),
                  pl.BlockSpec((1, 1, DV_C), lambda b, h, i: (h, 0, 0)),
                  pl.BlockSpec((1, DV_C), lambda b, h, i: (0, 0))],
        out_specs=pl.BlockSpec((1, rows, DV_C), lambda b, h, i: (b, i, h)),
        out_shape=jax.ShapeDtypeStruct((batch, seq, H_C * DV_C), BF16),
        scratch_shapes=[pltpu.VMEM((DK_C, DV_C), F32)],
        compiler_params=_params("parallel", "parallel", "arbitrary"),
        name="retention",
    )(p, p, p, p, cos, sin, intra, q_decay, k_decay, chunk_decay, ret_norm_g.reshape(1, DV_C))


def _gelu_tanh(x):
    return 0.5 * x * (1.0 + jnp.tanh(math.sqrt(2.0 / math.pi) * (x + 0.044715 * (x * x * x))))


def _sgu_kernel(zu_ref, zv_ref, lg_ref, lb_ref, w_ref, b_ref, o_ref):
    rows = zu_ref.shape[0]
    width = D_D // G_D
    v = _gelu_tanh(zv_ref[...])
    mu = jnp.mean(v, axis=-1, keepdims=True)
    var = jnp.mean(jnp.square(v - mu), axis=-1, keepdims=True)
    vn = ((v - mu) * lax.rsqrt(var + EPS) * lg_ref[...] + lb_ref[...]).astype(BF16)
    pos_i = lax.broadcasted_iota(jnp.int32, (SGU_LEN, SGU_LEN), 0)
    pos_j = lax.broadcasted_iota(jnp.int32, (SGU_LEN, SGU_LEN), 1)
    mask = (pos_j // CHUNK) <= (pos_i // CHUNK)
    for g in range(G_D):
        w = jnp.where(mask, w_ref[g], 0.0).astype(BF16)
        bias = b_ref[g]
        cols = slice(g * width, (g + 1) * width)
        for r in range(rows // SGU_LEN):
            sl = slice(r * SGU_LEN, (r + 1) * SGU_LEN)
            gate = _dot(w, vn[sl, cols]) + bias
            o_ref[sl, cols] = (_gelu_tanh(zu_ref[sl, cols]) * gate).astype(o_ref.dtype)


def _sgu(p, ln_g, ln_b, w_s, b_s, rows=512):
    t = p.shape[0]
    u_col = (2 * H_C * DK_C + 2 * H_C * DV_C) // D_D
    return pl.pallas_call(
        _sgu_kernel,
        grid=(t // rows,),
        in_specs=[pl.BlockSpec((rows, D_D), lambda i: (i, u_col)),
                  pl.BlockSpec((rows, D_D), lambda i: (i, u_col + 1)),
                  pl.BlockSpec((1, D_D), lambda i: (0, 0)),
                  pl.BlockSpec((1, D_D), lambda i: (0, 0)),
                  pl.BlockSpec((G_D, SGU_LEN, SGU_LEN), lambda i: (0, 0, 0)),
                  pl.BlockSpec((G_D, SGU_LEN, 1), lambda i: (0, 0, 0))],
        out_specs=pl.BlockSpec((rows, D_D), lambda i: (i, 0)),
        out_shape=jax.ShapeDtypeStruct((t, D_D), BF16),
        compiler_params=_params("parallel"),
        name="spatial_gate",
    )(p, p, ln_g.reshape(1, D_D), ln_b.reshape(1, D_D), w_s, b_s.reshape(G_D, SGU_LEN, 1))


def kernel(x, norm_mix_g, norm_ffn_g, final_norm_g, rel_bias, ab_w_in, ab_w_out, diff_lambda,
           diff_subln_g, cd_w_in, cd_w_out, ret_norm_g, sgu_ln_g, sgu_ln_b, sgu_w, sgu_b,
           ffn_w_up, ffn_conv_w, ffn_conv_b, ffn_w_down):
    batch, seq, d = x.shape
    t = batch * seq
    depth = norm_mix_g.shape[0]
    xt = x.reshape(t, d)
    for layer in range(depth):
        j = layer // 2
        if layer % 2 == 0:
            lam_init = 0.8 - 0.6 * math.exp(-0.3 * layer)
            p = _norm_matmul(xt, norm_mix_g[layer], ab_w_in[j].astype(BF16), BF16)
            p3 = p.reshape(batch, seq, p.shape[1])
            o_a = _stick_breaking(p3, batch, seq)
            o_b = _diff_attention(p3, rel_bias, diff_lambda[j], diff_subln_g[j], lam_init,
                                  batch, seq)
            xt = _out_proj(o_a.reshape(t, -1), o_b.reshape(t, -1), ab_w_out[j].astype(BF16), xt)
        else:
            p = _norm_matmul(xt, norm_mix_g[layer], cd_w_in[j].astype(BF16), F32)
            o_c = _retention(p.reshape(batch, seq, p.shape[1]), ret_norm_g[j], batch, seq)
            o_d = _sgu(p, sgu_ln_g[j], sgu_ln_b[j], sgu_w[j], sgu_b[j])
            xt = _out_proj(o_c.reshape(t, -1), o_d, cd_w_out[j].astype(BF16), xt)
        xt = _ffn(xt, norm_ffn_g[layer], ffn_w_up[layer].astype(BF16), ffn_conv_w[layer],
                  ffn_conv_b[layer], ffn_w_down[layer].astype(BF16), final_norm_g,
                  seq=seq, final_norm=(layer == depth - 1))
    return xt.reshape(batch, seq, d)
```

```python
import functools
import math

import numpy as np
import jax
import jax.numpy as jnp
from jax import lax
from jax.experimental import pallas as pl
from jax.experimental.pallas import tpu as pltpu

F32 = jnp.float32
BF16 = jnp.bfloat16

EPS = 1e-6
CHUNK = 64
H_A = 8
DH_A = 128
H_B = 4
DK_B = 128
DV_B = 2 * DK_B
NUM_BUCKETS = 32
MAX_DISTANCE = 128
H_C = 4
DK_C = 128
DV_C = 2 * DK_C
ROPE_BASE = 10000.0
D_D = 1024
G_D = 4
SGU_LEN = 128
CONV_WIDTH = 3

LANES = 128
ATT_BLOCK = 128
ATT_TILE = 512
VMEM_LIMIT = 56 * 1024 * 1024
LOG2E = math.log2(math.e)


def _params(*semantics):
    return pltpu.CompilerParams(dimension_semantics=semantics,
                                vmem_limit_bytes=VMEM_LIMIT)


def _dot(a, b):
    return jnp.dot(a, b, preferred_element_type=F32)


def _dot_nt(a, b):
    return lax.dot_general(a, b, (((1,), (1,)), ((), ())), preferred_element_type=F32)


def _rms_rows(x, g):
    return x * lax.rsqrt(jnp.mean(x * x, axis=-1, keepdims=True) + EPS) * g


def _norm_matmul_kernel(x_ref, g_ref, w_ref, o_ref, h_ref):
    @pl.when(pl.program_id(1) == 0)
    def _():
        h_ref[...] = _rms_rows(x_ref[...], g_ref[...]).astype(BF16)

    o_ref[...] = _dot(h_ref[...], w_ref[...]).astype(o_ref.dtype)


def _norm_matmul(x, g, w, layer, out_dtype, tm=512, tn=1024):
    t, d = x.shape
    n = w.shape[2]
    return pl.pallas_call(
        _norm_matmul_kernel,
        grid=(t // tm, n // tn),
        in_specs=[pl.BlockSpec((tm, d), lambda i, j: (i, 0)),
                  pl.BlockSpec((1, d), lambda i, j: (0, 0)),
                  pl.BlockSpec((None, d, tn), lambda i, j: (layer, 0, j))],
        out_specs=pl.BlockSpec((tm, tn), lambda i, j: (i, j)),
        out_shape=jax.ShapeDtypeStruct((t, n), out_dtype),
        scratch_shapes=[pltpu.VMEM((tm, d), BF16)],
        compiler_params=_params("parallel", "arbitrary"),
        name="norm_matmul",
    )(x, g.reshape(1, d), w)


def _out_proj_kernel(a1_ref, a2_ref, w_ref, x_ref, o_ref):
    half = a1_ref.shape[1]
    acc = _dot(a1_ref[...], w_ref[0:half, :])
    acc += _dot(a2_ref[...], w_ref[half:2 * half, :])
    o_ref[...] = x_ref[...] + acc


def _out_proj(a1, a2, w, layer, x, tm=512):
    t, half = a1.shape
    d = w.shape[2]
    return pl.pallas_call(
        _out_proj_kernel,
        grid=(t // tm,),
        in_specs=[pl.BlockSpec((tm, half), lambda i: (i, 0)),
                  pl.BlockSpec((tm, half), lambda i: (i, 0)),
                  pl.BlockSpec((None, 2 * half, d), lambda i: (layer, 0, 0)),
                  pl.BlockSpec((tm, d), lambda i: (i, 0))],
        out_specs=pl.BlockSpec((tm, d), lambda i: (i, 0)),
        out_shape=jax.ShapeDtypeStruct((t, d), F32),
        compiler_params=_params("parallel"),
        name="out_proj",
    )(a1, a2, w, x)


def _ffn_kernel(x_ref, g_ref, wa_ref, wg_ref, cwa_ref, cwg_ref, cba_ref, cbg_ref,
                wd_ref, fg_ref, o_ref, h_ref, acc_ref, act0_ref, act1_ref, carry_ref, up_ref,
                *, seq_tiles, final_norm, n_hidden_tiles):
    i = pl.program_id(0)
    j = pl.program_id(1)
    nj = pl.num_programs(1) - 1
    tm = x_ref.shape[0]

    def up_phase(out_ref):
        h = h_ref[...]

        def conv(w_ref, cw_ref, cb_ref, slot):
            u_ref = up_ref.at[slot]
            u_ref[0:8, :] = carry_ref[j, slot]
            u_ref[8:8 + tm, :] = _dot(h, w_ref[...])
            carry_ref[j, slot] = u_ref[tm:tm + 8, :]
            return (cb_ref[...] + cw_ref[0:1, :] * u_ref[6:6 + tm, :]
                    + cw_ref[1:2, :] * u_ref[7:7 + tm, :] + cw_ref[2:3, :] * u_ref[8:8 + tm, :])

        ca = conv(wa_ref, cwa_ref, cba_ref, 0)
        cg = conv(wg_ref, cwg_ref, cbg_ref, 1)
        out_ref[...] = (cg * (1.0 / (1.0 + jnp.exp(-cg))) * ca).astype(BF16)

    def down_phase(in_ref):
        acc_ref[...] += _dot(in_ref[...], wd_ref[...])

    @pl.when(((i % seq_tiles) == 0) & (j < nj))
    def _():
        carry_ref[j] = jnp.zeros(carry_ref.shape[1:], F32)

    @pl.when(j == 0)
    def _():
        x = x_ref[...]
        h_ref[...] = _rms_rows(x, g_ref[...]).astype(BF16)
        acc_ref[...] = x
        up_phase(act0_ref)

    @pl.when((j > 0) & (j < nj) & (j % 2 == 1))
    def _():
        up_phase(act1_ref)
        down_phase(act0_ref)

    @pl.when((j > 0) & (j < nj) & (j % 2 == 0))
    def _():
        up_phase(act0_ref)
        down_phase(act1_ref)

    @pl.when(j == nj)
    def _():
        last = act0_ref if (n_hidden_tiles % 2 == 1) else act1_ref
        y = acc_ref[...] + _dot(last[...], wd_ref[...])
        if final_norm:
            y = _rms_rows(y, fg_ref[...])
        o_ref[...] = y


def _ffn(x, g, w_up, conv_w, conv_b, w_down, final_g, layer, *, seq, final_norm, tm=512, tf=512):
    t, d = x.shape
    f = w_down.shape[1]
    nj = f // tf
    kern = functools.partial(_ffn_kernel, seq_tiles=seq // tm, final_norm=final_norm,
                             n_hidden_tiles=nj)

    def up_col(j):
        return jnp.minimum(j, nj - 1)

    def down_row(j):
        return jnp.maximum(j - 1, 0)

    return pl.pallas_call(
        kern,
        grid=(t // tm, nj + 1),
        in_specs=[pl.BlockSpec((tm, d), lambda i, j: (i, 0)),
                  pl.BlockSpec((None, 1, d), lambda i, j: (layer, 0, 0)),
                  pl.BlockSpec((None, d, tf), lambda i, j: (layer, 0, up_col(j))),
                  pl.BlockSpec((None, d, tf), lambda i, j: (layer, 0, up_col(j) + nj)),
                  pl.BlockSpec((None, CONV_WIDTH, tf), lambda i, j: (layer, 0, up_col(j))),
                  pl.BlockSpec((None, CONV_WIDTH, tf), lambda i, j: (layer, 0, up_col(j) + nj)),
                  pl.BlockSpec((None, 1, tf), lambda i, j: (layer, 0, up_col(j))),
                  pl.BlockSpec((None, 1, tf), lambda i, j: (layer, 0, up_col(j) + nj)),
                  pl.BlockSpec((None, tf, d), lambda i, j: (layer, down_row(j), 0)),
                  pl.BlockSpec((1, d), lambda i, j: (0, 0))],
        out_specs=pl.BlockSpec((tm, d), lambda i, j: (i, 0)),
        out_shape=jax.ShapeDtypeStruct((t, d), F32),
        scratch_shapes=[pltpu.VMEM((tm, d), BF16),
                        pltpu.VMEM((tm, d), F32),
                        pltpu.VMEM((tm, tf), BF16),
                        pltpu.VMEM((tm, tf), BF16),
                        pltpu.VMEM((nj, 2, 8, tf), F32),
                        pltpu.VMEM((2, tm + 8, tf), F32)],
        compiler_params=_params("arbitrary", "arbitrary"),
        name="conv_ffn",
    )(x, g[:, None, :], w_up, w_up, conv_w, conv_w, conv_b[:, None, :], conv_b[:, None, :],
      w_down, final_g.reshape(1, d))


def _suffix_sum_matrix():
    blk = ATT_BLOCK
    j = np.arange(blk)[:, None]
    s = np.arange(blk)[None, :]
    m = np.concatenate([(j > s).astype(np.float32), np.ones((blk, blk), np.float32)], axis=1)
    return jnp.asarray(np.concatenate([m, m], axis=0), dtype=BF16)


def _stick_kernel(q_ref, k_ref, v_ref, m_ref, o_ref, acc_ref, carry_ref):
    blk = ATT_BLOCK
    tile = q_ref.shape[1]
    nsub = tile // blk
    i = pl.program_id(2)
    q = q_ref[0]
    scale2 = DH_A ** -0.5 * LOG2E
    suffix = m_ref[...]

    acc_ref[...] = jnp.zeros_like(acc_ref)
    carry_ref[...] = jnp.zeros_like(carry_ref)

    def step(j, diagonal):
        start = pl.multiple_of(j * tile, tile)
        k = k_ref[0, pl.ds(start, tile), :]
        v = v_ref[0, pl.ds(start, tile), :]
        z_all = _dot_nt(q, k) * scale2
        carry = carry_ref[...]
        ws = [None] * nsub
        for c in reversed(range(nsub)):
            z = z_all[:, c * blk:(c + 1) * blk]
            if diagonal:
                row = lax.broadcasted_iota(jnp.int32, (tile, blk), 0)
                col = lax.broadcasted_iota(jnp.int32, (tile, blk), 1) + c * blk
                z = jnp.where(col < row, z, -1e4)
            log_sig = jnp.minimum(z, 0.0) - jnp.log2(1.0 + jnp.exp2(-jnp.abs(z)))
            log_1m = log_sig - z
            hi = log_1m.astype(BF16)
            lo = (log_1m - hi.astype(F32)).astype(BF16)
            sums = _dot(jnp.concatenate([hi, lo], axis=1), suffix)
            ws[c] = jnp.exp2(log_sig + sums[:, :blk] + carry).astype(BF16)
            carry = carry + sums[:, blk:]
        carry_ref[...] = carry
        acc_ref[...] += _dot(jnp.concatenate(ws, axis=1), v)

    step(i, True)

    def body(t, _):
        step(i - 1 - t, False)
        return 0

    lax.fori_loop(0, i, body, 0)
    o_ref[0] = acc_ref[...].astype(o_ref.dtype)


def _stick_breaking(p, batch, seq):
    blk = ATT_BLOCK
    tile = ATT_TILE
    return pl.pallas_call(
        _stick_kernel,
        grid=(batch, H_A, seq // tile),
        in_specs=[pl.BlockSpec((1, tile, DH_A), lambda b, h, i: (b, i, h)),
                  pl.BlockSpec((1, seq, DH_A), lambda b, h, i: (b, 0, H_A + h)),
                  pl.BlockSpec((1, seq, DH_A), lambda b, h, i: (b, 0, 2 * H_A + h)),
                  pl.BlockSpec((2 * blk, 2 * blk), lambda b, h, i: (0, 0))],
        out_specs=pl.BlockSpec((1, tile, DH_A), lambda b, h, i: (b, i, h)),
        out_shape=jax.ShapeDtypeStruct((batch, seq, H_A * DH_A), BF16),
        scratch_shapes=[pltpu.VMEM((tile, DH_A), F32),
                        pltpu.VMEM((tile, blk), F32)],
        compiler_params=_params("arbitrary", "arbitrary", "arbitrary"),
        name="stick_breaking",
    )(p, p, p, _suffix_sum_matrix())


def _rel_bucket(rel):
    nb = NUM_BUCKETS // 2
    max_exact = nb // 2
    ret = jnp.where(rel > 0, nb, 0)
    n = jnp.abs(rel)
    n_f = jnp.maximum(n, 1).astype(F32)
    large = max_exact + (jnp.log(n_f / max_exact) / math.log(MAX_DISTANCE / max_exact)
                         * (nb - max_exact)).astype(jnp.int32)
    large = jnp.minimum(large, nb - 1)
    return ret + jnp.where(n < max_exact, n, large)


def _near_buckets():
    blk = ATT_BLOCK
    qpos = blk + jnp.arange(blk)
    kpos = jnp.arange(2 * blk)
    near = _rel_bucket(kpos[None, :] - qpos[:, None]).astype(jnp.int32)
    far = _rel_bucket(jnp.full((1,), -2 * blk, jnp.int32)).astype(jnp.int32)
    return near, far


def _diff_kernel(far_ref, relb_ref, q_ref, k_ref, v_ref, bucket_ref, lam_ref, g_ref, o_ref,
                 bias_ref, m_ref, l_ref, acc_ref, *, lam_init):
    blk = ATT_BLOCK
    tile = q_ref.shape[1]
    nsub = tile // blk
    h = pl.program_id(1)
    i = pl.program_id(2)
    scale2 = DK_B ** -0.5 * LOG2E

    @pl.when(i == 0)
    def _():
        bucket = bucket_ref[...]
        far = relb_ref[far_ref[0], h]
        near = jnp.zeros((blk, 2 * blk), F32)
        for b in range(NUM_BUCKETS):
            near = jnp.where(bucket == b, relb_ref[b, h], near)
        near = (near - far) * LOG2E
        qpos = blk + lax.broadcasted_iota(jnp.int32, (blk, 2 * blk), 0)
        kpos = lax.broadcasted_iota(jnp.int32, (blk, 2 * blk), 1)
        near = jnp.where((kpos // CHUNK) <= (qpos // CHUNK), near, -jnp.inf)
        bias_ref[...] = jnp.zeros_like(bias_ref)
        for a in range(nsub):
            rows = slice(a * blk, (a + 1) * blk)
            bias_ref[0, rows, a * blk:(a + 1) * blk] = near[:, blk:]
            if a >= 1:
                bias_ref[0, rows, (a - 1) * blk:a * blk] = near[:, :blk]
            if a + 1 < nsub:
                bias_ref[0, rows, (a + 1) * blk:] = jnp.full((blk, tile - (a + 1) * blk),
                                                             -jnp.inf, F32)
        bias_ref[1, 0:blk, (nsub - 1) * blk:] = near[:, :blk]

    m_ref[...] = jnp.full_like(m_ref, -jnp.inf)
    l_ref[...] = jnp.zeros_like(l_ref)
    acc_ref[...] = jnp.zeros_like(acc_ref)

    def step(j, bias):
        start = pl.multiple_of(j * tile, tile)
        kk = k_ref[0, pl.ds(start, tile), :]
        v = v_ref[0, pl.ds(start, tile), :]
        for c in range(2):
            s = _dot_nt(q_ref[0, :, c * DK_B:(c + 1) * DK_B],
                        kk[:, c * DK_B:(c + 1) * DK_B]) * scale2
            if bias is not None:
                s = s + bias
            subs = [s[:, u * blk:(u + 1) * blk] for u in range(nsub)]
            m_prev = m_ref[c]
            m_cur = functools.reduce(jnp.maximum, subs)
            m_new = jnp.maximum(m_prev, jnp.max(m_cur, axis=-1, keepdims=True))
            alpha = jnp.exp2(m_prev - m_new)
            ps = [jnp.exp2(u - m_new) for u in subs]
            l_ref[c] = alpha * l_ref[c] + functools.reduce(jnp.add, ps)
            pv = _dot(jnp.concatenate([u.astype(BF16) for u in ps], axis=1), v)
            acc_ref[c] = jnp.concatenate([alpha] * (DV_B // blk), axis=1) * acc_ref[c] + pv
            m_ref[c] = m_new

    step(i, bias_ref[0])

    @pl.when(i > 0)
    def _():
        step(i - 1, bias_ref[1])

    def body(t, _):
        step(t, None)
        return 0

    lax.fori_loop(0, jnp.maximum(i - 1, 0), body, 0)

    lv = lam_ref[...]
    lam = (jnp.exp(jnp.sum(lv[0:1] * lv[1:2], axis=-1, keepdims=True))
           - jnp.exp(jnp.sum(lv[2:3] * lv[3:4], axis=-1, keepdims=True)) + lam_init)
    l1 = jnp.sum(l_ref[0], axis=-1, keepdims=True)
    l2 = jnp.sum(l_ref[1], axis=-1, keepdims=True)
    o = acc_ref[0] / l1 - lam * (acc_ref[1] / l2)
    o_ref[0] = (_rms_rows(o, g_ref[...]) * (1.0 - lam_init)).astype(o_ref.dtype)


def _diff_attention(p, rel_bias, lam_vecs, subln_g, lam_init, batch, seq):
    blk = ATT_BLOCK
    tile = ATT_TILE
    q_col = 3 * H_A * DH_A // DV_B
    k_col = q_col + H_B
    v_col = k_col + H_B
    near, far = _near_buckets()
    kern = functools.partial(_diff_kernel, lam_init=lam_init)
    grid_spec = pltpu.PrefetchScalarGridSpec(
        num_scalar_prefetch=2,
        grid=(batch, H_B, seq // tile),
        in_specs=[pl.BlockSpec((1, tile, DV_B), lambda b, h, i, *_: (b, i, q_col + h)),
                  pl.BlockSpec((1, seq, DV_B), lambda b, h, i, *_: (b, 0, k_col + h)),
                  pl.BlockSpec((1, seq, DV_B), lambda b, h, i, *_: (b, 0, v_col + h)),
                  pl.BlockSpec((blk, 2 * blk), lambda b, h, i, *_: (0, 0)),
                  pl.BlockSpec((4, DK_B), lambda b, h, i, *_: (0, 0)),
                  pl.BlockSpec((1, DV_B), lambda b, h, i, *_: (0, 0))],
        out_specs=pl.BlockSpec((1, tile, DV_B), lambda b, h, i, *_: (b, i, h)),
        scratch_shapes=[pltpu.VMEM((2, tile, tile), F32),
                        pltpu.VMEM((2, tile, blk), F32),
                        pltpu.VMEM((2, tile, blk), F32),
                        pltpu.VMEM((2, tile, DV_B), F32)])
    return pl.pallas_call(
        kern,
        grid_spec=grid_spec,
        out_shape=jax.ShapeDtypeStruct((batch, seq, H_B * DV_B), BF16),
        compiler_params=_params("arbitrary", "arbitrary", "arbitrary"),
        name="diff_attention",
    )(far, rel_bias.astype(F32), p, p, p, near, lam_vecs.astype(F32), subln_g.reshape(1, DV_B))


def _rotary_tables(seq):
    inv_freq = ROPE_BASE ** (-jnp.arange(0, DK_C, 2, dtype=F32) / DK_C)
    ang = jnp.arange(seq, dtype=F32)[:, None] * inv_freq[None, :]
    cos = jnp.concatenate([jnp.cos(ang), jnp.cos(ang)], axis=-1)
    sin = jnp.concatenate([-jnp.sin(ang), jnp.sin(ang)], axis=-1)
    return cos, sin


def _decay_tables():
    log_g = jnp.log(1.0 - 2.0 ** (-5.0 - jnp.arange(H_C, dtype=F32)))
    idx = jnp.arange(CHUNK, dtype=F32)
    intra = jnp.exp(log_g[:, None, None] * jnp.abs(idx[:, None] - idx[None, :]))
    q_decay = jnp.exp(log_g[:, None] * (idx + 1.0))
    k_decay = jnp.exp(log_g[:, None] * (CHUNK - 1.0 - idx))
    chunk_decay = jnp.exp(log_g * CHUNK)
    return (intra,
            jnp.broadcast_to(q_decay[:, :, None], (H_C, CHUNK, DV_C)),
            jnp.broadcast_to(k_decay[:, :, None], (H_C, CHUNK, DK_C)),
            jnp.broadcast_to(chunk_decay[:, None, None], (H_C, 1, DV_C)))


def _retention_kernel(q_ref, k_ref, v_ref, gate_ref, cos_ref, sin_ref, intra_ref, qd_ref,
                      kd_ref, cd_ref, g_ref, o_ref, state_ref):
    rows = q_ref.shape[1]

    @pl.when(pl.program_id(2) == 0)
    def _():
        state_ref[...] = jnp.zeros_like(state_ref)

    cos = cos_ref[...]
    sin = sin_ref[...]

    def rotary(t):
        return t * cos + pltpu.roll(t, DK_C // 2, 1) * sin

    q = (rotary(q_ref[0]) * (DK_C ** -0.5)).astype(BF16)
    k = rotary(k_ref[0])
    v = v_ref[0].astype(BF16)
    intra_decay = intra_ref[0]
    q_decay = qd_ref[0]
    k_decay = kd_ref[0]
    chunk_decay = cd_ref[0]

    state = state_ref[...]
    for c in range(rows // CHUNK):
        sl = slice(c * CHUNK, (c + 1) * CHUNK)
        qc = q[sl]
        kc = k[sl]
        vc = v[sl]
        scores = _dot_nt(qc, kc.astype(BF16)) * intra_decay
        y = _dot(scores.astype(BF16), vc) + _dot(qc, state.astype(BF16)) * q_decay
        kv = _dot((kc * k_decay).T.astype(BF16), vc)
        state = chunk_decay * state + kv
        gate = gate_ref[0, sl, :]
        y = _rms_rows(y, g_ref[...])
        o_ref[0, sl, :] = (gate * (1.0 / (1.0 + jnp.exp(-gate))) * y).astype(o_ref.dtype)
    state_ref[...] = state


def _retention(p, ret_norm_g, batch, seq, rows=512):
    k_col = H_C
    v_col = 2 * H_C * DK_C // DV_C
    gate_col = v_col + H_C
    cos, sin = _rotary_tables(seq)
    intra, q_decay, k_decay, chunk_decay = _decay_tables()
    return pl.pallas_call(
        _retention_kernel,
        grid=(batch, H_C, seq // rows),
        in_specs=[pl.BlockSpec((1, rows, DK_C), lambda b, h, i: (b, i, h)),
                  pl.BlockSpec((1, rows, DK_C), lambda b, h, i: (b, i, k_col + h)),
                  pl.BlockSpec((1, rows, DV_C), lambda b, h, i: (b, i, v_col + h)),
                  pl.BlockSpec((1, rows, DV_C), lambda b, h, i: (b, i, gate_col + h)),
                  pl.BlockSpec((rows, DK_C), lambda b, h, i: (i, 0)),
                  pl.BlockSpec((rows, DK_C), lambda b, h, i: (i, 0)),
                  pl.BlockSpec((1, CHUNK, CHUNK), lambda b, h, i: (h, 0, 0)),
                  pl.BlockSpec((1, CHUNK, DV_C), lambda b, h, i: (h, 0, 0)),
                  pl.BlockSpec((1, CHUNK, DK_C), lambda b, h, i: (h, 0, 0)),
                  pl.BlockSpec((1, 1, DV_C), lambda b, h, i: (h, 0, 0)),
                  pl.BlockSpec((1, DV_C), lambda b, h, i: (0, 0))],
        out_specs=pl.BlockSpec((1, rows, DV_C), lambda b, h, i: (b, i, h)),
        out_shape=jax.ShapeDtypeStruct((batch, seq, H_C * DV_C), BF16),
        scratch_shapes=[pltpu.VMEM((DK_C, DV_C), F32)],
        compiler_params=_params("parallel", "parallel", "arbitrary"),
        name="retention",
    )(p, p, p, p, cos, sin, intra, q_decay, k_decay, chunk_decay, ret_norm_g.reshape(1, DV_C))


def _gelu_tanh(x):
    return 0.5 * x * (1.0 + jnp.tanh(math.sqrt(2.0 / math.pi) * (x + 0.044715 * (x * x * x))))


def _sgu_kernel(zu_ref, zv_ref, lg_ref, lb_ref, w_ref, b_ref, o_ref):
    rows = zu_ref.shape[0]
    width = D_D // G_D
    v = _gelu_tanh(zv_ref[...])
    mu = jnp.mean(v, axis=-1, keepdims=True)
    var = jnp.mean(jnp.square(v - mu), axis=-1, keepdims=True)
    vn = ((v - mu) * lax.rsqrt(var + EPS) * lg_ref[...] + lb_ref[...]).astype(BF16)
    pos_i = lax.broadcasted_iota(jnp.int32, (SGU_LEN, SGU_LEN), 0)
    pos_j = lax.broadcasted_iota(jnp.int32, (SGU_LEN, SGU_LEN), 1)
    mask = (pos_j // CHUNK) <= (pos_i // CHUNK)
    for g in range(G_D):
        w = jnp.where(mask, w_ref[g], 0.0).astype(BF16)
        bias = b_ref[g]
        cols = slice(g * width, (g + 1) * width)
        for r in range(rows // SGU_LEN):
            sl = slice(r * SGU_LEN, (r + 1) * SGU_LEN)
            gate = _dot(w, vn[sl, cols]) + bias
            o_ref[sl, cols] = (_gelu_tanh(zu_ref[sl, cols]) * gate).astype(o_ref.dtype)


def _sgu(p, ln_g, ln_b, w_s, b_s, rows=512):
    t = p.shape[0]
    u_col = (2 * H_C * DK_C + 2 * H_C * DV_C) // D_D
    return pl.pallas_call(
        _sgu_kernel,
        grid=(t // rows,),
        in_specs=[pl.BlockSpec((rows, D_D), lambda i: (i, u_col)),
                  pl.BlockSpec((rows, D_D), lambda i: (i, u_col + 1)),
                  pl.BlockSpec((1, D_D), lambda i: (0, 0)),
                  pl.BlockSpec((1, D_D), lambda i: (0, 0)),
                  pl.BlockSpec((G_D, SGU_LEN, SGU_LEN), lambda i: (0, 0, 0)),
                  pl.BlockSpec((G_D, SGU_LEN, 1), lambda i: (0, 0, 0))],
        out_specs=pl.BlockSpec((rows, D_D), lambda i: (i, 0)),
        out_shape=jax.ShapeDtypeStruct((t, D_D), BF16),
        compiler_params=_params("parallel"),
        name="spatial_gate",
    )(p, p, ln_g.reshape(1, D_D), ln_b.reshape(1, D_D), w_s, b_s.reshape(G_D, SGU_LEN, 1))


def kernel(x, norm_mix_g, norm_ffn_g, final_norm_g, rel_bias, ab_w_in, ab_w_out, diff_lambda,
           diff_subln_g, cd_w_in, cd_w_out, ret_norm_g, sgu_ln_g, sgu_ln_b, sgu_w, sgu_b,
           ffn_w_up, ffn_conv_w, ffn_conv_b, ffn_w_down):
    batch, seq, d = x.shape
    t = batch * seq
    depth = norm_mix_g.shape[0]
    xt = x.reshape(t, d)
    ab_w_in, ab_w_out, cd_w_in, cd_w_out, ffn_w_up, ffn_w_down = (
        w.astype(BF16) for w in (ab_w_in, ab_w_out, cd_w_in, cd_w_out, ffn_w_up, ffn_w_down))
    for layer in range(depth):
        j = layer // 2
        if layer % 2 == 0:
            lam_init = 0.8 - 0.6 * math.exp(-0.3 * layer)
            p = _norm_matmul(xt, norm_mix_g[layer], ab_w_in, j, BF16)
            p3 = p.reshape(batch, seq, p.shape[1])
            o_a = _stick_breaking(p3, batch, seq)
            o_b = _diff_attention(p3, rel_bias, diff_lambda[j], diff_subln_g[j], lam_init,
                                  batch, seq)
            xt = _out_proj(o_a.reshape(t, -1), o_b.reshape(t, -1), ab_w_out, j, xt)
        else:
            p = _norm_matmul(xt, norm_mix_g[layer], cd_w_in, j, F32)
            o_c = _retention(p.reshape(batch, seq, p.shape[1]), ret_norm_g[j], batch, seq)
            o_d = _sgu(p, sgu_ln_g[j], sgu_ln_b[j], sgu_w[j], sgu_b[j])
            xt = _out_proj(o_c.reshape(t, -1), o_d, cd_w_out, j, xt)
        xt = _ffn(xt, norm_ffn_g, ffn_w_up, ffn_conv_w, ffn_conv_b, ffn_w_down, final_norm_g,
                  layer, seq=seq, final_norm=(layer == depth - 1))
    return xt.reshape(batch, seq, d)
```

```python
import functools
import math

import numpy as np
import jax
import jax.numpy as jnp
from jax import lax
from jax.experimental import pallas as pl
from jax.experimental.pallas import tpu as pltpu

F32 = jnp.float32
BF16 = jnp.bfloat16

EPS = 1e-6
CHUNK = 64
H_A = 8
DH_A = 128
H_B = 4
DK_B = 128
DV_B = 2 * DK_B
NUM_BUCKETS = 32
MAX_DISTANCE = 128
H_C = 4
DK_C = 128
DV_C = 2 * DK_C
ROPE_BASE = 10000.0
D_D = 1024
G_D = 4
SGU_LEN = 128
CONV_WIDTH = 3

LANES = 128
ATT_BLOCK = 128
ATT_TILE = 512
VMEM_LIMIT = 56 * 1024 * 1024
PROJ_TILE = 1024
FFN_TILE = 512
LOG2E = math.log2(math.e)


def _params(*semantics):
    return pltpu.CompilerParams(dimension_semantics=semantics,
                                vmem_limit_bytes=VMEM_LIMIT)


def _dot(a, b):
    return jnp.dot(a, b, preferred_element_type=F32)


def _dot_nt(a, b):
    return lax.dot_general(a, b, (((1,), (1,)), ((), ())), preferred_element_type=F32)


def _rms_rows(x, g):
    return x * lax.rsqrt(jnp.mean(x * x, axis=-1, keepdims=True) + EPS) * g


def _norm_matmul_kernel(x_ref, g_ref, w_ref, o_ref, h_ref):
    @pl.when(pl.program_id(1) == 0)
    def _():
        h_ref[...] = _rms_rows(x_ref[...], g_ref[...]).astype(BF16)

    o_ref[...] = _dot(h_ref[...], w_ref[...]).astype(o_ref.dtype)


def _column_tiles(w, tn):
    l, d, n = w.shape
    return w.astype(BF16).reshape(l, d, n // tn, tn).transpose(0, 2, 1, 3)


def _norm_matmul(x, g, w, layer, out_dtype, tm=1024):
    t, d = x.shape
    tn = w.shape[3]
    n = w.shape[1] * tn
    return pl.pallas_call(
        _norm_matmul_kernel,
        grid=(t // tm, n // tn),
        in_specs=[pl.BlockSpec((tm, d), lambda i, j: (i, 0)),
                  pl.BlockSpec((1, d), lambda i, j: (0, 0)),
                  pl.BlockSpec((None, None, d, tn), lambda i, j: (layer, j, 0, 0))],
        out_specs=pl.BlockSpec((tm, tn), lambda i, j: (i, j)),
        out_shape=jax.ShapeDtypeStruct((t, n), out_dtype),
        scratch_shapes=[pltpu.VMEM((tm, d), BF16)],
        compiler_params=_params("parallel", "arbitrary"),
        name="norm_matmul",
    )(x, g.reshape(1, d), w)


def _out_proj_kernel(a1_ref, a2_ref, w_ref, x_ref, o_ref):
    half = a1_ref.shape[1]
    acc = _dot(a1_ref[...], w_ref[0:half, :])
    acc += _dot(a2_ref[...], w_ref[half:2 * half, :])
    o_ref[...] = x_ref[...] + acc


def _out_proj(a1, a2, w, layer, x, tm=512):
    t, half = a1.shape
    d = w.shape[2]
    return pl.pallas_call(
        _out_proj_kernel,
        grid=(t // tm,),
        in_specs=[pl.BlockSpec((tm, half), lambda i: (i, 0)),
                  pl.BlockSpec((tm, half), lambda i: (i, 0)),
                  pl.BlockSpec((None, 2 * half, d), lambda i: (layer, 0, 0)),
                  pl.BlockSpec((tm, d), lambda i: (i, 0))],
        out_specs=pl.BlockSpec((tm, d), lambda i: (i, 0)),
        out_shape=jax.ShapeDtypeStruct((t, d), F32),
        compiler_params=_params("parallel"),
        name="out_proj",
    )(a1, a2, w, x)


def _ffn_kernel(x_ref, g_ref, wa_ref, wg_ref, cwa_ref, cwg_ref, cba_ref, cbg_ref,
                wd_ref, fg_ref, o_ref, h_ref, acc_ref, act0_ref, act1_ref, carry_ref, up_ref,
                *, seq_tiles, final_norm, n_hidden_tiles):
    i = pl.program_id(0)
    j = pl.program_id(1)
    nj = pl.num_programs(1) - 1
    tm = x_ref.shape[0]

    def up_phase(out_ref):
        h = h_ref[...]

        def conv(w_ref, cw_ref, cb_ref, slot):
            u_ref = up_ref.at[slot]
            u_ref[0:8, :] = carry_ref[j, slot]
            u_ref[8:8 + tm, :] = _dot(h, w_ref[...])
            carry_ref[j, slot] = u_ref[tm:tm + 8, :]
            return (cb_ref[...] + cw_ref[0:1, :] * u_ref[6:6 + tm, :]
                    + cw_ref[1:2, :] * u_ref[7:7 + tm, :] + cw_ref[2:3, :] * u_ref[8:8 + tm, :])

        ca = conv(wa_ref, cwa_ref, cba_ref, 0)
        cg = conv(wg_ref, cwg_ref, cbg_ref, 1)
        out_ref[...] = (cg * (1.0 / (1.0 + jnp.exp(-cg))) * ca).astype(BF16)

    def down_phase(in_ref):
        acc_ref[...] += _dot(in_ref[...], wd_ref[...])

    @pl.when(((i % seq_tiles) == 0) & (j < nj))
    def _():
        carry_ref[j] = jnp.zeros(carry_ref.shape[1:], F32)

    @pl.when(j == 0)
    def _():
        x = x_ref[...]
        h_ref[...] = _rms_rows(x, g_ref[...]).astype(BF16)
        acc_ref[...] = x
        up_phase(act0_ref)

    @pl.when((j > 0) & (j < nj) & (j % 2 == 1))
    def _():
        up_phase(act1_ref)
        down_phase(act0_ref)

    @pl.when((j > 0) & (j < nj) & (j % 2 == 0))
    def _():
        up_phase(act0_ref)
        down_phase(act1_ref)

    @pl.when(j == nj)
    def _():
        last = act0_ref if (n_hidden_tiles % 2 == 1) else act1_ref
        y = acc_ref[...] + _dot(last[...], wd_ref[...])
        if final_norm:
            y = _rms_rows(y, fg_ref[...])
        o_ref[...] = y


def _ffn(x, g, w_up, conv_w, conv_b, w_down, final_g, layer, *, seq, final_norm, tm=512):
    t, d = x.shape
    f = w_down.shape[1]
    tf = w_up.shape[3]
    nj = f // tf
    kern = functools.partial(_ffn_kernel, seq_tiles=seq // tm, final_norm=final_norm,
                             n_hidden_tiles=nj)

    def up_col(j):
        return jnp.minimum(j, nj - 1)

    def down_row(j):
        return jnp.maximum(j - 1, 0)

    return pl.pallas_call(
        kern,
        grid=(t // tm, nj + 1),
        in_specs=[pl.BlockSpec((tm, d), lambda i, j: (i, 0)),
                  pl.BlockSpec((None, 1, d), lambda i, j: (layer, 0, 0)),
                  pl.BlockSpec((None, None, d, tf), lambda i, j: (layer, up_col(j), 0, 0)),
                  pl.BlockSpec((None, None, d, tf), lambda i, j: (layer, up_col(j) + nj, 0, 0)),
                  pl.BlockSpec((None, CONV_WIDTH, tf), lambda i, j: (layer, 0, up_col(j))),
                  pl.BlockSpec((None, CONV_WIDTH, tf), lambda i, j: (layer, 0, up_col(j) + nj)),
                  pl.BlockSpec((None, 1, tf), lambda i, j: (layer, 0, up_col(j))),
                  pl.BlockSpec((None, 1, tf), lambda i, j: (layer, 0, up_col(j) + nj)),
                  pl.BlockSpec((None, tf, d), lambda i, j: (layer, down_row(j), 0)),
                  pl.BlockSpec((1, d), lambda i, j: (0, 0))],
        out_specs=pl.BlockSpec((tm, d), lambda i, j: (i, 0)),
        out_shape=jax.ShapeDtypeStruct((t, d), F32),
        scratch_shapes=[pltpu.VMEM((tm, d), BF16),
                        pltpu.VMEM((tm, d), F32),
                        pltpu.VMEM((tm, tf), BF16),
                        pltpu.VMEM((tm, tf), BF16),
                        pltpu.VMEM((nj, 2, 8, tf), F32),
                        pltpu.VMEM((2, tm + 8, tf), F32)],
        compiler_params=_params("arbitrary", "arbitrary"),
        name="conv_ffn",
    )(x, g[:, None, :], w_up, w_up, conv_w, conv_w, conv_b[:, None, :], conv_b[:, None, :],
      w_down, final_g.reshape(1, d))


def _suffix_sum_matrix():
    blk = ATT_BLOCK
    j = np.arange(blk)[:, None]
    s = np.arange(blk)[None, :]
    m = np.concatenate([(j > s).astype(np.float32), np.ones((blk, blk), np.float32)], axis=1)
    return jnp.asarray(np.concatenate([m, m], axis=0), dtype=BF16)


def _stick_kernel(q_ref, k_ref, v_ref, m_ref, o_ref, acc_ref, carry_ref):
    blk = ATT_BLOCK
    tile = q_ref.shape[1]
    nsub = tile // blk
    i = pl.program_id(2)
    q = q_ref[0]
    scale2 = DH_A ** -0.5 * LOG2E
    suffix = m_ref[...]

    acc_ref[...] = jnp.zeros_like(acc_ref)
    carry_ref[...] = jnp.zeros_like(carry_ref)

    def step(j, diagonal):
        start = pl.multiple_of(j * tile, tile)
        k = k_ref[0, pl.ds(start, tile), :]
        v = v_ref[0, pl.ds(start, tile), :]
        z_all = _dot_nt(q, k) * scale2
        carry = carry_ref[...]
        ws = [None] * nsub
        for c in reversed(range(nsub)):
            z = z_all[:, c * blk:(c + 1) * blk]
            if diagonal:
                row = lax.broadcasted_iota(jnp.int32, (tile, blk), 0)
                col = lax.broadcasted_iota(jnp.int32, (tile, blk), 1) + c * blk
                z = jnp.where(col < row, z, -1e4)
            log_sig = jnp.minimum(z, 0.0) - jnp.log2(1.0 + jnp.exp2(-jnp.abs(z)))
            log_1m = log_sig - z
            hi = log_1m.astype(BF16)
            lo = (log_1m - hi.astype(F32)).astype(BF16)
            sums = _dot(jnp.concatenate([hi, lo], axis=1), suffix)
            ws[c] = jnp.exp2(log_sig + sums[:, :blk] + carry).astype(BF16)
            carry = carry + sums[:, blk:]
        carry_ref[...] = carry
        acc_ref[...] += _dot(jnp.concatenate(ws, axis=1), v)

    step(i, True)

    def body(t, _):
        step(i - 1 - t, False)
        return 0

    lax.fori_loop(0, i, body, 0)
    o_ref[0] = acc_ref[...].astype(o_ref.dtype)


def _stick_breaking(p, batch, seq):
    blk = ATT_BLOCK
    tile = ATT_TILE
    return pl.pallas_call(
        _stick_kernel,
        grid=(batch, H_A, seq // tile),
        in_specs=[pl.BlockSpec((1, tile, DH_A), lambda b, h, i: (b, i, h)),
                  pl.BlockSpec((1, seq, DH_A), lambda b, h, i: (b, 0, H_A + h)),
                  pl.BlockSpec((1, seq, DH_A), lambda b, h, i: (b, 0, 2 * H_A + h)),
                  pl.BlockSpec((2 * blk, 2 * blk), lambda b, h, i: (0, 0))],
        out_specs=pl.BlockSpec((1, tile, DH_A), lambda b, h, i: (b, i, h)),
        out_shape=jax.ShapeDtypeStruct((batch, seq, H_A * DH_A), BF16),
        scratch_shapes=[pltpu.VMEM((tile, DH_A), F32),
                        pltpu.VMEM((tile, blk), F32)],
        compiler_params=_params("arbitrary", "arbitrary", "arbitrary"),
        name="stick_breaking",
    )(p, p, p, _suffix_sum_matrix())


def _rel_bucket(rel):
    nb = NUM_BUCKETS // 2
    max_exact = nb // 2
    ret = jnp.where(rel > 0, nb, 0)
    n = jnp.abs(rel)
    n_f = jnp.maximum(n, 1).astype(F32)
    large = max_exact + (jnp.log(n_f / max_exact) / math.log(MAX_DISTANCE / max_exact)
                         * (nb - max_exact)).astype(jnp.int32)
    large = jnp.minimum(large, nb - 1)
    return ret + jnp.where(n < max_exact, n, large)


def _near_buckets():
    blk = ATT_BLOCK
    qpos = blk + jnp.arange(blk)
    kpos = jnp.arange(2 * blk)
    near = _rel_bucket(kpos[None, :] - qpos[:, None]).astype(jnp.int32)
    far = _rel_bucket(jnp.full((1,), -2 * blk, jnp.int32)).astype(jnp.int32)
    return near, far


def _diff_kernel(far_ref, relb_ref, q_ref, k_ref, v_ref, bucket_ref, lam_ref, g_ref, o_ref,
                 bias_ref, m_ref, l_ref, acc_ref, *, lam_init):
    blk = ATT_BLOCK
    tile = q_ref.shape[1]
    nsub = tile // blk
    h = pl.program_id(1)
    i = pl.program_id(2)
    scale2 = DK_B ** -0.5 * LOG2E

    @pl.when(i == 0)
    def _():
        bucket = bucket_ref[...]
        far = relb_ref[far_ref[0], h]
        near = jnp.zeros((blk, 2 * blk), F32)
        for b in range(NUM_BUCKETS):
            near = jnp.where(bucket == b, relb_ref[b, h], near)
        near = (near - far) * LOG2E
        qpos = blk + lax.broadcasted_iota(jnp.int32, (blk, 2 * blk), 0)
        kpos = lax.broadcasted_iota(jnp.int32, (blk, 2 * blk), 1)
        near = jnp.where((kpos // CHUNK) <= (qpos // CHUNK), near, -jnp.inf)
        bias_ref[...] = jnp.zeros_like(bias_ref)
        for a in range(nsub):
            rows = slice(a * blk, (a + 1) * blk)
            bias_ref[0, rows, a * blk:(a + 1) * blk] = near[:, blk:]
            if a >= 1:
                bias_ref[0, rows, (a - 1) * blk:a * blk] = near[:, :blk]
            if a + 1 < nsub:
                bias_ref[0, rows, (a + 1) * blk:] = jnp.full((blk, tile - (a + 1) * blk),
                                                             -jnp.inf, F32)
        bias_ref[1, 0:blk, (nsub - 1) * blk:] = near[:, :blk]

    m_ref[...] = jnp.full_like(m_ref, -jnp.inf)
    l_ref[...] = jnp.zeros_like(l_ref)
    acc_ref[...] = jnp.zeros_like(acc_ref)

    def step(j, bias):
        start = pl.multiple_of(j * tile, tile)
        kk = k_ref[0, pl.ds(start, tile), :]
        v = v_ref[0, pl.ds(start, tile), :]
        for c in range(2):
            s = _dot_nt(q_ref[0, :, c * DK_B:(c + 1) * DK_B],
                        kk[:, c * DK_B:(c + 1) * DK_B]) * scale2
            if bias is not None:
                s = s + bias
            subs = [s[:, u * blk:(u + 1) * blk] for u in range(nsub)]
            m_prev = m_ref[c]
            m_cur = functools.reduce(jnp.maximum, subs)
            m_new = jnp.maximum(m_prev, jnp.max(m_cur, axis=-1, keepdims=True))
            alpha = jnp.exp2(m_prev - m_new)
            ps = [jnp.exp2(u - m_new) for u in subs]
            l_ref[c] = alpha * l_ref[c] + functools.reduce(jnp.add, ps)
            pv = _dot(jnp.concatenate([u.astype(BF16) for u in ps], axis=1), v)
            acc_ref[c] = jnp.concatenate([alpha] * (DV_B // blk), axis=1) * acc_ref[c] + pv
            m_ref[c] = m_new

    step(i, bias_ref[0])

    @pl.when(i > 0)
    def _():
        step(i - 1, bias_ref[1])

    def body(t, _):
        step(t, None)
        return 0

    lax.fori_loop(0, jnp.maximum(i - 1, 0), body, 0)

    lv = lam_ref[...]
    lam = (jnp.exp(jnp.sum(lv[0:1] * lv[1:2], axis=-1, keepdims=True))
           - jnp.exp(jnp.sum(lv[2:3] * lv[3:4], axis=-1, keepdims=True)) + lam_init)
    l1 = jnp.sum(l_ref[0], axis=-1, keepdims=True)
    l2 = jnp.sum(l_ref[1], axis=-1, keepdims=True)
    o = acc_ref[0] / l1 - lam * (acc_ref[1] / l2)
    o_ref[0] = (_rms_rows(o, g_ref[...]) * (1.0 - lam_init)).astype(o_ref.dtype)


def _diff_attention(p, rel_bias, lam_vecs, subln_g, lam_init, batch, seq):
    blk = ATT_BLOCK
    tile = ATT_TILE
    q_col = 3 * H_A * DH_A // DV_B
    k_col = q_col + H_B
    v_col = k_col + H_B
    near, far = _near_buckets()
    kern = functools.partial(_diff_kernel, lam_init=lam_init)
    grid_spec = pltpu.PrefetchScalarGridSpec(
        num_scalar_prefetch=2,
        grid=(batch, H_B, seq // tile),
        in_specs=[pl.BlockSpec((1, tile, DV_B), lambda b, h, i, *_: (b, i, q_col + h)),
                  pl.BlockSpec((1, seq, DV_B), lambda b, h, i, *_: (b, 0, k_col + h)),
                  pl.BlockSpec((1, seq, DV_B), lambda b, h, i, *_: (b, 0, v_col + h)),
                  pl.BlockSpec((blk, 2 * blk), lambda b, h, i, *_: (0, 0)),
                  pl.BlockSpec((4, DK_B), lambda b, h, i, *_: (0, 0)),
                  pl.BlockSpec((1, DV_B), lambda b, h, i, *_: (0, 0))],
        out_specs=pl.BlockSpec((1, tile, DV_B), lambda b, h, i, *_: (b, i, h)),
        scratch_shapes=[pltpu.VMEM((2, tile, tile), F32),
                        pltpu.VMEM((2, tile, blk), F32),
                        pltpu.VMEM((2, tile, blk), F32),
                        pltpu.VMEM((2, tile, DV_B), F32)])
    return pl.pallas_call(
        kern,
        grid_spec=grid_spec,
        out_shape=jax.ShapeDtypeStruct((batch, seq, H_B * DV_B), BF16),
        compiler_params=_params("arbitrary", "arbitrary", "arbitrary"),
        name="diff_attention",
    )(far, rel_bias.astype(F32), p, p, p, near, lam_vecs.astype(F32), subln_g.reshape(1, DV_B))


def _rotary_tables(seq):
    inv_freq = ROPE_BASE ** (-jnp.arange(0, DK_C, 2, dtype=F32) / DK_C)
    ang = jnp.arange(seq, dtype=F32)[:, None] * inv_freq[None, :]
    cos = jnp.concatenate([jnp.cos(ang), jnp.cos(ang)], axis=-1)
    sin = jnp.concatenate([-jnp.sin(ang), jnp.sin(ang)], axis=-1)
    return cos, sin


def _decay_tables():
    log_g = jnp.log(1.0 - 2.0 ** (-5.0 - jnp.arange(H_C, dtype=F32)))
    idx = jnp.arange(CHUNK, dtype=F32)
    intra = jnp.exp(log_g[:, None, None] * jnp.abs(idx[:, None] - idx[None, :]))
    q_decay = jnp.exp(log_g[:, None] * (idx + 1.0))
    k_decay = jnp.exp(log_g[:, None] * (CHUNK - 1.0 - idx))
    chunk_decay = jnp.exp(log_g * CHUNK)
    return (intra,
            jnp.broadcast_to(q_decay[:, :, None], (H_C, CHUNK, DV_C)),
            jnp.broadcast_to(k_decay[:, :, None], (H_C, CHUNK, DK_C)),
            jnp.broadcast_to(chunk_decay[:, None, None], (H_C, 1, DV_C)))


def _retention_kernel(q_ref, k_ref, v_ref, gate_ref, cos_ref, sin_ref, intra_ref, qd_ref,
                      kd_ref, cd_ref, g_ref, o_ref, state_ref):
    rows = q_ref.shape[1]

    @pl.when(pl.program_id(2) == 0)
    def _():
        state_ref[...] = jnp.zeros_like(state_ref)

    cos = cos_ref[...]
    sin = sin_ref[...]

    def rotary(t):
        return t * cos + pltpu.roll(t, DK_C // 2, 1) * sin

    q = (rotary(q_ref[0]) * (DK_C ** -0.5)).astype(BF16)
    k = rotary(k_ref[0])
    v = v_ref[0].astype(BF16)
    intra_decay = intra_ref[0]
    q_decay = qd_ref[0]
    k_decay = kd_ref[0]
    chunk_decay = cd_ref[0]

    state = state_ref[...]
    for c in range(rows // CHUNK):
        sl = slice(c * CHUNK, (c + 1) * CHUNK)
        qc = q[sl]
        kc = k[sl]
        vc = v[sl]
        scores = _dot_nt(qc, kc.astype(BF16)) * intra_decay
        y = _dot(scores.astype(BF16), vc) + _dot(qc, state.astype(BF16)) * q_decay
        kv = _dot((kc * k_decay).T.astype(BF16), vc)
        state = chunk_decay * state + kv
        gate = gate_ref[0, sl, :]
        y = _rms_rows(y, g_ref[...])
        o_ref[0, sl, :] = (gate * (1.0 / (1.0 + jnp.exp(-gate))) * y).astype(o_ref.dtype)
    state_ref[...] = state


def _retention(p, ret_norm_g, batch, seq, rows=512):
    k_col = H_C
    v_col = 2 * H_C * DK_C // DV_C
    gate_col = v_col + H_C
    cos, sin = _rotary_tables(seq)
    intra, q_decay, k_decay, chunk_decay = _decay_tables()
    return pl.pallas_call(
        _retention_kernel,
        grid=(batch, H_C, seq // rows),
        in_specs=[pl.BlockSpec((1, rows, DK_C), lambda b, h, i: (b, i, h)),
                  pl.BlockSpec((1, rows, DK_C), lambda b, h, i: (b, i, k_col + h)),
                  pl.BlockSpec((1, rows, DV_C), lambda b, h, i: (b, i, v_col + h)),
                  pl.BlockSpec((1, rows, DV_C), lambda b, h, i: (b, i, gate_col + h)),
                  pl.BlockSpec((rows, DK_C), lambda b, h, i: (i, 0)),
                  pl.BlockSpec((rows, DK_C), lambda b, h, i: (i, 0)),
                  pl.BlockSpec((1, CHUNK, CHUNK), lambda b, h, i: (h, 0, 0)),
                  pl.BlockSpec((1, CHUNK, DV_C), lambda b, h, i: (h, 0, 0)),
                  pl.BlockSpec((1, CHUNK, DK_C), lambda b, h, i: (h, 0, 0)),
                  pl.BlockSpec((1, 1, DV_C), lambda b, h, i: (h, 0, 0)),
                  pl.BlockSpec((1, DV_C), lambda b, h, i: (0, 0))],
        out_specs=pl.BlockSpec((1, rows, DV_C), lambda b, h, i: (b, i, h)),
        out_shape=jax.ShapeDtypeStruct((batch, seq, H_C * DV_C), BF16),
        scratch_shapes=[pltpu.VMEM((DK_C, DV_C), F32)],
        compiler_params=_params("parallel", "parallel", "arbitrary"),
        name="retention",
    )(p, p, p, p, cos, sin, intra, q_decay, k_decay, chunk_decay, ret_norm_g.reshape(1, DV_C))


def _gelu_tanh(x):
    return 0.5 * x * (1.0 + jnp.tanh(math.sqrt(2.0 / math.pi) * (x + 0.044715 * (x * x * x))))


def _sgu_kernel(zu_ref, zv_ref, lg_ref, lb_ref, w_ref, b_ref, o_ref):
    rows = zu_ref.shape[0]
    width = D_D // G_D
    v = _gelu_tanh(zv_ref[...])
    mu = jnp.mean(v, axis=-1, keepdims=True)
    var = jnp.mean(jnp.square(v - mu), axis=-1, keepdims=True)
    vn = ((v - mu) * lax.rsqrt(var + EPS) * lg_ref[...] + lb_ref[...]).astype(BF16)
    pos_i = lax.broadcasted_iota(jnp.int32, (SGU_LEN, SGU_LEN), 0)
    pos_j = lax.broadcasted_iota(jnp.int32, (SGU_LEN, SGU_LEN), 1)
    mask = (pos_j // CHUNK) <= (pos_i // CHUNK)
    for g in range(G_D):
        w = jnp.where(mask, w_ref[g], 0.0).astype(BF16)
        bias = b_ref[g]
        cols = slice(g * width, (g + 1) * width)
        for r in range(rows // SGU_LEN):
            sl = slice(r * SGU_LEN, (r + 1) * SGU_LEN)
            gate = _dot(w, vn[sl, cols]) + bias
            o_ref[sl, cols] = (_gelu_tanh(zu_ref[sl, cols]) * gate).astype(o_ref.dtype)


def _sgu(p, ln_g, ln_b, w_s, b_s, rows=512):
    t = p.shape[0]
    u_col = (2 * H_C * DK_C + 2 * H_C * DV_C) // D_D
    return pl.pallas_call(
        _sgu_kernel,
        grid=(t // rows,),
        in_specs=[pl.BlockSpec((rows, D_D), lambda i: (i, u_col)),
                  pl.BlockSpec((rows, D_D), lambda i: (i, u_col + 1)),
                  pl.BlockSpec((1, D_D), lambda i: (0, 0)),
                  pl.BlockSpec((1, D_D), lambda i: (0, 0)),
                  pl.BlockSpec((G_D, SGU_LEN, SGU_LEN), lambda i: (0, 0, 0)),
                  pl.BlockSpec((G_D, SGU_LEN, 1), lambda i: (0, 0, 0))],
        out_specs=pl.BlockSpec((rows, D_D), lambda i: (i, 0)),
        out_shape=jax.ShapeDtypeStruct((t, D_D), BF16),
        compiler_params=_params("parallel"),
        name="spatial_gate",
    )(p, p, ln_g.reshape(1, D_D), ln_b.reshape(1, D_D), w_s, b_s.reshape(G_D, SGU_LEN, 1))


def kernel(x, norm_mix_g, norm_ffn_g, final_norm_g, rel_bias, ab_w_in, ab_w_out, diff_lambda,
           diff_subln_g, cd_w_in, cd_w_out, ret_norm_g, sgu_ln_g, sgu_ln_b, sgu_w, sgu_b,
           ffn_w_up, ffn_conv_w, ffn_conv_b, ffn_w_down):
    batch, seq, d = x.shape
    t = batch * seq
    depth = norm_mix_g.shape[0]
    xt = x.reshape(t, d)
    ab_w_out, cd_w_out, ffn_w_down = (w.astype(BF16) for w in (ab_w_out, cd_w_out, ffn_w_down))
    ab_w_in, cd_w_in = (_column_tiles(w, PROJ_TILE) for w in (ab_w_in, cd_w_in))
    ffn_w_up = _column_tiles(ffn_w_up, FFN_TILE)
    for layer in range(depth):
        j = layer // 2
        if layer % 2 == 0:
            lam_init = 0.8 - 0.6 * math.exp(-0.3 * layer)
            p = _norm_matmul(xt, norm_mix_g[layer], ab_w_in, j, BF16)
            p3 = p.reshape(batch, seq, p.shape[1])
            o_a = _stick_breaking(p3, batch, seq)
            o_b = _diff_attention(p3, rel_bias, diff_lambda[j], diff_subln_g[j], lam_init,
                                  batch, seq)
            xt = _out_proj(o_a.reshape(t, -1), o_b.reshape(t, -1), ab_w_out, j, xt)
        else:
            p = _norm_matmul(xt, norm_mix_g[layer], cd_w_in, j, F32)
            o_c = _retention(p.reshape(batch, seq, p.shape[1]), ret_norm_g[j], batch, seq)
            o_d = _sgu(p, sgu_ln_g[j], sgu_ln_b[j], sgu_w[j], sgu_b[j])
            xt = _out_proj(o_c.reshape(t, -1), o_d, cd_w_out, j, xt)
        xt = _ffn(xt, norm_ffn_g, ffn_w_up, ffn_conv_w, ffn_conv_b, ffn_w_down, final_norm_g,
                  layer, seq=seq, final_norm=(layer == depth - 1))
    return xt.reshape(batch, seq, d)
```

```python
import functools
import math

import numpy as np
import jax
import jax.numpy as jnp
from jax import lax
from jax.experimental import pallas as pl
from jax.experimental.pallas import tpu as pltpu

F32 = jnp.float32
BF16 = jnp.bfloat16

EPS = 1e-6
CHUNK = 64
H_A = 8
DH_A = 128
H_B = 4
DK_B = 128
DV_B = 2 * DK_B
NUM_BUCKETS = 32
MAX_DISTANCE = 128
H_C = 4
DK_C = 128
DV_C = 2 * DK_C
ROPE_BASE = 10000.0
D_D = 1024
G_D = 4
SGU_LEN = 128
CONV_WIDTH = 3

LANES = 128
ATT_BLOCK = 128
ATT_TILE = 512
VMEM_LIMIT = 56 * 1024 * 1024
PROJ_ROWS = 1024
PROJ_TILE = 1024
FFN_ROWS = 1024
FFN_TILE = 512
LOG2E = math.log2(math.e)


def _params(*semantics):
    return pltpu.CompilerParams(dimension_semantics=semantics,
                                vmem_limit_bytes=VMEM_LIMIT)


def _dot(a, b):
    return jnp.dot(a, b, preferred_element_type=F32)


def _dot_nt(a, b):
    return lax.dot_general(a, b, (((1,), (1,)), ((), ())), preferred_element_type=F32)


def _rms_rows(x, g):
    return x * lax.rsqrt(jnp.mean(x * x, axis=-1, keepdims=True) + EPS) * g


def _norm_matmul_kernel(x_ref, g_ref, w_ref, o_ref, h_ref):
    @pl.when(pl.program_id(1) == 0)
    def _():
        h_ref[...] = _rms_rows(x_ref[...], g_ref[...]).astype(BF16)

    o_ref[...] = _dot(h_ref[...], w_ref[...]).astype(o_ref.dtype)


def _norm_matmul(x, g, w, layer, out_dtype, tm=PROJ_ROWS, tn=PROJ_TILE):
    t, d = x.shape
    n = w.shape[2]
    return pl.pallas_call(
        _norm_matmul_kernel,
        grid=(t // tm, n // tn),
        in_specs=[pl.BlockSpec((tm, d), lambda i, j: (i, 0)),
                  pl.BlockSpec((1, d), lambda i, j: (0, 0)),
                  pl.BlockSpec((None, d, tn), lambda i, j: (layer, 0, j))],
        out_specs=pl.BlockSpec((tm, tn), lambda i, j: (i, j)),
        out_shape=jax.ShapeDtypeStruct((t, n), out_dtype),
        scratch_shapes=[pltpu.VMEM((tm, d), BF16)],
        compiler_params=_params("parallel", "arbitrary"),
        name="norm_matmul",
    )(x, g.reshape(1, d), w)


def _out_proj_kernel(a1_ref, a2_ref, w_ref, x_ref, o_ref):
    half = a1_ref.shape[1]
    acc = _dot(a1_ref[...], w_ref[0:half, :])
    acc += _dot(a2_ref[...], w_ref[half:2 * half, :])
    o_ref[...] = x_ref[...] + acc


def _out_proj(a1, a2, w, layer, x, tm=512):
    t, half = a1.shape
    d = w.shape[2]
    return pl.pallas_call(
        _out_proj_kernel,
        grid=(t // tm,),
        in_specs=[pl.BlockSpec((tm, half), lambda i: (i, 0)),
                  pl.BlockSpec((tm, half), lambda i: (i, 0)),
                  pl.BlockSpec((None, 2 * half, d), lambda i: (layer, 0, 0)),
                  pl.BlockSpec((tm, d), lambda i: (i, 0))],
        out_specs=pl.BlockSpec((tm, d), lambda i: (i, 0)),
        out_shape=jax.ShapeDtypeStruct((t, d), F32),
        compiler_params=_params("parallel"),
        name="out_proj",
    )(a1, a2, w, x)


def _ffn_kernel(x_ref, g_ref, wa_ref, wg_ref, cwa_ref, cwg_ref, cba_ref, cbg_ref,
                wd_ref, fg_ref, o_ref, h_ref, act0_ref, act1_ref, carry_ref, up_ref,
                *, seq_tiles, final_norm, n_hidden_tiles):
    i = pl.program_id(0)
    j = pl.program_id(1)
    nj = pl.num_programs(1) - 1
    tm = x_ref.shape[0]

    def up_phase(out_ref):
        h = h_ref[...]

        def conv(w_ref, cw_ref, cb_ref, slot):
            u_ref = up_ref.at[slot]
            u_ref[0:8, :] = carry_ref[j, slot]
            u_ref[8:8 + tm, :] = _dot(h, w_ref[...])
            carry_ref[j, slot] = u_ref[tm:tm + 8, :]
            return (cb_ref[...] + cw_ref[0:1, :] * u_ref[6:6 + tm, :]
                    + cw_ref[1:2, :] * u_ref[7:7 + tm, :] + cw_ref[2:3, :] * u_ref[8:8 + tm, :])

        ca = conv(wa_ref, cwa_ref, cba_ref, 0)
        cg = conv(wg_ref, cwg_ref, cbg_ref, 1)
        out_ref[...] = (cg * (1.0 / (1.0 + jnp.exp(-cg))) * ca).astype(BF16)

    def down_phase(in_ref):
        o_ref[...] += _dot(in_ref[...], wd_ref[...])

    @pl.when(((i % seq_tiles) == 0) & (j < nj))
    def _():
        carry_ref[j] = jnp.zeros(carry_ref.shape[1:], F32)

    @pl.when(j == 0)
    def _():
        x = x_ref[...]
        h_ref[...] = _rms_rows(x, g_ref[...]).astype(BF16)
        o_ref[...] = x
        up_phase(act0_ref)

    @pl.when((j > 0) & (j < nj) & (j % 2 == 1))
    def _():
        up_phase(act1_ref)
        down_phase(act0_ref)

    @pl.when((j > 0) & (j < nj) & (j % 2 == 0))
    def _():
        up_phase(act0_ref)
        down_phase(act1_ref)

    @pl.when(j == nj)
    def _():
        last = act0_ref if (n_hidden_tiles % 2 == 1) else act1_ref
        y = o_ref[...] + _dot(last[...], wd_ref[...])
        if final_norm:
            y = _rms_rows(y, fg_ref[...])
        o_ref[...] = y


def _ffn(x, g, w_up, conv_w, conv_b, w_down, final_g, layer, *, seq, final_norm,
         tm=FFN_ROWS, tf=FFN_TILE):
    t, d = x.shape
    f = w_down.shape[1]
    nj = f // tf
    kern = functools.partial(_ffn_kernel, seq_tiles=seq // tm, final_norm=final_norm,
                             n_hidden_tiles=nj)

    def up_col(j):
        return jnp.minimum(j, nj - 1)

    def down_row(j):
        return jnp.maximum(j - 1, 0)

    return pl.pallas_call(
        kern,
        grid=(t // tm, nj + 1),
        in_specs=[pl.BlockSpec((tm, d), lambda i, j: (i, 0), pipeline_mode=pl.Buffered(1)),
                  pl.BlockSpec((None, 1, d), lambda i, j: (layer, 0, 0)),
                  pl.BlockSpec((None, d, tf), lambda i, j: (layer, 0, up_col(j))),
                  pl.BlockSpec((None, d, tf), lambda i, j: (layer, 0, up_col(j) + nj)),
                  pl.BlockSpec((None, CONV_WIDTH, tf), lambda i, j: (layer, 0, up_col(j))),
                  pl.BlockSpec((None, CONV_WIDTH, tf), lambda i, j: (layer, 0, up_col(j) + nj)),
                  pl.BlockSpec((None, 1, tf), lambda i, j: (layer, 0, up_col(j))),
                  pl.BlockSpec((None, 1, tf), lambda i, j: (layer, 0, up_col(j) + nj)),
                  pl.BlockSpec((None, tf, d), lambda i, j: (layer, down_row(j), 0)),
                  pl.BlockSpec((1, d), lambda i, j: (0, 0))],
        out_specs=pl.BlockSpec((tm, d), lambda i, j: (i, 0)),
        out_shape=jax.ShapeDtypeStruct((t, d), F32),
        scratch_shapes=[pltpu.VMEM((tm, d), BF16),
                        pltpu.VMEM((tm, tf), BF16),
                        pltpu.VMEM((tm, tf), BF16),
                        pltpu.VMEM((nj, 2, 8, tf), F32),
                        pltpu.VMEM((2, tm + 8, tf), F32)],
        compiler_params=_params("arbitrary", "arbitrary"),
        name="conv_ffn",
    )(x, g[:, None, :], w_up, w_up, conv_w, conv_w, conv_b[:, None, :], conv_b[:, None, :],
      w_down, final_g.reshape(1, d))


def _suffix_sum_matrix():
    blk = ATT_BLOCK
    j = np.arange(blk)[:, None]
    s = np.arange(blk)[None, :]
    m = np.concatenate([(j > s).astype(np.float32), np.ones((blk, blk), np.float32)], axis=1)
    return jnp.asarray(np.concatenate([m, m], axis=0), dtype=BF16)


def _stick_kernel(q_ref, k_ref, v_ref, m_ref, o_ref, acc_ref, carry_ref):
    blk = ATT_BLOCK
    tile = q_ref.shape[1]
    nsub = tile // blk
    i = pl.program_id(2)
    q = q_ref[0]
    scale2 = DH_A ** -0.5 * LOG2E
    suffix = m_ref[...]

    acc_ref[...] = jnp.zeros_like(acc_ref)
    carry_ref[...] = jnp.zeros_like(carry_ref)

    def step(j, diagonal):
        start = pl.multiple_of(j * tile, tile)
        k = k_ref[0, pl.ds(start, tile), :]
        v = v_ref[0, pl.ds(start, tile), :]
        z_all = _dot_nt(q, k) * scale2
        carry = carry_ref[...]
        ws = [None] * nsub
        for c in reversed(range(nsub)):
            z = z_all[:, c * blk:(c + 1) * blk]
            if diagonal:
                row = lax.broadcasted_iota(jnp.int32, (tile, blk), 0)
                col = lax.broadcasted_iota(jnp.int32, (tile, blk), 1) + c * blk
                z = jnp.where(col < row, z, -1e4)
            log_sig = jnp.minimum(z, 0.0) - jnp.log2(1.0 + jnp.exp2(-jnp.abs(z)))
            log_1m = log_sig - z
            hi = log_1m.astype(BF16)
            lo = (log_1m - hi.astype(F32)).astype(BF16)
            sums = _dot(jnp.concatenate([hi, lo], axis=1), suffix)
            ws[c] = jnp.exp2(log_sig + sums[:, :blk] + carry).astype(BF16)
            carry = carry + sums[:, blk:]
        carry_ref[...] = carry
        acc_ref[...] += _dot(jnp.concatenate(ws, axis=1), v)

    step(i, True)

    def body(t, _):
        step(i - 1 - t, False)
        return 0

    lax.fori_loop(0, i, body, 0)
    o_ref[0] = acc_ref[...].astype(o_ref.dtype)


def _stick_breaking(p, batch, seq):
    blk = ATT_BLOCK
    tile = ATT_TILE
    return pl.pallas_call(
        _stick_kernel,
        grid=(batch, H_A, seq // tile),
        in_specs=[pl.BlockSpec((1, tile, DH_A), lambda b, h, i: (b, i, h)),
                  pl.BlockSpec((1, seq, DH_A), lambda b, h, i: (b, 0, H_A + h)),
                  pl.BlockSpec((1, seq, DH_A), lambda b, h, i: (b, 0, 2 * H_A + h)),
                  pl.BlockSpec((2 * blk, 2 * blk), lambda b, h, i: (0, 0))],
        out_specs=pl.BlockSpec((1, tile, DH_A), lambda b, h, i: (b, i, h)),
        out_shape=jax.ShapeDtypeStruct((batch, seq, H_A * DH_A), BF16),
        scratch_shapes=[pltpu.VMEM((tile, DH_A), F32),
                        pltpu.VMEM((tile, blk), F32)],
        compiler_params=_params("arbitrary", "arbitrary", "arbitrary"),
        name="stick_breaking",
    )(p, p, p, _suffix_sum_matrix())


def _rel_bucket(rel):
    nb = NUM_BUCKETS // 2
    max_exact = nb // 2
    ret = jnp.where(rel > 0, nb, 0)
    n = jnp.abs(rel)
    n_f = jnp.maximum(n, 1).astype(F32)
    large = max_exact + (jnp.log(n_f / max_exact) / math.log(MAX_DISTANCE / max_exact)
                         * (nb - max_exact)).astype(jnp.int32)
    large = jnp.minimum(large, nb - 1)
    return ret + jnp.where(n < max_exact, n, large)


def _near_buckets():
    blk = ATT_BLOCK
    qpos = blk + jnp.arange(blk)
    kpos = jnp.arange(2 * blk)
    near = _rel_bucket(kpos[None, :] - qpos[:, None]).astype(jnp.int32)
    far = _rel_bucket(jnp.full((1,), -2 * blk, jnp.int32)).astype(jnp.int32)
    return near, far


def _diff_kernel(far_ref, relb_ref, q_ref, k_ref, v_ref, bucket_ref, lam_ref, g_ref, o_ref,
                 bias_ref, m_ref, l_ref, acc_ref, *, lam_init):
    blk = ATT_BLOCK
    tile = q_ref.shape[1]
    nsub = tile // blk
    h = pl.program_id(1)
    i = pl.program_id(2)
    scale2 = DK_B ** -0.5 * LOG2E

    @pl.when(i == 0)
    def _():
        bucket = bucket_ref[...]
        far = relb_ref[far_ref[0], h]
        near = jnp.zeros((blk, 2 * blk), F32)
        for b in range(NUM_BUCKETS):
            near = jnp.where(bucket == b, relb_ref[b, h], near)
        near = (near - far) * LOG2E
        qpos = blk + lax.broadcasted_iota(jnp.int32, (blk, 2 * blk), 0)
        kpos = lax.broadcasted_iota(jnp.int32, (blk, 2 * blk), 1)
        near = jnp.where((kpos // CHUNK) <= (qpos // CHUNK), near, -jnp.inf)
        bias_ref[...] = jnp.zeros_like(bias_ref)
        for a in range(nsub):
            rows = slice(a * blk, (a + 1) * blk)
            bias_ref[0, rows, a * blk:(a + 1) * blk] = near[:, blk:]
            if a >= 1:
                bias_ref[0, rows, (a - 1) * blk:a * blk] = near[:, :blk]
            if a + 1 < nsub:
                bias_ref[0, rows, (a + 1) * blk:] = jnp.full((blk, tile - (a + 1) * blk),
                                                             -jnp.inf, F32)
        bias_ref[1, 0:blk, (nsub - 1) * blk:] = near[:, :blk]

    m_ref[...] = jnp.full_like(m_ref, -jnp.inf)
    l_ref[...] = jnp.zeros_like(l_ref)
    acc_ref[...] = jnp.zeros_like(acc_ref)

    def step(j, bias):
        start = pl.multiple_of(j * tile, tile)
        kk = k_ref[0, pl.ds(start, tile), :]
        v = v_ref[0, pl.ds(start, tile), :]
        for c in range(2):
            s = _dot_nt(q_ref[0, :, c * DK_B:(c + 1) * DK_B],
                        kk[:, c * DK_B:(c + 1) * DK_B]) * scale2
            if bias is not None:
                s = s + bias
            subs = [s[:, u * blk:(u + 1) * blk] for u in range(nsub)]
            m_prev = m_ref[c]
            m_cur = functools.reduce(jnp.maximum, subs)
            m_new = jnp.maximum(m_prev, jnp.max(m_cur, axis=-1, keepdims=True))
            alpha = jnp.exp2(m_prev - m_new)
            ps = [jnp.exp2(u - m_new) for u in subs]
            l_ref[c] = alpha * l_ref[c] + functools.reduce(jnp.add, ps)
            pv = _dot(jnp.concatenate([u.astype(BF16) for u in ps], axis=1), v)
            acc_ref[c] = jnp.concatenate([alpha] * (DV_B // blk), axis=1) * acc_ref[c] + pv
            m_ref[c] = m_new

    step(i, bias_ref[0])

    @pl.when(i > 0)
    def _():
        step(i - 1, bias_ref[1])

    def body(t, _):
        step(t, None)
        return 0

    lax.fori_loop(0, jnp.maximum(i - 1, 0), body, 0)

    lv = lam_ref[...]
    lam = (jnp.exp(jnp.sum(lv[0:1] * lv[1:2], axis=-1, keepdims=True))
           - jnp.exp(jnp.sum(lv[2:3] * lv[3:4], axis=-1, keepdims=True)) + lam_init)
    l1 = jnp.sum(l_ref[0], axis=-1, keepdims=True)
    l2 = jnp.sum(l_ref[1], axis=-1, keepdims=True)
    o = acc_ref[0] / l1 - lam * (acc_ref[1] / l2)
    o_ref[0] = (_rms_rows(o, g_ref[...]) * (1.0 - lam_init)).astype(o_ref.dtype)


def _diff_attention(p, rel_bias, lam_vecs, subln_g, lam_init, batch, seq):
    blk = ATT_BLOCK
    tile = ATT_TILE
    q_col = 3 * H_A * DH_A // DV_B
    k_col = q_col + H_B
    v_col = k_col + H_B
    near, far = _near_buckets()
    kern = functools.partial(_diff_kernel, lam_init=lam_init)
    grid_spec = pltpu.PrefetchScalarGridSpec(
        num_scalar_prefetch=2,
        grid=(batch, H_B, seq // tile),
        in_specs=[pl.BlockSpec((1, tile, DV_B), lambda b, h, i, *_: (b, i, q_col + h)),
                  pl.BlockSpec((1, seq, DV_B), lambda b, h, i, *_: (b, 0, k_col + h)),
                  pl.BlockSpec((1, seq, DV_B), lambda b, h, i, *_: (b, 0, v_col + h)),
                  pl.BlockSpec((blk, 2 * blk), lambda b, h, i, *_: (0, 0)),
                  pl.BlockSpec((4, DK_B), lambda b, h, i, *_: (0, 0)),
                  pl.BlockSpec((1, DV_B), lambda b, h, i, *_: (0, 0))],
        out_specs=pl.BlockSpec((1, tile, DV_B), lambda b, h, i, *_: (b, i, h)),
        scratch_shapes=[pltpu.VMEM((2, tile, tile), F32),
                        pltpu.VMEM((2, tile, blk), F32),
                        pltpu.VMEM((2, tile, blk), F32),
                        pltpu.VMEM((2, tile, DV_B), F32)])
    return pl.pallas_call(
        kern,
        grid_spec=grid_spec,
        out_shape=jax.ShapeDtypeStruct((batch, seq, H_B * DV_B), BF16),
        compiler_params=_params("arbitrary", "arbitrary", "arbitrary"),
        name="diff_attention",
    )(far, rel_bias.astype(F32), p, p, p, near, lam_vecs.astype(F32), subln_g.reshape(1, DV_B))


def _rotary_tables(seq):
    inv_freq = ROPE_BASE ** (-jnp.arange(0, DK_C, 2, dtype=F32) / DK_C)
    ang = jnp.arange(seq, dtype=F32)[:, None] * inv_freq[None, :]
    cos = jnp.concatenate([jnp.cos(ang), jnp.cos(ang)], axis=-1)
    sin = jnp.concatenate([-jnp.sin(ang), jnp.sin(ang)], axis=-1)
    return cos, sin


def _decay_tables():
    log_g = jnp.log(1.0 - 2.0 ** (-5.0 - jnp.arange(H_C, dtype=F32)))
    idx = jnp.arange(CHUNK, dtype=F32)
    intra = jnp.exp(log_g[:, None, None] * jnp.abs(idx[:, None] - idx[None, :]))
    q_decay = jnp.exp(log_g[:, None] * (idx + 1.0))
    k_decay = jnp.exp(log_g[:, None] * (CHUNK - 1.0 - idx))
    chunk_decay = jnp.exp(log_g * CHUNK)
    return (intra,
            jnp.broadcast_to(q_decay[:, :, None], (H_C, CHUNK, DV_C)),
            jnp.broadcast_to(k_decay[:, :, None], (H_C, CHUNK, DK_C)),
            jnp.broadcast_to(chunk_decay[:, None, None], (H_C, 1, DV_C)))


def _retention_kernel(q_ref, k_ref, v_ref, gate_ref, cos_ref, sin_ref, intra_ref, qd_ref,
                      kd_ref, cd_ref, g_ref, o_ref, state_ref):
    rows = q_ref.shape[1]

    @pl.when(pl.program_id(2) == 0)
    def _():
        state_ref[...] = jnp.zeros_like(state_ref)

    cos = cos_ref[...]
    sin = sin_ref[...]

    def rotary(t):
        return t * cos + pltpu.roll(t, DK_C // 2, 1) * sin

    q = (rotary(q_ref[0]) * (DK_C ** -0.5)).astype(BF16)
    k = rotary(k_ref[0])
    v = v_ref[0].astype(BF16)
    intra_decay = intra_ref[0]
    q_decay = qd_ref[0]
    k_decay = kd_ref[0]
    chunk_decay = cd_ref[0]

    state = state_ref[...]
    for c in range(rows // CHUNK):
        sl = slice(c * CHUNK, (c + 1) * CHUNK)
        qc = q[sl]
        kc = k[sl]
        vc = v[sl]
        scores = _dot_nt(qc, kc.astype(BF16)) * intra_decay
        y = _dot(scores.astype(BF16), vc) + _dot(qc, state.astype(BF16)) * q_decay
        kv = _dot((kc * k_decay).T.astype(BF16), vc)
        state = chunk_decay * state + kv
        gate = gate_ref[0, sl, :]
        y = _rms_rows(y, g_ref[...])
        o_ref[0, sl, :] = (gate * (1.0 / (1.0 + jnp.exp(-gate))) * y).astype(o_ref.dtype)
    state_ref[...] = state


def _retention(p, ret_norm_g, batch, seq, rows=512):
    k_col = H_C
    v_col = 2 * H_C * DK_C // DV_C
    gate_col = v_col + H_C
    cos, sin = _rotary_tables(seq)
    intra, q_decay, k_decay, chunk_decay = _decay_tables()
    return pl.pallas_call(
        _retention_kernel,
        grid=(batch, H_C, seq // rows),
        in_specs=[pl.BlockSpec((1, rows, DK_C), lambda b, h, i: (b, i, h)),
                  pl.BlockSpec((1, rows, DK_C), lambda b, h, i: (b, i, k_col + h)),
                  pl.BlockSpec((1, rows, DV_C), lambda b, h, i: (b, i, v_col + h)),
                  pl.BlockSpec((1, rows, DV_C), lambda b, h, i: (b, i, gate_col + h)),
                  pl.BlockSpec((rows, DK_C), lambda b, h, i: (i, 0)),
                  pl.BlockSpec((rows, DK_C), lambda b, h, i: (i, 0)),
                  pl.BlockSpec((1, CHUNK, CHUNK), lambda b, h, i: (h, 0, 0)),
                  pl.BlockSpec((1, CHUNK, DV_C), lambda b, h, i: (h, 0, 0)),
                  pl.BlockSpec((1, CHUNK, DK_C), lambda b, h, i: (h, 0, 0)),
                  pl.BlockSpec((1, 1, DV_C), lambda b, h, i: (h, 0, 0)),
                  pl.BlockSpec((1, DV_C), lambda b, h, i: (0, 0))],
        out_specs=pl.BlockSpec((1, rows, DV_C), lambda b, h, i: (b, i, h)),
        out_shape=jax.ShapeDtypeStruct((batch, seq, H_C * DV_C), BF16),
        scratch_shapes=[pltpu.VMEM((DK_C, DV_C), F32)],
        compiler_params=_params("parallel", "parallel", "arbitrary"),
        name="retention",
    )(p, p, p, p, cos, sin, intra, q_decay, k_decay, chunk_decay, ret_norm_g.reshape(1, DV_C))


def _gelu_tanh(x):
    return 0.5 * x * (1.0 + jnp.tanh(math.sqrt(2.0 / math.pi) * (x + 0.044715 * (x * x * x))))


def _sgu_kernel(zu_ref, zv_ref, lg_ref, lb_ref, w_ref, b_ref, o_ref):
    rows = zu_ref.shape[0]
    width = D_D // G_D
    v = _gelu_tanh(zv_ref[...])
    mu = jnp.mean(v, axis=-1, keepdims=True)
    var = jnp.mean(jnp.square(v - mu), axis=-1, keepdims=True)
    vn = ((v - mu) * lax.rsqrt(var + EPS) * lg_ref[...] + lb_ref[...]).astype(BF16)
    pos_i = lax.broadcasted_iota(jnp.int32, (SGU_LEN, SGU_LEN), 0)
    pos_j = lax.broadcasted_iota(jnp.int32, (SGU_LEN, SGU_LEN), 1)
    mask = (pos_j // CHUNK) <= (pos_i // CHUNK)
    for g in range(G_D):
        w = jnp.where(mask, w_ref[g], 0.0).astype(BF16)
        bias = b_ref[g]
        cols = slice(g * width, (g + 1) * width)
        for r in range(rows // SGU_LEN):
            sl = slice(r * SGU_LEN, (r + 1) * SGU_LEN)
            gate = _dot(w, vn[sl, cols]) + bias
            o_ref[sl, cols] = (_gelu_tanh(zu_ref[sl, cols]) * gate).astype(o_ref.dtype)


def _sgu(p, ln_g, ln_b, w_s, b_s, rows=512):
    t = p.shape[0]
    u_col = (2 * H_C * DK_C + 2 * H_C * DV_C) // D_D
    return pl.pallas_call(
        _sgu_kernel,
        grid=(t // rows,),
        in_specs=[pl.BlockSpec((rows, D_D), lambda i: (i, u_col)),
                  pl.BlockSpec((rows, D_D), lambda i: (i, u_col + 1)),
                  pl.BlockSpec((1, D_D), lambda i: (0, 0)),
                  pl.BlockSpec((1, D_D), lambda i: (0, 0)),
                  pl.BlockSpec((G_D, SGU_LEN, SGU_LEN), lambda i: (0, 0, 0)),
                  pl.BlockSpec((G_D, SGU_LEN, 1), lambda i: (0, 0, 0))],
        out_specs=pl.BlockSpec((rows, D_D), lambda i: (i, 0)),
        out_shape=jax.ShapeDtypeStruct((t, D_D), BF16),
        compiler_params=_params("parallel"),
        name="spatial_gate",
    )(p, p, ln_g.reshape(1, D_D), ln_b.reshape(1, D_D), w_s, b_s.reshape(G_D, SGU_LEN, 1))


def kernel(x, norm_mix_g, norm_ffn_g, final_norm_g, rel_bias, ab_w_in, ab_w_out, diff_lambda,
           diff_subln_g, cd_w_in, cd_w_out, ret_norm_g, sgu_ln_g, sgu_ln_b, sgu_w, sgu_b,
           ffn_w_up, ffn_conv_w, ffn_conv_b, ffn_w_down):
    batch, seq, d = x.shape
    t = batch * seq
    depth = norm_mix_g.shape[0]
    xt = x.reshape(t, d)
    ab_w_in, ab_w_out, cd_w_in, cd_w_out, ffn_w_up, ffn_w_down = (
        w.astype(BF16) for w in (ab_w_in, ab_w_out, cd_w_in, cd_w_out, ffn_w_up, ffn_w_down))
    for layer in range(depth):
        j = layer // 2
        if layer % 2 == 0:
            lam_init = 0.8 - 0.6 * math.exp(-0.3 * layer)
            p = _norm_matmul(xt, norm_mix_g[layer], ab_w_in, j, BF16)
            p3 = p.reshape(batch, seq, p.shape[1])
            o_a = _stick_breaking(p3, batch, seq)
            o_b = _diff_attention(p3, rel_bias, diff_lambda[j], diff_subln_g[j], lam_init,
                                  batch, seq)
            xt = _out_proj(o_a.reshape(t, -1), o_b.reshape(t, -1), ab_w_out, j, xt)
        else:
            p = _norm_matmul(xt, norm_mix_g[layer], cd_w_in, j, F32)
            o_c = _retention(p.reshape(batch, seq, p.shape[1]), ret_norm_g[j], batch, seq)
            o_d = _sgu(p, sgu_ln_g[j], sgu_ln_b[j], sgu_w[j], sgu_b[j])
            xt = _out_proj(o_c.reshape(t, -1), o_d, cd_w_out, j, xt)
        xt = _ffn(xt, norm_ffn_g, ffn_w_up, ffn_conv_w, ffn_conv_b, ffn_w_down, final_norm_g,
                  layer, seq=seq, final_norm=(layer == depth - 1))
    return xt.reshape(batch, seq, d)
```

```python
import functools
import math

import numpy as np
import jax
import jax.numpy as jnp
from jax import lax
from jax.experimental import pallas as pl
from jax.experimental.pallas import tpu as pltpu

F32 = jnp.float32
BF16 = jnp.bfloat16

EPS = 1e-6
CHUNK = 64
H_A = 8
DH_A = 128
H_B = 4
DK_B = 128
DV_B = 2 * DK_B
NUM_BUCKETS = 32
MAX_DISTANCE = 128
H_C = 4
DK_C = 128
DV_C = 2 * DK_C
ROPE_BASE = 10000.0
D_D = 1024
G_D = 4
SGU_LEN = 128
CONV_WIDTH = 3

LANES = 128
ATT_BLOCK = 128
ATT_TILE = 512
VMEM_LIMIT = 56 * 1024 * 1024
PROJ_ROWS = 1024
PROJ_TILE = 1024
FFN_ROWS = 1024
FFN_TILE = 512
LOG2E = math.log2(math.e)
ZERO_WEIGHT_LOG2 = -150.0


def _params(*semantics):
    return pltpu.CompilerParams(dimension_semantics=semantics,
                                vmem_limit_bytes=VMEM_LIMIT)


def _dot(a, b):
    return jnp.dot(a, b, preferred_element_type=F32)


def _dot_nt(a, b):
    return lax.dot_general(a, b, (((1,), (1,)), ((), ())), preferred_element_type=F32)


def _rms_rows(x, g):
    return x * lax.rsqrt(jnp.mean(x * x, axis=-1, keepdims=True) + EPS) * g


def _norm_matmul_kernel(x_ref, g_ref, w_ref, o_ref, h_ref):
    @pl.when(pl.program_id(1) == 0)
    def _():
        h_ref[...] = _rms_rows(x_ref[...], g_ref[...]).astype(BF16)

    o_ref[...] = _dot(h_ref[...], w_ref[...]).astype(o_ref.dtype)


def _norm_matmul(x, g, w, layer, out_dtype, tm=PROJ_ROWS, tn=PROJ_TILE):
    t, d = x.shape
    n = w.shape[2]
    return pl.pallas_call(
        _norm_matmul_kernel,
        grid=(t // tm, n // tn),
        in_specs=[pl.BlockSpec((tm, d), lambda i, j: (i, 0)),
                  pl.BlockSpec((1, d), lambda i, j: (0, 0)),
                  pl.BlockSpec((None, d, tn), lambda i, j: (layer, 0, j))],
        out_specs=pl.BlockSpec((tm, tn), lambda i, j: (i, j)),
        out_shape=jax.ShapeDtypeStruct((t, n), out_dtype),
        scratch_shapes=[pltpu.VMEM((tm, d), BF16)],
        compiler_params=_params("parallel", "arbitrary"),
        name="norm_matmul",
    )(x, g.reshape(1, d), w)


def _out_proj_kernel(a1_ref, a2_ref, w_ref, x_ref, o_ref):
    half = a1_ref.shape[1]
    acc = _dot(a1_ref[...], w_ref[0:half, :])
    acc += _dot(a2_ref[...], w_ref[half:2 * half, :])
    o_ref[...] = x_ref[...] + acc


def _out_proj(a1, a2, w, layer, x, tm=512):
    t, half = a1.shape
    d = w.shape[2]
    return pl.pallas_call(
        _out_proj_kernel,
        grid=(t // tm,),
        in_specs=[pl.BlockSpec((tm, half), lambda i: (i, 0)),
                  pl.BlockSpec((tm, half), lambda i: (i, 0)),
                  pl.BlockSpec((None, 2 * half, d), lambda i: (layer, 0, 0)),
                  pl.BlockSpec((tm, d), lambda i: (i, 0))],
        out_specs=pl.BlockSpec((tm, d), lambda i: (i, 0)),
        out_shape=jax.ShapeDtypeStruct((t, d), F32),
        compiler_params=_params("parallel"),
        name="out_proj",
    )(a1, a2, w, x)


def _ffn_kernel(x_ref, g_ref, wa_ref, wg_ref, cwa_ref, cwg_ref, cba_ref, cbg_ref,
                wd_ref, fg_ref, o_ref, h_ref, act0_ref, act1_ref, carry_ref, up_ref,
                *, seq_tiles, final_norm, n_hidden_tiles):
    i = pl.program_id(0)
    j = pl.program_id(1)
    nj = pl.num_programs(1) - 1
    tm = x_ref.shape[0]

    def up_phase(out_ref):
        h = h_ref[...]

        def conv(w_ref, cw_ref, cb_ref, slot):
            u_ref = up_ref.at[slot]
            u_ref[0:8, :] = carry_ref[j, slot]
            u_ref[8:8 + tm, :] = _dot(h, w_ref[...])
            carry_ref[j, slot] = u_ref[tm:tm + 8, :]
            return (cb_ref[...] + cw_ref[0:1, :] * u_ref[6:6 + tm, :]
                    + cw_ref[1:2, :] * u_ref[7:7 + tm, :] + cw_ref[2:3, :] * u_ref[8:8 + tm, :])

        ca = conv(wa_ref, cwa_ref, cba_ref, 0)
        cg = conv(wg_ref, cwg_ref, cbg_ref, 1)
        out_ref[...] = (cg * (1.0 / (1.0 + jnp.exp(-cg))) * ca).astype(BF16)

    def down_phase(in_ref):
        o_ref[...] += _dot(in_ref[...], wd_ref[...])

    @pl.when(((i % seq_tiles) == 0) & (j < nj))
    def _():
        carry_ref[j] = jnp.zeros(carry_ref.shape[1:], F32)

    @pl.when(j == 0)
    def _():
        x = x_ref[...]
        h_ref[...] = _rms_rows(x, g_ref[...]).astype(BF16)
        o_ref[...] = x
        up_phase(act0_ref)

    @pl.when((j > 0) & (j < nj) & (j % 2 == 1))
    def _():
        up_phase(act1_ref)
        down_phase(act0_ref)

    @pl.when((j > 0) & (j < nj) & (j % 2 == 0))
    def _():
        up_phase(act0_ref)
        down_phase(act1_ref)

    @pl.when(j == nj)
    def _():
        last = act0_ref if (n_hidden_tiles % 2 == 1) else act1_ref
        y = o_ref[...] + _dot(last[...], wd_ref[...])
        if final_norm:
            y = _rms_rows(y, fg_ref[...])
        o_ref[...] = y


def _ffn(x, g, w_up, conv_w, conv_b, w_down, final_g, layer, *, seq, final_norm,
         tm=FFN_ROWS, tf=FFN_TILE):
    t, d = x.shape
    f = w_down.shape[1]
    nj = f // tf
    kern = functools.partial(_ffn_kernel, seq_tiles=seq // tm, final_norm=final_norm,
                             n_hidden_tiles=nj)

    def up_col(j):
        return jnp.minimum(j, nj - 1)

    def down_row(j):
        return jnp.maximum(j - 1, 0)

    return pl.pallas_call(
        kern,
        grid=(t // tm, nj + 1),
        in_specs=[pl.BlockSpec((tm, d), lambda i, j: (i, 0), pipeline_mode=pl.Buffered(1)),
                  pl.BlockSpec((None, 1, d), lambda i, j: (layer, 0, 0)),
                  pl.BlockSpec((None, d, tf), lambda i, j: (layer, 0, up_col(j))),
                  pl.BlockSpec((None, d, tf), lambda i, j: (layer, 0, up_col(j) + nj)),
                  pl.BlockSpec((None, CONV_WIDTH, tf), lambda i, j: (layer, 0, up_col(j))),
                  pl.BlockSpec((None, CONV_WIDTH, tf), lambda i, j: (layer, 0, up_col(j) + nj)),
                  pl.BlockSpec((None, 1, tf), lambda i, j: (layer, 0, up_col(j))),
                  pl.BlockSpec((None, 1, tf), lambda i, j: (layer, 0, up_col(j) + nj)),
                  pl.BlockSpec((None, tf, d), lambda i, j: (layer, down_row(j), 0)),
                  pl.BlockSpec((1, d), lambda i, j: (0, 0))],
        out_specs=pl.BlockSpec((tm, d), lambda i, j: (i, 0)),
        out_shape=jax.ShapeDtypeStruct((t, d), F32),
        scratch_shapes=[pltpu.VMEM((tm, d), BF16),
                        pltpu.VMEM((tm, tf), BF16),
                        pltpu.VMEM((tm, tf), BF16),
                        pltpu.VMEM((nj, 2, 8, tf), F32),
                        pltpu.VMEM((2, tm + 8, tf), F32)],
        compiler_params=_params("arbitrary", "arbitrary"),
        name="conv_ffn",
    )(x, g[:, None, :], w_up, w_up, conv_w, conv_w, conv_b[:, None, :], conv_b[:, None, :],
      w_down, final_g.reshape(1, d))


def _suffix_sum_matrix():
    blk = ATT_BLOCK
    j = np.arange(blk)[:, None]
    s = np.arange(blk)[None, :]
    m = np.concatenate([(j > s).astype(np.float32), np.ones((blk, blk), np.float32)], axis=1)
    return jnp.asarray(np.concatenate([m, m], axis=0), dtype=BF16)


def _stick_kernel(q_ref, k_ref, v_ref, m_ref, o_ref, acc_ref, carry_ref):
    blk = ATT_BLOCK
    tile = q_ref.shape[1]
    nsub = tile // blk
    i = pl.program_id(2)
    q = q_ref[0]
    scale2 = DH_A ** -0.5 * LOG2E
    suffix = m_ref[...]

    acc_ref[...] = jnp.zeros_like(acc_ref)
    carry_ref[...] = jnp.zeros_like(carry_ref)

    def step(j, diagonal):
        start = pl.multiple_of(j * tile, tile)
        k = k_ref[0, pl.ds(start, tile), :]
        v = v_ref[0, pl.ds(start, tile), :]
        z_all = _dot_nt(q, k) * scale2
        carry = carry_ref[...]
        ws = [None] * nsub
        for c in reversed(range(nsub)):
            z = z_all[:, c * blk:(c + 1) * blk]
            if diagonal:
                row = lax.broadcasted_iota(jnp.int32, (tile, blk), 0)
                col = lax.broadcasted_iota(jnp.int32, (tile, blk), 1) + c * blk
                z = jnp.where(col < row, z, -1e4)
            log_sig = jnp.minimum(z, 0.0) - jnp.log2(1.0 + jnp.exp2(-jnp.abs(z)))
            log_1m = log_sig - z
            hi = log_1m.astype(BF16)
            lo = (log_1m - hi.astype(F32)).astype(BF16)
            sums = _dot(jnp.concatenate([hi, lo], axis=1), suffix)
            ws[c] = jnp.exp2(log_sig + sums[:, :blk] + carry).astype(BF16)
            carry = carry + sums[:, blk:]
        carry_ref[...] = carry
        acc_ref[...] += _dot(jnp.concatenate(ws, axis=1), v)

    step(i, True)

    def more(t):
        return jnp.logical_and(t < i, jnp.max(carry_ref[...]) > ZERO_WEIGHT_LOG2)

    def body(t):
        step(i - 1 - t, False)
        return t + 1

    lax.while_loop(more, body, 0)
    o_ref[0] = acc_ref[...].astype(o_ref.dtype)


def _stick_breaking(p, batch, seq):
    blk = ATT_BLOCK
    tile = ATT_TILE
    return pl.pallas_call(
        _stick_kernel,
        grid=(batch, H_A, seq // tile),
        in_specs=[pl.BlockSpec((1, tile, DH_A), lambda b, h, i: (b, i, h)),
                  pl.BlockSpec((1, seq, DH_A), lambda b, h, i: (b, 0, H_A + h)),
                  pl.BlockSpec((1, seq, DH_A), lambda b, h, i: (b, 0, 2 * H_A + h)),
                  pl.BlockSpec((2 * blk, 2 * blk), lambda b, h, i: (0, 0))],
        out_specs=pl.BlockSpec((1, tile, DH_A), lambda b, h, i: (b, i, h)),
        out_shape=jax.ShapeDtypeStruct((batch, seq, H_A * DH_A), BF16),
        scratch_shapes=[pltpu.VMEM((tile, DH_A), F32),
                        pltpu.VMEM((tile, blk), F32)],
        compiler_params=_params("arbitrary", "arbitrary", "arbitrary"),
        name="stick_breaking",
    )(p, p, p, _suffix_sum_matrix())


def _rel_bucket(rel):
    nb = NUM_BUCKETS // 2
    max_exact = nb // 2
    ret = jnp.where(rel > 0, nb, 0)
    n = jnp.abs(rel)
    n_f = jnp.maximum(n, 1).astype(F32)
    large = max_exact + (jnp.log(n_f / max_exact) / math.log(MAX_DISTANCE / max_exact)
                         * (nb - max_exact)).astype(jnp.int32)
    large = jnp.minimum(large, nb - 1)
    return ret + jnp.where(n < max_exact, n, large)


def _near_buckets():
    blk = ATT_BLOCK
    qpos = blk + jnp.arange(blk)
    kpos = jnp.arange(2 * blk)
    near = _rel_bucket(kpos[None, :] - qpos[:, None]).astype(jnp.int32)
    far = _rel_bucket(jnp.full((1,), -2 * blk, jnp.int32)).astype(jnp.int32)
    return near, far


def _diff_kernel(far_ref, relb_ref, q_ref, k_ref, v_ref, bucket_ref, lam_ref, g_ref, o_ref,
                 bias_ref, m_ref, l_ref, acc_ref, s0_ref, s1_ref, *, lam_init):
    blk = ATT_BLOCK
    tile = q_ref.shape[1]
    nsub = tile // blk
    h = pl.program_id(1)
    i = pl.program_id(2)
    scale2 = DK_B ** -0.5 * LOG2E

    @pl.when(i == 0)
    def _():
        bucket = bucket_ref[...]
        far = relb_ref[far_ref[0], h]
        near = jnp.zeros((blk, 2 * blk), F32)
        for b in range(NUM_BUCKETS):
            near = jnp.where(bucket == b, relb_ref[b, h], near)
        near = (near - far) * LOG2E
        qpos = blk + lax.broadcasted_iota(jnp.int32, (blk, 2 * blk), 0)
        kpos = lax.broadcasted_iota(jnp.int32, (blk, 2 * blk), 1)
        near = jnp.where((kpos // CHUNK) <= (qpos // CHUNK), near, -jnp.inf)
        bias_ref[...] = jnp.zeros_like(bias_ref)
        for a in range(nsub):
            rows = slice(a * blk, (a + 1) * blk)
            bias_ref[0, rows, a * blk:(a + 1) * blk] = near[:, blk:]
            if a >= 1:
                bias_ref[0, rows, (a - 1) * blk:a * blk] = near[:, :blk]
            if a + 1 < nsub:
                bias_ref[0, rows, (a + 1) * blk:] = jnp.full((blk, tile - (a + 1) * blk),
                                                             -jnp.inf, F32)
        bias_ref[1, 0:blk, (nsub - 1) * blk:] = near[:, :blk]

    m_ref[...] = jnp.full_like(m_ref, -jnp.inf)
    l_ref[...] = jnp.zeros_like(l_ref)
    acc_ref[...] = jnp.zeros_like(acc_ref)

    def key_tile(idx):
        return jnp.where(idx == 0, i, jnp.where(idx == 1, i - 1, idx - 2))

    def logits_into(s_ref, idx):
        j = key_tile(jnp.minimum(idx, i))
        kk = k_ref[0, pl.ds(pl.multiple_of(j * tile, tile), tile), :]
        for c in range(2):
            s_ref[c] = _dot_nt(q_ref[0, :, c * DK_B:(c + 1) * DK_B],
                               kk[:, c * DK_B:(c + 1) * DK_B])

    def consume(s_ref, idx):
        j = key_tile(idx)
        v = v_ref[0, pl.ds(pl.multiple_of(j * tile, tile), tile), :]
        bias = bias_ref[jnp.minimum(idx, 2)]
        probs = []
        for c in range(2):
            s = s_ref[c] * scale2 + bias
            subs = [s[:, u * blk:(u + 1) * blk] for u in range(nsub)]
            m_prev = m_ref[c]
            m_cur = functools.reduce(jnp.maximum, subs)
            m_new = jnp.maximum(m_prev, jnp.max(m_cur, axis=-1, keepdims=True))
            alpha = jnp.exp2(m_prev - m_new)
            ps = [jnp.exp2(u - m_new) for u in subs]
            l_ref[c] = alpha * l_ref[c] + functools.reduce(jnp.add, ps)
            m_ref[c] = m_new
            probs.append((alpha, jnp.concatenate([u.astype(BF16) for u in ps], axis=1)))
        for c in range(2):
            alpha, p = probs[c]
            acc_ref[c] = jnp.concatenate([alpha] * (DV_B // blk), axis=1) * acc_ref[c] + _dot(p, v)

    logits_into(s0_ref, 0)

    def pair(t, _):
        idx = 2 * t
        logits_into(s1_ref, idx + 1)
        consume(s0_ref, idx)
        logits_into(s0_ref, idx + 2)
        consume(s1_ref, idx + 1)
        return 0

    lax.fori_loop(0, (i + 1) // 2, pair, 0)

    @pl.when(i % 2 == 0)
    def _():
        consume(s0_ref, i)

    lv = lam_ref[...]
    lam = (jnp.exp(jnp.sum(lv[0:1] * lv[1:2], axis=-1, keepdims=True))
           - jnp.exp(jnp.sum(lv[2:3] * lv[3:4], axis=-1, keepdims=True)) + lam_init)
    l1 = jnp.sum(l_ref[0], axis=-1, keepdims=True)
    l2 = jnp.sum(l_ref[1], axis=-1, keepdims=True)
    o = acc_ref[0] / l1 - lam * (acc_ref[1] / l2)
    o_ref[0] = (_rms_rows(o, g_ref[...]) * (1.0 - lam_init)).astype(o_ref.dtype)


def _diff_attention(p, rel_bias, lam_vecs, subln_g, lam_init, batch, seq):
    blk = ATT_BLOCK
    tile = ATT_TILE
    q_col = 3 * H_A * DH_A // DV_B
    k_col = q_col + H_B
    v_col = k_col + H_B
    near, far = _near_buckets()
    kern = functools.partial(_diff_kernel, lam_init=lam_init)
    grid_spec = pltpu.PrefetchScalarGridSpec(
        num_scalar_prefetch=2,
        grid=(batch, H_B, seq // tile),
        in_specs=[pl.BlockSpec((1, tile, DV_B), lambda b, h, i, *_: (b, i, q_col + h)),
                  pl.BlockSpec((1, seq, DV_B), lambda b, h, i, *_: (b, 0, k_col + h)),
                  pl.BlockSpec((1, seq, DV_B), lambda b, h, i, *_: (b, 0, v_col + h)),
                  pl.BlockSpec((blk, 2 * blk), lambda b, h, i, *_: (0, 0)),
                  pl.BlockSpec((4, DK_B), lambda b, h, i, *_: (0, 0)),
                  pl.BlockSpec((1, DV_B), lambda b, h, i, *_: (0, 0))],
        out_specs=pl.BlockSpec((1, tile, DV_B), lambda b, h, i, *_: (b, i, h)),
        scratch_shapes=[pltpu.VMEM((3, tile, tile), F32),
                        pltpu.VMEM((2, tile, blk), F32),
                        pltpu.VMEM((2, tile, blk), F32),
                        pltpu.VMEM((2, tile, DV_B), F32),
                        pltpu.VMEM((2, tile, tile), F32),
                        pltpu.VMEM((2, tile, tile), F32)])
    return pl.pallas_call(
        kern,
        grid_spec=grid_spec,
        out_shape=jax.ShapeDtypeStruct((batch, seq, H_B * DV_B), BF16),
        compiler_params=_params("arbitrary", "arbitrary", "arbitrary"),
        name="diff_attention",
    )(far, rel_bias.astype(F32), p, p, p, near, lam_vecs.astype(F32), subln_g.reshape(1, DV_B))


def _rotary_tables(seq):
    inv_freq = ROPE_BASE ** (-jnp.arange(0, DK_C, 2, dtype=F32) / DK_C)
    ang = jnp.arange(seq, dtype=F32)[:, None] * inv_freq[None, :]
    cos = jnp.concatenate([jnp.cos(ang), jnp.cos(ang)], axis=-1)
    sin = jnp.concatenate([-jnp.sin(ang), jnp.sin(ang)], axis=-1)
    return cos, sin


def _decay_tables():
    log_g = jnp.log(1.0 - 2.0 ** (-5.0 - jnp.arange(H_C, dtype=F32)))
    idx = jnp.arange(CHUNK, dtype=F32)
    intra = jnp.exp(log_g[:, None, None] * jnp.abs(idx[:, None] - idx[None, :]))
    q_decay = jnp.exp(log_g[:, None] * (idx + 1.0))
    k_decay = jnp.exp(log_g[:, None] * (CHUNK - 1.0 - idx))
    chunk_decay = jnp.exp(log_g * CHUNK)
    return (intra,
            jnp.broadcast_to(q_decay[:, :, None], (H_C, CHUNK, DV_C)),
            jnp.broadcast_to(k_decay[:, :, None], (H_C, CHUNK, DK_C)),
            jnp.broadcast_to(chunk_decay[:, None, None], (H_C, 1, DV_C)))


def _retention_kernel(q_ref, k_ref, v_ref, gate_ref, cos_ref, sin_ref, intra_ref, qd_ref,
                      kd_ref, cd_ref, g_ref, o_ref, state_ref):
    rows = q_ref.shape[1]

    @pl.when(pl.program_id(2) == 0)
    def _():
        state_ref[...] = jnp.zeros_like(state_ref)

    cos = cos_ref[...]
    sin = sin_ref[...]

    def rotary(t):
        return t * cos + pltpu.roll(t, DK_C // 2, 1) * sin

    q = (rotary(q_ref[0]) * (DK_C ** -0.5)).astype(BF16)
    k = rotary(k_ref[0])
    v = v_ref[0].astype(BF16)
    intra_decay = intra_ref[0]
    q_decay = qd_ref[0]
    k_decay = kd_ref[0]
    chunk_decay = cd_ref[0]

    state = state_ref[...]
    for c in range(rows // CHUNK):
        sl = slice(c * CHUNK, (c + 1) * CHUNK)
        qc = q[sl]
        kc = k[sl]
        vc = v[sl]
        scores = _dot_nt(qc, kc.astype(BF16)) * intra_decay
        y = _dot(scores.astype(BF16), vc) + _dot(qc, state.astype(BF16)) * q_decay
        kv = _dot((kc * k_decay).T.astype(BF16), vc)
        state = chunk_decay * state + kv
        gate = gate_ref[0, sl, :]
        y = _rms_rows(y, g_ref[...])
        o_ref[0, sl, :] = (gate * (1.0 / (1.0 + jnp.exp(-gate))) * y).astype(o_ref.dtype)
    state_ref[...] = state


def _retention(p, ret_norm_g, batch, seq, rows=512):
    k_col = H_C
    v_col = 2 * H_C * DK_C // DV_C
    gate_col = v_col + H_C
    cos, sin = _rotary_tables(seq)
    intra, q_decay, k_decay, chunk_decay = _decay_tables()
    return pl.pallas_call(
        _retention_kernel,
        grid=(batch, H_C, seq // rows),
        in_specs=[pl.BlockSpec((1, rows, DK_C), lambda b, h, i: (b, i, h)),
                  pl.BlockSpec((1, rows, DK_C), lambda b, h, i: (b, i, k_col + h)),
                  pl.BlockSpec((1, rows, DV_C), lambda b, h, i: (b, i, v_col + h)),
                  pl.BlockSpec((1, rows, DV_C), lambda b, h, i: (b, i, gate_col + h)),
                  pl.BlockSpec((rows, DK_C), lambda b, h, i: (i, 0)),
                  pl.BlockSpec((rows, DK_C), lambda b, h, i: (i, 0)),
                  pl.BlockSpec((1, CHUNK, CHUNK), lambda b, h, i: (h, 0, 0)),
                  pl.BlockSpec((1, CHUNK, DV_C), lambda b, h, i: (h, 0, 0)),
                  pl.BlockSpec((1, CHUNK, DK_C), lambda b, h, i: (h, 0, 0)),
                  pl.BlockSpec((1, 1, DV_C), lambda b, h, i: (h, 0, 0)),
                  pl.BlockSpec((1, DV_C), lambda b, h, i: (0, 0))],
        out_specs=pl.BlockSpec((1, rows, DV_C), lambda b, h, i: (b, i, h)),
        out_shape=jax.ShapeDtypeStruct((batch, seq, H_C * DV_C), BF16),
        scratch_shapes=[pltpu.VMEM((DK_C, DV_C), F32)],
        compiler_params=_params("parallel", "parallel", "arbitrary"),
        name="retention",
    )(p, p, p, p, cos, sin, intra, q_decay, k_decay, chunk_decay, ret_norm_g.reshape(1, DV_C))


def _gelu_tanh(x):
    return 0.5 * x * (1.0 + jnp.tanh(math.sqrt(2.0 / math.pi) * (x + 0.044715 * (x * x * x))))


def _sgu_kernel(zu_ref, zv_ref, lg_ref, lb_ref, w_ref, b_ref, o_ref):
    rows = zu_ref.shape[0]
    width = D_D // G_D
    v = _gelu_tanh(zv_ref[...])
    mu = jnp.mean(v, axis=-1, keepdims=True)
    var = jnp.mean(jnp.square(v - mu), axis=-1, keepdims=True)
    vn = ((v - mu) * lax.rsqrt(var + EPS) * lg_ref[...] + lb_ref[...]).astype(BF16)
    pos_i = lax.broadcasted_iota(jnp.int32, (SGU_LEN, SGU_LEN), 0)
    pos_j = lax.broadcasted_iota(jnp.int32, (SGU_LEN, SGU_LEN), 1)
    mask = (pos_j // CHUNK) <= (pos_i // CHUNK)
    for g in range(G_D):
        w = jnp.where(mask, w_ref[g], 0.0).astype(BF16)
        bias = b_ref[g]
        cols = slice(g * width, (g + 1) * width)
        for r in range(rows // SGU_LEN):
            sl = slice(r * SGU_LEN, (r + 1) * SGU_LEN)
            gate = _dot(w, vn[sl, cols]) + bias
            o_ref[sl, cols] = (_gelu_tanh(zu_ref[sl, cols]) * gate).astype(o_ref.dtype)


def _sgu(p, ln_g, ln_b, w_s, b_s, rows=512):
    t = p.shape[0]
    u_col = (2 * H_C * DK_C + 2 * H_C * DV_C) // D_D
    return pl.pallas_call(
        _sgu_kernel,
        grid=(t // rows,),
        in_specs=[pl.BlockSpec((rows, D_D), lambda i: (i, u_col)),
                  pl.BlockSpec((rows, D_D), lambda i: (i, u_col + 1)),
                  pl.BlockSpec((1, D_D), lambda i: (0, 0)),
                  pl.BlockSpec((1, D_D), lambda i: (0, 0)),
                  pl.BlockSpec((G_D, SGU_LEN, SGU_LEN), lambda i: (0, 0, 0)),
                  pl.BlockSpec((G_D, SGU_LEN, 1), lambda i: (0, 0, 0))],
        out_specs=pl.BlockSpec((rows, D_D), lambda i: (i, 0)),
        out_shape=jax.ShapeDtypeStruct((t, D_D), BF16),
        compiler_params=_params("parallel"),
        name="spatial_gate",
    )(p, p, ln_g.reshape(1, D_D), ln_b.reshape(1, D_D), w_s, b_s.reshape(G_D, SGU_LEN, 1))


def kernel(x, norm_mix_g, norm_ffn_g, final_norm_g, rel_bias, ab_w_in, ab_w_out, diff_lambda,
           diff_subln_g, cd_w_in, cd_w_out, ret_norm_g, sgu_ln_g, sgu_ln_b, sgu_w, sgu_b,
           ffn_w_up, ffn_conv_w, ffn_conv_b, ffn_w_down):
    batch, seq, d = x.shape
    t = batch * seq
    depth = norm_mix_g.shape[0]
    xt = x.reshape(t, d)
    ab_w_in, ab_w_out, cd_w_in, cd_w_out, ffn_w_up, ffn_w_down = (
        w.astype(BF16) for w in (ab_w_in, ab_w_out, cd_w_in, cd_w_out, ffn_w_up, ffn_w_down))
    for layer in range(depth):
        j = layer // 2
        if layer % 2 == 0:
            lam_init = 0.8 - 0.6 * math.exp(-0.3 * layer)
            p = _norm_matmul(xt, norm_mix_g[layer], ab_w_in, j, BF16)
            p3 = p.reshape(batch, seq, p.shape[1])
            o_a = _stick_breaking(p3, batch, seq)
            o_b = _diff_attention(p3, rel_bias, diff_lambda[j], diff_subln_g[j], lam_init,
                                  batch, seq)
            xt = _out_proj(o_a.reshape(t, -1), o_b.reshape(t, -1), ab_w_out, j, xt)
        else:
            p = _norm_matmul(xt, norm_mix_g[layer], cd_w_in, j, F32)
            o_c = _retention(p.reshape(batch, seq, p.shape[1]), ret_norm_g[j], batch, seq)
            o_d = _sgu(p, sgu_ln_g[j], sgu_ln_b[j], sgu_w[j], sgu_b[j])
            xt = _out_proj(o_c.reshape(t, -1), o_d, cd_w_out, j, xt)
        xt = _ffn(xt, norm_ffn_g, ffn_w_up, ffn_conv_w, ffn_conv_b, ffn_w_down, final_norm_g,
                  layer, seq=seq, final_norm=(layer == depth - 1))
    return xt.reshape(batch, seq, d)
```

```python
import functools
import math

import numpy as np
import jax
import jax.numpy as jnp
from jax import lax
from jax.experimental import pallas as pl
from jax.experimental.pallas import tpu as pltpu

F32 = jnp.float32
BF16 = jnp.bfloat16

EPS = 1e-6
CHUNK = 64
H_A = 8
DH_A = 128
H_B = 4
DK_B = 128
DV_B = 2 * DK_B
NUM_BUCKETS = 32
MAX_DISTANCE = 128
H_C = 4
DK_C = 128
DV_C = 2 * DK_C
ROPE_BASE = 10000.0
D_D = 1024
G_D = 4
SGU_LEN = 128
CONV_WIDTH = 3

LANES = 128
BF16_SUBLANES = 16
ATT_BLOCK = 128
ATT_TILE = 512
VMEM_LIMIT = 56 * 1024 * 1024
PROJ_ROWS = 1024
PROJ_TILE = 1024
FFN_ROWS = 1024
FFN_TILE = 512
LOG2E = math.log2(math.e)
ZERO_WEIGHT_LOG2 = -150.0


def _params(*semantics):
    return pltpu.CompilerParams(dimension_semantics=semantics,
                                vmem_limit_bytes=VMEM_LIMIT)


def _dot(a, b):
    return jnp.dot(a, b, preferred_element_type=F32)


def _dot_nt(a, b):
    return lax.dot_general(a, b, (((1,), (1,)), ((), ())), preferred_element_type=F32)


def _rms_rows(x, g):
    return x * lax.rsqrt(jnp.mean(x * x, axis=-1, keepdims=True) + EPS) * g


def _with_riders(body, n_in, n_riders):
    def kern(*refs):
        ins = refs[:n_in]
        rider_in = refs[n_in:n_in + n_riders]
        out = refs[n_in + n_riders]
        rider_out = refs[n_in + n_riders + 1:n_in + 2 * n_riders + 1]
        scratch = refs[n_in + 2 * n_riders + 1:]
        for src, dst in zip(rider_in, rider_out):
            dst[...] = src[...].astype(BF16)
        body(*ins, out, *scratch)
    return kern


def _rider_specs(riders, grid):
    steps = math.prod(grid)

    def linear_step(*g):
        n = g[0]
        for size, idx in zip(grid[1:], g[1:len(grid)]):
            n = n * size + idx
        return n

    specs, shapes = [], []
    for w in riders:
        rows, cols = w.shape
        block_rows, rem = divmod(rows, steps)
        assert rem == 0 and block_rows % BF16_SUBLANES == 0, (w.shape, steps)
        specs.append(pl.BlockSpec((block_rows, cols), lambda *g: (linear_step(*g), 0)))
        shapes.append(jax.ShapeDtypeStruct((rows, cols), BF16))
    return specs, shapes


def _norm_matmul_kernel(x_ref, g_ref, w_ref, o_ref, h_ref):
    @pl.when(pl.program_id(1) == 0)
    def _():
        h_ref[...] = _rms_rows(x_ref[...], g_ref[...]).astype(BF16)

    o_ref[...] = _dot(h_ref[...], w_ref[...]).astype(o_ref.dtype)


def _norm_matmul(x, g, w, layer, out_dtype, tm=PROJ_ROWS, tn=PROJ_TILE):
    t, d = x.shape
    n = w.shape[2]
    return pl.pallas_call(
        _norm_matmul_kernel,
        grid=(t // tm, n // tn),
        in_specs=[pl.BlockSpec((tm, d), lambda i, j: (i, 0)),
                  pl.BlockSpec((1, d), lambda i, j: (0, 0)),
                  pl.BlockSpec((None, d, tn), lambda i, j: (layer, 0, j))],
        out_specs=pl.BlockSpec((tm, tn), lambda i, j: (i, j)),
        out_shape=jax.ShapeDtypeStruct((t, n), out_dtype),
        scratch_shapes=[pltpu.VMEM((tm, d), BF16)],
        compiler_params=_params("parallel", "arbitrary"),
        name="norm_matmul",
    )(x, g.reshape(1, d), w)


def _out_proj_kernel(a1_ref, a2_ref, w_ref, x_ref, o_ref):
    half = a1_ref.shape[1]
    acc = _dot(a1_ref[...], w_ref[0:half, :])
    acc += _dot(a2_ref[...], w_ref[half:2 * half, :])
    o_ref[...] = x_ref[...] + acc


def _out_proj(a1, a2, w, layer, x, tm=512):
    t, half = a1.shape
    d = w.shape[2]
    return pl.pallas_call(
        _out_proj_kernel,
        grid=(t // tm,),
        in_specs=[pl.BlockSpec((tm, half), lambda i: (i, 0)),
                  pl.BlockSpec((tm, half), lambda i: (i, 0)),
                  pl.BlockSpec((None, 2 * half, d), lambda i: (layer, 0, 0)),
                  pl.BlockSpec((tm, d), lambda i: (i, 0))],
        out_specs=pl.BlockSpec((tm, d), lambda i: (i, 0)),
        out_shape=jax.ShapeDtypeStruct((t, d), F32),
        compiler_params=_params("parallel"),
        name="out_proj",
    )(a1, a2, w, x)


def _ffn_kernel(x_ref, g_ref, wa_ref, wg_ref, cwa_ref, cwg_ref, cba_ref, cbg_ref,
                wd_ref, fg_ref, o_ref, h_ref, act0_ref, act1_ref, carry_ref, up_ref,
                *, seq_tiles, final_norm, n_hidden_tiles):
    i = pl.program_id(0)
    j = pl.program_id(1)
    nj = pl.num_programs(1) - 1
    tm = x_ref.shape[0]

    def up_phase(out_ref):
        h = h_ref[...]

        def conv(w_ref, cw_ref, cb_ref, slot):
            u_ref = up_ref.at[slot]
            u_ref[0:8, :] = carry_ref[j, slot]
            u_ref[8:8 + tm, :] = _dot(h, w_ref[...])
            carry_ref[j, slot] = u_ref[tm:tm + 8, :]
            return (cb_ref[...] + cw_ref[0:1, :] * u_ref[6:6 + tm, :]
                    + cw_ref[1:2, :] * u_ref[7:7 + tm, :] + cw_ref[2:3, :] * u_ref[8:8 + tm, :])

        ca = conv(wa_ref, cwa_ref, cba_ref, 0)
        cg = conv(wg_ref, cwg_ref, cbg_ref, 1)
        out_ref[...] = (cg * (1.0 / (1.0 + jnp.exp(-cg))) * ca).astype(BF16)

    def down_phase(in_ref):
        o_ref[...] += _dot(in_ref[...], wd_ref[...])

    @pl.when(((i % seq_tiles) == 0) & (j < nj))
    def _():
        carry_ref[j] = jnp.zeros(carry_ref.shape[1:], F32)

    @pl.when(j == 0)
    def _():
        x = x_ref[...]
        h_ref[...] = _rms_rows(x, g_ref[...]).astype(BF16)
        o_ref[...] = x
        up_phase(act0_ref)

    @pl.when((j > 0) & (j < nj) & (j % 2 == 1))
    def _():
        up_phase(act1_ref)
        down_phase(act0_ref)

    @pl.when((j > 0) & (j < nj) & (j % 2 == 0))
    def _():
        up_phase(act0_ref)
        down_phase(act1_ref)

    @pl.when(j == nj)
    def _():
        last = act0_ref if (n_hidden_tiles % 2 == 1) else act1_ref
        y = o_ref[...] + _dot(last[...], wd_ref[...])
        if final_norm:
            y = _rms_rows(y, fg_ref[...])
        o_ref[...] = y


def _ffn(x, g, w_up, conv_w, conv_b, w_down, final_g, layer, *, seq, final_norm,
         tm=FFN_ROWS, tf=FFN_TILE):
    t, d = x.shape
    f = w_down.shape[1]
    nj = f // tf
    kern = functools.partial(_ffn_kernel, seq_tiles=seq // tm, final_norm=final_norm,
                             n_hidden_tiles=nj)

    def up_col(j):
        return jnp.minimum(j, nj - 1)

    def down_row(j):
        return jnp.maximum(j - 1, 0)

    return pl.pallas_call(
        kern,
        grid=(t // tm, nj + 1),
        in_specs=[pl.BlockSpec((tm, d), lambda i, j: (i, 0), pipeline_mode=pl.Buffered(1)),
                  pl.BlockSpec((None, 1, d), lambda i, j: (layer, 0, 0)),
                  pl.BlockSpec((None, d, tf), lambda i, j: (layer, 0, up_col(j))),
                  pl.BlockSpec((None, d, tf), lambda i, j: (layer, 0, up_col(j) + nj)),
                  pl.BlockSpec((None, CONV_WIDTH, tf), lambda i, j: (layer, 0, up_col(j))),
                  pl.BlockSpec((None, CONV_WIDTH, tf), lambda i, j: (layer, 0, up_col(j) + nj)),
                  pl.BlockSpec((None, 1, tf), lambda i, j: (layer, 0, up_col(j))),
                  pl.BlockSpec((None, 1, tf), lambda i, j: (layer, 0, up_col(j) + nj)),
                  pl.BlockSpec((None, tf, d), lambda i, j: (layer, down_row(j), 0)),
                  pl.BlockSpec((1, d), lambda i, j: (0, 0))],
        out_specs=pl.BlockSpec((tm, d), lambda i, j: (i, 0)),
        out_shape=jax.ShapeDtypeStruct((t, d), F32),
        scratch_shapes=[pltpu.VMEM((tm, d), BF16),
                        pltpu.VMEM((tm, tf), BF16),
                        pltpu.VMEM((tm, tf), BF16),
                        pltpu.VMEM((nj, 2, 8, tf), F32),
                        pltpu.VMEM((2, tm + 8, tf), F32)],
        compiler_params=_params("arbitrary", "arbitrary"),
        name="conv_ffn",
    )(x, g[:, None, :], w_up, w_up, conv_w, conv_w, conv_b[:, None, :], conv_b[:, None, :],
      w_down, final_g.reshape(1, d))


def _suffix_sum_matrix():
    blk = ATT_BLOCK
    j = np.arange(blk)[:, None]
    s = np.arange(blk)[None, :]
    m = np.concatenate([(j > s).astype(np.float32), np.ones((blk, blk), np.float32)], axis=1)
    return jnp.asarray(np.concatenate([m, m], axis=0), dtype=BF16)


def _stick_kernel(q_ref, k_ref, v_ref, m_ref, o_ref, acc_ref, carry_ref):
    blk = ATT_BLOCK
    tile = q_ref.shape[1]
    nsub = tile // blk
    i = pl.program_id(2)
    q = q_ref[0]
    scale2 = DH_A ** -0.5 * LOG2E
    suffix = m_ref[...]

    acc_ref[...] = jnp.zeros_like(acc_ref)
    carry_ref[...] = jnp.zeros_like(carry_ref)

    def step(j, diagonal):
        start = pl.multiple_of(j * tile, tile)
        k = k_ref[0, pl.ds(start, tile), :]
        v = v_ref[0, pl.ds(start, tile), :]
        z_all = _dot_nt(q, k) * scale2
        carry = carry_ref[...]
        ws = [None] * nsub
        for c in reversed(range(nsub)):
            z = z_all[:, c * blk:(c + 1) * blk]
            if diagonal:
                row = lax.broadcasted_iota(jnp.int32, (tile, blk), 0)
                col = lax.broadcasted_iota(jnp.int32, (tile, blk), 1) + c * blk
                z = jnp.where(col < row, z, -1e4)
            log_sig = jnp.minimum(z, 0.0) - jnp.log2(1.0 + jnp.exp2(-jnp.abs(z)))
            log_1m = log_sig - z
            hi = log_1m.astype(BF16)
            lo = (log_1m - hi.astype(F32)).astype(BF16)
            sums = _dot(jnp.concatenate([hi, lo], axis=1), suffix)
            ws[c] = jnp.exp2(log_sig + sums[:, :blk] + carry).astype(BF16)
            carry = carry + sums[:, blk:]
        carry_ref[...] = carry
        acc_ref[...] += _dot(jnp.concatenate(ws, axis=1), v)

    step(i, True)

    def more(t):
        return jnp.logical_and(t < i, jnp.max(carry_ref[...]) > ZERO_WEIGHT_LOG2)

    def body(t):
        step(i - 1 - t, False)
        return t + 1

    lax.while_loop(more, body, 0)
    o_ref[0] = acc_ref[...].astype(o_ref.dtype)


def _stick_breaking(p, batch, seq, riders=()):
    blk = ATT_BLOCK
    tile = ATT_TILE
    grid = (batch, H_A, seq // tile)
    rider_specs, rider_shapes = _rider_specs(riders, grid)
    out = pl.pallas_call(
        _with_riders(_stick_kernel, 4, len(riders)),
        grid=grid,
        in_specs=[pl.BlockSpec((1, tile, DH_A), lambda b, h, i: (b, i, h)),
                  pl.BlockSpec((1, seq, DH_A), lambda b, h, i: (b, 0, H_A + h)),
                  pl.BlockSpec((1, seq, DH_A), lambda b, h, i: (b, 0, 2 * H_A + h)),
                  pl.BlockSpec((2 * blk, 2 * blk), lambda b, h, i: (0, 0))] + rider_specs,
        out_specs=[pl.BlockSpec((1, tile, DH_A), lambda b, h, i: (b, i, h))] + rider_specs,
        out_shape=[jax.ShapeDtypeStruct((batch, seq, H_A * DH_A), BF16)] + rider_shapes,
        scratch_shapes=[pltpu.VMEM((tile, DH_A), F32),
                        pltpu.VMEM((tile, blk), F32)],
        compiler_params=_params("arbitrary", "arbitrary", "arbitrary"),
        name="stick_breaking",
    )(p, p, p, _suffix_sum_matrix(), *riders)
    return out[0], out[1:]


def _rel_bucket(rel):
    nb = NUM_BUCKETS // 2
    max_exact = nb // 2
    ret = jnp.where(rel > 0, nb, 0)
    n = jnp.abs(rel)
    n_f = jnp.maximum(n, 1).astype(F32)
    large = max_exact + (jnp.log(n_f / max_exact) / math.log(MAX_DISTANCE / max_exact)
                         * (nb - max_exact)).astype(jnp.int32)
    large = jnp.minimum(large, nb - 1)
    return ret + jnp.where(n < max_exact, n, large)


def _near_buckets():
    blk = ATT_BLOCK
    qpos = blk + jnp.arange(blk)
    kpos = jnp.arange(2 * blk)
    near = _rel_bucket(kpos[None, :] - qpos[:, None]).astype(jnp.int32)
    far = _rel_bucket(jnp.full((1,), -2 * blk, jnp.int32)).astype(jnp.int32)
    return near, far


def _diff_kernel(far_ref, relb_ref, q_ref, k_ref, v_ref, bucket_ref, lam_ref, g_ref, o_ref,
                 bias_ref, m_ref, l_ref, acc_ref, s0_ref, s1_ref, *, lam_init):
    blk = ATT_BLOCK
    tile = q_ref.shape[1]
    nsub = tile // blk
    h = pl.program_id(1)
    i = pl.program_id(2)
    scale2 = DK_B ** -0.5 * LOG2E

    @pl.when(i == 0)
    def _():
        bucket = bucket_ref[...]
        far = relb_ref[far_ref[0], h]
        near = jnp.zeros((blk, 2 * blk), F32)
        for b in range(NUM_BUCKETS):
            near = jnp.where(bucket == b, relb_ref[b, h], near)
        near = (near - far) * LOG2E
        qpos = blk + lax.broadcasted_iota(jnp.int32, (blk, 2 * blk), 0)
        kpos = lax.broadcasted_iota(jnp.int32, (blk, 2 * blk), 1)
        near = jnp.where((kpos // CHUNK) <= (qpos // CHUNK), near, -jnp.inf)
        bias_ref[...] = jnp.zeros_like(bias_ref)
        for a in range(nsub):
            rows = slice(a * blk, (a + 1) * blk)
            bias_ref[0, rows, a * blk:(a + 1) * blk] = near[:, blk:]
            if a >= 1:
                bias_ref[0, rows, (a - 1) * blk:a * blk] = near[:, :blk]
            if a + 1 < nsub:
                bias_ref[0, rows, (a + 1) * blk:] = jnp.full((blk, tile - (a + 1) * blk),
                                                             -jnp.inf, F32)
        bias_ref[1, 0:blk, (nsub - 1) * blk:] = near[:, :blk]

    m_ref[...] = jnp.full_like(m_ref, -jnp.inf)
    l_ref[...] = jnp.zeros_like(l_ref)
    acc_ref[...] = jnp.zeros_like(acc_ref)

    def key_tile(idx):
        return jnp.where(idx == 0, i, jnp.where(idx == 1, i - 1, idx - 2))

    def logits_into(s_ref, idx):
        j = key_tile(jnp.minimum(idx, i))
        kk = k_ref[0, pl.ds(pl.multiple_of(j * tile, tile), tile), :]
        for c in range(2):
            s_ref[c] = _dot_nt(q_ref[0, :, c * DK_B:(c + 1) * DK_B],
                               kk[:, c * DK_B:(c + 1) * DK_B])

    def consume(s_ref, idx):
        j = key_tile(idx)
        v = v_ref[0, pl.ds(pl.multiple_of(j * tile, tile), tile), :]
        bias = bias_ref[jnp.minimum(idx, 2)]
        probs = []
        for c in range(2):
            s = s_ref[c] * scale2 + bias
            subs = [s[:, u * blk:(u + 1) * blk] for u in range(nsub)]
            m_prev = m_ref[c]
            m_cur = functools.reduce(jnp.maximum, subs)
            m_new = jnp.maximum(m_prev, jnp.max(m_cur, axis=-1, keepdims=True))
            alpha = jnp.exp2(m_prev - m_new)
            ps = [jnp.exp2(u - m_new) for u in subs]
            l_ref[c] = alpha * l_ref[c] + functools.reduce(jnp.add, ps)
            m_ref[c] = m_new
            probs.append((alpha, jnp.concatenate([u.astype(BF16) for u in ps], axis=1)))
        for c in range(2):
            alpha, p = probs[c]
            acc_ref[c] = jnp.concatenate([alpha] * (DV_B // blk), axis=1) * acc_ref[c] + _dot(p, v)

    logits_into(s0_ref, 0)

    def pair(t, _):
        idx = 2 * t
        logits_into(s1_ref, idx + 1)
        consume(s0_ref, idx)
        logits_into(s0_ref, idx + 2)
        consume(s1_ref, idx + 1)
        return 0

    lax.fori_loop(0, (i + 1) // 2, pair, 0)

    @pl.when(i % 2 == 0)
    def _():
        consume(s0_ref, i)

    lv = lam_ref[...]
    lam = (jnp.exp(jnp.sum(lv[0:1] * lv[1:2], axis=-1, keepdims=True))
           - jnp.exp(jnp.sum(lv[2:3] * lv[3:4], axis=-1, keepdims=True)) + lam_init)
    l1 = jnp.sum(l_ref[0], axis=-1, keepdims=True)
    l2 = jnp.sum(l_ref[1], axis=-1, keepdims=True)
    o = acc_ref[0] / l1 - lam * (acc_ref[1] / l2)
    o_ref[0] = (_rms_rows(o, g_ref[...]) * (1.0 - lam_init)).astype(o_ref.dtype)


def _diff_attention(p, rel_bias, lam_vecs, subln_g, lam_init, batch, seq, riders=()):
    blk = ATT_BLOCK
    tile = ATT_TILE
    q_col = 3 * H_A * DH_A // DV_B
    k_col = q_col + H_B
    v_col = k_col + H_B
    near, far = _near_buckets()
    grid = (batch, H_B, seq // tile)
    rider_specs, rider_shapes = _rider_specs(riders, grid)
    kern = _with_riders(functools.partial(_diff_kernel, lam_init=lam_init), 8, len(riders))
    grid_spec = pltpu.PrefetchScalarGridSpec(
        num_scalar_prefetch=2,
        grid=grid,
        in_specs=[pl.BlockSpec((1, tile, DV_B), lambda b, h, i, *_: (b, i, q_col + h)),
                  pl.BlockSpec((1, seq, DV_B), lambda b, h, i, *_: (b, 0, k_col + h)),
                  pl.BlockSpec((1, seq, DV_B), lambda b, h, i, *_: (b, 0, v_col + h)),
                  pl.BlockSpec((blk, 2 * blk), lambda b, h, i, *_: (0, 0)),
                  pl.BlockSpec((4, DK_B), lambda b, h, i, *_: (0, 0)),
                  pl.BlockSpec((1, DV_B), lambda b, h, i, *_: (0, 0))] + rider_specs,
        out_specs=[pl.BlockSpec((1, tile, DV_B), lambda b, h, i, *_: (b, i, h))] + rider_specs,
        scratch_shapes=[pltpu.VMEM((3, tile, tile), F32),
                        pltpu.VMEM((2, tile, blk), F32),
                        pltpu.VMEM((2, tile, blk), F32),
                        pltpu.VMEM((2, tile, DV_B), F32),
                        pltpu.VMEM((2, tile, tile), F32),
                        pltpu.VMEM((2, tile, tile), F32)])
    out = pl.pallas_call(
        kern,
        grid_spec=grid_spec,
        out_shape=[jax.ShapeDtypeStruct((batch, seq, H_B * DV_B), BF16)] + rider_shapes,
        compiler_params=_params("arbitrary", "arbitrary", "arbitrary"),
        name="diff_attention",
    )(far, rel_bias.astype(F32), p, p, p, near, lam_vecs.astype(F32), subln_g.reshape(1, DV_B),
      *riders)
    return out[0], out[1:]


def _rotary_tables(seq):
    inv_freq = ROPE_BASE ** (-jnp.arange(0, DK_C, 2, dtype=F32) / DK_C)
    ang = jnp.arange(seq, dtype=F32)[:, None] * inv_freq[None, :]
    cos = jnp.concatenate([jnp.cos(ang), jnp.cos(ang)], axis=-1)
    sin = jnp.concatenate([-jnp.sin(ang), jnp.sin(ang)], axis=-1)
    return cos, sin


def _decay_tables():
    log_g = jnp.log(1.0 - 2.0 ** (-5.0 - jnp.arange(H_C, dtype=F32)))
    idx = jnp.arange(CHUNK, dtype=F32)
    intra = jnp.exp(log_g[:, None, None] * jnp.abs(idx[:, None] - idx[None, :]))
    q_decay = jnp.exp(log_g[:, None] * (idx + 1.0))
    k_decay = jnp.exp(log_g[:, None] * (CHUNK - 1.0 - idx))
    chunk_decay = jnp.exp(log_g * CHUNK)
    return (intra,
            jnp.broadcast_to(q_decay[:, :, None], (H_C, CHUNK, DV_C)),
            jnp.broadcast_to(k_decay[:, :, None], (H_C, CHUNK, DK_C)),
            jnp.broadcast_to(chunk_decay[:, None, None], (H_C, 1, DV_C)))


def _retention_kernel(q_ref, k_ref, v_ref, gate_ref, cos_ref, sin_ref, intra_ref, qd_ref,
                      kd_ref, cd_ref, g_ref, o_ref, state_ref):
    rows = q_ref.shape[1]

    @pl.when(pl.program_id(2) == 0)
    def _():
        state_ref[...] = jnp.zeros_like(state_ref)

    cos = cos_ref[...]
    sin = sin_ref[...]

    def rotary(t):
        return t * cos + pltpu.roll(t, DK_C // 2, 1) * sin

    q = (rotary(q_ref[0]) * (DK_C ** -0.5)).astype(BF16)
    k = rotary(k_ref[0])
    v = v_ref[0].astype(BF16)
    intra_decay = intra_ref[0]
    q_decay = qd_ref[0]
    k_decay = kd_ref[0]
    chunk_decay = cd_ref[0]

    state = state_ref[...]
    for c in range(rows // CHUNK):
        sl = slice(c * CHUNK, (c + 1) * CHUNK)
        qc = q[sl]
        kc = k[sl]
        vc = v[sl]
        scores = _dot_nt(qc, kc.astype(BF16)) * intra_decay
        y = _dot(scores.astype(BF16), vc) + _dot(qc, state.astype(BF16)) * q_decay
        kv = _dot((kc * k_decay).T.astype(BF16), vc)
        state = chunk_decay * state + kv
        gate = gate_ref[0, sl, :]
        y = _rms_rows(y, g_ref[...])
        o_ref[0, sl, :] = (gate * (1.0 / (1.0 + jnp.exp(-gate))) * y).astype(o_ref.dtype)
    state_ref[...] = state


def _retention(p, ret_norm_g, batch, seq, rows=512):
    k_col = H_C
    v_col = 2 * H_C * DK_C // DV_C
    gate_col = v_col + H_C
    cos, sin = _rotary_tables(seq)
    intra, q_decay, k_decay, chunk_decay = _decay_tables()
    return pl.pallas_call(
        _retention_kernel,
        grid=(batch, H_C, seq // rows),
        in_specs=[pl.BlockSpec((1, rows, DK_C), lambda b, h, i: (b, i, h)),
                  pl.BlockSpec((1, rows, DK_C), lambda b, h, i: (b, i, k_col + h)),
                  pl.BlockSpec((1, rows, DV_C), lambda b, h, i: (b, i, v_col + h)),
                  pl.BlockSpec((1, rows, DV_C), lambda b, h, i: (b, i, gate_col + h)),
                  pl.BlockSpec((rows, DK_C), lambda b, h, i: (i, 0)),
                  pl.BlockSpec((rows, DK_C), lambda b, h, i: (i, 0)),
                  pl.BlockSpec((1, CHUNK, CHUNK), lambda b, h, i: (h, 0, 0)),
                  pl.BlockSpec((1, CHUNK, DV_C), lambda b, h, i: (h, 0, 0)),
                  pl.BlockSpec((1, CHUNK, DK_C), lambda b, h, i: (h, 0, 0)),
                  pl.BlockSpec((1, 1, DV_C), lambda b, h, i: (h, 0, 0)),
                  pl.BlockSpec((1, DV_C), lambda b, h, i: (0, 0))],
        out_specs=pl.BlockSpec((1, rows, DV_C), lambda b, h, i: (b, i, h)),
        out_shape=jax.ShapeDtypeStruct((batch, seq, H_C * DV_C), BF16),
        scratch_shapes=[pltpu.VMEM((DK_C, DV_C), F32)],
        compiler_params=_params("parallel", "parallel", "arbitrary"),
        name="retention",
    )(p, p, p, p, cos, sin, intra, q_decay, k_decay, chunk_decay, ret_norm_g.reshape(1, DV_C))


def _gelu_tanh(x):
    return 0.5 * x * (1.0 + jnp.tanh(math.sqrt(2.0 / math.pi) * (x + 0.044715 * (x * x * x))))


def _sgu_kernel(zu_ref, zv_ref, lg_ref, lb_ref, w_ref, b_ref, o_ref):
    rows = zu_ref.shape[0]
    width = D_D // G_D
    v = _gelu_tanh(zv_ref[...])
    mu = jnp.mean(v, axis=-1, keepdims=True)
    var = jnp.mean(jnp.square(v - mu), axis=-1, keepdims=True)
    vn = ((v - mu) * lax.rsqrt(var + EPS) * lg_ref[...] + lb_ref[...]).astype(BF16)
    pos_i = lax.broadcasted_iota(jnp.int32, (SGU_LEN, SGU_LEN), 0)
    pos_j = lax.broadcasted_iota(jnp.int32, (SGU_LEN, SGU_LEN), 1)
    mask = (pos_j // CHUNK) <= (pos_i // CHUNK)
    for g in range(G_D):
        w = jnp.where(mask, w_ref[g], 0.0).astype(BF16)
        bias = b_ref[g]
        cols = slice(g * width, (g + 1) * width)
        for r in range(rows // SGU_LEN):
            sl = slice(r * SGU_LEN, (r + 1) * SGU_LEN)
            gate = _dot(w, vn[sl, cols]) + bias
            o_ref[sl, cols] = (_gelu_tanh(zu_ref[sl, cols]) * gate).astype(o_ref.dtype)


def _sgu(p, ln_g, ln_b, w_s, b_s, rows=512):
    t = p.shape[0]
    u_col = (2 * H_C * DK_C + 2 * H_C * DV_C) // D_D
    return pl.pallas_call(
        _sgu_kernel,
        grid=(t // rows,),
        in_specs=[pl.BlockSpec((rows, D_D), lambda i: (i, u_col)),
                  pl.BlockSpec((rows, D_D), lambda i: (i, u_col + 1)),
                  pl.BlockSpec((1, D_D), lambda i: (0, 0)),
                  pl.BlockSpec((1, D_D), lambda i: (0, 0)),
                  pl.BlockSpec((G_D, SGU_LEN, SGU_LEN), lambda i: (0, 0, 0)),
                  pl.BlockSpec((G_D, SGU_LEN, 1), lambda i: (0, 0, 0))],
        out_specs=pl.BlockSpec((rows, D_D), lambda i: (i, 0)),
        out_shape=jax.ShapeDtypeStruct((t, D_D), BF16),
        compiler_params=_params("parallel"),
        name="spatial_gate",
    )(p, p, ln_g.reshape(1, D_D), ln_b.reshape(1, D_D), w_s, b_s.reshape(G_D, SGU_LEN, 1))


def kernel(x, norm_mix_g, norm_ffn_g, final_norm_g, rel_bias, ab_w_in, ab_w_out, diff_lambda,
           diff_subln_g, cd_w_in, cd_w_out, ret_norm_g, sgu_ln_g, sgu_ln_b, sgu_w, sgu_b,
           ffn_w_up, ffn_conv_w, ffn_conv_b, ffn_w_down):
    batch, seq, d = x.shape
    t = batch * seq
    depth = norm_mix_g.shape[0]
    xt = x.reshape(t, d)
    def rows(w):
        return w.reshape(-1, w.shape[-1])

    ab_w_in = ab_w_in.astype(BF16)
    for layer in range(depth):
        j = layer // 2
        if layer % 2 == 0:
            lam_init = 0.8 - 0.6 * math.exp(-0.3 * layer)
            p = _norm_matmul(xt, norm_mix_g[layer], ab_w_in, j, BF16)
            p3 = p.reshape(batch, seq, p.shape[1])
            if layer == 0:
                o_a, (up_bf,) = _stick_breaking(p3, batch, seq, (rows(ffn_w_up),))
                o_b, (down_bf, abo_bf, cdi_bf, cdo_bf) = _diff_attention(
                    p3, rel_bias, diff_lambda[j], diff_subln_g[j], lam_init, batch, seq,
                    (rows(ffn_w_down), rows(ab_w_out), rows(cd_w_in), rows(cd_w_out)))
                ffn_w_up, ffn_w_down, ab_w_out, cd_w_in, cd_w_out = (
                    b.reshape(w.shape) for b, w in ((up_bf, ffn_w_up), (down_bf, ffn_w_down),
                                                    (abo_bf, ab_w_out), (cdi_bf, cd_w_in),
                                                    (cdo_bf, cd_w_out)))
            else:
                o_a, _ = _stick_breaking(p3, batch, seq)
                o_b, _ = _diff_attention(p3, rel_bias, diff_lambda[j], diff_subln_g[j], lam_init,
                                         batch, seq)
            xt = _out_proj(o_a.reshape(t, -1), o_b.reshape(t, -1), ab_w_out, j, xt)
        else:
            p = _norm_matmul(xt, norm_mix_g[layer], cd_w_in, j, F32)
            o_c = _retention(p.reshape(batch, seq, p.shape[1]), ret_norm_g[j], batch, seq)
            o_d = _sgu(p, sgu_ln_g[j], sgu_ln_b[j], sgu_w[j], sgu_b[j])
            xt = _out_proj(o_c.reshape(t, -1), o_d, cd_w_out, j, xt)
        xt = _ffn(xt, norm_ffn_g, ffn_w_up, ffn_conv_w, ffn_conv_b, ffn_w_down, final_norm_g,
                  layer, seq=seq, final_norm=(layer == depth - 1))
    return xt.reshape(batch, seq, d)
```

```python
import functools
import math

import numpy as np
import jax
import jax.numpy as jnp
from jax import lax
from jax.experimental import pallas as pl
from jax.experimental.pallas import tpu as pltpu

F32 = jnp.float32
BF16 = jnp.bfloat16

EPS = 1e-6
CHUNK = 64
H_A = 8
DH_A = 128
H_B = 4
DK_B = 128
DV_B = 2 * DK_B
NUM_BUCKETS = 32
MAX_DISTANCE = 128
H_C = 4
DK_C = 128
DV_C = 2 * DK_C
ROPE_BASE = 10000.0
D_D = 1024
G_D = 4
SGU_LEN = 128
CONV_WIDTH = 3

LANES = 128
BF16_SUBLANES = 16
ATT_BLOCK = 128
ATT_TILE = 512
RET_BLOCK = 256
VMEM_LIMIT = 62 * 1024 * 1024
PROJ_ROWS = 1024
PROJ_TILE = 1024
FFN_ROWS = 1024
FFN_TILE = 512
LOG2E = math.log2(math.e)
ZERO_WEIGHT_LOG2 = -150.0


def _params(*semantics):
    return pltpu.CompilerParams(dimension_semantics=semantics,
                                vmem_limit_bytes=VMEM_LIMIT)


def _dot(a, b):
    return jnp.dot(a, b, preferred_element_type=F32)


def _dot_nt(a, b):
    return lax.dot_general(a, b, (((1,), (1,)), ((), ())), preferred_element_type=F32)


def _rms_rows(x, g):
    return x * lax.rsqrt(jnp.mean(x * x, axis=-1, keepdims=True) + EPS) * g


def _with_riders(body, n_in, n_riders):
    def kern(*refs):
        ins = refs[:n_in]
        rider_in = refs[n_in:n_in + n_riders]
        out = refs[n_in + n_riders]
        rider_out = refs[n_in + n_riders + 1:n_in + 2 * n_riders + 1]
        scratch = refs[n_in + 2 * n_riders + 1:]
        for src, dst in zip(rider_in, rider_out):
            dst[...] = src[...].astype(BF16)
        body(*ins, out, *scratch)
    return kern


def _rider_specs(riders, grid):
    steps = math.prod(grid)

    def linear_step(*g):
        n = g[0]
        for size, idx in zip(grid[1:], g[1:len(grid)]):
            n = n * size + idx
        return n

    specs, shapes = [], []
    for w in riders:
        rows, cols = w.shape
        block_rows, rem = divmod(rows, steps)
        assert rem == 0 and block_rows % BF16_SUBLANES == 0, (w.shape, steps)
        specs.append(pl.BlockSpec((block_rows, cols), lambda *g: (linear_step(*g), 0)))
        shapes.append(jax.ShapeDtypeStruct((rows, cols), BF16))
    return specs, shapes


def _norm_matmul_kernel(x_ref, g_ref, w_ref, o_ref, h_ref):
    @pl.when(pl.program_id(1) == 0)
    def _():
        h_ref[...] = _rms_rows(x_ref[...], g_ref[...]).astype(BF16)

    o_ref[...] = _dot(h_ref[...], w_ref[...]).astype(o_ref.dtype)


def _norm_matmul(x, g, w, layer, out_dtype, tm=PROJ_ROWS, tn=PROJ_TILE):
    t, d = x.shape
    n = w.shape[2]
    return pl.pallas_call(
        _norm_matmul_kernel,
        grid=(t // tm, n // tn),
        in_specs=[pl.BlockSpec((tm, d), lambda i, j: (i, 0)),
                  pl.BlockSpec((1, d), lambda i, j: (0, 0)),
                  pl.BlockSpec((None, d, tn), lambda i, j: (layer, 0, j))],
        out_specs=pl.BlockSpec((tm, tn), lambda i, j: (i, j)),
        out_shape=jax.ShapeDtypeStruct((t, n), out_dtype),
        scratch_shapes=[pltpu.VMEM((tm, d), BF16)],
        compiler_params=_params("parallel", "arbitrary"),
        name="norm_matmul",
    )(x, g.reshape(1, d), w)


def _out_proj_kernel(a1_ref, a2_ref, w_ref, x_ref, o_ref):
    half = a1_ref.shape[1]
    acc = _dot(a1_ref[...], w_ref[0:half, :])
    acc += _dot(a2_ref[...], w_ref[half:2 * half, :])
    o_ref[...] = x_ref[...] + acc


def _out_proj(a1, a2, w, layer, x, tm=512):
    t, half = a1.shape
    d = w.shape[2]
    return pl.pallas_call(
        _out_proj_kernel,
        grid=(t // tm,),
        in_specs=[pl.BlockSpec((tm, half), lambda i: (i, 0)),
                  pl.BlockSpec((tm, half), lambda i: (i, 0)),
                  pl.BlockSpec((None, 2 * half, d), lambda i: (layer, 0, 0)),
                  pl.BlockSpec((tm, d), lambda i: (i, 0))],
        out_specs=pl.BlockSpec((tm, d), lambda i: (i, 0)),
        out_shape=jax.ShapeDtypeStruct((t, d), F32),
        compiler_params=_params("parallel"),
        name="out_proj",
    )(a1, a2, w, x)


def _ffn_kernel(x_ref, g_ref, wa_ref, wg_ref, cwa_ref, cwg_ref, cba_ref, cbg_ref,
                wd_ref, fg_ref, o_ref, h_ref, act0_ref, act1_ref, carry_ref, up_ref,
                *, seq_tiles, final_norm, n_hidden_tiles):
    i = pl.program_id(0)
    j = pl.program_id(1)
    nj = pl.num_programs(1) - 1
    tm = x_ref.shape[0]

    def up_phase(out_ref):
        h = h_ref[...]

        def conv(w_ref, cw_ref, cb_ref, slot):
            u_ref = up_ref.at[slot]
            u_ref[0:8, :] = carry_ref[j, slot]
            u_ref[8:8 + tm, :] = _dot(h, w_ref[...])
            carry_ref[j, slot] = u_ref[tm:tm + 8, :]
            return (cb_ref[...] + cw_ref[0:1, :] * u_ref[6:6 + tm, :]
                    + cw_ref[1:2, :] * u_ref[7:7 + tm, :] + cw_ref[2:3, :] * u_ref[8:8 + tm, :])

        ca = conv(wa_ref, cwa_ref, cba_ref, 0)
        cg = conv(wg_ref, cwg_ref, cbg_ref, 1)
        out_ref[...] = (cg * (1.0 / (1.0 + jnp.exp(-cg))) * ca).astype(BF16)

    def down_phase(in_ref):
        o_ref[...] += _dot(in_ref[...], wd_ref[...])

    @pl.when(((i % seq_tiles) == 0) & (j < nj))
    def _():
        carry_ref[j] = jnp.zeros(carry_ref.shape[1:], F32)

    @pl.when(j == 0)
    def _():
        x = x_ref[...]
        h_ref[...] = _rms_rows(x, g_ref[...]).astype(BF16)
        o_ref[...] = x
        up_phase(act0_ref)

    @pl.when((j > 0) & (j < nj) & (j % 2 == 1))
    def _():
        up_phase(act1_ref)
        down_phase(act0_ref)

    @pl.when((j > 0) & (j < nj) & (j % 2 == 0))
    def _():
        up_phase(act0_ref)
        down_phase(act1_ref)

    @pl.when(j == nj)
    def _():
        last = act0_ref if (n_hidden_tiles % 2 == 1) else act1_ref
        y = o_ref[...] + _dot(last[...], wd_ref[...])
        if final_norm:
            y = _rms_rows(y, fg_ref[...])
        o_ref[...] = y


def _ffn(x, g, w_up, conv_w, conv_b, w_down, final_g, layer, *, seq, final_norm,
         tm=FFN_ROWS, tf=FFN_TILE):
    t, d = x.shape
    f = w_down.shape[1]
    nj = f // tf
    kern = functools.partial(_ffn_kernel, seq_tiles=seq // tm, final_norm=final_norm,
                             n_hidden_tiles=nj)

    def up_col(j):
        return jnp.minimum(j, nj - 1)

    def down_row(j):
        return jnp.maximum(j - 1, 0)

    return pl.pallas_call(
        kern,
        grid=(t // tm, nj + 1),
        in_specs=[pl.BlockSpec((tm, d), lambda i, j: (i, 0)),
                  pl.BlockSpec((None, 1, d), lambda i, j: (layer, 0, 0)),
                  pl.BlockSpec((None, d, tf), lambda i, j: (layer, 0, up_col(j))),
                  pl.BlockSpec((None, d, tf), lambda i, j: (layer, 0, up_col(j) + nj)),
                  pl.BlockSpec((None, CONV_WIDTH, tf), lambda i, j: (layer, 0, up_col(j))),
                  pl.BlockSpec((None, CONV_WIDTH, tf), lambda i, j: (layer, 0, up_col(j) + nj)),
                  pl.BlockSpec((None, 1, tf), lambda i, j: (layer, 0, up_col(j))),
                  pl.BlockSpec((None, 1, tf), lambda i, j: (layer, 0, up_col(j) + nj)),
                  pl.BlockSpec((None, tf, d), lambda i, j: (layer, down_row(j), 0)),
                  pl.BlockSpec((1, d), lambda i, j: (0, 0))],
        out_specs=pl.BlockSpec((tm, d), lambda i, j: (i, 0)),
        out_shape=jax.ShapeDtypeStruct((t, d), F32),
        scratch_shapes=[pltpu.VMEM((tm, d), BF16),
                        pltpu.VMEM((tm, tf), BF16),
                        pltpu.VMEM((tm, tf), BF16),
                        pltpu.VMEM((nj, 2, 8, tf), F32),
                        pltpu.VMEM((2, tm + 8, tf), F32)],
        compiler_params=_params("arbitrary", "arbitrary"),
        name="conv_ffn",
    )(x, g[:, None, :], w_up, w_up, conv_w, conv_w, conv_b[:, None, :], conv_b[:, None, :],
      w_down, final_g.reshape(1, d))


def _suffix_sum_matrix():
    blk = ATT_BLOCK
    j = np.arange(blk)[:, None]
    s = np.arange(blk)[None, :]
    m = np.concatenate([(j > s).astype(np.float32), np.ones((blk, blk), np.float32)], axis=1)
    return jnp.asarray(np.concatenate([m, m], axis=0), dtype=BF16)


def _stick_kernel(q_ref, k_ref, v_ref, m_ref, o_ref, acc_ref, carry_ref):
    blk = ATT_BLOCK
    tile = q_ref.shape[1]
    nsub = tile // blk
    i = pl.program_id(2)
    q = q_ref[0]
    scale2 = DH_A ** -0.5 * LOG2E
    suffix = m_ref[...]

    acc_ref[...] = jnp.zeros_like(acc_ref)
    carry_ref[...] = jnp.zeros_like(carry_ref)

    def step(j, diagonal):
        start = pl.multiple_of(j * tile, tile)
        k = k_ref[0, pl.ds(start, tile), :]
        v = v_ref[0, pl.ds(start, tile), :]
        z_all = _dot_nt(q, k) * scale2
        carry = carry_ref[...]
        ws = [None] * nsub
        for c in reversed(range(nsub)):
            z = z_all[:, c * blk:(c + 1) * blk]
            if diagonal:
                row = lax.broadcasted_iota(jnp.int32, (tile, blk), 0)
                col = lax.broadcasted_iota(jnp.int32, (tile, blk), 1) + c * blk
                z = jnp.where(col < row, z, -1e4)
            log_sig = jnp.minimum(z, 0.0) - jnp.log2(1.0 + jnp.exp2(-jnp.abs(z)))
            log_1m = log_sig - z
            hi = log_1m.astype(BF16)
            lo = (log_1m - hi.astype(F32)).astype(BF16)
            sums = _dot(jnp.concatenate([hi, lo], axis=1), suffix)
            ws[c] = jnp.exp2(log_sig + sums[:, :blk] + carry).astype(BF16)
            carry = carry + sums[:, blk:]
        carry_ref[...] = carry
        acc_ref[...] += _dot(jnp.concatenate(ws, axis=1), v)

    step(i, True)

    def more(t):
        return jnp.logical_and(t < i, jnp.max(carry_ref[...]) > ZERO_WEIGHT_LOG2)

    def body(t):
        step(i - 1 - t, False)
        return t + 1

    lax.while_loop(more, body, 0)
    o_ref[0] = acc_ref[...].astype(o_ref.dtype)


def _stick_breaking(p, batch, seq, riders=()):
    blk = ATT_BLOCK
    tile = ATT_TILE
    grid = (batch, H_A, seq // tile)
    rider_specs, rider_shapes = _rider_specs(riders, grid)
    out = pl.pallas_call(
        _with_riders(_stick_kernel, 4, len(riders)),
        grid=grid,
        in_specs=[pl.BlockSpec((1, tile, DH_A), lambda b, h, i: (b, i, h)),
                  pl.BlockSpec((1, seq, DH_A), lambda b, h, i: (b, 0, H_A + h)),
                  pl.BlockSpec((1, seq, DH_A), lambda b, h, i: (b, 0, 2 * H_A + h)),
                  pl.BlockSpec((2 * blk, 2 * blk), lambda b, h, i: (0, 0))] + rider_specs,
        out_specs=[pl.BlockSpec((1, tile, DH_A), lambda b, h, i: (b, i, h))] + rider_specs,
        out_shape=[jax.ShapeDtypeStruct((batch, seq, H_A * DH_A), BF16)] + rider_shapes,
        scratch_shapes=[pltpu.VMEM((tile, DH_A), F32),
                        pltpu.VMEM((tile, blk), F32)],
        compiler_params=_params("arbitrary", "arbitrary", "arbitrary"),
        name="stick_breaking",
    )(p, p, p, _suffix_sum_matrix(), *riders)
    return out[0], out[1:]


def _rel_bucket(rel):
    nb = NUM_BUCKETS // 2
    max_exact = nb // 2
    ret = jnp.where(rel > 0, nb, 0)
    n = jnp.abs(rel)
    n_f = jnp.maximum(n, 1).astype(F32)
    large = max_exact + (jnp.log(n_f / max_exact) / math.log(MAX_DISTANCE / max_exact)
                         * (nb - max_exact)).astype(jnp.int32)
    large = jnp.minimum(large, nb - 1)
    return ret + jnp.where(n < max_exact, n, large)


def _near_buckets():
    blk = ATT_BLOCK
    qpos = blk + jnp.arange(blk)
    kpos = jnp.arange(2 * blk)
    near = _rel_bucket(kpos[None, :] - qpos[:, None]).astype(jnp.int32)
    far = _rel_bucket(jnp.full((1,), -2 * blk, jnp.int32)).astype(jnp.int32)
    return near, far


def _diff_kernel(far_ref, relb_ref, q_ref, k_ref, v_ref, bucket_ref, lam_ref, g_ref, o_ref,
                 bias_ref, m_ref, l_ref, acc_ref, s0_ref, s1_ref, *, lam_init):
    blk = ATT_BLOCK
    tile = q_ref.shape[1]
    nsub = tile // blk
    h = pl.program_id(1)
    i = pl.program_id(2)
    scale2 = DK_B ** -0.5 * LOG2E

    @pl.when(i == 0)
    def _():
        bucket = bucket_ref[...]
        far = relb_ref[far_ref[0], h]
        near = jnp.zeros((blk, 2 * blk), F32)
        for b in range(NUM_BUCKETS):
            near = jnp.where(bucket == b, relb_ref[b, h], near)
        near = (near - far) * LOG2E
        qpos = blk + lax.broadcasted_iota(jnp.int32, (blk, 2 * blk), 0)
        kpos = lax.broadcasted_iota(jnp.int32, (blk, 2 * blk), 1)
        near = jnp.where((kpos // CHUNK) <= (qpos // CHUNK), near, -jnp.inf)
        bias_ref[...] = jnp.zeros_like(bias_ref)
        for a in range(nsub):
            rows = slice(a * blk, (a + 1) * blk)
            bias_ref[0, rows, a * blk:(a + 1) * blk] = near[:, blk:]
            if a >= 1:
                bias_ref[0, rows, (a - 1) * blk:a * blk] = near[:, :blk]
            if a + 1 < nsub:
                bias_ref[0, rows, (a + 1) * blk:] = jnp.full((blk, tile - (a + 1) * blk),
                                                             -jnp.inf, F32)
        bias_ref[1, 0:blk, (nsub - 1) * blk:] = near[:, :blk]

    m_ref[...] = jnp.full_like(m_ref, -jnp.inf)
    l_ref[...] = jnp.zeros_like(l_ref)
    acc_ref[...] = jnp.zeros_like(acc_ref)

    def key_tile(idx):
        return jnp.where(idx == 0, i, jnp.where(idx == 1, i - 1, idx - 2))

    def logits_into(s_ref, idx):
        j = key_tile(jnp.minimum(idx, i))
        kk = k_ref[0, pl.ds(pl.multiple_of(j * tile, tile), tile), :]
        for c in range(2):
            s_ref[c] = _dot_nt(q_ref[0, :, c * DK_B:(c + 1) * DK_B],
                               kk[:, c * DK_B:(c + 1) * DK_B])

    def consume(s_ref, idx):
        j = key_tile(idx)
        v = v_ref[0, pl.ds(pl.multiple_of(j * tile, tile), tile), :]
        bias = bias_ref[jnp.minimum(idx, 2)]
        probs = []
        for c in range(2):
            s = s_ref[c] * scale2 + bias
            subs = [s[:, u * blk:(u + 1) * blk] for u in range(nsub)]
            m_prev = m_ref[c]
            m_cur = functools.reduce(jnp.maximum, subs)
            m_new = jnp.maximum(m_prev, jnp.max(m_cur, axis=-1, keepdims=True))
            alpha = jnp.exp2(m_prev - m_new)
            ps = [jnp.exp2(u - m_new) for u in subs]
            l_ref[c] = alpha * l_ref[c] + functools.reduce(jnp.add, ps)
            m_ref[c] = m_new
            probs.append((alpha, jnp.concatenate([u.astype(BF16) for u in ps], axis=1)))
        for c in range(2):
            alpha, p = probs[c]
            acc_ref[c] = jnp.concatenate([alpha] * (DV_B // blk), axis=1) * acc_ref[c] + _dot(p, v)

    logits_into(s0_ref, 0)

    def pair(t, _):
        idx = 2 * t
        logits_into(s1_ref, idx + 1)
        consume(s0_ref, idx)
        logits_into(s0_ref, idx + 2)
        consume(s1_ref, idx + 1)
        return 0

    lax.fori_loop(0, (i + 1) // 2, pair, 0)

    @pl.when(i % 2 == 0)
    def _():
        consume(s0_ref, i)

    lv = lam_ref[...]
    lam = (jnp.exp(jnp.sum(lv[0:1] * lv[1:2], axis=-1, keepdims=True))
           - jnp.exp(jnp.sum(lv[2:3] * lv[3:4], axis=-1, keepdims=True)) + lam_init)
    l1 = jnp.sum(l_ref[0], axis=-1, keepdims=True)
    l2 = jnp.sum(l_ref[1], axis=-1, keepdims=True)
    o = acc_ref[0] / l1 - lam * (acc_ref[1] / l2)
    o_ref[0] = (_rms_rows(o, g_ref[...]) * (1.0 - lam_init)).astype(o_ref.dtype)


def _diff_attention(p, rel_bias, lam_vecs, subln_g, lam_init, batch, seq, riders=()):
    blk = ATT_BLOCK
    tile = ATT_TILE
    q_col = 3 * H_A * DH_A // DV_B
    k_col = q_col + H_B
    v_col = k_col + H_B
    near, far = _near_buckets()
    grid = (batch, H_B, seq // tile)
    rider_specs, rider_shapes = _rider_specs(riders, grid)
    kern = _with_riders(functools.partial(_diff_kernel, lam_init=lam_init), 8, len(riders))
    grid_spec = pltpu.PrefetchScalarGridSpec(
        num_scalar_prefetch=2,
        grid=grid,
        in_specs=[pl.BlockSpec((1, tile, DV_B), lambda b, h, i, *_: (b, i, q_col + h)),
                  pl.BlockSpec((1, seq, DV_B), lambda b, h, i, *_: (b, 0, k_col + h)),
                  pl.BlockSpec((1, seq, DV_B), lambda b, h, i, *_: (b, 0, v_col + h)),
                  pl.BlockSpec((blk, 2 * blk), lambda b, h, i, *_: (0, 0)),
                  pl.BlockSpec((4, DK_B), lambda b, h, i, *_: (0, 0)),
                  pl.BlockSpec((1, DV_B), lambda b, h, i, *_: (0, 0))] + rider_specs,
        out_specs=[pl.BlockSpec((1, tile, DV_B), lambda b, h, i, *_: (b, i, h))] + rider_specs,
        scratch_shapes=[pltpu.VMEM((3, tile, tile), F32),
                        pltpu.VMEM((2, tile, blk), F32),
                        pltpu.VMEM((2, tile, blk), F32),
                        pltpu.VMEM((2, tile, DV_B), F32),
                        pltpu.VMEM((2, tile, tile), F32),
                        pltpu.VMEM((2, tile, tile), F32)])
    out = pl.pallas_call(
        kern,
        grid_spec=grid_spec,
        out_shape=[jax.ShapeDtypeStruct((batch, seq, H_B * DV_B), BF16)] + rider_shapes,
        compiler_params=_params("arbitrary", "arbitrary", "arbitrary"),
        name="diff_attention",
    )(far, rel_bias.astype(F32), p, p, p, near, lam_vecs.astype(F32), subln_g.reshape(1, DV_B),
      *riders)
    return out[0], out[1:]


def _rotary_tables(seq):
    inv_freq = ROPE_BASE ** (-jnp.arange(0, DK_C, 2, dtype=F32) / DK_C)
    ang = jnp.arange(seq, dtype=F32)[:, None] * inv_freq[None, :]
    cos = jnp.concatenate([jnp.cos(ang), jnp.cos(ang)], axis=-1)
    sin = jnp.concatenate([-jnp.sin(ang), jnp.sin(ang)], axis=-1)
    return cos, sin


def _decay_tables():
    log_g = jnp.log(1.0 - 2.0 ** (-5.0 - jnp.arange(H_C, dtype=F32)))
    idx = jnp.arange(RET_BLOCK, dtype=F32)
    chunk = jnp.arange(RET_BLOCK) // CHUNK
    visible = chunk[None, :] <= chunk[:, None]
    intra = jnp.where(visible[None],
                      jnp.exp(log_g[:, None, None] * jnp.abs(idx[:, None] - idx[None, :])), 0.0)
    q_decay = jnp.exp(log_g[:, None] * (idx + 1.0))
    k_decay = jnp.exp(log_g[:, None] * (RET_BLOCK - 1.0 - idx))
    block_decay = jnp.exp(log_g * RET_BLOCK)
    return (intra,
            jnp.broadcast_to(q_decay[:, :, None], (H_C, RET_BLOCK, DV_C)),
            jnp.broadcast_to(k_decay[:, :, None], (H_C, RET_BLOCK, DK_C)),
            jnp.broadcast_to(block_decay[:, None, None], (H_C, 1, DV_C)))


def _retention_kernel(q_ref, k_ref, v_ref, gate_ref, cos_ref, sin_ref, intra_ref, qd_ref,
                      kd_ref, bd_ref, g_ref, o_ref, state_ref):
    rows = q_ref.shape[1]

    @pl.when(pl.program_id(1) == 0)
    def _():
        state_ref[...] = jnp.zeros_like(state_ref)

    cos = cos_ref[...]
    sin = sin_ref[...]

    def rotary(t):
        return t * cos + pltpu.roll(t, DK_C // 2, 1) * sin

    for h in range(H_C):
        qk_cols = slice(h * DK_C, (h + 1) * DK_C)
        v_cols = slice(h * DV_C, (h + 1) * DV_C)
        q = (rotary(q_ref[0, :, qk_cols]) * (DK_C ** -0.5)).astype(BF16)
        k = rotary(k_ref[0, :, qk_cols])
        state = state_ref[h]
        for r in range(0, rows, RET_BLOCK):
            sl = slice(r, r + RET_BLOCK)
            qc = q[sl]
            kc = k[sl]
            vc = v_ref[0, sl, v_cols].astype(BF16)
            scores = _dot_nt(qc, kc.astype(BF16)) * intra_ref[h]
            y = _dot(scores.astype(BF16), vc) + _dot(qc, state.astype(BF16)) * qd_ref[h]
            state = bd_ref[h] * state + _dot((kc * kd_ref[h]).T.astype(BF16), vc)
            gate = gate_ref[0, sl, v_cols]
            y = _rms_rows(y, g_ref[...])
            o_ref[0, sl, v_cols] = (gate * (1.0 / (1.0 + jnp.exp(-gate))) * y).astype(o_ref.dtype)
        state_ref[h] = state


def _retention(p, ret_norm_g, batch, seq, rows=512):
    qk_width = H_C * DK_C
    v_width = H_C * DV_C
    cos, sin = _rotary_tables(seq)
    intra, q_decay, k_decay, block_decay = _decay_tables()

    def whole(a):
        return pl.BlockSpec(a.shape, lambda b, i: (0,) * a.ndim)

    return pl.pallas_call(
        _retention_kernel,
        grid=(batch, seq // rows),
        in_specs=[pl.BlockSpec((1, rows, qk_width), lambda b, i: (b, i, 0)),
                  pl.BlockSpec((1, rows, qk_width), lambda b, i: (b, i, 1)),
                  pl.BlockSpec((1, rows, v_width), lambda b, i: (b, i, 2 * qk_width // v_width)),
                  pl.BlockSpec((1, rows, v_width), lambda b, i: (b, i, 2 * qk_width // v_width + 1)),
                  pl.BlockSpec((rows, DK_C), lambda b, i: (i, 0)),
                  pl.BlockSpec((rows, DK_C), lambda b, i: (i, 0)),
                  whole(intra), whole(q_decay), whole(k_decay), whole(block_decay),
                  pl.BlockSpec((1, DV_C), lambda b, i: (0, 0))],
        out_specs=pl.BlockSpec((1, rows, v_width), lambda b, i: (b, i, 0)),
        out_shape=jax.ShapeDtypeStruct((batch, seq, v_width), BF16),
        scratch_shapes=[pltpu.VMEM((H_C, DK_C, DV_C), F32)],
        compiler_params=_params("arbitrary", "arbitrary"),
        name="retention",
    )(p, p, p, p, cos, sin, intra, q_decay, k_decay, block_decay, ret_norm_g.reshape(1, DV_C))


def _gelu_tanh(x):
    return 0.5 * x * (1.0 + jnp.tanh(math.sqrt(2.0 / math.pi) * (x + 0.044715 * (x * x * x))))


def _sgu_kernel(zu_ref, zv_ref, lg_ref, lb_ref, w_ref, b_ref, o_ref):
    rows = zu_ref.shape[0]
    width = D_D // G_D
    v = _gelu_tanh(zv_ref[...])
    mu = jnp.mean(v, axis=-1, keepdims=True)
    var = jnp.mean(jnp.square(v - mu), axis=-1, keepdims=True)
    vn = ((v - mu) * lax.rsqrt(var + EPS) * lg_ref[...] + lb_ref[...]).astype(BF16)
    pos_i = lax.broadcasted_iota(jnp.int32, (SGU_LEN, SGU_LEN), 0)
    pos_j = lax.broadcasted_iota(jnp.int32, (SGU_LEN, SGU_LEN), 1)
    mask = (pos_j // CHUNK) <= (pos_i // CHUNK)
    for g in range(G_D):
        w = jnp.where(mask, w_ref[g], 0.0).astype(BF16)
        bias = b_ref[g]
        cols = slice(g * width, (g + 1) * width)
        for r in range(rows // SGU_LEN):
            sl = slice(r * SGU_LEN, (r + 1) * SGU_LEN)
            gate = _dot(w, vn[sl, cols]) + bias
            o_ref[sl, cols] = (_gelu_tanh(zu_ref[sl, cols]) * gate).astype(o_ref.dtype)


def _sgu(p, ln_g, ln_b, w_s, b_s, rows=512):
    t = p.shape[0]
    u_col = (2 * H_C * DK_C + 2 * H_C * DV_C) // D_D
    return pl.pallas_call(
        _sgu_kernel,
        grid=(t // rows,),
        in_specs=[pl.BlockSpec((rows, D_D), lambda i: (i, u_col)),
                  pl.BlockSpec((rows, D_D), lambda i: (i, u_col + 1)),
                  pl.BlockSpec((1, D_D), lambda i: (0, 0)),
                  pl.BlockSpec((1, D_D), lambda i: (0, 0)),
                  pl.BlockSpec((G_D, SGU_LEN, SGU_LEN), lambda i: (0, 0, 0)),
                  pl.BlockSpec((G_D, SGU_LEN, 1), lambda i: (0, 0, 0))],
        out_specs=pl.BlockSpec((rows, D_D), lambda i: (i, 0)),
        out_shape=jax.ShapeDtypeStruct((t, D_D), BF16),
        compiler_params=_params("parallel"),
        name="spatial_gate",
    )(p, p, ln_g.reshape(1, D_D), ln_b.reshape(1, D_D), w_s, b_s.reshape(G_D, SGU_LEN, 1))


def kernel(x, norm_mix_g, norm_ffn_g, final_norm_g, rel_bias, ab_w_in, ab_w_out, diff_lambda,
           diff_subln_g, cd_w_in, cd_w_out, ret_norm_g, sgu_ln_g, sgu_ln_b, sgu_w, sgu_b,
           ffn_w_up, ffn_conv_w, ffn_conv_b, ffn_w_down):
    batch, seq, d = x.shape
    t = batch * seq
    depth = norm_mix_g.shape[0]
    xt = x.reshape(t, d)
    def rows(w):
        return w.reshape(-1, w.shape[-1])

    ab_w_in = ab_w_in.astype(BF16)
    for layer in range(depth):
        j = layer // 2
        if layer % 2 == 0:
            lam_init = 0.8 - 0.6 * math.exp(-0.3 * layer)
            p = _norm_matmul(xt, norm_mix_g[layer], ab_w_in, j, BF16)
            p3 = p.reshape(batch, seq, p.shape[1])
            if layer == 0:
                o_a, (up_bf,) = _stick_breaking(p3, batch, seq, (rows(ffn_w_up),))
                o_b, (down_bf, abo_bf, cdi_bf, cdo_bf) = _diff_attention(
                    p3, rel_bias, diff_lambda[j], diff_subln_g[j], lam_init, batch, seq,
                    (rows(ffn_w_down), rows(ab_w_out), rows(cd_w_in), rows(cd_w_out)))
                ffn_w_up, ffn_w_down, ab_w_out, cd_w_in, cd_w_out = (
                    b.reshape(w.shape) for b, w in ((up_bf, ffn_w_up), (down_bf, ffn_w_down),
                                                    (abo_bf, ab_w_out), (cdi_bf, cd_w_in),
                                                    (cdo_bf, cd_w_out)))
            else:
                o_a, _ = _stick_breaking(p3, batch, seq)
                o_b, _ = _diff_attention(p3, rel_bias, diff_lambda[j], diff_subln_g[j], lam_init,
                                         batch, seq)
            xt = _out_proj(o_a.reshape(t, -1), o_b.reshape(t, -1), ab_w_out, j, xt)
        else:
            p = _norm_matmul(xt, norm_mix_g[layer], cd_w_in, j, F32)
            o_c = _retention(p.reshape(batch, seq, p.shape[1]), ret_norm_g[j], batch, seq)
            o_d = _sgu(p, sgu_ln_g[j], sgu_ln_b[j], sgu_w[j], sgu_b[j])
            xt = _out_proj(o_c.reshape(t, -1), o_d, cd_w_out, j, xt)
        xt = _ffn(xt, norm_ffn_g, ffn_w_up, ffn_conv_w, ffn_conv_b, ffn_w_down, final_norm_g,
                  layer, seq=seq, final_norm=(layer == depth - 1))
    return xt.reshape(batch, seq, d)
```

```python
import functools
import math

import numpy as np
import jax
import jax.numpy as jnp
from jax import lax
from jax.experimental import pallas as pl
from jax.experimental.pallas import tpu as pltpu

F32 = jnp.float32
BF16 = jnp.bfloat16

EPS = 1e-6
CHUNK = 64
H_A = 8
DH_A = 128
H_B = 4
DK_B = 128
DV_B = 2 * DK_B
NUM_BUCKETS = 32
MAX_DISTANCE = 128
H_C = 4
DK_C = 128
DV_C = 2 * DK_C
ROPE_BASE = 10000.0
D_D = 1024
G_D = 4
SGU_LEN = 128
CONV_WIDTH = 3

LANES = 128
BF16_SUBLANES = 16
ATT_BLOCK = 128
ATT_TILE = 512
RET_BLOCK = 256
VMEM_LIMIT = 62 * 1024 * 1024
PROJ_ROWS = 1024
PROJ_TILE = 1024
FFN_ROWS = 1024
FFN_TILE = 512
GATE_ROWS = 32
LOG2E = math.log2(math.e)
ZERO_WEIGHT_LOG2 = -150.0


def _params(*semantics):
    return pltpu.CompilerParams(dimension_semantics=semantics,
                                vmem_limit_bytes=VMEM_LIMIT)


def _dot(a, b):
    return jnp.dot(a, b, preferred_element_type=F32)


def _dot_nt(a, b):
    return lax.dot_general(a, b, (((1,), (1,)), ((), ())), preferred_element_type=F32)


def _rms_rows(x, g):
    return x * lax.rsqrt(jnp.mean(x * x, axis=-1, keepdims=True) + EPS) * g


def _with_riders(body, n_in, n_riders):
    def kern(*refs):
        ins = refs[:n_in]
        rider_in = refs[n_in:n_in + n_riders]
        out = refs[n_in + n_riders]
        rider_out = refs[n_in + n_riders + 1:n_in + 2 * n_riders + 1]
        scratch = refs[n_in + 2 * n_riders + 1:]
        for src, dst in zip(rider_in, rider_out):
            dst[...] = src[...].astype(BF16)
        body(*ins, out, *scratch)
    return kern


def _rider_specs(riders, grid):
    steps = math.prod(grid)

    def linear_step(*g):
        n = g[0]
        for size, idx in zip(grid[1:], g[1:len(grid)]):
            n = n * size + idx
        return n

    specs, shapes = [], []
    for w in riders:
        rows, cols = w.shape
        block_rows, rem = divmod(rows, steps)
        assert rem == 0 and block_rows % BF16_SUBLANES == 0, (w.shape, steps)
        specs.append(pl.BlockSpec((block_rows, cols), lambda *g: (linear_step(*g), 0)))
        shapes.append(jax.ShapeDtypeStruct((rows, cols), BF16))
    return specs, shapes


def _norm_matmul_kernel(x_ref, g_ref, w_ref, o_ref, h_ref):
    @pl.when(pl.program_id(1) == 0)
    def _():
        h_ref[...] = _rms_rows(x_ref[...], g_ref[...]).astype(BF16)

    o_ref[...] = _dot(h_ref[...], w_ref[...]).astype(o_ref.dtype)


def _norm_matmul(x, g, w, layer, out_dtype, tm=PROJ_ROWS, tn=PROJ_TILE):
    t, d = x.shape
    n = w.shape[2]
    return pl.pallas_call(
        _norm_matmul_kernel,
        grid=(t // tm, n // tn),
        in_specs=[pl.BlockSpec((tm, d), lambda i, j: (i, 0)),
                  pl.BlockSpec((1, d), lambda i, j: (0, 0)),
                  pl.BlockSpec((None, d, tn), lambda i, j: (layer, 0, j))],
        out_specs=pl.BlockSpec((tm, tn), lambda i, j: (i, j)),
        out_shape=jax.ShapeDtypeStruct((t, n), out_dtype),
        scratch_shapes=[pltpu.VMEM((tm, d), BF16)],
        compiler_params=_params("parallel", "arbitrary"),
        name="norm_matmul",
    )(x, g.reshape(1, d), w)


def _out_proj_kernel(a1_ref, a2_ref, w_ref, x_ref, o_ref):
    half = a1_ref.shape[1]
    acc = _dot(a1_ref[...], w_ref[0:half, :])
    acc += _dot(a2_ref[...], w_ref[half:2 * half, :])
    o_ref[...] = x_ref[...] + acc


def _out_proj(a1, a2, w, layer, x, tm=512):
    t, half = a1.shape
    d = w.shape[2]
    return pl.pallas_call(
        _out_proj_kernel,
        grid=(t // tm,),
        in_specs=[pl.BlockSpec((tm, half), lambda i: (i, 0)),
                  pl.BlockSpec((tm, half), lambda i: (i, 0)),
                  pl.BlockSpec((None, 2 * half, d), lambda i: (layer, 0, 0)),
                  pl.BlockSpec((tm, d), lambda i: (i, 0))],
        out_specs=pl.BlockSpec((tm, d), lambda i: (i, 0)),
        out_shape=jax.ShapeDtypeStruct((t, d), F32),
        compiler_params=_params("parallel"),
        name="out_proj",
    )(a1, a2, w, x)


def _ffn_kernel(x_ref, g_ref, wa_ref, wg_ref, cwa_ref, cwg_ref, cba_ref, cbg_ref,
                wd_ref, fg_ref, o_ref, h_ref, act0_ref, act1_ref, carry_ref, up0_ref, up1_ref,
                *, seq_tiles, final_norm, n_hidden_tiles):
    i = pl.program_id(0)
    j = pl.program_id(1)
    nj = n_hidden_tiles
    tm = x_ref.shape[0]
    ups = (up0_ref, up1_ref)
    acts = (act0_ref, act1_ref)

    def up_phase(parity):
        h = h_ref[...]
        for branch, w_ref in ((0, wa_ref), (1, wg_ref)):
            u_ref = ups[parity].at[branch]
            u_ref[0:8, :] = carry_ref[j, branch]
            u_ref[8:8 + tm, :] = _dot(h, w_ref[...])
            carry_ref[j, branch] = u_ref[tm:tm + 8, :]

    def gate_phase(parity):
        def conv(branch, cw_ref, cb_ref, r):
            u_ref = ups[parity].at[branch]
            return (cb_ref[...] + cw_ref[0:1, :] * u_ref[6 + r:6 + r + GATE_ROWS, :]
                    + cw_ref[1:2, :] * u_ref[7 + r:7 + r + GATE_ROWS, :]
                    + cw_ref[2:3, :] * u_ref[8 + r:8 + r + GATE_ROWS, :])

        for r in range(0, tm, GATE_ROWS):
            ca = conv(0, cwa_ref, cba_ref, r)
            cg = conv(1, cwg_ref, cbg_ref, r)
            acts[parity][r:r + GATE_ROWS, :] = (cg * (1.0 / (1.0 + jnp.exp(-cg))) * ca).astype(BF16)

    def down_phase(parity):
        o_ref[...] += _dot(acts[parity][...], wd_ref[...])

    @pl.when(((i % seq_tiles) == 0) & (j < nj))
    def _():
        carry_ref[j] = jnp.zeros(carry_ref.shape[1:], F32)

    @pl.when(j == 0)
    def _():
        x = x_ref[...]
        h_ref[...] = _rms_rows(x, g_ref[...]).astype(BF16)
        o_ref[...] = x
        up_phase(0)

    @pl.when(j == 1)
    def _():
        gate_phase(0)
        up_phase(1)

    for parity in range(2):
        @pl.when((j >= 2) & (j < nj) & (j % 2 == parity))
        def _():
            gate_phase(1 - parity)
            up_phase(parity)
            down_phase(parity)

    @pl.when(j == nj)
    def _():
        gate_phase((nj - 1) % 2)
        down_phase(nj % 2)

    @pl.when(j == nj + 1)
    def _():
        y = o_ref[...] + _dot(acts[(nj - 1) % 2][...], wd_ref[...])
        if final_norm:
            y = _rms_rows(y, fg_ref[...])
        o_ref[...] = y


def _ffn(x, g, w_up, conv_w, conv_b, w_down, final_g, layer, *, seq, final_norm,
         tm=FFN_ROWS, tf=FFN_TILE):
    t, d = x.shape
    f = w_down.shape[1]
    nj = f // tf
    assert nj >= 2
    kern = functools.partial(_ffn_kernel, seq_tiles=seq // tm, final_norm=final_norm,
                             n_hidden_tiles=nj)

    def tile_of(j, lag):
        return jnp.clip(j - lag, 0, nj - 1)

    return pl.pallas_call(
        kern,
        grid=(t // tm, nj + 2),
        in_specs=[pl.BlockSpec((tm, d), lambda i, j: (i, 0), pipeline_mode=pl.Buffered(1)),
                  pl.BlockSpec((None, 1, d), lambda i, j: (layer, 0, 0)),
                  pl.BlockSpec((None, d, tf), lambda i, j: (layer, 0, tile_of(j, 0))),
                  pl.BlockSpec((None, d, tf), lambda i, j: (layer, 0, tile_of(j, 0) + nj)),
                  pl.BlockSpec((None, CONV_WIDTH, tf), lambda i, j: (layer, 0, tile_of(j, 1))),
                  pl.BlockSpec((None, CONV_WIDTH, tf), lambda i, j: (layer, 0, tile_of(j, 1) + nj)),
                  pl.BlockSpec((None, 1, tf), lambda i, j: (layer, 0, tile_of(j, 1))),
                  pl.BlockSpec((None, 1, tf), lambda i, j: (layer, 0, tile_of(j, 1) + nj)),
                  pl.BlockSpec((None, tf, d), lambda i, j: (layer, tile_of(j, 2), 0)),
                  pl.BlockSpec((1, d), lambda i, j: (0, 0))],
        out_specs=pl.BlockSpec((tm, d), lambda i, j: (i, 0)),
        out_shape=jax.ShapeDtypeStruct((t, d), F32),
        scratch_shapes=[pltpu.VMEM((tm, d), BF16),
                        pltpu.VMEM((tm, tf), BF16),
                        pltpu.VMEM((tm, tf), BF16),
                        pltpu.VMEM((nj, 2, 8, tf), F32),
                        pltpu.VMEM((2, tm + 8, tf), F32),
                        pltpu.VMEM((2, tm + 8, tf), F32)],
        compiler_params=_params("arbitrary", "arbitrary"),
        name="conv_ffn",
    )(x, g[:, None, :], w_up, w_up, conv_w, conv_w, conv_b[:, None, :], conv_b[:, None, :],
      w_down, final_g.reshape(1, d))


def _suffix_sum_matrix():
    blk = ATT_BLOCK
    j = np.arange(blk)[:, None]
    s = np.arange(blk)[None, :]
    m = np.concatenate([(j > s).astype(np.float32), np.ones((blk, blk), np.float32)], axis=1)
    return jnp.asarray(np.concatenate([m, m], axis=0), dtype=BF16)


def _stick_kernel(q_ref, k_ref, v_ref, m_ref, o_ref, acc_ref, carry_ref):
    blk = ATT_BLOCK
    tile = q_ref.shape[1]
    nsub = tile // blk
    i = pl.program_id(2)
    q = q_ref[0]
    scale2 = DH_A ** -0.5 * LOG2E
    suffix = m_ref[...]

    acc_ref[...] = jnp.zeros_like(acc_ref)
    carry_ref[...] = jnp.zeros_like(carry_ref)

    def step(j, diagonal):
        start = pl.multiple_of(j * tile, tile)
        k = k_ref[0, pl.ds(start, tile), :]
        v = v_ref[0, pl.ds(start, tile), :]
        z_all = _dot_nt(q, k) * scale2
        carry = carry_ref[...]
        ws = [None] * nsub
        for c in reversed(range(nsub)):
            z = z_all[:, c * blk:(c + 1) * blk]
            if diagonal:
                row = lax.broadcasted_iota(jnp.int32, (tile, blk), 0)
                col = lax.broadcasted_iota(jnp.int32, (tile, blk), 1) + c * blk
                z = jnp.where(col < row, z, -1e4)
            log_sig = jnp.minimum(z, 0.0) - jnp.log2(1.0 + jnp.exp2(-jnp.abs(z)))
            log_1m = log_sig - z
            hi = log_1m.astype(BF16)
            lo = (log_1m - hi.astype(F32)).astype(BF16)
            sums = _dot(jnp.concatenate([hi, lo], axis=1), suffix)
            ws[c] = jnp.exp2(log_sig + sums[:, :blk] + carry).astype(BF16)
            carry = carry + sums[:, blk:]
        carry_ref[...] = carry
        acc_ref[...] += _dot(jnp.concatenate(ws, axis=1), v)

    step(i, True)

    def more(t):
        return jnp.logical_and(t < i, jnp.max(carry_ref[...]) > ZERO_WEIGHT_LOG2)

    def body(t):
        step(i - 1 - t, False)
        return t + 1

    lax.while_loop(more, body, 0)
    o_ref[0] = acc_ref[...].astype(o_ref.dtype)


def _stick_breaking(p, batch, seq, riders=()):
    blk = ATT_BLOCK
    tile = ATT_TILE
    grid = (batch, H_A, seq // tile)
    rider_specs, rider_shapes = _rider_specs(riders, grid)
    out = pl.pallas_call(
        _with_riders(_stick_kernel, 4, len(riders)),
        grid=grid,
        in_specs=[pl.BlockSpec((1, tile, DH_A), lambda b, h, i: (b, i, h)),
                  pl.BlockSpec((1, seq, DH_A), lambda b, h, i: (b, 0, H_A + h)),
                  pl.BlockSpec((1, seq, DH_A), lambda b, h, i: (b, 0, 2 * H_A + h)),
                  pl.BlockSpec((2 * blk, 2 * blk), lambda b, h, i: (0, 0))] + rider_specs,
        out_specs=[pl.BlockSpec((1, tile, DH_A), lambda b, h, i: (b, i, h))] + rider_specs,
        out_shape=[jax.ShapeDtypeStruct((batch, seq, H_A * DH_A), BF16)] + rider_shapes,
        scratch_shapes=[pltpu.VMEM((tile, DH_A), F32),
                        pltpu.VMEM((tile, blk), F32)],
        compiler_params=_params("arbitrary", "arbitrary", "arbitrary"),
        name="stick_breaking",
    )(p, p, p, _suffix_sum_matrix(), *riders)
    return out[0], out[1:]


def _rel_bucket(rel):
    nb = NUM_BUCKETS // 2
    max_exact = nb // 2
    ret = jnp.where(rel > 0, nb, 0)
    n = jnp.abs(rel)
    n_f = jnp.maximum(n, 1).astype(F32)
    large = max_exact + (jnp.log(n_f / max_exact) / math.log(MAX_DISTANCE / max_exact)
                         * (nb - max_exact)).astype(jnp.int32)
    large = jnp.minimum(large, nb - 1)
    return ret + jnp.where(n < max_exact, n, large)


def _near_buckets():
    blk = ATT_BLOCK
    qpos = blk + jnp.arange(blk)
    kpos = jnp.arange(2 * blk)
    near = _rel_bucket(kpos[None, :] - qpos[:, None]).astype(jnp.int32)
    far = _rel_bucket(jnp.full((1,), -2 * blk, jnp.int32)).astype(jnp.int32)
    return near, far


def _diff_kernel(far_ref, relb_ref, q_ref, k_ref, v_ref, bucket_ref, lam_ref, g_ref, o_ref,
                 bias_ref, m_ref, l_ref, acc_ref, s0_ref, s1_ref, *, lam_init):
    blk = ATT_BLOCK
    tile = q_ref.shape[1]
    nsub = tile // blk
    h = pl.program_id(1)
    i = pl.program_id(2)
    scale2 = DK_B ** -0.5 * LOG2E

    @pl.when(i == 0)
    def _():
        bucket = bucket_ref[...]
        far = relb_ref[far_ref[0], h]
        near = jnp.zeros((blk, 2 * blk), F32)
        for b in range(NUM_BUCKETS):
            near = jnp.where(bucket == b, relb_ref[b, h], near)
        near = (near - far) * LOG2E
        qpos = blk + lax.broadcasted_iota(jnp.int32, (blk, 2 * blk), 0)
        kpos = lax.broadcasted_iota(jnp.int32, (blk, 2 * blk), 1)
        near = jnp.where((kpos // CHUNK) <= (qpos // CHUNK), near, -jnp.inf)
        bias_ref[...] = jnp.zeros_like(bias_ref)
        for a in range(nsub):
            rows = slice(a * blk, (a + 1) * blk)
            bias_ref[0, rows, a * blk:(a + 1) * blk] = near[:, blk:]
            if a >= 1:
                bias_ref[0, rows, (a - 1) * blk:a * blk] = near[:, :blk]
            if a + 1 < nsub:
                bias_ref[0, rows, (a + 1) * blk:] = jnp.full((blk, tile - (a + 1) * blk),
                                                             -jnp.inf, F32)
        bias_ref[1, 0:blk, (nsub - 1) * blk:] = near[:, :blk]

    m_ref[...] = jnp.full_like(m_ref, -jnp.inf)
    l_ref[...] = jnp.zeros_like(l_ref)
    acc_ref[...] = jnp.zeros_like(acc_ref)

    def key_tile(idx):
        return jnp.where(idx == 0, i, jnp.where(idx == 1, i - 1, idx - 2))

    def logits_into(s_ref, idx):
        j = key_tile(jnp.minimum(idx, i))
        kk = k_ref[0, pl.ds(pl.multiple_of(j * tile, tile), tile), :]
        for c in range(2):
            s_ref[c] = _dot_nt(q_ref[0, :, c * DK_B:(c + 1) * DK_B],
                               kk[:, c * DK_B:(c + 1) * DK_B])

    def consume(s_ref, idx):
        j = key_tile(idx)
        v = v_ref[0, pl.ds(pl.multiple_of(j * tile, tile), tile), :]
        bias = bias_ref[jnp.minimum(idx, 2)]
        probs = []
        for c in range(2):
            s = s_ref[c] * scale2 + bias
            subs = [s[:, u * blk:(u + 1) * blk] for u in range(nsub)]
            m_prev = m_ref[c]
            m_cur = functools.reduce(jnp.maximum, subs)
            m_new = jnp.maximum(m_prev, jnp.max(m_cur, axis=-1, keepdims=True))
            alpha = jnp.exp2(m_prev - m_new)
            ps = [jnp.exp2(u - m_new) for u in subs]
            l_ref[c] = alpha * l_ref[c] + functools.reduce(jnp.add, ps)
            m_ref[c] = m_new
            probs.append((alpha, jnp.concatenate([u.astype(BF16) for u in ps], axis=1)))
        for c in range(2):
            alpha, p = probs[c]
            acc_ref[c] = jnp.concatenate([alpha] * (DV_B // blk), axis=1) * acc_ref[c] + _dot(p, v)

    logits_into(s0_ref, 0)

    def pair(t, _):
        idx = 2 * t
        logits_into(s1_ref, idx + 1)
        consume(s0_ref, idx)
        logits_into(s0_ref, idx + 2)
        consume(s1_ref, idx + 1)
        return 0

    lax.fori_loop(0, (i + 1) // 2, pair, 0)

    @pl.when(i % 2 == 0)
    def _():
        consume(s0_ref, i)

    lv = lam_ref[...]
    lam = (jnp.exp(jnp.sum(lv[0:1] * lv[1:2], axis=-1, keepdims=True))
           - jnp.exp(jnp.sum(lv[2:3] * lv[3:4], axis=-1, keepdims=True)) + lam_init)
    l1 = jnp.sum(l_ref[0], axis=-1, keepdims=True)
    l2 = jnp.sum(l_ref[1], axis=-1, keepdims=True)
    o = acc_ref[0] / l1 - lam * (acc_ref[1] / l2)
    o_ref[0] = (_rms_rows(o, g_ref[...]) * (1.0 - lam_init)).astype(o_ref.dtype)


def _diff_attention(p, rel_bias, lam_vecs, subln_g, lam_init, batch, seq, riders=()):
    blk = ATT_BLOCK
    tile = ATT_TILE
    q_col = 3 * H_A * DH_A // DV_B
    k_col = q_col + H_B
    v_col = k_col + H_B
    near, far = _near_buckets()
    grid = (batch, H_B, seq // tile)
    rider_specs, rider_shapes = _rider_specs(riders, grid)
    kern = _with_riders(functools.partial(_diff_kernel, lam_init=lam_init), 8, len(riders))
    grid_spec = pltpu.PrefetchScalarGridSpec(
        num_scalar_prefetch=2,
        grid=grid,
        in_specs=[pl.BlockSpec((1, tile, DV_B), lambda b, h, i, *_: (b, i, q_col + h)),
                  pl.BlockSpec((1, seq, DV_B), lambda b, h, i, *_: (b, 0, k_col + h)),
                  pl.BlockSpec((1, seq, DV_B), lambda b, h, i, *_: (b, 0, v_col + h)),
                  pl.BlockSpec((blk, 2 * blk), lambda b, h, i, *_: (0, 0)),
                  pl.BlockSpec((4, DK_B), lambda b, h, i, *_: (0, 0)),
                  pl.BlockSpec((1, DV_B), lambda b, h, i, *_: (0, 0))] + rider_specs,
        out_specs=[pl.BlockSpec((1, tile, DV_B), lambda b, h, i, *_: (b, i, h))] + rider_specs,
        scratch_shapes=[pltpu.VMEM((3, tile, tile), F32),
                        pltpu.VMEM((2, tile, blk), F32),
                        pltpu.VMEM((2, tile, blk), F32),
                        pltpu.VMEM((2, tile, DV_B), F32),
                        pltpu.VMEM((2, tile, tile), F32),
                        pltpu.VMEM((2, tile, tile), F32)])
    out = pl.pallas_call(
        kern,
        grid_spec=grid_spec,
        out_shape=[jax.ShapeDtypeStruct((batch, seq, H_B * DV_B), BF16)] + rider_shapes,
        compiler_params=_params("arbitrary", "arbitrary", "arbitrary"),
        name="diff_attention",
    )(far, rel_bias.astype(F32), p, p, p, near, lam_vecs.astype(F32), subln_g.reshape(1, DV_B),
      *riders)
    return out[0], out[1:]


def _rotary_tables(seq):
    inv_freq = ROPE_BASE ** (-jnp.arange(0, DK_C, 2, dtype=F32) / DK_C)
    ang = jnp.arange(seq, dtype=F32)[:, None] * inv_freq[None, :]
    cos = jnp.concatenate([jnp.cos(ang), jnp.cos(ang)], axis=-1)
    sin = jnp.concatenate([-jnp.sin(ang), jnp.sin(ang)], axis=-1)
    return cos, sin


def _decay_tables():
    log_g = jnp.log(1.0 - 2.0 ** (-5.0 - jnp.arange(H_C, dtype=F32)))
    idx = jnp.arange(RET_BLOCK, dtype=F32)
    chunk = jnp.arange(RET_BLOCK) // CHUNK
    visible = chunk[None, :] <= chunk[:, None]
    intra = jnp.where(visible[None],
                      jnp.exp(log_g[:, None, None] * jnp.abs(idx[:, None] - idx[None, :])), 0.0)
    q_decay = jnp.exp(log_g[:, None] * (idx + 1.0))
    k_decay = jnp.exp(log_g[:, None] * (RET_BLOCK - 1.0 - idx))
    block_decay = jnp.exp(log_g * RET_BLOCK)
    return (intra,
            jnp.broadcast_to(q_decay[:, :, None], (H_C, RET_BLOCK, DV_C)),
            jnp.broadcast_to(k_decay[:, :, None], (H_C, RET_BLOCK, DK_C)),
            jnp.broadcast_to(block_decay[:, None, None], (H_C, 1, DV_C)))


def _retention_kernel(q_ref, k_ref, v_ref, gate_ref, cos_ref, sin_ref, intra_ref, qd_ref,
                      kd_ref, bd_ref, g_ref, o_ref, state_ref):
    rows = q_ref.shape[1]

    @pl.when(pl.program_id(1) == 0)
    def _():
        state_ref[...] = jnp.zeros_like(state_ref)

    cos = cos_ref[...]
    sin = sin_ref[...]

    def rotary(t):
        return t * cos + pltpu.roll(t, DK_C // 2, 1) * sin

    for h in range(H_C):
        qk_cols = slice(h * DK_C, (h + 1) * DK_C)
        v_cols = slice(h * DV_C, (h + 1) * DV_C)
        q = (rotary(q_ref[0, :, qk_cols]) * (DK_C ** -0.5)).astype(BF16)
        k = rotary(k_ref[0, :, qk_cols])
        state = state_ref[h]
        for r in range(0, rows, RET_BLOCK):
            sl = slice(r, r + RET_BLOCK)
            qc = q[sl]
            kc = k[sl]
            vc = v_ref[0, sl, v_cols].astype(BF16)
            scores = _dot_nt(qc, kc.astype(BF16)) * intra_ref[h]
            y = _dot(scores.astype(BF16), vc) + _dot(qc, state.astype(BF16)) * qd_ref[h]
            state = bd_ref[h] * state + _dot((kc * kd_ref[h]).T.astype(BF16), vc)
            gate = gate_ref[0, sl, v_cols]
            y = _rms_rows(y, g_ref[...])
            o_ref[0, sl, v_cols] = (gate * (1.0 / (1.0 + jnp.exp(-gate))) * y).astype(o_ref.dtype)
        state_ref[h] = state


def _retention(p, ret_norm_g, batch, seq, rows=512):
    qk_width = H_C * DK_C
    v_width = H_C * DV_C
    cos, sin = _rotary_tables(seq)
    intra, q_decay, k_decay, block_decay = _decay_tables()

    def whole(a):
        return pl.BlockSpec(a.shape, lambda b, i: (0,) * a.ndim)

    return pl.pallas_call(
        _retention_kernel,
        grid=(batch, seq // rows),
        in_specs=[pl.BlockSpec((1, rows, qk_width), lambda b, i: (b, i, 0)),
                  pl.BlockSpec((1, rows, qk_width), lambda b, i: (b, i, 1)),
                  pl.BlockSpec((1, rows, v_width), lambda b, i: (b, i, 2 * qk_width // v_width)),
                  pl.BlockSpec((1, rows, v_width), lambda b, i: (b, i, 2 * qk_width // v_width + 1)),
                  pl.BlockSpec((rows, DK_C), lambda b, i: (i, 0)),
                  pl.BlockSpec((rows, DK_C), lambda b, i: (i, 0)),
                  whole(intra), whole(q_decay), whole(k_decay), whole(block_decay),
                  pl.BlockSpec((1, DV_C), lambda b, i: (0, 0))],
        out_specs=pl.BlockSpec((1, rows, v_width), lambda b, i: (b, i, 0)),
        out_shape=jax.ShapeDtypeStruct((batch, seq, v_width), BF16),
        scratch_shapes=[pltpu.VMEM((H_C, DK_C, DV_C), F32)],
        compiler_params=_params("arbitrary", "arbitrary"),
        name="retention",
    )(p, p, p, p, cos, sin, intra, q_decay, k_decay, block_decay, ret_norm_g.reshape(1, DV_C))


def _gelu_tanh(x):
    return 0.5 * x * (1.0 + jnp.tanh(math.sqrt(2.0 / math.pi) * (x + 0.044715 * (x * x * x))))


def _sgu_kernel(zu_ref, zv_ref, lg_ref, lb_ref, w_ref, b_ref, o_ref):
    rows = zu_ref.shape[0]
    width = D_D // G_D
    v = _gelu_tanh(zv_ref[...])
    mu = jnp.mean(v, axis=-1, keepdims=True)
    var = jnp.mean(jnp.square(v - mu), axis=-1, keepdims=True)
    vn = ((v - mu) * lax.rsqrt(var + EPS) * lg_ref[...] + lb_ref[...]).astype(BF16)
    pos_i = lax.broadcasted_iota(jnp.int32, (SGU_LEN, SGU_LEN), 0)
    pos_j = lax.broadcasted_iota(jnp.int32, (SGU_LEN, SGU_LEN), 1)
    mask = (pos_j // CHUNK) <= (pos_i // CHUNK)
    for g in range(G_D):
        w = jnp.where(mask, w_ref[g], 0.0).astype(BF16)
        bias = b_ref[g]
        cols = slice(g * width, (g + 1) * width)
        for r in range(rows // SGU_LEN):
            sl = slice(r * SGU_LEN, (r + 1) * SGU_LEN)
            gate = _dot(w, vn[sl, cols]) + bias
            o_ref[sl, cols] = (_gelu_tanh(zu_ref[sl, cols]) * gate).astype(o_ref.dtype)


def _sgu(p, ln_g, ln_b, w_s, b_s, rows=512):
    t = p.shape[0]
    u_col = (2 * H_C * DK_C + 2 * H_C * DV_C) // D_D
    return pl.pallas_call(
        _sgu_kernel,
        grid=(t // rows,),
        in_specs=[pl.BlockSpec((rows, D_D), lambda i: (i, u_col)),
                  pl.BlockSpec((rows, D_D), lambda i: (i, u_col + 1)),
                  pl.BlockSpec((1, D_D), lambda i: (0, 0)),
                  pl.BlockSpec((1, D_D), lambda i: (0, 0)),
                  pl.BlockSpec((G_D, SGU_LEN, SGU_LEN), lambda i: (0, 0, 0)),
                  pl.BlockSpec((G_D, SGU_LEN, 1), lambda i: (0, 0, 0))],
        out_specs=pl.BlockSpec((rows, D_D), lambda i: (i, 0)),
        out_shape=jax.ShapeDtypeStruct((t, D_D), BF16),
        compiler_params=_params("parallel"),
        name="spatial_gate",
    )(p, p, ln_g.reshape(1, D_D), ln_b.reshape(1, D_D), w_s, b_s.reshape(G_D, SGU_LEN, 1))


def kernel(x, norm_mix_g, norm_ffn_g, final_norm_g, rel_bias, ab_w_in, ab_w_out, diff_lambda,
           diff_subln_g, cd_w_in, cd_w_out, ret_norm_g, sgu_ln_g, sgu_ln_b, sgu_w, sgu_b,
           ffn_w_up, ffn_conv_w, ffn_conv_b, ffn_w_down):
    batch, seq, d = x.shape
    t = batch * seq
    depth = norm_mix_g.shape[0]
    xt = x.reshape(t, d)
    def rows(w):
        return w.reshape(-1, w.shape[-1])

    ab_w_in = ab_w_in.astype(BF16)
    for layer in range(depth):
        j = layer // 2
        if layer % 2 == 0:
            lam_init = 0.8 - 0.6 * math.exp(-0.3 * layer)
            p = _norm_matmul(xt, norm_mix_g[layer], ab_w_in, j, BF16)
            p3 = p.reshape(batch, seq, p.shape[1])
            if layer == 0:
                o_a, (up_bf,) = _stick_breaking(p3, batch, seq, (rows(ffn_w_up),))
                o_b, (down_bf, abo_bf, cdi_bf, cdo_bf) = _diff_attention(
                    p3, rel_bias, diff_lambda[j], diff_subln_g[j], lam_init, batch, seq,
                    (rows(ffn_w_down), rows(ab_w_out), rows(cd_w_in), rows(cd_w_out)))
                ffn_w_up, ffn_w_down, ab_w_out, cd_w_in, cd_w_out = (
                    b.reshape(w.shape) for b, w in ((up_bf, ffn_w_up), (down_bf, ffn_w_down),
                                                    (abo_bf, ab_w_out), (cdi_bf, cd_w_in),
                                                    (cdo_bf, cd_w_out)))
            else:
                o_a, _ = _stick_breaking(p3, batch, seq)
                o_b, _ = _diff_attention(p3, rel_bias, diff_lambda[j], diff_subln_g[j], lam_init,
                                         batch, seq)
            xt = _out_proj(o_a.reshape(t, -1), o_b.reshape(t, -1), ab_w_out, j, xt)
        else:
            p = _norm_matmul(xt, norm_mix_g[layer], cd_w_in, j, F32)
            o_c = _retention(p.reshape(batch, seq, p.shape[1]), ret_norm_g[j], batch, seq)
            o_d = _sgu(p, sgu_ln_g[j], sgu_ln_b[j], sgu_w[j], sgu_b[j])
            xt = _out_proj(o_c.reshape(t, -1), o_d, cd_w_out, j, xt)
        xt = _ffn(xt, norm_ffn_g, ffn_w_up, ffn_conv_w, ffn_conv_b, ffn_w_down, final_norm_g,
                  layer, seq=seq, final_norm=(layer == depth - 1))
    return xt.reshape(batch, seq, d)
```

```python
import functools
import math

import numpy as np
import jax
import jax.numpy as jnp
from jax import lax
from jax.experimental import pallas as pl
from jax.experimental.pallas import tpu as pltpu

F32 = jnp.float32
BF16 = jnp.bfloat16

EPS = 1e-6
CHUNK = 64
H_A = 8
DH_A = 128
H_B = 4
DK_B = 128
DV_B = 2 * DK_B
NUM_BUCKETS = 32
MAX_DISTANCE = 128
H_C = 4
DK_C = 128
DV_C = 2 * DK_C
ROPE_BASE = 10000.0
D_D = 1024
G_D = 4
SGU_LEN = 128
CONV_WIDTH = 3

LANES = 128
BF16_SUBLANES = 16
ATT_BLOCK = 128
ATT_TILE = 512
RET_BLOCK = 256
VMEM_LIMIT = 62 * 1024 * 1024
PROJ_ROWS = 1024
PROJ_TILE = 1024
FFN_ROWS = 1024
FFN_TILE = 512
LOG2E = math.log2(math.e)
ZERO_WEIGHT_LOG2 = -150.0


def _params(*semantics):
    return pltpu.CompilerParams(dimension_semantics=semantics,
                                vmem_limit_bytes=VMEM_LIMIT)


def _dot(a, b):
    return jnp.dot(a, b, preferred_element_type=F32)


def _dot_nt(a, b):
    return lax.dot_general(a, b, (((1,), (1,)), ((), ())), preferred_element_type=F32)


def _rms_rows(x, g):
    return x * lax.rsqrt(jnp.mean(x * x, axis=-1, keepdims=True) + EPS) * g


def _with_riders(body, n_in, n_riders):
    def kern(*refs):
        ins = refs[:n_in]
        rider_in = refs[n_in:n_in + n_riders]
        out = refs[n_in + n_riders]
        rider_out = refs[n_in + n_riders + 1:n_in + 2 * n_riders + 1]
        scratch = refs[n_in + 2 * n_riders + 1:]
        for src, dst in zip(rider_in, rider_out):
            dst[...] = src[...].astype(BF16)
        body(*ins, out, *scratch)
    return kern


def _rider_specs(riders, grid):
    steps = math.prod(grid)

    def linear_step(*g):
        n = g[0]
        for size, idx in zip(grid[1:], g[1:len(grid)]):
            n = n * size + idx
        return n

    specs, shapes = [], []
    for w in riders:
        rows, cols = w.shape
        block_rows, rem = divmod(rows, steps)
        assert rem == 0 and block_rows % BF16_SUBLANES == 0, (w.shape, steps)
        specs.append(pl.BlockSpec((block_rows, cols), lambda *g: (linear_step(*g), 0)))
        shapes.append(jax.ShapeDtypeStruct((rows, cols), BF16))
    return specs, shapes


def _norm_matmul_kernel(x_ref, g_ref, w_ref, o_ref, h_ref):
    @pl.when(pl.program_id(1) == 0)
    def _():
        h_ref[...] = _rms_rows(x_ref[...], g_ref[...]).astype(BF16)

    o_ref[...] = _dot(h_ref[...], w_ref[...]).astype(o_ref.dtype)


def _norm_matmul(x, g, w, layer, out_dtype, tm=PROJ_ROWS, tn=PROJ_TILE):
    t, d = x.shape
    n = w.shape[2]
    return pl.pallas_call(
        _norm_matmul_kernel,
        grid=(t // tm, n // tn),
        in_specs=[pl.BlockSpec((tm, d), lambda i, j: (i, 0)),
                  pl.BlockSpec((1, d), lambda i, j: (0, 0)),
                  pl.BlockSpec((None, d, tn), lambda i, j: (layer, 0, j))],
        out_specs=pl.BlockSpec((tm, tn), lambda i, j: (i, j)),
        out_shape=jax.ShapeDtypeStruct((t, n), out_dtype),
        scratch_shapes=[pltpu.VMEM((tm, d), BF16)],
        compiler_params=_params("parallel", "arbitrary"),
        name="norm_matmul",
    )(x, g.reshape(1, d), w)


def _out_proj_kernel(a1_ref, a2_ref, w_ref, x_ref, o_ref):
    half = a1_ref.shape[1]
    acc = _dot(a1_ref[...], w_ref[0:half, :])
    acc += _dot(a2_ref[...], w_ref[half:2 * half, :])
    o_ref[...] = x_ref[...] + acc


def _out_proj(a1, a2, w, layer, x, tm=512):
    t, half = a1.shape
    d = w.shape[2]
    return pl.pallas_call(
        _out_proj_kernel,
        grid=(t // tm,),
        in_specs=[pl.BlockSpec((tm, half), lambda i: (i, 0)),
                  pl.BlockSpec((tm, half), lambda i: (i, 0)),
                  pl.BlockSpec((None, 2 * half, d), lambda i: (layer, 0, 0)),
                  pl.BlockSpec((tm, d), lambda i: (i, 0))],
        out_specs=pl.BlockSpec((tm, d), lambda i: (i, 0)),
        out_shape=jax.ShapeDtypeStruct((t, d), F32),
        compiler_params=_params("parallel"),
        name="out_proj",
    )(a1, a2, w, x)


def _ffn_kernel(x_ref, g_ref, wa_ref, wg_ref, cwa_ref, cwg_ref, cba_ref, cbg_ref,
                wd_ref, fg_ref, o_ref, h_ref, act0_ref, act1_ref, carry_ref,
                *, seq_tiles, final_norm, n_hidden_tiles):
    i = pl.program_id(0)
    j = pl.program_id(1)
    nj = pl.num_programs(1) - 1
    tm = x_ref.shape[0]

    def up_phase(out_ref):
        h = h_ref[...]

        def gate(cg, ca):
            return (cg * (1.0 / (1.0 + jnp.exp(-cg))) * ca).astype(BF16)

        def conv(w_ref, cw_ref, cb_ref, branch):
            up = _dot(h, w_ref[...])
            w0 = cw_ref[0:1, :]
            w1 = cw_ref[1:2, :]
            w2 = cw_ref[2:3, :]
            b = cb_ref[...]
            c = b + w0 * pltpu.roll(up, 2, 0) + w1 * pltpu.roll(up, 1, 0) + w2 * up
            both = jnp.concatenate([carry_ref[j, branch], up[0:8, :]], axis=0)
            top = (b + w0 * pltpu.roll(both, 2, 0)[8:16, :]
                   + w1 * pltpu.roll(both, 1, 0)[8:16, :] + w2 * both[8:16, :])
            carry_ref[j, branch] = up[tm - 8:tm, :]
            return c, top

        ca, ta = conv(wa_ref, cwa_ref, cba_ref, 0)
        cg, tg = conv(wg_ref, cwg_ref, cbg_ref, 1)
        out_ref[...] = gate(cg, ca)
        out_ref[0:8, :] = gate(tg, ta)

    def down_phase(in_ref):
        o_ref[...] += _dot(in_ref[...], wd_ref[...])

    @pl.when(((i % seq_tiles) == 0) & (j < nj))
    def _():
        carry_ref[j] = jnp.zeros(carry_ref.shape[1:], F32)

    @pl.when(j == 0)
    def _():
        x = x_ref[...]
        h_ref[...] = _rms_rows(x, g_ref[...]).astype(BF16)
        o_ref[...] = x
        up_phase(act0_ref)

    @pl.when((j > 0) & (j < nj) & (j % 2 == 1))
    def _():
        up_phase(act1_ref)
        down_phase(act0_ref)

    @pl.when((j > 0) & (j < nj) & (j % 2 == 0))
    def _():
        up_phase(act0_ref)
        down_phase(act1_ref)

    @pl.when(j == nj)
    def _():
        last = act0_ref if (n_hidden_tiles % 2 == 1) else act1_ref
        y = o_ref[...] + _dot(last[...], wd_ref[...])
        if final_norm:
            y = _rms_rows(y, fg_ref[...])
        o_ref[...] = y


def _ffn(x, g, w_up, conv_w, conv_b, w_down, final_g, layer, *, seq, final_norm,
         tm=FFN_ROWS, tf=FFN_TILE):
    t, d = x.shape
    f = w_down.shape[1]
    nj = f // tf
    kern = functools.partial(_ffn_kernel, seq_tiles=seq // tm, final_norm=final_norm,
                             n_hidden_tiles=nj)

    def up_col(j):
        return jnp.minimum(j, nj - 1)

    def down_row(j):
        return jnp.maximum(j - 1, 0)

    return pl.pallas_call(
        kern,
        grid=(t // tm, nj + 1),
        in_specs=[pl.BlockSpec((tm, d), lambda i, j: (i, 0)),
                  pl.BlockSpec((None, 1, d), lambda i, j: (layer, 0, 0)),
                  pl.BlockSpec((None, d, tf), lambda i, j: (layer, 0, up_col(j))),
                  pl.BlockSpec((None, d, tf), lambda i, j: (layer, 0, up_col(j) + nj)),
                  pl.BlockSpec((None, CONV_WIDTH, tf), lambda i, j: (layer, 0, up_col(j))),
                  pl.BlockSpec((None, CONV_WIDTH, tf), lambda i, j: (layer, 0, up_col(j) + nj)),
                  pl.BlockSpec((None, 1, tf), lambda i, j: (layer, 0, up_col(j))),
                  pl.BlockSpec((None, 1, tf), lambda i, j: (layer, 0, up_col(j) + nj)),
                  pl.BlockSpec((None, tf, d), lambda i, j: (layer, down_row(j), 0)),
                  pl.BlockSpec((1, d), lambda i, j: (0, 0))],
        out_specs=pl.BlockSpec((tm, d), lambda i, j: (i, 0)),
        out_shape=jax.ShapeDtypeStruct((t, d), F32),
        scratch_shapes=[pltpu.VMEM((tm, d), BF16),
                        pltpu.VMEM((tm, tf), BF16),
                        pltpu.VMEM((tm, tf), BF16),
                        pltpu.VMEM((nj, 2, 8, tf), F32)],
        compiler_params=_params("arbitrary", "arbitrary"),
        name="conv_ffn",
    )(x, g[:, None, :], w_up, w_up, conv_w, conv_w, conv_b[:, None, :], conv_b[:, None, :],
      w_down, final_g.reshape(1, d))


def _suffix_sum_matrix():
    blk = ATT_BLOCK
    j = np.arange(blk)[:, None]
    s = np.arange(blk)[None, :]
    m = np.concatenate([(j > s).astype(np.float32), np.ones((blk, blk), np.float32)], axis=1)
    return jnp.asarray(np.concatenate([m, m], axis=0), dtype=BF16)


def _stick_kernel(q_ref, k_ref, v_ref, m_ref, o_ref, acc_ref, carry_ref):
    blk = ATT_BLOCK
    tile = q_ref.shape[1]
    nsub = tile // blk
    i = pl.program_id(2)
    q = q_ref[0]
    scale2 = DH_A ** -0.5 * LOG2E
    suffix = m_ref[...]

    acc_ref[...] = jnp.zeros_like(acc_ref)
    carry_ref[...] = jnp.zeros_like(carry_ref)

    def step(j, diagonal):
        start = pl.multiple_of(j * tile, tile)
        k = k_ref[0, pl.ds(start, tile), :]
        v = v_ref[0, pl.ds(start, tile), :]
        z_all = _dot_nt(q, k) * scale2
        carry = carry_ref[...]
        ws = [None] * nsub
        for c in reversed(range(nsub)):
            z = z_all[:, c * blk:(c + 1) * blk]
            if diagonal:
                row = lax.broadcasted_iota(jnp.int32, (tile, blk), 0)
                col = lax.broadcasted_iota(jnp.int32, (tile, blk), 1) + c * blk
                z = jnp.where(col < row, z, -1e4)
            log_sig = jnp.minimum(z, 0.0) - jnp.log2(1.0 + jnp.exp2(-jnp.abs(z)))
            log_1m = log_sig - z
            hi = log_1m.astype(BF16)
            lo = (log_1m - hi.astype(F32)).astype(BF16)
            sums = _dot(jnp.concatenate([hi, lo], axis=1), suffix)
            ws[c] = jnp.exp2(log_sig + sums[:, :blk] + carry).astype(BF16)
            carry = carry + sums[:, blk:]
        carry_ref[...] = carry
        acc_ref[...] += _dot(jnp.concatenate(ws, axis=1), v)

    step(i, True)

    def more(t):
        return jnp.logical_and(t < i, jnp.max(carry_ref[...]) > ZERO_WEIGHT_LOG2)

    def body(t):
        step(i - 1 - t, False)
        return t + 1

    lax.while_loop(more, body, 0)
    o_ref[0] = acc_ref[...].astype(o_ref.dtype)


def _stick_breaking(p, batch, seq, riders=()):
    blk = ATT_BLOCK
    tile = ATT_TILE
    grid = (batch, H_A, seq // tile)
    rider_specs, rider_shapes = _rider_specs(riders, grid)
    out = pl.pallas_call(
        _with_riders(_stick_kernel, 4, len(riders)),
        grid=grid,
        in_specs=[pl.BlockSpec((1, tile, DH_A), lambda b, h, i: (b, i, h)),
                  pl.BlockSpec((1, seq, DH_A), lambda b, h, i: (b, 0, H_A + h)),
                  pl.BlockSpec((1, seq, DH_A), lambda b, h, i: (b, 0, 2 * H_A + h)),
                  pl.BlockSpec((2 * blk, 2 * blk), lambda b, h, i: (0, 0))] + rider_specs,
        out_specs=[pl.BlockSpec((1, tile, DH_A), lambda b, h, i: (b, i, h))] + rider_specs,
        out_shape=[jax.ShapeDtypeStruct((batch, seq, H_A * DH_A), BF16)] + rider_shapes,
        scratch_shapes=[pltpu.VMEM((tile, DH_A), F32),
                        pltpu.VMEM((tile, blk), F32)],
        compiler_params=_params("arbitrary", "arbitrary", "arbitrary"),
        name="stick_breaking",
    )(p, p, p, _suffix_sum_matrix(), *riders)
    return out[0], out[1:]


def _rel_bucket(rel):
    nb = NUM_BUCKETS // 2
    max_exact = nb // 2
    ret = jnp.where(rel > 0, nb, 0)
    n = jnp.abs(rel)
    n_f = jnp.maximum(n, 1).astype(F32)
    large = max_exact + (jnp.log(n_f / max_exact) / math.log(MAX_DISTANCE / max_exact)
                         * (nb - max_exact)).astype(jnp.int32)
    large = jnp.minimum(large, nb - 1)
    return ret + jnp.where(n < max_exact, n, large)


def _near_buckets():
    blk = ATT_BLOCK
    qpos = blk + jnp.arange(blk)
    kpos = jnp.arange(2 * blk)
    near = _rel_bucket(kpos[None, :] - qpos[:, None]).astype(jnp.int32)
    far = _rel_bucket(jnp.full((1,), -2 * blk, jnp.int32)).astype(jnp.int32)
    return near, far


def _diff_kernel(far_ref, relb_ref, q_ref, k_ref, v_ref, bucket_ref, lam_ref, g_ref, o_ref,
                 bias_ref, m_ref, l_ref, acc_ref, s0_ref, s1_ref, *, lam_init):
    blk = ATT_BLOCK
    tile = q_ref.shape[1]
    nsub = tile // blk
    h = pl.program_id(1)
    i = pl.program_id(2)
    scale2 = DK_B ** -0.5 * LOG2E

    @pl.when(i == 0)
    def _():
        bucket = bucket_ref[...]
        far = relb_ref[far_ref[0], h]
        near = jnp.zeros((blk, 2 * blk), F32)
        for b in range(NUM_BUCKETS):
            near = jnp.where(bucket == b, relb_ref[b, h], near)
        near = (near - far) * LOG2E
        qpos = blk + lax.broadcasted_iota(jnp.int32, (blk, 2 * blk), 0)
        kpos = lax.broadcasted_iota(jnp.int32, (blk, 2 * blk), 1)
        near = jnp.where((kpos // CHUNK) <= (qpos // CHUNK), near, -jnp.inf)
        bias_ref[...] = jnp.zeros_like(bias_ref)
        for a in range(nsub):
            rows = slice(a * blk, (a + 1) * blk)
            bias_ref[0, rows, a * blk:(a + 1) * blk] = near[:, blk:]
            if a >= 1:
                bias_ref[0, rows, (a - 1) * blk:a * blk] = near[:, :blk]
            if a + 1 < nsub:
                bias_ref[0, rows, (a + 1) * blk:] = jnp.full((blk, tile - (a + 1) * blk),
                                                             -jnp.inf, F32)
        bias_ref[1, 0:blk, (nsub - 1) * blk:] = near[:, :blk]

    m_ref[...] = jnp.full_like(m_ref, -jnp.inf)
    l_ref[...] = jnp.zeros_like(l_ref)
    acc_ref[...] = jnp.zeros_like(acc_ref)

    def key_tile(idx):
        return jnp.where(idx == 0, i, jnp.where(idx == 1, i - 1, idx - 2))

    def logits_into(s_ref, idx):
        j = key_tile(jnp.minimum(idx, i))
        kk = k_ref[0, pl.ds(pl.multiple_of(j * tile, tile), tile), :]
        for c in range(2):
            s_ref[c] = _dot_nt(q_ref[0, :, c * DK_B:(c + 1) * DK_B],
                               kk[:, c * DK_B:(c + 1) * DK_B])

    def consume(s_ref, idx):
        j = key_tile(idx)
        v = v_ref[0, pl.ds(pl.multiple_of(j * tile, tile), tile), :]
        bias = bias_ref[jnp.minimum(idx, 2)]
        probs = []
        for c in range(2):
            s = s_ref[c] * scale2 + bias
            subs = [s[:, u * blk:(u + 1) * blk] for u in range(nsub)]
            m_prev = m_ref[c]
            m_cur = functools.reduce(jnp.maximum, subs)
            m_new = jnp.maximum(m_prev, jnp.max(m_cur, axis=-1, keepdims=True))
            alpha = jnp.exp2(m_prev - m_new)
            ps = [jnp.exp2(u - m_new) for u in subs]
            l_ref[c] = alpha * l_ref[c] + functools.reduce(jnp.add, ps)
            m_ref[c] = m_new
            probs.append((alpha, jnp.concatenate([u.astype(BF16) for u in ps], axis=1)))
        for c in range(2):
            alpha, p = probs[c]
            acc_ref[c] = jnp.concatenate([alpha] * (DV_B // blk), axis=1) * acc_ref[c] + _dot(p, v)

    logits_into(s0_ref, 0)

    def pair(t, _):
        idx = 2 * t
        logits_into(s1_ref, idx + 1)
        consume(s0_ref, idx)
        logits_into(s0_ref, idx + 2)
        consume(s1_ref, idx + 1)
        return 0

    lax.fori_loop(0, (i + 1) // 2, pair, 0)

    @pl.when(i % 2 == 0)
    def _():
        consume(s0_ref, i)

    lv = lam_ref[...]
    lam = (jnp.exp(jnp.sum(lv[0:1] * lv[1:2], axis=-1, keepdims=True))
           - jnp.exp(jnp.sum(lv[2:3] * lv[3:4], axis=-1, keepdims=True)) + lam_init)
    l1 = jnp.sum(l_ref[0], axis=-1, keepdims=True)
    l2 = jnp.sum(l_ref[1], axis=-1, keepdims=True)
    o = acc_ref[0] / l1 - lam * (acc_ref[1] / l2)
    o_ref[0] = (_rms_rows(o, g_ref[...]) * (1.0 - lam_init)).astype(o_ref.dtype)


def _diff_attention(p, rel_bias, lam_vecs, subln_g, lam_init, batch, seq, riders=()):
    blk = ATT_BLOCK
    tile = ATT_TILE
    q_col = 3 * H_A * DH_A // DV_B
    k_col = q_col + H_B
    v_col = k_col + H_B
    near, far = _near_buckets()
    grid = (batch, H_B, seq // tile)
    rider_specs, rider_shapes = _rider_specs(riders, grid)
    kern = _with_riders(functools.partial(_diff_kernel, lam_init=lam_init), 8, len(riders))
    grid_spec = pltpu.PrefetchScalarGridSpec(
        num_scalar_prefetch=2,
        grid=grid,
        in_specs=[pl.BlockSpec((1, tile, DV_B), lambda b, h, i, *_: (b, i, q_col + h)),
                  pl.BlockSpec((1, seq, DV_B), lambda b, h, i, *_: (b, 0, k_col + h)),
                  pl.BlockSpec((1, seq, DV_B), lambda b, h, i, *_: (b, 0, v_col + h)),
                  pl.BlockSpec((blk, 2 * blk), lambda b, h, i, *_: (0, 0)),
                  pl.BlockSpec((4, DK_B), lambda b, h, i, *_: (0, 0)),
                  pl.BlockSpec((1, DV_B), lambda b, h, i, *_: (0, 0))] + rider_specs,
        out_specs=[pl.BlockSpec((1, tile, DV_B), lambda b, h, i, *_: (b, i, h))] + rider_specs,
        scratch_shapes=[pltpu.VMEM((3, tile, tile), F32),
                        pltpu.VMEM((2, tile, blk), F32),
                        pltpu.VMEM((2, tile, blk), F32),
                        pltpu.VMEM((2, tile, DV_B), F32),
                        pltpu.VMEM((2, tile, tile), F32),
                        pltpu.VMEM((2, tile, tile), F32)])
    out = pl.pallas_call(
        kern,
        grid_spec=grid_spec,
        out_shape=[jax.ShapeDtypeStruct((batch, seq, H_B * DV_B), BF16)] + rider_shapes,
        compiler_params=_params("arbitrary", "arbitrary", "arbitrary"),
        name="diff_attention",
    )(far, rel_bias.astype(F32), p, p, p, near, lam_vecs.astype(F32), subln_g.reshape(1, DV_B),
      *riders)
    return out[0], out[1:]


def _rotary_tables(seq):
    inv_freq = ROPE_BASE ** (-jnp.arange(0, DK_C, 2, dtype=F32) / DK_C)
    ang = jnp.arange(seq, dtype=F32)[:, None] * inv_freq[None, :]
    cos = jnp.concatenate([jnp.cos(ang), jnp.cos(ang)], axis=-1)
    sin = jnp.concatenate([-jnp.sin(ang), jnp.sin(ang)], axis=-1)
    return cos, sin


def _decay_tables():
    log_g = jnp.log(1.0 - 2.0 ** (-5.0 - jnp.arange(H_C, dtype=F32)))
    idx = jnp.arange(RET_BLOCK, dtype=F32)
    chunk = jnp.arange(RET_BLOCK) // CHUNK
    visible = chunk[None, :] <= chunk[:, None]
    intra = jnp.where(visible[None],
                      jnp.exp(log_g[:, None, None] * jnp.abs(idx[:, None] - idx[None, :])), 0.0)
    q_decay = jnp.exp(log_g[:, None] * (idx + 1.0))
    k_decay = jnp.exp(log_g[:, None] * (RET_BLOCK - 1.0 - idx))
    block_decay = jnp.exp(log_g * RET_BLOCK)
    return (intra,
            jnp.broadcast_to(q_decay[:, :, None], (H_C, RET_BLOCK, DV_C)),
            jnp.broadcast_to(k_decay[:, :, None], (H_C, RET_BLOCK, DK_C)),
            jnp.broadcast_to(block_decay[:, None, None], (H_C, 1, DV_C)))


def _retention_kernel(q_ref, k_ref, v_ref, gate_ref, cos_ref, sin_ref, intra_ref, qd_ref,
                      kd_ref, bd_ref, g_ref, o_ref, state_ref):
    rows = q_ref.shape[1]

    @pl.when(pl.program_id(1) == 0)
    def _():
        state_ref[...] = jnp.zeros_like(state_ref)

    cos = cos_ref[...]
    sin = sin_ref[...]

    def rotary(t):
        return t * cos + pltpu.roll(t, DK_C // 2, 1) * sin

    for h in range(H_C):
        qk_cols = slice(h * DK_C, (h + 1) * DK_C)
        v_cols = slice(h * DV_C, (h + 1) * DV_C)
        q = (rotary(q_ref[0, :, qk_cols]) * (DK_C ** -0.5)).astype(BF16)
        k = rotary(k_ref[0, :, qk_cols])
        state = state_ref[h]
        for r in range(0, rows, RET_BLOCK):
            sl = slice(r, r + RET_BLOCK)
            qc = q[sl]
            kc = k[sl]
            vc = v_ref[0, sl, v_cols].astype(BF16)
            scores = _dot_nt(qc, kc.astype(BF16)) * intra_ref[h]
            y = _dot(scores.astype(BF16), vc) + _dot(qc, state.astype(BF16)) * qd_ref[h]
            state = bd_ref[h] * state + _dot((kc * kd_ref[h]).T.astype(BF16), vc)
            gate = gate_ref[0, sl, v_cols]
            y = _rms_rows(y, g_ref[...])
            o_ref[0, sl, v_cols] = (gate * (1.0 / (1.0 + jnp.exp(-gate))) * y).astype(o_ref.dtype)
        state_ref[h] = state


def _retention(p, ret_norm_g, batch, seq, rows=512):
    qk_width = H_C * DK_C
    v_width = H_C * DV_C
    cos, sin = _rotary_tables(seq)
    intra, q_decay, k_decay, block_decay = _decay_tables()

    def whole(a):
        return pl.BlockSpec(a.shape, lambda b, i: (0,) * a.ndim)

    return pl.pallas_call(
        _retention_kernel,
        grid=(batch, seq // rows),
        in_specs=[pl.BlockSpec((1, rows, qk_width), lambda b, i: (b, i, 0)),
                  pl.BlockSpec((1, rows, qk_width), lambda b, i: (b, i, 1)),
                  pl.BlockSpec((1, rows, v_width), lambda b, i: (b, i, 2 * qk_width // v_width)),
                  pl.BlockSpec((1, rows, v_width), lambda b, i: (b, i, 2 * qk_width // v_width + 1)),
                  pl.BlockSpec((rows, DK_C), lambda b, i: (i, 0)),
                  pl.BlockSpec((rows, DK_C), lambda b, i: (i, 0)),
                  whole(intra), whole(q_decay), whole(k_decay), whole(block_decay),
                  pl.BlockSpec((1, DV_C), lambda b, i: (0, 0))],
        out_specs=pl.BlockSpec((1, rows, v_width), lambda b, i: (b, i, 0)),
        out_shape=jax.ShapeDtypeStruct((batch, seq, v_width), BF16),
        scratch_shapes=[pltpu.VMEM((H_C, DK_C, DV_C), F32)],
        compiler_params=_params("arbitrary", "arbitrary"),
        name="retention",
    )(p, p, p, p, cos, sin, intra, q_decay, k_decay, block_decay, ret_norm_g.reshape(1, DV_C))


def _gelu_tanh(x):
    return 0.5 * x * (1.0 + jnp.tanh(math.sqrt(2.0 / math.pi) * (x + 0.044715 * (x * x * x))))


def _sgu_kernel(zu_ref, zv_ref, lg_ref, lb_ref, w_ref, b_ref, o_ref):
    rows = zu_ref.shape[0]
    width = D_D // G_D
    v = _gelu_tanh(zv_ref[...])
    mu = jnp.mean(v, axis=-1, keepdims=True)
    var = jnp.mean(jnp.square(v - mu), axis=-1, keepdims=True)
    vn = ((v - mu) * lax.rsqrt(var + EPS) * lg_ref[...] + lb_ref[...]).astype(BF16)
    pos_i = lax.broadcasted_iota(jnp.int32, (SGU_LEN, SGU_LEN), 0)
    pos_j = lax.broadcasted_iota(jnp.int32, (SGU_LEN, SGU_LEN), 1)
    mask = (pos_j // CHUNK) <= (pos_i // CHUNK)
    for g in range(G_D):
        w = jnp.where(mask, w_ref[g], 0.0).astype(BF16)
        bias = b_ref[g]
        cols = slice(g * width, (g + 1) * width)
        for r in range(rows // SGU_LEN):
            sl = slice(r * SGU_LEN, (r + 1) * SGU_LEN)
            gate = _dot(w, vn[sl, cols]) + bias
            o_ref[sl, cols] = (_gelu_tanh(zu_ref[sl, cols]) * gate).astype(o_ref.dtype)


def _sgu(p, ln_g, ln_b, w_s, b_s, rows=512):
    t = p.shape[0]
    u_col = (2 * H_C * DK_C + 2 * H_C * DV_C) // D_D
    return pl.pallas_call(
        _sgu_kernel,
        grid=(t // rows,),
        in_specs=[pl.BlockSpec((rows, D_D), lambda i: (i, u_col)),
                  pl.BlockSpec((rows, D_D), lambda i: (i, u_col + 1)),
                  pl.BlockSpec((1, D_D), lambda i: (0, 0)),
                  pl.BlockSpec((1, D_D), lambda i: (0, 0)),
                  pl.BlockSpec((G_D, SGU_LEN, SGU_LEN), lambda i: (0, 0, 0)),
                  pl.BlockSpec((G_D, SGU_LEN, 1), lambda i: (0, 0, 0))],
        out_specs=pl.BlockSpec((rows, D_D), lambda i: (i, 0)),
        out_shape=jax.ShapeDtypeStruct((t, D_D), BF16),
        compiler_params=_params("parallel"),
        name="spatial_gate",
    )(p, p, ln_g.reshape(1, D_D), ln_b.reshape(1, D_D), w_s, b_s.reshape(G_D, SGU_LEN, 1))


def kernel(x, norm_mix_g, norm_ffn_g, final_norm_g, rel_bias, ab_w_in, ab_w_out, diff_lambda,
           diff_subln_g, cd_w_in, cd_w_out, ret_norm_g, sgu_ln_g, sgu_ln_b, sgu_w, sgu_b,
           ffn_w_up, ffn_conv_w, ffn_conv_b, ffn_w_down):
    batch, seq, d = x.shape
    t = batch * seq
    depth = norm_mix_g.shape[0]
    xt = x.reshape(t, d)
    def rows(w):
        return w.reshape(-1, w.shape[-1])

    ab_w_in = ab_w_in.astype(BF16)
    for layer in range(depth):
        j = layer // 2
        if layer % 2 == 0:
            lam_init = 0.8 - 0.6 * math.exp(-0.3 * layer)
            p = _norm_matmul(xt, norm_mix_g[layer], ab_w_in, j, BF16)
            p3 = p.reshape(batch, seq, p.shape[1])
            if layer == 0:
                o_a, (up_bf,) = _stick_breaking(p3, batch, seq, (rows(ffn_w_up),))
                o_b, (down_bf, abo_bf, cdi_bf, cdo_bf) = _diff_attention(
                    p3, rel_bias, diff_lambda[j], diff_subln_g[j], lam_init, batch, seq,
                    (rows(ffn_w_down), rows(ab_w_out), rows(cd_w_in), rows(cd_w_out)))
                ffn_w_up, ffn_w_down, ab_w_out, cd_w_in, cd_w_out = (
                    b.reshape(w.shape) for b, w in ((up_bf, ffn_w_up), (down_bf, ffn_w_down),
                                                    (abo_bf, ab_w_out), (cdi_bf, cd_w_in),
                                                    (cdo_bf, cd_w_out)))
            else:
                o_a, _ = _stick_breaking(p3, batch, seq)
                o_b, _ = _diff_attention(p3, rel_bias, diff_lambda[j], diff_subln_g[j], lam_init,
                                         batch, seq)
            xt = _out_proj(o_a.reshape(t, -1), o_b.reshape(t, -1), ab_w_out, j, xt)
        else:
            p = _norm_matmul(xt, norm_mix_g[layer], cd_w_in, j, F32)
            o_c = _retention(p.reshape(batch, seq, p.shape[1]), ret_norm_g[j], batch, seq)
            o_d = _sgu(p, sgu_ln_g[j], sgu_ln_b[j], sgu_w[j], sgu_b[j])
            xt = _out_proj(o_c.reshape(t, -1), o_d, cd_w_out, j, xt)
        xt = _ffn(xt, norm_ffn_g, ffn_w_up, ffn_conv_w, ffn_conv_b, ffn_w_down, final_norm_g,
                  layer, seq=seq, final_norm=(layer == depth - 1))
    return xt.reshape(batch, seq, d)
```

```python
import functools
import math

import numpy as np
import jax
import jax.numpy as jnp
from jax import lax
from jax.experimental import pallas as pl
from jax.experimental.pallas import tpu as pltpu

F32 = jnp.float32
BF16 = jnp.bfloat16

EPS = 1e-6
CHUNK = 64
H_A = 8
DH_A = 128
H_B = 4
DK_B = 128
DV_B = 2 * DK_B
NUM_BUCKETS = 32
MAX_DISTANCE = 128
H_C = 4
DK_C = 128
DV_C = 2 * DK_C
ROPE_BASE = 10000.0
D_D = 1024
G_D = 4
SGU_LEN = 128
CONV_WIDTH = 3

LANES = 128
BF16_SUBLANES = 16
ATT_BLOCK = 128
ATT_TILE = 512
RET_BLOCK = 256
VMEM_LIMIT = 62 * 1024 * 1024
PROJ_ROWS = 1024
PROJ_TILE = 1024
FFN_ROWS = 1024
FFN_TILE = 512
LOG2E = math.log2(math.e)
ZERO_WEIGHT_LOG2 = -150.0


def _params(*semantics):
    return pltpu.CompilerParams(dimension_semantics=semantics,
                                vmem_limit_bytes=VMEM_LIMIT)


def _dot(a, b):
    return jnp.dot(a, b, preferred_element_type=F32)


def _dot_nt(a, b):
    return lax.dot_general(a, b, (((1,), (1,)), ((), ())), preferred_element_type=F32)


def _rms_rows(x, g):
    return x * lax.rsqrt(jnp.mean(x * x, axis=-1, keepdims=True) + EPS) * g


def _with_riders(body, n_in, n_riders):
    def kern(*refs):
        ins = refs[:n_in]
        rider_in = refs[n_in:n_in + n_riders]
        out = refs[n_in + n_riders]
        rider_out = refs[n_in + n_riders + 1:n_in + 2 * n_riders + 1]
        scratch = refs[n_in + 2 * n_riders + 1:]
        for src, dst in zip(rider_in, rider_out):
            dst[...] = src[...].astype(BF16)
        body(*ins, out, *scratch)
    return kern


def _rider_specs(riders, grid):
    steps = math.prod(grid)

    def linear_step(*g):
        n = g[0]
        for size, idx in zip(grid[1:], g[1:len(grid)]):
            n = n * size + idx
        return n

    specs, shapes = [], []
    for w in riders:
        rows, cols = w.shape
        block_rows, rem = divmod(rows, steps)
        assert rem == 0 and block_rows % BF16_SUBLANES == 0, (w.shape, steps)
        specs.append(pl.BlockSpec((block_rows, cols), lambda *g: (linear_step(*g), 0)))
        shapes.append(jax.ShapeDtypeStruct((rows, cols), BF16))
    return specs, shapes


def _norm_matmul_kernel(x_ref, g_ref, w_ref, o_ref, h_ref):
    @pl.when(pl.program_id(1) == 0)
    def _():
        h_ref[...] = _rms_rows(x_ref[...], g_ref[...]).astype(BF16)

    o_ref[...] = _dot(h_ref[...], w_ref[...]).astype(o_ref.dtype)


def _norm_matmul(x, g, w, layer, out_dtype, tm=PROJ_ROWS, tn=PROJ_TILE):
    t, d = x.shape
    n = w.shape[2]
    return pl.pallas_call(
        _norm_matmul_kernel,
        grid=(t // tm, n // tn),
        in_specs=[pl.BlockSpec((tm, d), lambda i, j: (i, 0)),
                  pl.BlockSpec((1, d), lambda i, j: (0, 0)),
                  pl.BlockSpec((None, d, tn), lambda i, j: (layer, 0, j))],
        out_specs=pl.BlockSpec((tm, tn), lambda i, j: (i, j)),
        out_shape=jax.ShapeDtypeStruct((t, n), out_dtype),
        scratch_shapes=[pltpu.VMEM((tm, d), BF16)],
        compiler_params=_params("parallel", "arbitrary"),
        name="norm_matmul",
    )(x, g.reshape(1, d), w)


def _out_proj_kernel(a1_ref, a2_ref, w_ref, x_ref, o_ref):
    half = a1_ref.shape[1]
    acc = _dot(a1_ref[...], w_ref[0:half, :])
    acc += _dot(a2_ref[...], w_ref[half:2 * half, :])
    o_ref[...] = x_ref[...] + acc


def _out_proj(a1, a2, w, layer, x, tm=512):
    t, half = a1.shape
    d = w.shape[2]
    return pl.pallas_call(
        _out_proj_kernel,
        grid=(t // tm,),
        in_specs=[pl.BlockSpec((tm, half), lambda i: (i, 0)),
                  pl.BlockSpec((tm, half), lambda i: (i, 0)),
                  pl.BlockSpec((None, 2 * half, d), lambda i: (layer, 0, 0)),
                  pl.BlockSpec((tm, d), lambda i: (i, 0))],
        out_specs=pl.BlockSpec((tm, d), lambda i: (i, 0)),
        out_shape=jax.ShapeDtypeStruct((t, d), F32),
        compiler_params=_params("parallel"),
        name="out_proj",
    )(a1, a2, w, x)


def _ffn_kernel(x_ref, g_ref, wa_ref, wg_ref, cwa_ref, cwg_ref, cba_ref, cbg_ref,
                wd_ref, wd_last_ref, fg_ref, o_ref, h_ref, act0_ref, act1_ref, carry_ref,
                *, seq_tiles, final_norm, n_hidden_tiles):
    i = pl.program_id(0)
    j = pl.program_id(1)
    nj = n_hidden_tiles
    tm = x_ref.shape[0]
    acts = (act0_ref, act1_ref)

    def up_phase(out_ref):
        h = h_ref[...]

        def gate(cg, ca):
            return (cg * (1.0 / (1.0 + jnp.exp(-cg))) * ca).astype(BF16)

        def conv(w_ref, cw_ref, cb_ref, branch):
            up = _dot(h, w_ref[...])
            w0 = cw_ref[0:1, :]
            w1 = cw_ref[1:2, :]
            w2 = cw_ref[2:3, :]
            b = cb_ref[...]
            c = b + w0 * pltpu.roll(up, 2, 0) + w1 * pltpu.roll(up, 1, 0) + w2 * up
            both = jnp.concatenate([carry_ref[j, branch], up[0:8, :]], axis=0)
            top = (b + w0 * pltpu.roll(both, 2, 0)[8:16, :]
                   + w1 * pltpu.roll(both, 1, 0)[8:16, :] + w2 * both[8:16, :])
            carry_ref[j, branch] = up[tm - 8:tm, :]
            return c, top

        ca, ta = conv(wa_ref, cwa_ref, cba_ref, 0)
        cg, tg = conv(wg_ref, cwg_ref, cbg_ref, 1)
        out_ref[...] = gate(cg, ca)
        out_ref[0:8, :] = gate(tg, ta)

    def down_phase(in_ref):
        o_ref[...] += _dot(in_ref[...], wd_ref[...])

    @pl.when((i % seq_tiles) == 0)
    def _():
        carry_ref[j] = jnp.zeros(carry_ref.shape[1:], F32)

    @pl.when(j == 0)
    def _():
        x = x_ref[...]
        h_ref[...] = _rms_rows(x, g_ref[...]).astype(BF16)
        o_ref[...] = x
        up_phase(act0_ref)

    for parity in range(2):
        @pl.when((j > 0) & (j < nj - 1) & (j % 2 == parity))
        def _():
            up_phase(acts[parity])
            down_phase(acts[1 - parity])

    @pl.when(j == nj - 1)
    def _():
        parity = (n_hidden_tiles - 1) % 2
        up_phase(acts[parity])
        down_phase(acts[1 - parity])
        y = o_ref[...] + _dot(acts[parity][...], wd_last_ref[...])
        if final_norm:
            y = _rms_rows(y, fg_ref[...])
        o_ref[...] = y


def _ffn(x, g, w_up, conv_w, conv_b, w_down, final_g, layer, *, seq, final_norm,
         tm=FFN_ROWS, tf=FFN_TILE):
    t, d = x.shape
    f = w_down.shape[1]
    nj = f // tf
    assert nj >= 2
    kern = functools.partial(_ffn_kernel, seq_tiles=seq // tm, final_norm=final_norm,
                             n_hidden_tiles=nj)

    def down_row(j):
        return jnp.maximum(j - 1, 0)

    return pl.pallas_call(
        kern,
        grid=(t // tm, nj),
        in_specs=[pl.BlockSpec((tm, d), lambda i, j: (i, 0)),
                  pl.BlockSpec((None, 1, d), lambda i, j: (layer, 0, 0)),
                  pl.BlockSpec((None, d, tf), lambda i, j: (layer, 0, j)),
                  pl.BlockSpec((None, d, tf), lambda i, j: (layer, 0, j + nj)),
                  pl.BlockSpec((None, CONV_WIDTH, tf), lambda i, j: (layer, 0, j)),
                  pl.BlockSpec((None, CONV_WIDTH, tf), lambda i, j: (layer, 0, j + nj)),
                  pl.BlockSpec((None, 1, tf), lambda i, j: (layer, 0, j)),
                  pl.BlockSpec((None, 1, tf), lambda i, j: (layer, 0, j + nj)),
                  pl.BlockSpec((None, tf, d), lambda i, j: (layer, down_row(j), 0)),
                  pl.BlockSpec((None, tf, d), lambda i, j: (layer, nj - 1, 0)),
                  pl.BlockSpec((1, d), lambda i, j: (0, 0))],
        out_specs=pl.BlockSpec((tm, d), lambda i, j: (i, 0)),
        out_shape=jax.ShapeDtypeStruct((t, d), F32),
        scratch_shapes=[pltpu.VMEM((tm, d), BF16),
                        pltpu.VMEM((tm, tf), BF16),
                        pltpu.VMEM((tm, tf), BF16),
                        pltpu.VMEM((nj, 2, 8, tf), F32)],
        compiler_params=_params("arbitrary", "arbitrary"),
        name="conv_ffn",
    )(x, g[:, None, :], w_up, w_up, conv_w, conv_w, conv_b[:, None, :], conv_b[:, None, :],
      w_down, w_down, final_g.reshape(1, d))


def _suffix_sum_matrix():
    blk = ATT_BLOCK
    j = np.arange(blk)[:, None]
    s = np.arange(blk)[None, :]
    m = np.concatenate([(j > s).astype(np.float32), np.ones((blk, blk), np.float32)], axis=1)
    return jnp.asarray(np.concatenate([m, m], axis=0), dtype=BF16)


def _stick_kernel(q_ref, k_ref, v_ref, m_ref, o_ref, acc_ref, carry_ref):
    blk = ATT_BLOCK
    tile = q_ref.shape[1]
    nsub = tile // blk
    i = pl.program_id(2)
    q = q_ref[0]
    scale2 = DH_A ** -0.5 * LOG2E
    suffix = m_ref[...]

    acc_ref[...] = jnp.zeros_like(acc_ref)
    carry_ref[...] = jnp.zeros_like(carry_ref)

    def step(j, diagonal):
        start = pl.multiple_of(j * tile, tile)
        k = k_ref[0, pl.ds(start, tile), :]
        v = v_ref[0, pl.ds(start, tile), :]
        z_all = _dot_nt(q, k) * scale2
        carry = carry_ref[...]
        ws = [None] * nsub
        for c in reversed(range(nsub)):
            z = z_all[:, c * blk:(c + 1) * blk]
            if diagonal:
                row = lax.broadcasted_iota(jnp.int32, (tile, blk), 0)
                col = lax.broadcasted_iota(jnp.int32, (tile, blk), 1) + c * blk
                z = jnp.where(col < row, z, -1e4)
            log_sig = jnp.minimum(z, 0.0) - jnp.log2(1.0 + jnp.exp2(-jnp.abs(z)))
            log_1m = log_sig - z
            hi = log_1m.astype(BF16)
            lo = (log_1m - hi.astype(F32)).astype(BF16)
            sums = _dot(jnp.concatenate([hi, lo], axis=1), suffix)
            ws[c] = jnp.exp2(log_sig + sums[:, :blk] + carry).astype(BF16)
            carry = carry + sums[:, blk:]
        carry_ref[...] = carry
        acc_ref[...] += _dot(jnp.concatenate(ws, axis=1), v)

    step(i, True)

    def more(t):
        return jnp.logical_and(t < i, jnp.max(carry_ref[...]) > ZERO_WEIGHT_LOG2)

    def body(t):
        step(i - 1 - t, False)
        return t + 1

    lax.while_loop(more, body, 0)
    o_ref[0] = acc_ref[...].astype(o_ref.dtype)


def _stick_breaking(p, batch, seq, riders=()):
    blk = ATT_BLOCK
    tile = ATT_TILE
    grid = (batch, H_A, seq // tile)
    rider_specs, rider_shapes = _rider_specs(riders, grid)
    out = pl.pallas_call(
        _with_riders(_stick_kernel, 4, len(riders)),
        grid=grid,
        in_specs=[pl.BlockSpec((1, tile, DH_A), lambda b, h, i: (b, i, h)),
                  pl.BlockSpec((1, seq, DH_A), lambda b, h, i: (b, 0, H_A + h)),
                  pl.BlockSpec((1, seq, DH_A), lambda b, h, i: (b, 0, 2 * H_A + h)),
                  pl.BlockSpec((2 * blk, 2 * blk), lambda b, h, i: (0, 0))] + rider_specs,
        out_specs=[pl.BlockSpec((1, tile, DH_A), lambda b, h, i: (b, i, h))] + rider_specs,
        out_shape=[jax.ShapeDtypeStruct((batch, seq, H_A * DH_A), BF16)] + rider_shapes,
        scratch_shapes=[pltpu.VMEM((tile, DH_A), F32),
                        pltpu.VMEM((tile, blk), F32)],
        compiler_params=_params("arbitrary", "arbitrary", "arbitrary"),
        name="stick_breaking",
    )(p, p, p, _suffix_sum_matrix(), *riders)
    return out[0], out[1:]


def _rel_bucket(rel):
    nb = NUM_BUCKETS // 2
    max_exact = nb // 2
    ret = jnp.where(rel > 0, nb, 0)
    n = jnp.abs(rel)
    n_f = jnp.maximum(n, 1).astype(F32)
    large = max_exact + (jnp.log(n_f / max_exact) / math.log(MAX_DISTANCE / max_exact)
                         * (nb - max_exact)).astype(jnp.int32)
    large = jnp.minimum(large, nb - 1)
    return ret + jnp.where(n < max_exact, n, large)


def _near_buckets():
    blk = ATT_BLOCK
    qpos = blk + jnp.arange(blk)
    kpos = jnp.arange(2 * blk)
    near = _rel_bucket(kpos[None, :] - qpos[:, None]).astype(jnp.int32)
    far = _rel_bucket(jnp.full((1,), -2 * blk, jnp.int32)).astype(jnp.int32)
    return near, far


def _diff_kernel(far_ref, relb_ref, q_ref, k_ref, v_ref, bucket_ref, lam_ref, g_ref, o_ref,
                 bias_ref, m_ref, l_ref, acc_ref, s0_ref, s1_ref, *, lam_init):
    blk = ATT_BLOCK
    tile = q_ref.shape[1]
    nsub = tile // blk
    h = pl.program_id(1)
    i = pl.program_id(2)
    scale2 = DK_B ** -0.5 * LOG2E

    @pl.when(i == 0)
    def _():
        bucket = bucket_ref[...]
        far = relb_ref[far_ref[0], h]
        near = jnp.zeros((blk, 2 * blk), F32)
        for b in range(NUM_BUCKETS):
            near = jnp.where(bucket == b, relb_ref[b, h], near)
        near = (near - far) * LOG2E
        qpos = blk + lax.broadcasted_iota(jnp.int32, (blk, 2 * blk), 0)
        kpos = lax.broadcasted_iota(jnp.int32, (blk, 2 * blk), 1)
        near = jnp.where((kpos // CHUNK) <= (qpos // CHUNK), near, -jnp.inf)
        bias_ref[...] = jnp.zeros_like(bias_ref)
        for a in range(nsub):
            rows = slice(a * blk, (a + 1) * blk)
            bias_ref[0, rows, a * blk:(a + 1) * blk] = near[:, blk:]
            if a >= 1:
                bias_ref[0, rows, (a - 1) * blk:a * blk] = near[:, :blk]
            if a + 1 < nsub:
                bias_ref[0, rows, (a + 1) * blk:] = jnp.full((blk, tile - (a + 1) * blk),
                                                             -jnp.inf, F32)
        bias_ref[1, 0:blk, (nsub - 1) * blk:] = near[:, :blk]

    m_ref[...] = jnp.full_like(m_ref, -jnp.inf)
    l_ref[...] = jnp.zeros_like(l_ref)
    acc_ref[...] = jnp.zeros_like(acc_ref)

    def key_tile(idx):
        return jnp.where(idx == 0, i, jnp.where(idx == 1, i - 1, idx - 2))

    def logits_into(s_ref, idx):
        j = key_tile(jnp.minimum(idx, i))
        kk = k_ref[0, pl.ds(pl.multiple_of(j * tile, tile), tile), :]
        for c in range(2):
            s_ref[c] = _dot_nt(q_ref[0, :, c * DK_B:(c + 1) * DK_B],
                               kk[:, c * DK_B:(c + 1) * DK_B])

    def consume(s_ref, idx):
        j = key_tile(idx)
        v = v_ref[0, pl.ds(pl.multiple_of(j * tile, tile), tile), :]
        bias = bias_ref[jnp.minimum(idx, 2)]
        probs = []
        for c in range(2):
            s = s_ref[c] * scale2 + bias
            subs = [s[:, u * blk:(u + 1) * blk] for u in range(nsub)]
            m_prev = m_ref[c]
            m_cur = functools.reduce(jnp.maximum, subs)
            m_new = jnp.maximum(m_prev, jnp.max(m_cur, axis=-1, keepdims=True))
            alpha = jnp.exp2(m_prev - m_new)
            ps = [jnp.exp2(u - m_new) for u in subs]
            l_ref[c] = alpha * l_ref[c] + functools.reduce(jnp.add, ps)
            m_ref[c] = m_new
            probs.append((alpha, jnp.concatenate([u.astype(BF16) for u in ps], axis=1)))
        for c in range(2):
            alpha, p = probs[c]
            acc_ref[c] = jnp.concatenate([alpha] * (DV_B // blk), axis=1) * acc_ref[c] + _dot(p, v)

    logits_into(s0_ref, 0)

    def pair(t, _):
        idx = 2 * t
        logits_into(s1_ref, idx + 1)
        consume(s0_ref, idx)
        logits_into(s0_ref, idx + 2)
        consume(s1_ref, idx + 1)
        return 0

    lax.fori_loop(0, (i + 1) // 2, pair, 0)

    @pl.when(i % 2 == 0)
    def _():
        consume(s0_ref, i)

    lv = lam_ref[...]
    lam = (jnp.exp(jnp.sum(lv[0:1] * lv[1:2], axis=-1, keepdims=True))
           - jnp.exp(jnp.sum(lv[2:3] * lv[3:4], axis=-1, keepdims=True)) + lam_init)
    l1 = jnp.sum(l_ref[0], axis=-1, keepdims=True)
    l2 = jnp.sum(l_ref[1], axis=-1, keepdims=True)
    o = acc_ref[0] / l1 - lam * (acc_ref[1] / l2)
    o_ref[0] = (_rms_rows(o, g_ref[...]) * (1.0 - lam_init)).astype(o_ref.dtype)


def _diff_attention(p, rel_bias, lam_vecs, subln_g, lam_init, batch, seq, riders=()):
    blk = ATT_BLOCK
    tile = ATT_TILE
    q_col = 3 * H_A * DH_A // DV_B
    k_col = q_col + H_B
    v_col = k_col + H_B
    near, far = _near_buckets()
    grid = (batch, H_B, seq // tile)
    rider_specs, rider_shapes = _rider_specs(riders, grid)
    kern = _with_riders(functools.partial(_diff_kernel, lam_init=lam_init), 8, len(riders))
    grid_spec = pltpu.PrefetchScalarGridSpec(
        num_scalar_prefetch=2,
        grid=grid,
        in_specs=[pl.BlockSpec((1, tile, DV_B), lambda b, h, i, *_: (b, i, q_col + h)),
                  pl.BlockSpec((1, seq, DV_B), lambda b, h, i, *_: (b, 0, k_col + h)),
                  pl.BlockSpec((1, seq, DV_B), lambda b, h, i, *_: (b, 0, v_col + h)),
                  pl.BlockSpec((blk, 2 * blk), lambda b, h, i, *_: (0, 0)),
                  pl.BlockSpec((4, DK_B), lambda b, h, i, *_: (0, 0)),
                  pl.BlockSpec((1, DV_B), lambda b, h, i, *_: (0, 0))] + rider_specs,
        out_specs=[pl.BlockSpec((1, tile, DV_B), lambda b, h, i, *_: (b, i, h))] + rider_specs,
        scratch_shapes=[pltpu.VMEM((3, tile, tile), F32),
                        pltpu.VMEM((2, tile, blk), F32),
                        pltpu.VMEM((2, tile, blk), F32),
                        pltpu.VMEM((2, tile, DV_B), F32),
                        pltpu.VMEM((2, tile, tile), F32),
                        pltpu.VMEM((2, tile, tile), F32)])
    out = pl.pallas_call(
        kern,
        grid_spec=grid_spec,
        out_shape=[jax.ShapeDtypeStruct((batch, seq, H_B * DV_B), BF16)] + rider_shapes,
        compiler_params=_params("arbitrary", "arbitrary", "arbitrary"),
        name="diff_attention",
    )(far, rel_bias.astype(F32), p, p, p, near, lam_vecs.astype(F32), subln_g.reshape(1, DV_B),
      *riders)
    return out[0], out[1:]


def _rotary_tables(seq):
    inv_freq = ROPE_BASE ** (-jnp.arange(0, DK_C, 2, dtype=F32) / DK_C)
    ang = jnp.arange(seq, dtype=F32)[:, None] * inv_freq[None, :]
    cos = jnp.concatenate([jnp.cos(ang), jnp.cos(ang)], axis=-1)
    sin = jnp.concatenate([-jnp.sin(ang), jnp.sin(ang)], axis=-1)
    return cos, sin


def _decay_tables():
    log_g = jnp.log(1.0 - 2.0 ** (-5.0 - jnp.arange(H_C, dtype=F32)))
    idx = jnp.arange(RET_BLOCK, dtype=F32)
    chunk = jnp.arange(RET_BLOCK) // CHUNK
    visible = chunk[None, :] <= chunk[:, None]
    intra = jnp.where(visible[None],
                      jnp.exp(log_g[:, None, None] * jnp.abs(idx[:, None] - idx[None, :])), 0.0)
    q_decay = jnp.exp(log_g[:, None] * (idx + 1.0))
    k_decay = jnp.exp(log_g[:, None] * (RET_BLOCK - 1.0 - idx))
    block_decay = jnp.exp(log_g * RET_BLOCK)
    return (intra,
            jnp.broadcast_to(q_decay[:, :, None], (H_C, RET_BLOCK, DV_C)),
            jnp.broadcast_to(k_decay[:, :, None], (H_C, RET_BLOCK, DK_C)),
            jnp.broadcast_to(block_decay[:, None, None], (H_C, 1, DV_C)))


def _retention_kernel(q_ref, k_ref, v_ref, gate_ref, cos_ref, sin_ref, intra_ref, qd_ref,
                      kd_ref, bd_ref, g_ref, o_ref, state_ref):
    rows = q_ref.shape[1]

    @pl.when(pl.program_id(1) == 0)
    def _():
        state_ref[...] = jnp.zeros_like(state_ref)

    cos = cos_ref[...]
    sin = sin_ref[...]

    def rotary(t):
        return t * cos + pltpu.roll(t, DK_C // 2, 1) * sin

    for h in range(H_C):
        qk_cols = slice(h * DK_C, (h + 1) * DK_C)
        v_cols = slice(h * DV_C, (h + 1) * DV_C)
        q = (rotary(q_ref[0, :, qk_cols]) * (DK_C ** -0.5)).astype(BF16)
        k = rotary(k_ref[0, :, qk_cols])
        state = state_ref[h]
        for r in range(0, rows, RET_BLOCK):
            sl = slice(r, r + RET_BLOCK)
            qc = q[sl]
            kc = k[sl]
            vc = v_ref[0, sl, v_cols].astype(BF16)
            scores = _dot_nt(qc, kc.astype(BF16)) * intra_ref[h]
            y = _dot(scores.astype(BF16), vc) + _dot(qc, state.astype(BF16)) * qd_ref[h]
            state = bd_ref[h] * state + _dot((kc * kd_ref[h]).T.astype(BF16), vc)
            gate = gate_ref[0, sl, v_cols]
            y = _rms_rows(y, g_ref[...])
            o_ref[0, sl, v_cols] = (gate * (1.0 / (1.0 + jnp.exp(-gate))) * y).astype(o_ref.dtype)
        state_ref[h] = state


def _retention(p, ret_norm_g, batch, seq, rows=512):
    qk_width = H_C * DK_C
    v_width = H_C * DV_C
    cos, sin = _rotary_tables(seq)
    intra, q_decay, k_decay, block_decay = _decay_tables()

    def whole(a):
        return pl.BlockSpec(a.shape, lambda b, i: (0,) * a.ndim)

    return pl.pallas_call(
        _retention_kernel,
        grid=(batch, seq // rows),
        in_specs=[pl.BlockSpec((1, rows, qk_width), lambda b, i: (b, i, 0)),
                  pl.BlockSpec((1, rows, qk_width), lambda b, i: (b, i, 1)),
                  pl.BlockSpec((1, rows, v_width), lambda b, i: (b, i, 2 * qk_width // v_width)),
                  pl.BlockSpec((1, rows, v_width), lambda b, i: (b, i, 2 * qk_width // v_width + 1)),
                  pl.BlockSpec((rows, DK_C), lambda b, i: (i, 0)),
                  pl.BlockSpec((rows, DK_C), lambda b, i: (i, 0)),
                  whole(intra), whole(q_decay), whole(k_decay), whole(block_decay),
                  pl.BlockSpec((1, DV_C), lambda b, i: (0, 0))],
        out_specs=pl.BlockSpec((1, rows, v_width), lambda b, i: (b, i, 0)),
        out_shape=jax.ShapeDtypeStruct((batch, seq, v_width), BF16),
        scratch_shapes=[pltpu.VMEM((H_C, DK_C, DV_C), F32)],
        compiler_params=_params("arbitrary", "arbitrary"),
        name="retention",
    )(p, p, p, p, cos, sin, intra, q_decay, k_decay, block_decay, ret_norm_g.reshape(1, DV_C))


def _gelu_tanh(x):
    return 0.5 * x * (1.0 + jnp.tanh(math.sqrt(2.0 / math.pi) * (x + 0.044715 * (x * x * x))))


def _sgu_kernel(zu_ref, zv_ref, lg_ref, lb_ref, w_ref, b_ref, o_ref):
    rows = zu_ref.shape[0]
    width = D_D // G_D
    v = _gelu_tanh(zv_ref[...])
    mu = jnp.mean(v, axis=-1, keepdims=True)
    var = jnp.mean(jnp.square(v - mu), axis=-1, keepdims=True)
    vn = ((v - mu) * lax.rsqrt(var + EPS) * lg_ref[...] + lb_ref[...]).astype(BF16)
    pos_i = lax.broadcasted_iota(jnp.int32, (SGU_LEN, SGU_LEN), 0)
    pos_j = lax.broadcasted_iota(jnp.int32, (SGU_LEN, SGU_LEN), 1)
    mask = (pos_j // CHUNK) <= (pos_i // CHUNK)
    for g in range(G_D):
        w = jnp.where(mask, w_ref[g], 0.0).astype(BF16)
        bias = b_ref[g]
        cols = slice(g * width, (g + 1) * width)
        for r in range(rows // SGU_LEN):
            sl = slice(r * SGU_LEN, (r + 1) * SGU_LEN)
            gate = _dot(w, vn[sl, cols]) + bias
            o_ref[sl, cols] = (_gelu_tanh(zu_ref[sl, cols]) * gate).astype(o_ref.dtype)


def _sgu(p, ln_g, ln_b, w_s, b_s, rows=512):
    t = p.shape[0]
    u_col = (2 * H_C * DK_C + 2 * H_C * DV_C) // D_D
    return pl.pallas_call(
        _sgu_kernel,
        grid=(t // rows,),
        in_specs=[pl.BlockSpec((rows, D_D), lambda i: (i, u_col)),
                  pl.BlockSpec((rows, D_D), lambda i: (i, u_col + 1)),
                  pl.BlockSpec((1, D_D), lambda i: (0, 0)),
                  pl.BlockSpec((1, D_D), lambda i: (0, 0)),
                  pl.BlockSpec((G_D, SGU_LEN, SGU_LEN), lambda i: (0, 0, 0)),
                  pl.BlockSpec((G_D, SGU_LEN, 1), lambda i: (0, 0, 0))],
        out_specs=pl.BlockSpec((rows, D_D), lambda i: (i, 0)),
        out_shape=jax.ShapeDtypeStruct((t, D_D), BF16),
        compiler_params=_params("parallel"),
        name="spatial_gate",
    )(p, p, ln_g.reshape(1, D_D), ln_b.reshape(1, D_D), w_s, b_s.reshape(G_D, SGU_LEN, 1))


def kernel(x, norm_mix_g, norm_ffn_g, final_norm_g, rel_bias, ab_w_in, ab_w_out, diff_lambda,
           diff_subln_g, cd_w_in, cd_w_out, ret_norm_g, sgu_ln_g, sgu_ln_b, sgu_w, sgu_b,
           ffn_w_up, ffn_conv_w, ffn_conv_b, ffn_w_down):
    batch, seq, d = x.shape
    t = batch * seq
    depth = norm_mix_g.shape[0]
    xt = x.reshape(t, d)
    def rows(w):
        return w.reshape(-1, w.shape[-1])

    ab_w_in = ab_w_in.astype(BF16)
    for layer in range(depth):
        j = layer // 2
        if layer % 2 == 0:
            lam_init = 0.8 - 0.6 * math.exp(-0.3 * layer)
            p = _norm_matmul(xt, norm_mix_g[layer], ab_w_in, j, BF16)
            p3 = p.reshape(batch, seq, p.shape[1])
            if layer == 0:
                o_a, (up_bf,) = _stick_breaking(p3, batch, seq, (rows(ffn_w_up),))
                o_b, (down_bf, abo_bf, cdi_bf, cdo_bf) = _diff_attention(
                    p3, rel_bias, diff_lambda[j], diff_subln_g[j], lam_init, batch, seq,
                    (rows(ffn_w_down), rows(ab_w_out), rows(cd_w_in), rows(cd_w_out)))
                ffn_w_up, ffn_w_down, ab_w_out, cd_w_in, cd_w_out = (
                    b.reshape(w.shape) for b, w in ((up_bf, ffn_w_up), (down_bf, ffn_w_down),
                                                    (abo_bf, ab_w_out), (cdi_bf, cd_w_in),
                                                    (cdo_bf, cd_w_out)))
            else:
                o_a, _ = _stick_breaking(p3, batch, seq)
                o_b, _ = _diff_attention(p3, rel_bias, diff_lambda[j], diff_subln_g[j], lam_init,
                                         batch, seq)
            xt = _out_proj(o_a.reshape(t, -1), o_b.reshape(t, -1), ab_w_out, j, xt)
        else:
            p = _norm_matmul(xt, norm_mix_g[layer], cd_w_in, j, F32)
            o_c = _retention(p.reshape(batch, seq, p.shape[1]), ret_norm_g[j], batch, seq)
            o_d = _sgu(p, sgu_ln_g[j], sgu_ln_b[j], sgu_w[j], sgu_b[j])
            xt = _out_proj(o_c.reshape(t, -1), o_d, cd_w_out, j, xt)
        xt = _ffn(xt, norm_ffn_g, ffn_w_up, ffn_conv_w, ffn_conv_b, ffn_w_down, final_norm_g,
                  layer, seq=seq, final_norm=(layer == depth - 1))
    return xt.reshape(batch, seq, d)
```

```python
import functools
import math

import numpy as np
import jax
import jax.numpy as jnp
from jax import lax
from jax.experimental import pallas as pl
from jax.experimental.pallas import tpu as pltpu

F32 = jnp.float32
BF16 = jnp.bfloat16

EPS = 1e-6
CHUNK = 64
H_A = 8
DH_A = 128
H_B = 4
DK_B = 128
DV_B = 2 * DK_B
NUM_BUCKETS = 32
MAX_DISTANCE = 128
H_C = 4
DK_C = 128
DV_C = 2 * DK_C
ROPE_BASE = 10000.0
D_D = 1024
G_D = 4
SGU_LEN = 128
CONV_WIDTH = 3

LANES = 128
BF16_SUBLANES = 16
ATT_BLOCK = 128
ATT_TILE = 512
RET_BLOCK = 256
VMEM_LIMIT = 62 * 1024 * 1024
PROJ_ROWS = 1024
PROJ_TILE = 1024
FFN_ROWS = 1024
FFN_TILE = 512
LOG2E = math.log2(math.e)
ZERO_WEIGHT_LOG2 = -150.0


def _params(*semantics):
    return pltpu.CompilerParams(dimension_semantics=semantics,
                                vmem_limit_bytes=VMEM_LIMIT)


def _dot(a, b):
    return jnp.dot(a, b, preferred_element_type=F32)


def _dot_nt(a, b):
    return lax.dot_general(a, b, (((1,), (1,)), ((), ())), preferred_element_type=F32)


def _rms_rows(x, g):
    return x * lax.rsqrt(jnp.mean(x * x, axis=-1, keepdims=True) + EPS) * g


def _with_riders(body, n_in, n_riders):
    def kern(*refs):
        ins = refs[:n_in]
        rider_in = refs[n_in:n_in + n_riders]
        out = refs[n_in + n_riders]
        rider_out = refs[n_in + n_riders + 1:n_in + 2 * n_riders + 1]
        scratch = refs[n_in + 2 * n_riders + 1:]
        for src, dst in zip(rider_in, rider_out):
            dst[...] = src[...].astype(BF16)
        body(*ins, out, *scratch)
    return kern


def _rider_specs(riders, grid):
    steps = math.prod(grid)

    def linear_step(*g):
        n = g[0]
        for size, idx in zip(grid[1:], g[1:len(grid)]):
            n = n * size + idx
        return n

    specs, shapes = [], []
    for w in riders:
        rows, cols = w.shape
        block_rows, rem = divmod(rows, steps)
        assert rem == 0 and block_rows % BF16_SUBLANES == 0, (w.shape, steps)
        specs.append(pl.BlockSpec((block_rows, cols), lambda *g: (linear_step(*g), 0)))
        shapes.append(jax.ShapeDtypeStruct((rows, cols), BF16))
    return specs, shapes


def _norm_matmul_kernel(x_ref, g_ref, w_ref, o_ref, h_ref):
    @pl.when(pl.program_id(1) == 0)
    def _():
        h_ref[...] = _rms_rows(x_ref[...], g_ref[...]).astype(BF16)

    o_ref[...] = _dot(h_ref[...], w_ref[...]).astype(o_ref.dtype)


def _norm_matmul(x, g, w, layer, out_dtype, tm=PROJ_ROWS, tn=PROJ_TILE):
    t, d = x.shape
    n = w.shape[2]
    return pl.pallas_call(
        _norm_matmul_kernel,
        grid=(t // tm, n // tn),
        in_specs=[pl.BlockSpec((tm, d), lambda i, j: (i, 0)),
                  pl.BlockSpec((1, d), lambda i, j: (0, 0)),
                  pl.BlockSpec((None, d, tn), lambda i, j: (layer, 0, j))],
        out_specs=pl.BlockSpec((tm, tn), lambda i, j: (i, j)),
        out_shape=jax.ShapeDtypeStruct((t, n), out_dtype),
        scratch_shapes=[pltpu.VMEM((tm, d), BF16)],
        compiler_params=_params("parallel", "arbitrary"),
        name="norm_matmul",
    )(x, g.reshape(1, d), w)


def _out_proj_kernel(a1_ref, a2_ref, w_ref, x_ref, o_ref):
    half = a1_ref.shape[1]
    acc = _dot(a1_ref[...], w_ref[0:half, :])
    acc += _dot(a2_ref[...], w_ref[half:2 * half, :])
    o_ref[...] = x_ref[...] + acc


def _out_proj(a1, a2, w, layer, x, tm=512):
    t, half = a1.shape
    d = w.shape[2]
    return pl.pallas_call(
        _out_proj_kernel,
        grid=(t // tm,),
        in_specs=[pl.BlockSpec((tm, half), lambda i: (i, 0)),
                  pl.BlockSpec((tm, half), lambda i: (i, 0)),
                  pl.BlockSpec((None, 2 * half, d), lambda i: (layer, 0, 0)),
                  pl.BlockSpec((tm, d), lambda i: (i, 0))],
        out_specs=pl.BlockSpec((tm, d), lambda i: (i, 0)),
        out_shape=jax.ShapeDtypeStruct((t, d), F32),
        compiler_params=_params("parallel"),
        name="out_proj",
    )(a1, a2, w, x)


def _ffn_kernel(x_ref, g_ref, wa_ref, wg_ref, cwa_ref, cwg_ref, cba_ref, cbg_ref,
                wd_ref, wd_last_ref, fg_ref, o_ref, h_ref, act0_ref, act1_ref, carry_ref,
                *, seq_tiles, final_norm, n_hidden_tiles):
    i = pl.program_id(0)
    j = pl.program_id(1)
    nj = n_hidden_tiles
    tm = x_ref.shape[0]
    acts = (act0_ref, act1_ref)

    def up_phase(out_ref):
        h = h_ref[...]

        def gate(cg, ca):
            return (cg * (1.0 / (1.0 + jnp.exp(-cg))) * ca).astype(BF16)

        def conv(w_ref, cw_ref, cb_ref, branch):
            up = _dot(h, w_ref[...])
            w0 = cw_ref[0:1, :]
            w1 = cw_ref[1:2, :]
            w2 = cw_ref[2:3, :]
            b = cb_ref[...]
            c = b + w0 * pltpu.roll(up, 2, 0) + w1 * pltpu.roll(up, 1, 0) + w2 * up
            both = jnp.concatenate([carry_ref[j, branch], up[0:8, :]], axis=0)
            top = (b + w0 * pltpu.roll(both, 2, 0)[8:16, :]
                   + w1 * pltpu.roll(both, 1, 0)[8:16, :] + w2 * both[8:16, :])
            carry_ref[j, branch] = up[tm - 8:tm, :]
            return c, top

        ca, ta = conv(wa_ref, cwa_ref, cba_ref, 0)
        cg, tg = conv(wg_ref, cwg_ref, cbg_ref, 1)
        out_ref[...] = gate(cg, ca)
        out_ref[0:8, :] = gate(tg, ta)

    def down_phase(in_ref):
        o_ref[...] += _dot(in_ref[...], wd_ref[...])

    @pl.when((i % seq_tiles) == 0)
    def _():
        carry_ref[j] = jnp.zeros(carry_ref.shape[1:], F32)

    @pl.when(j == 0)
    def _():
        x = x_ref[...]
        h_ref[...] = _rms_rows(x, g_ref[...]).astype(BF16)
        o_ref[...] = x
        up_phase(act0_ref)

    for parity in range(2):
        @pl.when((j > 0) & (j < nj - 1) & (j % 2 == parity))
        def _():
            up_phase(acts[parity])
            down_phase(acts[1 - parity])

    @pl.when(j == nj - 1)
    def _():
        parity = (n_hidden_tiles - 1) % 2
        up_phase(acts[parity])
        down_phase(acts[1 - parity])
        y = o_ref[...] + _dot(acts[parity][...], wd_last_ref[...])
        if final_norm:
            y = _rms_rows(y, fg_ref[...])
        o_ref[...] = y


def _ffn(x, g, w_up, conv_w, conv_b, w_down, final_g, layer, *, seq, final_norm,
         tm=FFN_ROWS, tf=FFN_TILE):
    t, d = x.shape
    f = w_down.shape[1]
    nj = f // tf
    assert nj >= 2
    kern = functools.partial(_ffn_kernel, seq_tiles=seq // tm, final_norm=final_norm,
                             n_hidden_tiles=nj)

    def down_row(j):
        return jnp.maximum(j - 1, 0)

    return pl.pallas_call(
        kern,
        grid=(t // tm, nj),
        in_specs=[pl.BlockSpec((tm, d), lambda i, j: (i, 0)),
                  pl.BlockSpec((None, 1, d), lambda i, j: (layer, 0, 0)),
                  pl.BlockSpec((None, d, tf), lambda i, j: (layer, 0, j)),
                  pl.BlockSpec((None, d, tf), lambda i, j: (layer, 0, j + nj)),
                  pl.BlockSpec((None, CONV_WIDTH, tf), lambda i, j: (layer, 0, j)),
                  pl.BlockSpec((None, CONV_WIDTH, tf), lambda i, j: (layer, 0, j + nj)),
                  pl.BlockSpec((None, 1, tf), lambda i, j: (layer, 0, j)),
                  pl.BlockSpec((None, 1, tf), lambda i, j: (layer, 0, j + nj)),
                  pl.BlockSpec((None, tf, d), lambda i, j: (layer, down_row(j), 0)),
                  pl.BlockSpec((None, tf, d), lambda i, j: (layer, nj - 1, 0)),
                  pl.BlockSpec((1, d), lambda i, j: (0, 0))],
        out_specs=pl.BlockSpec((tm, d), lambda i, j: (i, 0)),
        out_shape=jax.ShapeDtypeStruct((t, d), F32),
        scratch_shapes=[pltpu.VMEM((tm, d), BF16),
                        pltpu.VMEM((tm, tf), BF16),
                        pltpu.VMEM((tm, tf), BF16),
                        pltpu.VMEM((nj, 2, 8, tf), F32)],
        compiler_params=_params("arbitrary", "arbitrary"),
        name="conv_ffn",
    )(x, g[:, None, :], w_up, w_up, conv_w, conv_w, conv_b[:, None, :], conv_b[:, None, :],
      w_down, w_down, final_g.reshape(1, d))


def _suffix_sum_matrix():
    blk = ATT_BLOCK
    j = np.arange(blk)[:, None]
    s = np.arange(blk)[None, :]
    m = np.concatenate([(j > s).astype(np.float32), np.ones((blk, blk), np.float32)], axis=1)
    return jnp.asarray(np.concatenate([m, m], axis=0), dtype=BF16)


def _stick_kernel(q_ref, k_ref, v_ref, m_ref, o_ref, acc_ref, carry_ref):
    blk = ATT_BLOCK
    tile = q_ref.shape[1]
    half = tile // 2
    i = pl.program_id(2)
    scale2 = DH_A ** -0.5 * LOG2E
    suffix = m_ref[...]

    acc_ref[...] = jnp.zeros_like(acc_ref)
    carry_ref[...] = jnp.zeros_like(carry_ref)

    def sweep(r0, r1, key_start, n_sub, diagonal=False):
        rows = slice(r0, r1)
        k = k_ref[0, pl.ds(key_start, n_sub * blk), :]
        v = v_ref[0, pl.ds(key_start, n_sub * blk), :]
        z_all = _dot_nt(q_ref[0, rows, :], k) * scale2
        carry = carry_ref[rows, :]
        ws = [None] * n_sub
        for c in reversed(range(n_sub)):
            z = z_all[:, c * blk:(c + 1) * blk]
            if diagonal:
                row = lax.broadcasted_iota(jnp.int32, z.shape, 0) + r0
                col = lax.broadcasted_iota(jnp.int32, z.shape, 1) + c * blk
                z = jnp.where(col < row, z, -1e4)
            log_sig = jnp.minimum(z, 0.0) - jnp.log2(1.0 + jnp.exp2(-jnp.abs(z)))
            log_1m = log_sig - z
            hi = log_1m.astype(BF16)
            lo = (log_1m - hi.astype(F32)).astype(BF16)
            sums = _dot(jnp.concatenate([hi, lo], axis=1), suffix)
            ws[c] = jnp.exp2(log_sig + sums[:, :blk] + carry).astype(BF16)
            carry = carry + sums[:, blk:]
        carry_ref[rows, :] = carry
        acc_ref[rows, :] += _dot(jnp.concatenate(ws, axis=1), v)

    def alive(r0, r1):
        return jnp.max(carry_ref[r0:r1, :]) > ZERO_WEIGHT_LOG2

    tile_start = pl.multiple_of(i * tile, tile)
    n_sub = tile // blk
    sweep(0, tile, tile_start, n_sub, diagonal=True)

    @pl.when(i > 0)
    def _():
        prev = pl.multiple_of(tile_start - tile, tile)
        sweep(0, half, prev + half, n_sub // 2)

        @pl.when(alive(half, tile))
        def _():
            sweep(half, tile, prev + half, n_sub // 2)

        @pl.when(alive(0, tile))
        def _():
            sweep(0, tile, prev, n_sub // 2)

    def more(t):
        return jnp.logical_and(t < i, alive(0, tile))

    def body(t):
        sweep(0, tile, pl.multiple_of(tile_start - (t + 1) * tile, tile), n_sub)
        return t + 1

    lax.while_loop(more, body, 1)
    o_ref[0] = acc_ref[...].astype(o_ref.dtype)


def _stick_breaking(p, batch, seq, riders=()):
    blk = ATT_BLOCK
    tile = ATT_TILE
    grid = (batch, H_A, seq // tile)
    rider_specs, rider_shapes = _rider_specs(riders, grid)
    out = pl.pallas_call(
        _with_riders(_stick_kernel, 4, len(riders)),
        grid=grid,
        in_specs=[pl.BlockSpec((1, tile, DH_A), lambda b, h, i: (b, i, h)),
                  pl.BlockSpec((1, seq, DH_A), lambda b, h, i: (b, 0, H_A + h)),
                  pl.BlockSpec((1, seq, DH_A), lambda b, h, i: (b, 0, 2 * H_A + h)),
                  pl.BlockSpec((2 * blk, 2 * blk), lambda b, h, i: (0, 0))] + rider_specs,
        out_specs=[pl.BlockSpec((1, tile, DH_A), lambda b, h, i: (b, i, h))] + rider_specs,
        out_shape=[jax.ShapeDtypeStruct((batch, seq, H_A * DH_A), BF16)] + rider_shapes,
        scratch_shapes=[pltpu.VMEM((tile, DH_A), F32),
                        pltpu.VMEM((tile, blk), F32)],
        compiler_params=_params("arbitrary", "arbitrary", "arbitrary"),
        name="stick_breaking",
    )(p, p, p, _suffix_sum_matrix(), *riders)
    return out[0], out[1:]


def _rel_bucket(rel):
    nb = NUM_BUCKETS // 2
    max_exact = nb // 2
    ret = jnp.where(rel > 0, nb, 0)
    n = jnp.abs(rel)
    n_f = jnp.maximum(n, 1).astype(F32)
    large = max_exact + (jnp.log(n_f / max_exact) / math.log(MAX_DISTANCE / max_exact)
                         * (nb - max_exact)).astype(jnp.int32)
    large = jnp.minimum(large, nb - 1)
    return ret + jnp.where(n < max_exact, n, large)


def _near_buckets():
    blk = ATT_BLOCK
    qpos = blk + jnp.arange(blk)
    kpos = jnp.arange(2 * blk)
    near = _rel_bucket(kpos[None, :] - qpos[:, None]).astype(jnp.int32)
    far = _rel_bucket(jnp.full((1,), -2 * blk, jnp.int32)).astype(jnp.int32)
    return near, far


def _diff_kernel(far_ref, relb_ref, q_ref, k_ref, v_ref, bucket_ref, lam_ref, g_ref, o_ref,
                 bias_ref, m_ref, l_ref, acc_ref, s0_ref, s1_ref, *, lam_init):
    blk = ATT_BLOCK
    tile = q_ref.shape[1]
    nsub = tile // blk
    h = pl.program_id(1)
    i = pl.program_id(2)
    scale2 = DK_B ** -0.5 * LOG2E

    @pl.when(i == 0)
    def _():
        bucket = bucket_ref[...]
        far = relb_ref[far_ref[0], h]
        near = jnp.zeros((blk, 2 * blk), F32)
        for b in range(NUM_BUCKETS):
            near = jnp.where(bucket == b, relb_ref[b, h], near)
        near = (near - far) * LOG2E
        qpos = blk + lax.broadcasted_iota(jnp.int32, (blk, 2 * blk), 0)
        kpos = lax.broadcasted_iota(jnp.int32, (blk, 2 * blk), 1)
        near = jnp.where((kpos // CHUNK) <= (qpos // CHUNK), near, -jnp.inf)
        bias_ref[...] = jnp.zeros_like(bias_ref)
        for a in range(nsub):
            rows = slice(a * blk, (a + 1) * blk)
            bias_ref[0, rows, a * blk:(a + 1) * blk] = near[:, blk:]
            if a >= 1:
                bias_ref[0, rows, (a - 1) * blk:a * blk] = near[:, :blk]
            if a + 1 < nsub:
                bias_ref[0, rows, (a + 1) * blk:] = jnp.full((blk, tile - (a + 1) * blk),
                                                             -jnp.inf, F32)
        bias_ref[1, 0:blk, (nsub - 1) * blk:] = near[:, :blk]

    m_ref[...] = jnp.full_like(m_ref, -jnp.inf)
    l_ref[...] = jnp.zeros_like(l_ref)
    acc_ref[...] = jnp.zeros_like(acc_ref)

    def key_tile(idx):
        return jnp.where(idx == 0, i, jnp.where(idx == 1, i - 1, idx - 2))

    def logits_into(s_ref, idx):
        j = key_tile(jnp.minimum(idx, i))
        kk = k_ref[0, pl.ds(pl.multiple_of(j * tile, tile), tile), :]
        for c in range(2):
            s_ref[c] = _dot_nt(q_ref[0, :, c * DK_B:(c + 1) * DK_B],
                               kk[:, c * DK_B:(c + 1) * DK_B])

    def consume(s_ref, idx):
        j = key_tile(idx)
        v = v_ref[0, pl.ds(pl.multiple_of(j * tile, tile), tile), :]
        bias = bias_ref[jnp.minimum(idx, 2)]
        probs = []
        for c in range(2):
            s = s_ref[c] * scale2 + bias
            subs = [s[:, u * blk:(u + 1) * blk] for u in range(nsub)]
            m_prev = m_ref[c]
            m_cur = functools.reduce(jnp.maximum, subs)
            m_new = jnp.maximum(m_prev, jnp.max(m_cur, axis=-1, keepdims=True))
            alpha = jnp.exp2(m_prev - m_new)
            ps = [jnp.exp2(u - m_new) for u in subs]
            l_ref[c] = alpha * l_ref[c] + functools.reduce(jnp.add, ps)
            m_ref[c] = m_new
            probs.append((alpha, jnp.concatenate([u.astype(BF16) for u in ps], axis=1)))
        for c in range(2):
            alpha, p = probs[c]
            acc_ref[c] = jnp.concatenate([alpha] * (DV_B // blk), axis=1) * acc_ref[c] + _dot(p, v)

    logits_into(s0_ref, 0)

    def pair(t, _):
        idx = 2 * t
        logits_into(s1_ref, idx + 1)
        consume(s0_ref, idx)
        logits_into(s0_ref, idx + 2)
        consume(s1_ref, idx + 1)
        return 0

    lax.fori_loop(0, (i + 1) // 2, pair, 0)

    @pl.when(i % 2 == 0)
    def _():
        consume(s0_ref, i)

    lv = lam_ref[...]
    lam = (jnp.exp(jnp.sum(lv[0:1] * lv[1:2], axis=-1, keepdims=True))
           - jnp.exp(jnp.sum(lv[2:3] * lv[3:4], axis=-1, keepdims=True)) + lam_init)
    l1 = jnp.sum(l_ref[0], axis=-1, keepdims=True)
    l2 = jnp.sum(l_ref[1], axis=-1, keepdims=True)
    o = acc_ref[0] / l1 - lam * (acc_ref[1] / l2)
    o_ref[0] = (_rms_rows(o, g_ref[...]) * (1.0 - lam_init)).astype(o_ref.dtype)


def _diff_attention(p, rel_bias, lam_vecs, subln_g, lam_init, batch, seq, riders=()):
    blk = ATT_BLOCK
    tile = ATT_TILE
    q_col = 3 * H_A * DH_A // DV_B
    k_col = q_col + H_B
    v_col = k_col + H_B
    near, far = _near_buckets()
    grid = (batch, H_B, seq // tile)
    rider_specs, rider_shapes = _rider_specs(riders, grid)
    kern = _with_riders(functools.partial(_diff_kernel, lam_init=lam_init), 8, len(riders))
    grid_spec = pltpu.PrefetchScalarGridSpec(
        num_scalar_prefetch=2,
        grid=grid,
        in_specs=[pl.BlockSpec((1, tile, DV_B), lambda b, h, i, *_: (b, i, q_col + h)),
                  pl.BlockSpec((1, seq, DV_B), lambda b, h, i, *_: (b, 0, k_col + h)),
                  pl.BlockSpec((1, seq, DV_B), lambda b, h, i, *_: (b, 0, v_col + h)),
                  pl.BlockSpec((blk, 2 * blk), lambda b, h, i, *_: (0, 0)),
                  pl.BlockSpec((4, DK_B), lambda b, h, i, *_: (0, 0)),
                  pl.BlockSpec((1, DV_B), lambda b, h, i, *_: (0, 0))] + rider_specs,
        out_specs=[pl.BlockSpec((1, tile, DV_B), lambda b, h, i, *_: (b, i, h))] + rider_specs,
        scratch_shapes=[pltpu.VMEM((3, tile, tile), F32),
                        pltpu.VMEM((2, tile, blk), F32),
                        pltpu.VMEM((2, tile, blk), F32),
                        pltpu.VMEM((2, tile, DV_B), F32),
                        pltpu.VMEM((2, tile, tile), F32),
                        pltpu.VMEM((2, tile, tile), F32)])
    out = pl.pallas_call(
        kern,
        grid_spec=grid_spec,
        out_shape=[jax.ShapeDtypeStruct((batch, seq, H_B * DV_B), BF16)] + rider_shapes,
        compiler_params=_params("arbitrary", "arbitrary", "arbitrary"),
        name="diff_attention",
    )(far, rel_bias.astype(F32), p, p, p, near, lam_vecs.astype(F32), subln_g.reshape(1, DV_B),
      *riders)
    return out[0], out[1:]


def _rotary_tables(seq):
    inv_freq = ROPE_BASE ** (-jnp.arange(0, DK_C, 2, dtype=F32) / DK_C)
    ang = jnp.arange(seq, dtype=F32)[:, None] * inv_freq[None, :]
    cos = jnp.concatenate([jnp.cos(ang), jnp.cos(ang)], axis=-1)
    sin = jnp.concatenate([-jnp.sin(ang), jnp.sin(ang)], axis=-1)
    return cos, sin


def _decay_tables():
    log_g = jnp.log(1.0 - 2.0 ** (-5.0 - jnp.arange(H_C, dtype=F32)))
    idx = jnp.arange(RET_BLOCK, dtype=F32)
    chunk = jnp.arange(RET_BLOCK) // CHUNK
    visible = chunk[None, :] <= chunk[:, None]
    intra = jnp.where(visible[None],
                      jnp.exp(log_g[:, None, None] * jnp.abs(idx[:, None] - idx[None, :])), 0.0)
    q_decay = jnp.exp(log_g[:, None] * (idx + 1.0))
    k_decay = jnp.exp(log_g[:, None] * (RET_BLOCK - 1.0 - idx))
    block_decay = jnp.exp(log_g * RET_BLOCK)
    return (intra,
            jnp.broadcast_to(q_decay[:, :, None], (H_C, RET_BLOCK, DV_C)),
            jnp.broadcast_to(k_decay[:, :, None], (H_C, RET_BLOCK, DK_C)),
            jnp.broadcast_to(block_decay[:, None, None], (H_C, 1, DV_C)))


def _retention_kernel(q_ref, k_ref, v_ref, gate_ref, cos_ref, sin_ref, intra_ref, qd_ref,
                      kd_ref, bd_ref, g_ref, o_ref, state_ref):
    rows = q_ref.shape[1]

    @pl.when(pl.program_id(1) == 0)
    def _():
        state_ref[...] = jnp.zeros_like(state_ref)

    cos = cos_ref[...]
    sin = sin_ref[...]

    def rotary(t):
        return t * cos + pltpu.roll(t, DK_C // 2, 1) * sin

    for h in range(H_C):
        qk_cols = slice(h * DK_C, (h + 1) * DK_C)
        v_cols = slice(h * DV_C, (h + 1) * DV_C)
        q = (rotary(q_ref[0, :, qk_cols]) * (DK_C ** -0.5)).astype(BF16)
        k = rotary(k_ref[0, :, qk_cols])
        state = state_ref[h]
        for r in range(0, rows, RET_BLOCK):
            sl = slice(r, r + RET_BLOCK)
            qc = q[sl]
            kc = k[sl]
            vc = v_ref[0, sl, v_cols].astype(BF16)
            scores = _dot_nt(qc, kc.astype(BF16)) * intra_ref[h]
            y = _dot(scores.astype(BF16), vc) + _dot(qc, state.astype(BF16)) * qd_ref[h]
            state = bd_ref[h] * state + _dot((kc * kd_ref[h]).T.astype(BF16), vc)
            gate = gate_ref[0, sl, v_cols]
            y = _rms_rows(y, g_ref[...])
            o_ref[0, sl, v_cols] = (gate * (1.0 / (1.0 + jnp.exp(-gate))) * y).astype(o_ref.dtype)
        state_ref[h] = state


def _retention(p, ret_norm_g, batch, seq, rows=512):
    qk_width = H_C * DK_C
    v_width = H_C * DV_C
    cos, sin = _rotary_tables(seq)
    intra, q_decay, k_decay, block_decay = _decay_tables()

    def whole(a):
        return pl.BlockSpec(a.shape, lambda b, i: (0,) * a.ndim)

    return pl.pallas_call(
        _retention_kernel,
        grid=(batch, seq // rows),
        in_specs=[pl.BlockSpec((1, rows, qk_width), lambda b, i: (b, i, 0)),
                  pl.BlockSpec((1, rows, qk_width), lambda b, i: (b, i, 1)),
                  pl.BlockSpec((1, rows, v_width), lambda b, i: (b, i, 2 * qk_width // v_width)),
                  pl.BlockSpec((1, rows, v_width), lambda b, i: (b, i, 2 * qk_width // v_width + 1)),
                  pl.BlockSpec((rows, DK_C), lambda b, i: (i, 0)),
                  pl.BlockSpec((rows, DK_C), lambda b, i: (i, 0)),
                  whole(intra), whole(q_decay), whole(k_decay), whole(block_decay),
                  pl.BlockSpec((1, DV_C), lambda b, i: (0, 0))],
        out_specs=pl.BlockSpec((1, rows, v_width), lambda b, i: (b, i, 0)),
        out_shape=jax.ShapeDtypeStruct((batch, seq, v_width), BF16),
        scratch_shapes=[pltpu.VMEM((H_C, DK_C, DV_C), F32)],
        compiler_params=_params("arbitrary", "arbitrary"),
        name="retention",
    )(p, p, p, p, cos, sin, intra, q_decay, k_decay, block_decay, ret_norm_g.reshape(1, DV_C))


def _gelu_tanh(x):
    return 0.5 * x * (1.0 + jnp.tanh(math.sqrt(2.0 / math.pi) * (x + 0.044715 * (x * x * x))))


def _sgu_kernel(zu_ref, zv_ref, lg_ref, lb_ref, w_ref, b_ref, o_ref):
    rows = zu_ref.shape[0]
    width = D_D // G_D
    v = _gelu_tanh(zv_ref[...])
    mu = jnp.mean(v, axis=-1, keepdims=True)
    var = jnp.mean(jnp.square(v - mu), axis=-1, keepdims=True)
    vn = ((v - mu) * lax.rsqrt(var + EPS) * lg_ref[...] + lb_ref[...]).astype(BF16)
    pos_i = lax.broadcasted_iota(jnp.int32, (SGU_LEN, SGU_LEN), 0)
    pos_j = lax.broadcasted_iota(jnp.int32, (SGU_LEN, SGU_LEN), 1)
    mask = (pos_j // CHUNK) <= (pos_i // CHUNK)
    for g in range(G_D):
        w = jnp.where(mask, w_ref[g], 0.0).astype(BF16)
        bias = b_ref[g]
        cols = slice(g * width, (g + 1) * width)
        for r in range(rows // SGU_LEN):
            sl = slice(r * SGU_LEN, (r + 1) * SGU_LEN)
            gate = _dot(w, vn[sl, cols]) + bias
            o_ref[sl, cols] = (_gelu_tanh(zu_ref[sl, cols]) * gate).astype(o_ref.dtype)


def _sgu(p, ln_g, ln_b, w_s, b_s, rows=512):
    t = p.shape[0]
    u_col = (2 * H_C * DK_C + 2 * H_C * DV_C) // D_D
    return pl.pallas_call(
        _sgu_kernel,
        grid=(t // rows,),
        in_specs=[pl.BlockSpec((rows, D_D), lambda i: (i, u_col)),
                  pl.BlockSpec((rows, D_D), lambda i: (i, u_col + 1)),
                  pl.BlockSpec((1, D_D), lambda i: (0, 0)),
                  pl.BlockSpec((1, D_D), lambda i: (0, 0)),
                  pl.BlockSpec((G_D, SGU_LEN, SGU_LEN), lambda i: (0, 0, 0)),
                  pl.BlockSpec((G_D, SGU_LEN, 1), lambda i: (0, 0, 0))],
        out_specs=pl.BlockSpec((rows, D_D), lambda i: (i, 0)),
        out_shape=jax.ShapeDtypeStruct((t, D_D), BF16),
        compiler_params=_params("parallel"),
        name="spatial_gate",
    )(p, p, ln_g.reshape(1, D_D), ln_b.reshape(1, D_D), w_s, b_s.reshape(G_D, SGU_LEN, 1))


def kernel(x, norm_mix_g, norm_ffn_g, final_norm_g, rel_bias, ab_w_in, ab_w_out, diff_lambda,
           diff_subln_g, cd_w_in, cd_w_out, ret_norm_g, sgu_ln_g, sgu_ln_b, sgu_w, sgu_b,
           ffn_w_up, ffn_conv_w, ffn_conv_b, ffn_w_down):
    batch, seq, d = x.shape
    t = batch * seq
    depth = norm_mix_g.shape[0]
    xt = x.reshape(t, d)
    def rows(w):
        return w.reshape(-1, w.shape[-1])

    ab_w_in = ab_w_in.astype(BF16)
    for layer in range(depth):
        j = layer // 2
        if layer % 2 == 0:
            lam_init = 0.8 - 0.6 * math.exp(-0.3 * layer)
            p = _norm_matmul(xt, norm_mix_g[layer], ab_w_in, j, BF16)
            p3 = p.reshape(batch, seq, p.shape[1])
            if layer == 0:
                o_a, (up_bf,) = _stick_breaking(p3, batch, seq, (rows(ffn_w_up),))
                o_b, (down_bf, abo_bf, cdi_bf, cdo_bf) = _diff_attention(
                    p3, rel_bias, diff_lambda[j], diff_subln_g[j], lam_init, batch, seq,
                    (rows(ffn_w_down), rows(ab_w_out), rows(cd_w_in), rows(cd_w_out)))
                ffn_w_up, ffn_w_down, ab_w_out, cd_w_in, cd_w_out = (
                    b.reshape(w.shape) for b, w in ((up_bf, ffn_w_up), (down_bf, ffn_w_down),
                                                    (abo_bf, ab_w_out), (cdi_bf, cd_w_in),
                                                    (cdo_bf, cd_w_out)))
            else:
                o_a, _ = _stick_breaking(p3, batch, seq)
                o_b, _ = _diff_attention(p3, rel_bias, diff_lambda[j], diff_subln_g[j], lam_init,
                                         batch, seq)
            xt = _out_proj(o_a.reshape(t, -1), o_b.reshape(t, -1), ab_w_out, j, xt)
        else:
            p = _norm_matmul(xt, norm_mix_g[layer], cd_w_in, j, F32)
            o_c = _retention(p.reshape(batch, seq, p.shape[1]), ret_norm_g[j], batch, seq)
            o_d = _sgu(p, sgu_ln_g[j], sgu_ln_b[j], sgu_w[j], sgu_b[j])
            xt = _out_proj(o_c.reshape(t, -1), o_d, cd_w_out, j, xt)
        xt = _ffn(xt, norm_ffn_g, ffn_w_up, ffn_conv_w, ffn_conv_b, ffn_w_down, final_norm_g,
                  layer, seq=seq, final_norm=(layer == depth - 1))
    return xt.reshape(batch, seq, d)
```

```python
import functools
import math

import numpy as np
import jax
import jax.numpy as jnp
from jax import lax
from jax.experimental import pallas as pl
from jax.experimental.pallas import tpu as pltpu

F32 = jnp.float32
BF16 = jnp.bfloat16

EPS = 1e-6
CHUNK = 64
H_A = 8
DH_A = 128
H_B = 4
DK_B = 128
DV_B = 2 * DK_B
NUM_BUCKETS = 32
MAX_DISTANCE = 128
H_C = 4
DK_C = 128
DV_C = 2 * DK_C
ROPE_BASE = 10000.0
D_D = 1024
G_D = 4
SGU_LEN = 128
CONV_WIDTH = 3

SUBLANES = 8
BF16_SUBLANES = 16
ATT_BLOCK = 128
ATT_TILE = 512
RET_BLOCK = 256
VMEM_LIMIT = 62 * 1024 * 1024
PROJ_ROWS = 1024
AB_PROJ_TILE = 2048
CD_PROJ_TILE = 1280
FFN_ROWS = 1024
FFN_TILE = 512
MIXER_ROWS = 512
MASKED_LOGIT = -1e4
LOG2E = math.log2(math.e)
ZERO_WEIGHT_LOG2 = -150.0


def _params(*semantics):
    return pltpu.CompilerParams(dimension_semantics=semantics,
                                vmem_limit_bytes=VMEM_LIMIT)


def _dot(a, b):
    return jnp.dot(a, b, preferred_element_type=F32)


def _dot_nt(a, b):
    return lax.dot_general(a, b, (((1,), (1,)), ((), ())), preferred_element_type=F32)


def _rms_rows(x, g):
    return x * lax.rsqrt(jnp.mean(x * x, axis=-1, keepdims=True) + EPS) * g


def _with_riders(body, n_in, n_riders):
    def kern(*refs):
        ins = refs[:n_in]
        rider_in = refs[n_in:n_in + n_riders]
        out = refs[n_in + n_riders]
        rider_out = refs[n_in + n_riders + 1:n_in + 2 * n_riders + 1]
        scratch = refs[n_in + 2 * n_riders + 1:]
        for src, dst in zip(rider_in, rider_out):
            dst[...] = src[...].astype(BF16)
        body(*ins, out, *scratch)
    return kern


def _rider_specs(riders, grid):
    steps = math.prod(grid)

    def linear_step(*g):
        n = g[0]
        for size, idx in zip(grid[1:], g[1:len(grid)]):
            n = n * size + idx
        return n

    specs, shapes = [], []
    for w in riders:
        rows, cols = w.shape
        block_rows, rem = divmod(rows, steps)
        assert rem == 0 and block_rows % BF16_SUBLANES == 0, (w.shape, steps)
        specs.append(pl.BlockSpec((block_rows, cols), lambda *g: (linear_step(*g), 0)))
        shapes.append(jax.ShapeDtypeStruct((rows, cols), BF16))
    return specs, shapes


def _norm_matmul_kernel(x_ref, g_ref, w_ref, o_ref, h_ref):
    @pl.when(pl.program_id(1) == 0)
    def _():
        h_ref[...] = _rms_rows(x_ref[...], g_ref[...]).astype(BF16)

    o_ref[...] = _dot(h_ref[...], w_ref[...]).astype(o_ref.dtype)


def _norm_matmul(x, g, w, layer, out_dtype, tn, tm=PROJ_ROWS):
    t, d = x.shape
    n = w.shape[2]
    return pl.pallas_call(
        _norm_matmul_kernel,
        grid=(t // tm, n // tn),
        in_specs=[pl.BlockSpec((tm, d), lambda i, j: (i, 0)),
                  pl.BlockSpec((1, d), lambda i, j: (0, 0)),
                  pl.BlockSpec((None, d, tn), lambda i, j: (layer, 0, j))],
        out_specs=pl.BlockSpec((tm, tn), lambda i, j: (i, j)),
        out_shape=jax.ShapeDtypeStruct((t, n), out_dtype),
        scratch_shapes=[pltpu.VMEM((tm, d), BF16)],
        compiler_params=_params("parallel", "arbitrary"),
        name="norm_matmul",
    )(x, g.reshape(1, d), w)


def _out_proj_kernel(a1_ref, a2_ref, w_ref, x_ref, o_ref):
    half = a1_ref.shape[1]
    acc = _dot(a1_ref[...], w_ref[0:half, :])
    acc += _dot(a2_ref[...], w_ref[half:2 * half, :])
    o_ref[...] = x_ref[...] + acc


def _out_proj(a1, a2, w, layer, x, tm=MIXER_ROWS):
    t, half = a1.shape
    d = w.shape[2]
    return pl.pallas_call(
        _out_proj_kernel,
        grid=(t // tm,),
        in_specs=[pl.BlockSpec((tm, half), lambda i: (i, 0)),
                  pl.BlockSpec((tm, half), lambda i: (i, 0)),
                  pl.BlockSpec((None, 2 * half, d), lambda i: (layer, 0, 0)),
                  pl.BlockSpec((tm, d), lambda i: (i, 0))],
        out_specs=pl.BlockSpec((tm, d), lambda i: (i, 0)),
        out_shape=jax.ShapeDtypeStruct((t, d), F32),
        compiler_params=_params("parallel"),
        name="out_proj",
    )(a1, a2, w, x)


def _ffn_kernel(x_ref, g_ref, wa_ref, wg_ref, cwa_ref, cwg_ref, cba_ref, cbg_ref,
                wd_ref, wd_last_ref, fg_ref, o_ref, h_ref, act0_ref, act1_ref, carry_ref,
                *, seq_tiles, final_norm, n_hidden_tiles):
    i = pl.program_id(0)
    j = pl.program_id(1)
    nj = n_hidden_tiles
    tm = x_ref.shape[0]
    acts = (act0_ref, act1_ref)

    def up_phase(out_ref):
        h = h_ref[...]

        def gate(cg, ca):
            return (cg * (1.0 / (1.0 + jnp.exp(-cg))) * ca).astype(BF16)

        def conv(w_ref, cw_ref, cb_ref, branch):
            up = _dot(h, w_ref[...])
            w0 = cw_ref[0:1, :]
            w1 = cw_ref[1:2, :]
            w2 = cw_ref[2:3, :]
            b = cb_ref[...]
            c = b + w0 * pltpu.roll(up, 2, 0) + w1 * pltpu.roll(up, 1, 0) + w2 * up
            both = jnp.concatenate([carry_ref[j, branch], up[0:SUBLANES, :]], axis=0)
            head = slice(SUBLANES, 2 * SUBLANES)
            top = (b + w0 * pltpu.roll(both, 2, 0)[head, :]
                   + w1 * pltpu.roll(both, 1, 0)[head, :] + w2 * both[head, :])
            carry_ref[j, branch] = up[tm - SUBLANES:tm, :]
            return c, top

        ca, ta = conv(wa_ref, cwa_ref, cba_ref, 0)
        cg, tg = conv(wg_ref, cwg_ref, cbg_ref, 1)
        out_ref[...] = gate(cg, ca)
        out_ref[0:SUBLANES, :] = gate(tg, ta)

    def down_phase(in_ref):
        o_ref[...] += _dot(in_ref[...], wd_ref[...])

    @pl.when((i % seq_tiles) == 0)
    def _():
        carry_ref[j] = jnp.zeros(carry_ref.shape[1:], F32)

    @pl.when(j == 0)
    def _():
        x = x_ref[...]
        h_ref[...] = _rms_rows(x, g_ref[...]).astype(BF16)
        o_ref[...] = x
        up_phase(act0_ref)

    for parity in range(2):
        @pl.when((j > 0) & (j < nj - 1) & (j % 2 == parity))
        def _():
            up_phase(acts[parity])
            down_phase(acts[1 - parity])

    @pl.when(j == nj - 1)
    def _():
        parity = (n_hidden_tiles - 1) % 2
        up_phase(acts[parity])
        down_phase(acts[1 - parity])
        y = o_ref[...] + _dot(acts[parity][...], wd_last_ref[...])
        if final_norm:
            y = _rms_rows(y, fg_ref[...])
        o_ref[...] = y


def _ffn(x, g, w_up, conv_w, conv_b, w_down, final_g, layer, *, seq, final_norm,
         tm=FFN_ROWS, tf=FFN_TILE):
    t, d = x.shape
    f = w_down.shape[1]
    nj = f // tf
    assert nj >= 2
    kern = functools.partial(_ffn_kernel, seq_tiles=seq // tm, final_norm=final_norm,
                             n_hidden_tiles=nj)

    def down_row(j):
        return jnp.maximum(j - 1, 0)

    return pl.pallas_call(
        kern,
        grid=(t // tm, nj),
        in_specs=[pl.BlockSpec((tm, d), lambda i, j: (i, 0)),
                  pl.BlockSpec((None, 1, d), lambda i, j: (layer, 0, 0)),
                  pl.BlockSpec((None, d, tf), lambda i, j: (layer, 0, j)),
                  pl.BlockSpec((None, d, tf), lambda i, j: (layer, 0, j + nj)),
                  pl.BlockSpec((None, CONV_WIDTH, tf), lambda i, j: (layer, 0, j)),
                  pl.BlockSpec((None, CONV_WIDTH, tf), lambda i, j: (layer, 0, j + nj)),
                  pl.BlockSpec((None, 1, tf), lambda i, j: (layer, 0, j)),
                  pl.BlockSpec((None, 1, tf), lambda i, j: (layer, 0, j + nj)),
                  pl.BlockSpec((None, tf, d), lambda i, j: (layer, down_row(j), 0)),
                  pl.BlockSpec((None, tf, d), lambda i, j: (layer, nj - 1, 0)),
                  pl.BlockSpec((1, d), lambda i, j: (0, 0))],
        out_specs=pl.BlockSpec((tm, d), lambda i, j: (i, 0)),
        out_shape=jax.ShapeDtypeStruct((t, d), F32),
        scratch_shapes=[pltpu.VMEM((tm, d), BF16),
                        pltpu.VMEM((tm, tf), BF16),
                        pltpu.VMEM((tm, tf), BF16),
                        pltpu.VMEM((nj, 2, SUBLANES, tf), F32)],
        compiler_params=_params("arbitrary", "arbitrary"),
        name="conv_ffn",
    )(x, g[:, None, :], w_up, w_up, conv_w, conv_w, conv_b[:, None, :], conv_b[:, None, :],
      w_down, w_down, final_g.reshape(1, d))


def _suffix_sum_matrix():
    blk = ATT_BLOCK
    j = np.arange(blk)[:, None]
    s = np.arange(blk)[None, :]
    m = np.concatenate([(j > s).astype(np.float32), np.ones((blk, blk), np.float32)], axis=1)
    return jnp.asarray(np.concatenate([m, m], axis=0), dtype=BF16)


def _stick_kernel(q_ref, k_ref, v_ref, m_ref, o_ref, acc_ref, carry_ref):
    blk = ATT_BLOCK
    tile = q_ref.shape[1]
    half = tile // 2
    i = pl.program_id(2)
    scale2 = DH_A ** -0.5 * LOG2E
    suffix = m_ref[...]

    acc_ref[...] = jnp.zeros_like(acc_ref)
    carry_ref[...] = jnp.zeros_like(carry_ref)

    def sweep(r0, r1, key_start, n_sub, diagonal=False):
        rows = slice(r0, r1)
        k = k_ref[0, pl.ds(key_start, n_sub * blk), :]
        v = v_ref[0, pl.ds(key_start, n_sub * blk), :]
        z_all = _dot_nt(q_ref[0, rows, :], k) * scale2
        carry = carry_ref[rows, :]
        ws = [None] * n_sub
        for c in reversed(range(n_sub)):
            z = z_all[:, c * blk:(c + 1) * blk]
            if diagonal:
                row = lax.broadcasted_iota(jnp.int32, z.shape, 0) + r0
                col = lax.broadcasted_iota(jnp.int32, z.shape, 1) + c * blk
                z = jnp.where(col < row, z, MASKED_LOGIT)
            log_sig = jnp.minimum(z, 0.0) - jnp.log2(1.0 + jnp.exp2(-jnp.abs(z)))
            log_1m = log_sig - z
            hi = log_1m.astype(BF16)
            lo = (log_1m - hi.astype(F32)).astype(BF16)
            sums = _dot(jnp.concatenate([hi, lo], axis=1), suffix)
            ws[c] = jnp.exp2(log_sig + sums[:, :blk] + carry).astype(BF16)
            carry = carry + sums[:, blk:]
        carry_ref[rows, :] = carry
        acc_ref[rows, :] += _dot(jnp.concatenate(ws, axis=1), v)

    def alive(r0, r1):
        return jnp.max(carry_ref[r0:r1, :]) > ZERO_WEIGHT_LOG2

    tile_start = pl.multiple_of(i * tile, tile)
    n_sub = tile // blk
    sweep(0, tile, tile_start, n_sub, diagonal=True)

    @pl.when(i > 0)
    def _():
        prev = pl.multiple_of(tile_start - tile, tile)
        sweep(0, half, prev + half, n_sub // 2)

        @pl.when(alive(half, tile))
        def _():
            sweep(half, tile, prev + half, n_sub // 2)

        @pl.when(alive(0, tile))
        def _():
            sweep(0, tile, prev, n_sub // 2)

    def more(t):
        return jnp.logical_and(t < i, alive(0, tile))

    def body(t):
        sweep(0, tile, pl.multiple_of(tile_start - (t + 1) * tile, tile), n_sub)
        return t + 1

    lax.while_loop(more, body, 1)
    o_ref[0] = acc_ref[...].astype(o_ref.dtype)


def _stick_breaking(p, batch, seq, riders=()):
    blk = ATT_BLOCK
    tile = ATT_TILE
    grid = (batch, H_A, seq // tile)
    rider_specs, rider_shapes = _rider_specs(riders, grid)
    out = pl.pallas_call(
        _with_riders(_stick_kernel, 4, len(riders)),
        grid=grid,
        in_specs=[pl.BlockSpec((1, tile, DH_A), lambda b, h, i: (b, i, h)),
                  pl.BlockSpec((1, seq, DH_A), lambda b, h, i: (b, 0, H_A + h)),
                  pl.BlockSpec((1, seq, DH_A), lambda b, h, i: (b, 0, 2 * H_A + h)),
                  pl.BlockSpec((2 * blk, 2 * blk), lambda b, h, i: (0, 0))] + rider_specs,
        out_specs=[pl.BlockSpec((1, tile, DH_A), lambda b, h, i: (b, i, h))] + rider_specs,
        out_shape=[jax.ShapeDtypeStruct((batch, seq, H_A * DH_A), BF16)] + rider_shapes,
        scratch_shapes=[pltpu.VMEM((tile, DH_A), F32),
                        pltpu.VMEM((tile, blk), F32)],
        compiler_params=_params("arbitrary", "arbitrary", "arbitrary"),
        name="stick_breaking",
    )(p, p, p, _suffix_sum_matrix(), *riders)
    return out[0], out[1:]


def _rel_bucket(rel):
    nb = NUM_BUCKETS // 2
    max_exact = nb // 2
    ret = jnp.where(rel > 0, nb, 0)
    n = jnp.abs(rel)
    n_f = jnp.maximum(n, 1).astype(F32)
    large = max_exact + (jnp.log(n_f / max_exact) / math.log(MAX_DISTANCE / max_exact)
                         * (nb - max_exact)).astype(jnp.int32)
    large = jnp.minimum(large, nb - 1)
    return ret + jnp.where(n < max_exact, n, large)


def _near_buckets():
    blk = ATT_BLOCK
    qpos = blk + jnp.arange(blk)
    kpos = jnp.arange(2 * blk)
    near = _rel_bucket(kpos[None, :] - qpos[:, None]).astype(jnp.int32)
    far = _rel_bucket(jnp.full((1,), -2 * blk, jnp.int32)).astype(jnp.int32)
    return near, far


def _diff_kernel(far_ref, relb_ref, q_ref, k_ref, v_ref, bucket_ref, lam_ref, g_ref, o_ref,
                 bias_ref, m_ref, l_ref, acc_ref, s0_ref, s1_ref, *, lam_init):
    blk = ATT_BLOCK
    tile = q_ref.shape[1]
    nsub = tile // blk
    h = pl.program_id(1)
    i = pl.program_id(2)
    scale2 = DK_B ** -0.5 * LOG2E

    @pl.when(i == 0)
    def _():
        bucket = bucket_ref[...]
        far = relb_ref[far_ref[0], h]
        near = jnp.zeros((blk, 2 * blk), F32)
        for b in range(NUM_BUCKETS):
            near = jnp.where(bucket == b, relb_ref[b, h], near)
        near = (near - far) * LOG2E
        qpos = blk + lax.broadcasted_iota(jnp.int32, (blk, 2 * blk), 0)
        kpos = lax.broadcasted_iota(jnp.int32, (blk, 2 * blk), 1)
        near = jnp.where((kpos // CHUNK) <= (qpos // CHUNK), near, -jnp.inf)
        bias_ref[...] = jnp.zeros_like(bias_ref)
        for a in range(nsub):
            rows = slice(a * blk, (a + 1) * blk)
            bias_ref[0, rows, a * blk:(a + 1) * blk] = near[:, blk:]
            if a >= 1:
                bias_ref[0, rows, (a - 1) * blk:a * blk] = near[:, :blk]
            if a + 1 < nsub:
                bias_ref[0, rows, (a + 1) * blk:] = jnp.full((blk, tile - (a + 1) * blk),
                                                             -jnp.inf, F32)
        bias_ref[1, 0:blk, (nsub - 1) * blk:] = near[:, :blk]

    m_ref[...] = jnp.full_like(m_ref, -jnp.inf)
    l_ref[...] = jnp.zeros_like(l_ref)
    acc_ref[...] = jnp.zeros_like(acc_ref)

    def key_tile(idx):
        return jnp.where(idx == 0, i, jnp.where(idx == 1, i - 1, idx - 2))

    def logits_into(s_ref, idx):
        j = key_tile(jnp.minimum(idx, i))
        kk = k_ref[0, pl.ds(pl.multiple_of(j * tile, tile), tile), :]
        for c in range(2):
            s_ref[c] = _dot_nt(q_ref[0, :, c * DK_B:(c + 1) * DK_B],
                               kk[:, c * DK_B:(c + 1) * DK_B])

    def consume(s_ref, idx):
        j = key_tile(idx)
        v = v_ref[0, pl.ds(pl.multiple_of(j * tile, tile), tile), :]
        bias = bias_ref[jnp.minimum(idx, 2)]
        probs = []
        for c in range(2):
            s = s_ref[c] * scale2 + bias
            subs = [s[:, u * blk:(u + 1) * blk] for u in range(nsub)]
            m_prev = m_ref[c]
            m_cur = functools.reduce(jnp.maximum, subs)
            m_new = jnp.maximum(m_prev, jnp.max(m_cur, axis=-1, keepdims=True))
            alpha = jnp.exp2(m_prev - m_new)
            ps = [jnp.exp2(u - m_new) for u in subs]
            l_ref[c] = alpha * l_ref[c] + functools.reduce(jnp.add, ps)
            m_ref[c] = m_new
            probs.append((alpha, jnp.concatenate([u.astype(BF16) for u in ps], axis=1)))
        for c in range(2):
            alpha, p = probs[c]
            acc_ref[c] = jnp.concatenate([alpha] * (DV_B // blk), axis=1) * acc_ref[c] + _dot(p, v)

    logits_into(s0_ref, 0)

    def pair(t, _):
        idx = 2 * t
        logits_into(s1_ref, idx + 1)
        consume(s0_ref, idx)
        logits_into(s0_ref, idx + 2)
        consume(s1_ref, idx + 1)
        return 0

    lax.fori_loop(0, (i + 1) // 2, pair, 0)

    @pl.when(i % 2 == 0)
    def _():
        consume(s0_ref, i)

    lv = lam_ref[...]
    lam = (jnp.exp(jnp.sum(lv[0:1] * lv[1:2], axis=-1, keepdims=True))
           - jnp.exp(jnp.sum(lv[2:3] * lv[3:4], axis=-1, keepdims=True)) + lam_init)
    l1 = jnp.sum(l_ref[0], axis=-1, keepdims=True)
    l2 = jnp.sum(l_ref[1], axis=-1, keepdims=True)
    o = acc_ref[0] / l1 - lam * (acc_ref[1] / l2)
    o_ref[0] = (_rms_rows(o, g_ref[...]) * (1.0 - lam_init)).astype(o_ref.dtype)


def _diff_attention(p, rel_bias, lam_vecs, subln_g, lam_init, batch, seq, riders=()):
    blk = ATT_BLOCK
    tile = ATT_TILE
    q_col = 3 * H_A * DH_A // DV_B
    k_col = q_col + H_B
    v_col = k_col + H_B
    near, far = _near_buckets()
    grid = (batch, H_B, seq // tile)
    rider_specs, rider_shapes = _rider_specs(riders, grid)
    kern = _with_riders(functools.partial(_diff_kernel, lam_init=lam_init), 8, len(riders))
    grid_spec = pltpu.PrefetchScalarGridSpec(
        num_scalar_prefetch=2,
        grid=grid,
        in_specs=[pl.BlockSpec((1, tile, DV_B), lambda b, h, i, *_: (b, i, q_col + h)),
                  pl.BlockSpec((1, seq, DV_B), lambda b, h, i, *_: (b, 0, k_col + h)),
                  pl.BlockSpec((1, seq, DV_B), lambda b, h, i, *_: (b, 0, v_col + h)),
                  pl.BlockSpec((blk, 2 * blk), lambda b, h, i, *_: (0, 0)),
                  pl.BlockSpec((4, DK_B), lambda b, h, i, *_: (0, 0)),
                  pl.BlockSpec((1, DV_B), lambda b, h, i, *_: (0, 0))] + rider_specs,
        out_specs=[pl.BlockSpec((1, tile, DV_B), lambda b, h, i, *_: (b, i, h))] + rider_specs,
        scratch_shapes=[pltpu.VMEM((3, tile, tile), F32),
                        pltpu.VMEM((2, tile, blk), F32),
                        pltpu.VMEM((2, tile, blk), F32),
                        pltpu.VMEM((2, tile, DV_B), F32),
                        pltpu.VMEM((2, tile, tile), F32),
                        pltpu.VMEM((2, tile, tile), F32)])
    out = pl.pallas_call(
        kern,
        grid_spec=grid_spec,
        out_shape=[jax.ShapeDtypeStruct((batch, seq, H_B * DV_B), BF16)] + rider_shapes,
        compiler_params=_params("arbitrary", "arbitrary", "arbitrary"),
        name="diff_attention",
    )(far, rel_bias.astype(F32), p, p, p, near, lam_vecs.astype(F32), subln_g.reshape(1, DV_B),
      *riders)
    return out[0], out[1:]


def _rotary_tables(seq):
    inv_freq = ROPE_BASE ** (-jnp.arange(0, DK_C, 2, dtype=F32) / DK_C)
    ang = jnp.arange(seq, dtype=F32)[:, None] * inv_freq[None, :]
    cos = jnp.concatenate([jnp.cos(ang), jnp.cos(ang)], axis=-1)
    sin = jnp.concatenate([-jnp.sin(ang), jnp.sin(ang)], axis=-1)
    return cos, sin


def _decay_tables():
    log_g = jnp.log(1.0 - 2.0 ** (-5.0 - jnp.arange(H_C, dtype=F32)))
    idx = jnp.arange(RET_BLOCK, dtype=F32)
    chunk = jnp.arange(RET_BLOCK) // CHUNK
    visible = chunk[None, :] <= chunk[:, None]
    intra = jnp.where(visible[None],
                      jnp.exp(log_g[:, None, None] * jnp.abs(idx[:, None] - idx[None, :])), 0.0)
    q_decay = jnp.exp(log_g[:, None] * (idx + 1.0))
    k_decay = jnp.exp(log_g[:, None] * (RET_BLOCK - 1.0 - idx))
    block_decay = jnp.exp(log_g * RET_BLOCK)
    return (intra,
            jnp.broadcast_to(q_decay[:, :, None], (H_C, RET_BLOCK, DV_C)),
            jnp.broadcast_to(k_decay[:, :, None], (H_C, RET_BLOCK, DK_C)),
            jnp.broadcast_to(block_decay[:, None, None], (H_C, 1, DV_C)))


def _retention_kernel(q_ref, k_ref, v_ref, gate_ref, cos_ref, sin_ref, intra_ref, qd_ref,
                      kd_ref, bd_ref, g_ref, o_ref, state_ref):
    rows = q_ref.shape[1]

    @pl.when(pl.program_id(1) == 0)
    def _():
        state_ref[...] = jnp.zeros_like(state_ref)

    cos = cos_ref[...]
    sin = sin_ref[...]

    def rotary(t):
        return t * cos + pltpu.roll(t, DK_C // 2, 1) * sin

    for h in range(H_C):
        qk_cols = slice(h * DK_C, (h + 1) * DK_C)
        v_cols = slice(h * DV_C, (h + 1) * DV_C)
        q = (rotary(q_ref[0, :, qk_cols]) * (DK_C ** -0.5)).astype(BF16)
        k = rotary(k_ref[0, :, qk_cols])
        state = state_ref[h]
        for r in range(0, rows, RET_BLOCK):
            sl = slice(r, r + RET_BLOCK)
            qc = q[sl]
            kc = k[sl]
            vc = v_ref[0, sl, v_cols].astype(BF16)
            scores = _dot_nt(qc, kc.astype(BF16)) * intra_ref[h]
            y = _dot(scores.astype(BF16), vc) + _dot(qc, state.astype(BF16)) * qd_ref[h]
            state = bd_ref[h] * state + _dot((kc * kd_ref[h]).T.astype(BF16), vc)
            gate = gate_ref[0, sl, v_cols]
            y = _rms_rows(y, g_ref[...])
            o_ref[0, sl, v_cols] = (gate * (1.0 / (1.0 + jnp.exp(-gate))) * y).astype(o_ref.dtype)
        state_ref[h] = state


def _retention(p, ret_norm_g, batch, seq, rows=MIXER_ROWS):
    qk_width = H_C * DK_C
    v_width = H_C * DV_C
    cos, sin = _rotary_tables(seq)
    intra, q_decay, k_decay, block_decay = _decay_tables()

    def whole(a):
        return pl.BlockSpec(a.shape, lambda b, i: (0,) * a.ndim)

    return pl.pallas_call(
        _retention_kernel,
        grid=(batch, seq // rows),
        in_specs=[pl.BlockSpec((1, rows, qk_width), lambda b, i: (b, i, 0)),
                  pl.BlockSpec((1, rows, qk_width), lambda b, i: (b, i, 1)),
                  pl.BlockSpec((1, rows, v_width), lambda b, i: (b, i, 2 * qk_width // v_width)),
                  pl.BlockSpec((1, rows, v_width), lambda b, i: (b, i, 2 * qk_width // v_width + 1)),
                  pl.BlockSpec((rows, DK_C), lambda b, i: (i, 0)),
                  pl.BlockSpec((rows, DK_C), lambda b, i: (i, 0)),
                  whole(intra), whole(q_decay), whole(k_decay), whole(block_decay),
                  pl.BlockSpec((1, DV_C), lambda b, i: (0, 0))],
        out_specs=pl.BlockSpec((1, rows, v_width), lambda b, i: (b, i, 0)),
        out_shape=jax.ShapeDtypeStruct((batch, seq, v_width), BF16),
        scratch_shapes=[pltpu.VMEM((H_C, DK_C, DV_C), F32)],
        compiler_params=_params("arbitrary", "arbitrary"),
        name="retention",
    )(p, p, p, p, cos, sin, intra, q_decay, k_decay, block_decay, ret_norm_g.reshape(1, DV_C))


def _gelu_tanh(x):
    return 0.5 * x * (1.0 + jnp.tanh(math.sqrt(2.0 / math.pi) * (x + 0.044715 * (x * x * x))))


def _sgu_kernel(zu_ref, zv_ref, lg_ref, lb_ref, w_ref, b_ref, o_ref):
    rows = zu_ref.shape[0]
    width = D_D // G_D
    v = _gelu_tanh(zv_ref[...])
    mu = jnp.mean(v, axis=-1, keepdims=True)
    var = jnp.mean(jnp.square(v - mu), axis=-1, keepdims=True)
    vn = ((v - mu) * lax.rsqrt(var + EPS) * lg_ref[...] + lb_ref[...]).astype(BF16)
    pos_i = lax.broadcasted_iota(jnp.int32, (SGU_LEN, SGU_LEN), 0)
    pos_j = lax.broadcasted_iota(jnp.int32, (SGU_LEN, SGU_LEN), 1)
    mask = (pos_j // CHUNK) <= (pos_i // CHUNK)
    for g in range(G_D):
        w = jnp.where(mask, w_ref[g], 0.0).astype(BF16)
        bias = b_ref[g]
        cols = slice(g * width, (g + 1) * width)
        for r in range(rows // SGU_LEN):
            sl = slice(r * SGU_LEN, (r + 1) * SGU_LEN)
            gate = _dot(w, vn[sl, cols]) + bias
            o_ref[sl, cols] = (_gelu_tanh(zu_ref[sl, cols]) * gate).astype(o_ref.dtype)


def _sgu(p, ln_g, ln_b, w_s, b_s, rows=MIXER_ROWS):
    t = p.shape[0]
    u_col = (2 * H_C * DK_C + 2 * H_C * DV_C) // D_D
    return pl.pallas_call(
        _sgu_kernel,
        grid=(t // rows,),
        in_specs=[pl.BlockSpec((rows, D_D), lambda i: (i, u_col)),
                  pl.BlockSpec((rows, D_D), lambda i: (i, u_col + 1)),
                  pl.BlockSpec((1, D_D), lambda i: (0, 0)),
                  pl.BlockSpec((1, D_D), lambda i: (0, 0)),
                  pl.BlockSpec((G_D, SGU_LEN, SGU_LEN), lambda i: (0, 0, 0)),
                  pl.BlockSpec((G_D, SGU_LEN, 1), lambda i: (0, 0, 0))],
        out_specs=pl.BlockSpec((rows, D_D), lambda i: (i, 0)),
        out_shape=jax.ShapeDtypeStruct((t, D_D), BF16),
        compiler_params=_params("parallel"),
        name="spatial_gate",
    )(p, p, ln_g.reshape(1, D_D), ln_b.reshape(1, D_D), w_s, b_s.reshape(G_D, SGU_LEN, 1))


def kernel(x, norm_mix_g, norm_ffn_g, final_norm_g, rel_bias, ab_w_in, ab_w_out, diff_lambda,
           diff_subln_g, cd_w_in, cd_w_out, ret_norm_g, sgu_ln_g, sgu_ln_b, sgu_w, sgu_b,
           ffn_w_up, ffn_conv_w, ffn_conv_b, ffn_w_down):
    batch, seq, d = x.shape
    t = batch * seq
    depth = norm_mix_g.shape[0]
    xt = x.reshape(t, d)
    def rows(w):
        return w.reshape(-1, w.shape[-1])

    ab_w_in = ab_w_in.astype(BF16)
    for layer in range(depth):
        j = layer // 2
        if layer % 2 == 0:
            lam_init = 0.8 - 0.6 * math.exp(-0.3 * layer)
            p = _norm_matmul(xt, norm_mix_g[layer], ab_w_in, j, BF16, AB_PROJ_TILE)
            p3 = p.reshape(batch, seq, p.shape[1])
            if layer == 0:
                o_a, (up_bf,) = _stick_breaking(p3, batch, seq, (rows(ffn_w_up),))
                o_b, (down_bf, abo_bf, cdi_bf, cdo_bf) = _diff_attention(
                    p3, rel_bias, diff_lambda[j], diff_subln_g[j], lam_init, batch, seq,
                    (rows(ffn_w_down), rows(ab_w_out), rows(cd_w_in), rows(cd_w_out)))
                ffn_w_up, ffn_w_down, ab_w_out, cd_w_in, cd_w_out = (
                    b.reshape(w.shape) for b, w in ((up_bf, ffn_w_up), (down_bf, ffn_w_down),
                                                    (abo_bf, ab_w_out), (cdi_bf, cd_w_in),
                                                    (cdo_bf, cd_w_out)))
            else:
                o_a, _ = _stick_breaking(p3, batch, seq)
                o_b, _ = _diff_attention(p3, rel_bias, diff_lambda[j], diff_subln_g[j], lam_init,
                                         batch, seq)
            xt = _out_proj(o_a.reshape(t, -1), o_b.reshape(t, -1), ab_w_out, j, xt)
        else:
            p = _norm_matmul(xt, norm_mix_g[layer], cd_w_in, j, F32, CD_PROJ_TILE)
            o_c = _retention(p.reshape(batch, seq, p.shape[1]), ret_norm_g[j], batch, seq)
            o_d = _sgu(p, sgu_ln_g[j], sgu_ln_b[j], sgu_w[j], sgu_b[j])
            xt = _out_proj(o_c.reshape(t, -1), o_d, cd_w_out, j, xt)
        xt = _ffn(xt, norm_ffn_g, ffn_w_up, ffn_conv_w, ffn_conv_b, ffn_w_down, final_norm_g,
                  layer, seq=seq, final_norm=(layer == depth - 1))
    return xt.reshape(batch, seq, d)
```

```python
import functools
import math

import numpy as np
import jax
import jax.numpy as jnp
from jax import lax
from jax.experimental import pallas as pl
from jax.experimental.pallas import tpu as pltpu

F32 = jnp.float32
BF16 = jnp.bfloat16

EPS = 1e-6
CHUNK = 64
H_A = 8
DH_A = 128
H_B = 4
DK_B = 128
DV_B = 2 * DK_B
NUM_BUCKETS = 32
MAX_DISTANCE = 128
H_C = 4
DK_C = 128
DV_C = 2 * DK_C
ROPE_BASE = 10000.0
D_D = 1024
G_D = 4
SGU_LEN = 128
CONV_WIDTH = 3

SUBLANES = 8
BF16_SUBLANES = 16
ATT_BLOCK = 128
ATT_TILE = 512
RET_BLOCK = 256
VMEM_LIMIT = 62 * 1024 * 1024
PROJ_ROWS = 1024
AB_PROJ_TILE = 2048
CD_PROJ_TILE = 1280
FFN_ROWS = 1024
FFN_TILE = 512
MIXER_ROWS = 512
MASKED_LOGIT = -1e4
LOG2E = math.log2(math.e)
ZERO_WEIGHT_LOG2 = -150.0


def _params(*semantics):
    return pltpu.CompilerParams(dimension_semantics=semantics,
                                vmem_limit_bytes=VMEM_LIMIT)


def _dot(a, b):
    return jnp.dot(a, b, preferred_element_type=F32)


def _dot_nt(a, b):
    return lax.dot_general(a, b, (((1,), (1,)), ((), ())), preferred_element_type=F32)


def _rms_rows(x, g):
    return x * lax.rsqrt(jnp.mean(x * x, axis=-1, keepdims=True) + EPS) * g


def _with_riders(body, n_in, n_riders):
    def kern(*refs):
        ins = refs[:n_in]
        rider_in = refs[n_in:n_in + n_riders]
        out = refs[n_in + n_riders]
        rider_out = refs[n_in + n_riders + 1:n_in + 2 * n_riders + 1]
        scratch = refs[n_in + 2 * n_riders + 1:]
        for src, dst in zip(rider_in, rider_out):
            dst[...] = src[...].astype(BF16)
        body(*ins, out, *scratch)
    return kern


def _rider_specs(riders, grid):
    steps = math.prod(grid)

    def linear_step(*g):
        n = g[0]
        for size, idx in zip(grid[1:], g[1:len(grid)]):
            n = n * size + idx
        return n

    specs, shapes = [], []
    for w in riders:
        rows, cols = w.shape
        block_rows, rem = divmod(rows, steps)
        assert rem == 0 and block_rows % BF16_SUBLANES == 0, (w.shape, steps)
        specs.append(pl.BlockSpec((block_rows, cols), lambda *g: (linear_step(*g), 0)))
        shapes.append(jax.ShapeDtypeStruct((rows, cols), BF16))
    return specs, shapes


def _norm_matmul_kernel(x_ref, g_ref, w_ref, o_ref, h_ref):
    @pl.when(pl.program_id(1) == 0)
    def _():
        h_ref[...] = _rms_rows(x_ref[...], g_ref[...]).astype(BF16)

    o_ref[...] = _dot(h_ref[...], w_ref[...]).astype(o_ref.dtype)


def _norm_matmul(x, g, w, layer, out_dtype, tn, tm=PROJ_ROWS):
    t, d = x.shape
    n = w.shape[2]
    return pl.pallas_call(
        _norm_matmul_kernel,
        grid=(t // tm, n // tn),
        in_specs=[pl.BlockSpec((tm, d), lambda i, j: (i, 0)),
                  pl.BlockSpec((1, d), lambda i, j: (0, 0)),
                  pl.BlockSpec((None, d, tn), lambda i, j: (layer, 0, j))],
        out_specs=pl.BlockSpec((tm, tn), lambda i, j: (i, j)),
        out_shape=jax.ShapeDtypeStruct((t, n), out_dtype),
        scratch_shapes=[pltpu.VMEM((tm, d), BF16)],
        compiler_params=_params("parallel", "arbitrary"),
        name="norm_matmul",
    )(x, g.reshape(1, d), w)


def _out_proj_kernel(a1_ref, a2_ref, w_ref, x_ref, o_ref):
    half = a1_ref.shape[1]
    acc = _dot(a1_ref[...], w_ref[0:half, :])
    acc += _dot(a2_ref[...], w_ref[half:2 * half, :])
    o_ref[...] = x_ref[...] + acc


def _out_proj(a1, a2, w, layer, x, tm=MIXER_ROWS):
    t, half = a1.shape
    d = w.shape[2]
    return pl.pallas_call(
        _out_proj_kernel,
        grid=(t // tm,),
        in_specs=[pl.BlockSpec((tm, half), lambda i: (i, 0)),
                  pl.BlockSpec((tm, half), lambda i: (i, 0)),
                  pl.BlockSpec((None, 2 * half, d), lambda i: (layer, 0, 0)),
                  pl.BlockSpec((tm, d), lambda i: (i, 0))],
        out_specs=pl.BlockSpec((tm, d), lambda i: (i, 0)),
        out_shape=jax.ShapeDtypeStruct((t, d), F32),
        compiler_params=_params("parallel"),
        name="out_proj",
    )(a1, a2, w, x)


def _ffn_kernel(x_ref, g_ref, wa_ref, wg_ref, cwa_ref, cwg_ref, cba_ref, cbg_ref,
                wd_ref, wd_last_ref, fg_ref, o_ref, h_ref, act0_ref, act1_ref, carry_ref,
                *, seq_tiles, final_norm, n_hidden_tiles):
    i = pl.program_id(0)
    j = pl.program_id(1)
    nj = n_hidden_tiles
    tm = x_ref.shape[0]
    acts = (act0_ref, act1_ref)

    def up_phase(out_ref):
        h = h_ref[...]

        def gate(cg, ca):
            return (cg * (1.0 / (1.0 + jnp.exp(-cg))) * ca).astype(BF16)

        def conv(w_ref, cw_ref, cb_ref, branch):
            up = _dot(h, w_ref[...])
            w0 = cw_ref[0:1, :]
            w1 = cw_ref[1:2, :]
            w2 = cw_ref[2:3, :]
            b = cb_ref[...]
            c = b + w0 * pltpu.roll(up, 2, 0) + w1 * pltpu.roll(up, 1, 0) + w2 * up
            both = jnp.concatenate([carry_ref[j, branch], up[0:SUBLANES, :]], axis=0)
            head = slice(SUBLANES, 2 * SUBLANES)
            top = (b + w0 * pltpu.roll(both, 2, 0)[head, :]
                   + w1 * pltpu.roll(both, 1, 0)[head, :] + w2 * both[head, :])
            carry_ref[j, branch] = up[tm - SUBLANES:tm, :]
            return c, top

        ca, ta = conv(wa_ref, cwa_ref, cba_ref, 0)
        cg, tg = conv(wg_ref, cwg_ref, cbg_ref, 1)
        out_ref[...] = gate(cg, ca)
        out_ref[0:SUBLANES, :] = gate(tg, ta)

    def down_phase(in_ref):
        o_ref[...] += _dot(in_ref[...], wd_ref[...])

    @pl.when((i % seq_tiles) == 0)
    def _():
        carry_ref[j] = jnp.zeros(carry_ref.shape[1:], F32)

    @pl.when(j == 0)
    def _():
        x = x_ref[...]
        h_ref[...] = _rms_rows(x, g_ref[...]).astype(BF16)
        o_ref[...] = x
        up_phase(act0_ref)

    for parity in range(2):
        @pl.when((j > 0) & (j < nj - 1) & (j % 2 == parity))
        def _():
            up_phase(acts[parity])
            down_phase(acts[1 - parity])

    @pl.when(j == nj - 1)
    def _():
        parity = (n_hidden_tiles - 1) % 2
        up_phase(acts[parity])
        down_phase(acts[1 - parity])
        y = o_ref[...] + _dot(acts[parity][...], wd_last_ref[...])
        if final_norm:
            y = _rms_rows(y, fg_ref[...])
        o_ref[...] = y


def _ffn(x, g, w_up, conv_w, conv_b, w_down, final_g, layer, *, seq, final_norm,
         tm=FFN_ROWS, tf=FFN_TILE):
    t, d = x.shape
    f = w_down.shape[1]
    nj = f // tf
    assert nj >= 2
    kern = functools.partial(_ffn_kernel, seq_tiles=seq // tm, final_norm=final_norm,
                             n_hidden_tiles=nj)

    def down_row(j):
        return jnp.maximum(j - 1, 0)

    return pl.pallas_call(
        kern,
        grid=(t // tm, nj),
        in_specs=[pl.BlockSpec((tm, d), lambda i, j: (i, 0)),
                  pl.BlockSpec((None, 1, d), lambda i, j: (layer, 0, 0)),
                  pl.BlockSpec((None, d, tf), lambda i, j: (layer, 0, j)),
                  pl.BlockSpec((None, d, tf), lambda i, j: (layer, 0, j + nj)),
                  pl.BlockSpec((None, CONV_WIDTH, tf), lambda i, j: (layer, 0, j)),
                  pl.BlockSpec((None, CONV_WIDTH, tf), lambda i, j: (layer, 0, j + nj)),
                  pl.BlockSpec((None, 1, tf), lambda i, j: (layer, 0, j)),
                  pl.BlockSpec((None, 1, tf), lambda i, j: (layer, 0, j + nj)),
                  pl.BlockSpec((None, tf, d), lambda i, j: (layer, down_row(j), 0)),
                  pl.BlockSpec((None, tf, d), lambda i, j: (layer, nj - 1, 0)),
                  pl.BlockSpec((1, d), lambda i, j: (0, 0))],
        out_specs=pl.BlockSpec((tm, d), lambda i, j: (i, 0)),
        out_shape=jax.ShapeDtypeStruct((t, d), F32),
        scratch_shapes=[pltpu.VMEM((tm, d), BF16),
                        pltpu.VMEM((tm, tf), BF16),
                        pltpu.VMEM((tm, tf), BF16),
                        pltpu.VMEM((nj, 2, SUBLANES, tf), F32)],
        compiler_params=_params("arbitrary", "arbitrary"),
        name="conv_ffn",
    )(x, g[:, None, :], w_up, w_up, conv_w, conv_w, conv_b[:, None, :], conv_b[:, None, :],
      w_down, w_down, final_g.reshape(1, d))


def _suffix_sum_matrix():
    blk = ATT_BLOCK
    j = np.arange(blk)[:, None]
    s = np.arange(blk)[None, :]
    m = np.concatenate([(j > s).astype(np.float32), np.ones((blk, blk), np.float32)], axis=1)
    return jnp.asarray(np.concatenate([m, m], axis=0), dtype=BF16)


def _stick_kernel(q_ref, k_ref, v_ref, m_ref, o_ref, acc_ref, carry_ref):
    blk = ATT_BLOCK
    tile = q_ref.shape[1]
    half = tile // 2
    i = pl.program_id(2)
    scale2 = DH_A ** -0.5 * LOG2E
    suffix = m_ref[...]

    acc_ref[...] = jnp.zeros_like(acc_ref)
    carry_ref[...] = jnp.zeros_like(carry_ref)

    def sweep(r0, r1, key_start, n_sub, diagonal=False):
        rows = slice(r0, r1)
        k = k_ref[0, pl.ds(key_start, n_sub * blk), :]
        v = v_ref[0, pl.ds(key_start, n_sub * blk), :]
        z_all = _dot_nt(q_ref[0, rows, :], k) * scale2
        carry = carry_ref[rows, :]
        ws = [None] * n_sub
        for c in reversed(range(n_sub)):
            z = z_all[:, c * blk:(c + 1) * blk]
            if diagonal:
                row = lax.broadcasted_iota(jnp.int32, z.shape, 0) + r0
                col = lax.broadcasted_iota(jnp.int32, z.shape, 1) + c * blk
                z = jnp.where(col < row, z, MASKED_LOGIT)
            log_sig = jnp.minimum(z, 0.0) - jnp.log2(1.0 + jnp.exp2(-jnp.abs(z)))
            log_1m = log_sig - z
            hi = log_1m.astype(BF16)
            lo = (log_1m - hi.astype(F32)).astype(BF16)
            sums = _dot(jnp.concatenate([hi, lo], axis=1), suffix)
            ws[c] = jnp.exp2(log_sig + sums[:, :blk] + carry).astype(BF16)
            carry = carry + sums[:, blk:]
        carry_ref[rows, :] = carry
        acc_ref[rows, :] += _dot(jnp.concatenate(ws, axis=1), v)

    def alive(r0, r1):
        return jnp.max(carry_ref[r0:r1, :]) > ZERO_WEIGHT_LOG2

    tile_start = pl.multiple_of(i * tile, tile)
    n_sub = tile // blk
    sweep(0, tile, tile_start, n_sub, diagonal=True)

    @pl.when(i > 0)
    def _():
        prev = pl.multiple_of(tile_start - tile, tile)
        sweep(0, half, prev + half, n_sub // 2)

        @pl.when(alive(half, tile))
        def _():
            sweep(half, tile, prev + half, n_sub // 2)

        @pl.when(alive(0, tile))
        def _():
            sweep(0, tile, prev, n_sub // 2)

            def more(t):
                return jnp.logical_and(t < i, alive(0, tile))

            def body(t):
                sweep(0, tile, pl.multiple_of(tile_start - (t + 1) * tile, tile), n_sub)
                return t + 1

            lax.while_loop(more, body, 1)

    o_ref[0] = acc_ref[...].astype(o_ref.dtype)


def _stick_breaking(p, batch, seq, riders=()):
    blk = ATT_BLOCK
    tile = ATT_TILE
    grid = (batch, H_A, seq // tile)
    rider_specs, rider_shapes = _rider_specs(riders, grid)
    out = pl.pallas_call(
        _with_riders(_stick_kernel, 4, len(riders)),
        grid=grid,
        in_specs=[pl.BlockSpec((1, tile, DH_A), lambda b, h, i: (b, i, h)),
                  pl.BlockSpec((1, seq, DH_A), lambda b, h, i: (b, 0, H_A + h)),
                  pl.BlockSpec((1, seq, DH_A), lambda b, h, i: (b, 0, 2 * H_A + h)),
                  pl.BlockSpec((2 * blk, 2 * blk), lambda b, h, i: (0, 0))] + rider_specs,
        out_specs=[pl.BlockSpec((1, tile, DH_A), lambda b, h, i: (b, i, h))] + rider_specs,
        out_shape=[jax.ShapeDtypeStruct((batch, seq, H_A * DH_A), BF16)] + rider_shapes,
        scratch_shapes=[pltpu.VMEM((tile, DH_A), F32),
                        pltpu.VMEM((tile, blk), F32)],
        compiler_params=_params("arbitrary", "arbitrary", "arbitrary"),
        name="stick_breaking",
    )(p, p, p, _suffix_sum_matrix(), *riders)
    return out[0], out[1:]


def _rel_bucket(rel):
    nb = NUM_BUCKETS // 2
    max_exact = nb // 2
    ret = jnp.where(rel > 0, nb, 0)
    n = jnp.abs(rel)
    n_f = jnp.maximum(n, 1).astype(F32)
    large = max_exact + (jnp.log(n_f / max_exact) / math.log(MAX_DISTANCE / max_exact)
                         * (nb - max_exact)).astype(jnp.int32)
    large = jnp.minimum(large, nb - 1)
    return ret + jnp.where(n < max_exact, n, large)


def _near_buckets():
    blk = ATT_BLOCK
    qpos = blk + jnp.arange(blk)
    kpos = jnp.arange(2 * blk)
    near = _rel_bucket(kpos[None, :] - qpos[:, None]).astype(jnp.int32)
    far = _rel_bucket(jnp.full((1,), -2 * blk, jnp.int32)).astype(jnp.int32)
    return near, far


def _diff_kernel(far_ref, relb_ref, q_ref, k_ref, v_ref, bucket_ref, lam_ref, g_ref, o_ref,
                 bias_ref, m_ref, l_ref, acc_ref, s0_ref, s1_ref, *, lam_init):
    blk = ATT_BLOCK
    tile = q_ref.shape[1]
    nsub = tile // blk
    h = pl.program_id(1)
    i = pl.program_id(2)
    scale2 = DK_B ** -0.5 * LOG2E

    @pl.when(i == 0)
    def _():
        bucket = bucket_ref[...]
        far = relb_ref[far_ref[0], h]
        near = jnp.zeros((blk, 2 * blk), F32)
        for b in range(NUM_BUCKETS):
            near = jnp.where(bucket == b, relb_ref[b, h], near)
        near = (near - far) * LOG2E
        qpos = blk + lax.broadcasted_iota(jnp.int32, (blk, 2 * blk), 0)
        kpos = lax.broadcasted_iota(jnp.int32, (blk, 2 * blk), 1)
        near = jnp.where((kpos // CHUNK) <= (qpos // CHUNK), near, -jnp.inf)
        bias_ref[...] = jnp.zeros_like(bias_ref)
        for a in range(nsub):
            rows = slice(a * blk, (a + 1) * blk)
            bias_ref[0, rows, a * blk:(a + 1) * blk] = near[:, blk:]
            if a >= 1:
                bias_ref[0, rows, (a - 1) * blk:a * blk] = near[:, :blk]
            if a + 1 < nsub:
                bias_ref[0, rows, (a + 1) * blk:] = jnp.full((blk, tile - (a + 1) * blk),
                                                             -jnp.inf, F32)
        bias_ref[1, 0:blk, (nsub - 1) * blk:] = near[:, :blk]

    m_ref[...] = jnp.full_like(m_ref, -jnp.inf)
    l_ref[...] = jnp.zeros_like(l_ref)
    acc_ref[...] = jnp.zeros_like(acc_ref)

    def key_tile(idx):
        return jnp.where(idx == 0, i, jnp.where(idx == 1, i - 1, idx - 2))

    def logits_into(s_ref, idx):
        j = key_tile(jnp.minimum(idx, i))
        kk = k_ref[0, pl.ds(pl.multiple_of(j * tile, tile), tile), :]
        for c in range(2):
            s_ref[c] = _dot_nt(q_ref[0, :, c * DK_B:(c + 1) * DK_B],
                               kk[:, c * DK_B:(c + 1) * DK_B])

    def consume(s_ref, idx):
        j = key_tile(idx)
        v = v_ref[0, pl.ds(pl.multiple_of(j * tile, tile), tile), :]
        bias = bias_ref[jnp.minimum(idx, 2)]
        probs = []
        for c in range(2):
            s = s_ref[c] * scale2 + bias
            subs = [s[:, u * blk:(u + 1) * blk] for u in range(nsub)]
            m_prev = m_ref[c]
            m_cur = functools.reduce(jnp.maximum, subs)
            m_new = jnp.maximum(m_prev, jnp.max(m_cur, axis=-1, keepdims=True))
            alpha = jnp.exp2(m_prev - m_new)
            ps = [jnp.exp2(u - m_new) for u in subs]
            l_ref[c] = alpha * l_ref[c] + functools.reduce(jnp.add, ps)
            m_ref[c] = m_new
            probs.append((alpha, jnp.concatenate([u.astype(BF16) for u in ps], axis=1)))
        for c in range(2):
            alpha, p = probs[c]
            acc_ref[c] = jnp.concatenate([alpha] * (DV_B // blk), axis=1) * acc_ref[c] + _dot(p, v)

    logits_into(s0_ref, 0)

    def pair(t, _):
        idx = 2 * t
        logits_into(s1_ref, idx + 1)
        consume(s0_ref, idx)
        logits_into(s0_ref, idx + 2)
        consume(s1_ref, idx + 1)
        return 0

    lax.fori_loop(0, (i + 1) // 2, pair, 0)

    @pl.when(i % 2 == 0)
    def _():
        consume(s0_ref, i)

    lv = lam_ref[...]
    lam = (jnp.exp(jnp.sum(lv[0:1] * lv[1:2], axis=-1, keepdims=True))
           - jnp.exp(jnp.sum(lv[2:3] * lv[3:4], axis=-1, keepdims=True)) + lam_init)
    l1 = jnp.sum(l_ref[0], axis=-1, keepdims=True)
    l2 = jnp.sum(l_ref[1], axis=-1, keepdims=True)
    o = acc_ref[0] / l1 - lam * (acc_ref[1] / l2)
    o_ref[0] = (_rms_rows(o, g_ref[...]) * (1.0 - lam_init)).astype(o_ref.dtype)


def _diff_attention(p, rel_bias, lam_vecs, subln_g, lam_init, batch, seq, riders=()):
    blk = ATT_BLOCK
    tile = ATT_TILE
    q_col = 3 * H_A * DH_A // DV_B
    k_col = q_col + H_B
    v_col = k_col + H_B
    near, far = _near_buckets()
    grid = (batch, H_B, seq // tile)
    rider_specs, rider_shapes = _rider_specs(riders, grid)
    kern = _with_riders(functools.partial(_diff_kernel, lam_init=lam_init), 8, len(riders))
    grid_spec = pltpu.PrefetchScalarGridSpec(
        num_scalar_prefetch=2,
        grid=grid,
        in_specs=[pl.BlockSpec((1, tile, DV_B), lambda b, h, i, *_: (b, i, q_col + h)),
                  pl.BlockSpec((1, seq, DV_B), lambda b, h, i, *_: (b, 0, k_col + h)),
                  pl.BlockSpec((1, seq, DV_B), lambda b, h, i, *_: (b, 0, v_col + h)),
                  pl.BlockSpec((blk, 2 * blk), lambda b, h, i, *_: (0, 0)),
                  pl.BlockSpec((4, DK_B), lambda b, h, i, *_: (0, 0)),
                  pl.BlockSpec((1, DV_B), lambda b, h, i, *_: (0, 0))] + rider_specs,
        out_specs=[pl.BlockSpec((1, tile, DV_B), lambda b, h, i, *_: (b, i, h))] + rider_specs,
        scratch_shapes=[pltpu.VMEM((3, tile, tile), F32),
                        pltpu.VMEM((2, tile, blk), F32),
                        pltpu.VMEM((2, tile, blk), F32),
                        pltpu.VMEM((2, tile, DV_B), F32),
                        pltpu.VMEM((2, tile, tile), F32),
                        pltpu.VMEM((2, tile, tile), F32)])
    out = pl.pallas_call(
        kern,
        grid_spec=grid_spec,
        out_shape=[jax.ShapeDtypeStruct((batch, seq, H_B * DV_B), BF16)] + rider_shapes,
        compiler_params=_params("arbitrary", "arbitrary", "arbitrary"),
        name="diff_attention",
    )(far, rel_bias.astype(F32), p, p, p, near, lam_vecs.astype(F32), subln_g.reshape(1, DV_B),
      *riders)
    return out[0], out[1:]


def _rotary_tables(seq):
    inv_freq = ROPE_BASE ** (-jnp.arange(0, DK_C, 2, dtype=F32) / DK_C)
    ang = jnp.arange(seq, dtype=F32)[:, None] * inv_freq[None, :]
    cos = jnp.concatenate([jnp.cos(ang), jnp.cos(ang)], axis=-1)
    sin = jnp.concatenate([-jnp.sin(ang), jnp.sin(ang)], axis=-1)
    return cos, sin


def _decay_tables():
    log_g = jnp.log(1.0 - 2.0 ** (-5.0 - jnp.arange(H_C, dtype=F32)))
    idx = jnp.arange(RET_BLOCK, dtype=F32)
    chunk = jnp.arange(RET_BLOCK) // CHUNK
    visible = chunk[None, :] <= chunk[:, None]
    intra = jnp.where(visible[None],
                      jnp.exp(log_g[:, None, None] * jnp.abs(idx[:, None] - idx[None, :])), 0.0)
    q_decay = jnp.exp(log_g[:, None] * (idx + 1.0))
    k_decay = jnp.exp(log_g[:, None] * (RET_BLOCK - 1.0 - idx))
    block_decay = jnp.exp(log_g * RET_BLOCK)
    return (intra,
            jnp.broadcast_to(q_decay[:, :, None], (H_C, RET_BLOCK, DV_C)),
            jnp.broadcast_to(k_decay[:, :, None], (H_C, RET_BLOCK, DK_C)),
            jnp.broadcast_to(block_decay[:, None, None], (H_C, 1, DV_C)))


def _retention_kernel(q_ref, k_ref, v_ref, gate_ref, cos_ref, sin_ref, intra_ref, qd_ref,
                      kd_ref, bd_ref, g_ref, o_ref, state_ref):
    rows = q_ref.shape[1]

    @pl.when(pl.program_id(1) == 0)
    def _():
        state_ref[...] = jnp.zeros_like(state_ref)

    cos = cos_ref[...]
    sin = sin_ref[...]

    def rotary(t):
        return t * cos + pltpu.roll(t, DK_C // 2, 1) * sin

    for h in range(H_C):
        qk_cols = slice(h * DK_C, (h + 1) * DK_C)
        v_cols = slice(h * DV_C, (h + 1) * DV_C)
        q = (rotary(q_ref[0, :, qk_cols]) * (DK_C ** -0.5)).astype(BF16)
        k = rotary(k_ref[0, :, qk_cols])
        state = state_ref[h]
        for r in range(0, rows, RET_BLOCK):
            sl = slice(r, r + RET_BLOCK)
            qc = q[sl]
            kc = k[sl]
            vc = v_ref[0, sl, v_cols].astype(BF16)
            scores = _dot_nt(qc, kc.astype(BF16)) * intra_ref[h]
            y = _dot(scores.astype(BF16), vc) + _dot(qc, state.astype(BF16)) * qd_ref[h]
            state = bd_ref[h] * state + _dot((kc * kd_ref[h]).T.astype(BF16), vc)
            gate = gate_ref[0, sl, v_cols]
            y = _rms_rows(y, g_ref[...])
            o_ref[0, sl, v_cols] = (gate * (1.0 / (1.0 + jnp.exp(-gate))) * y).astype(o_ref.dtype)
        state_ref[h] = state


def _retention(p, ret_norm_g, batch, seq, rows=MIXER_ROWS):
    qk_width = H_C * DK_C
    v_width = H_C * DV_C
    cos, sin = _rotary_tables(seq)
    intra, q_decay, k_decay, block_decay = _decay_tables()

    def whole(a):
        return pl.BlockSpec(a.shape, lambda b, i: (0,) * a.ndim)

    return pl.pallas_call(
        _retention_kernel,
        grid=(batch, seq // rows),
        in_specs=[pl.BlockSpec((1, rows, qk_width), lambda b, i: (b, i, 0)),
                  pl.BlockSpec((1, rows, qk_width), lambda b, i: (b, i, 1)),
                  pl.BlockSpec((1, rows, v_width), lambda b, i: (b, i, 2 * qk_width // v_width)),
                  pl.BlockSpec((1, rows, v_width), lambda b, i: (b, i, 2 * qk_width // v_width + 1)),
                  pl.BlockSpec((rows, DK_C), lambda b, i: (i, 0)),
                  pl.BlockSpec((rows, DK_C), lambda b, i: (i, 0)),
                  whole(intra), whole(q_decay), whole(k_decay), whole(block_decay),
                  pl.BlockSpec((1, DV_C), lambda b, i: (0, 0))],
        out_specs=pl.BlockSpec((1, rows, v_width), lambda b, i: (b, i, 0)),
        out_shape=jax.ShapeDtypeStruct((batch, seq, v_width), BF16),
        scratch_shapes=[pltpu.VMEM((H_C, DK_C, DV_C), F32)],
        compiler_params=_params("arbitrary", "arbitrary"),
        name="retention",
    )(p, p, p, p, cos, sin, intra, q_decay, k_decay, block_decay, ret_norm_g.reshape(1, DV_C))


def _gelu_tanh(x):
    return 0.5 * x * (1.0 + jnp.tanh(math.sqrt(2.0 / math.pi) * (x + 0.044715 * (x * x * x))))


def _sgu_kernel(zu_ref, zv_ref, lg_ref, lb_ref, w_ref, b_ref, o_ref):
    rows = zu_ref.shape[0]
    width = D_D // G_D
    v = _gelu_tanh(zv_ref[...])
    mu = jnp.mean(v, axis=-1, keepdims=True)
    var = jnp.mean(jnp.square(v - mu), axis=-1, keepdims=True)
    vn = ((v - mu) * lax.rsqrt(var + EPS) * lg_ref[...] + lb_ref[...]).astype(BF16)
    pos_i = lax.broadcasted_iota(jnp.int32, (SGU_LEN, SGU_LEN), 0)
    pos_j = lax.broadcasted_iota(jnp.int32, (SGU_LEN, SGU_LEN), 1)
    mask = (pos_j // CHUNK) <= (pos_i // CHUNK)
    for g in range(G_D):
        w = jnp.where(mask, w_ref[g], 0.0).astype(BF16)
        bias = b_ref[g]
        cols = slice(g * width, (g + 1) * width)
        for r in range(rows // SGU_LEN):
            sl = slice(r * SGU_LEN, (r + 1) * SGU_LEN)
            gate = _dot(w, vn[sl, cols]) + bias
            o_ref[sl, cols] = (_gelu_tanh(zu_ref[sl, cols]) * gate).astype(o_ref.dtype)


def _sgu(p, ln_g, ln_b, w_s, b_s, rows=MIXER_ROWS):
    t = p.shape[0]
    u_col = (2 * H_C * DK_C + 2 * H_C * DV_C) // D_D
    return pl.pallas_call(
        _sgu_kernel,
        grid=(t // rows,),
        in_specs=[pl.BlockSpec((rows, D_D), lambda i: (i, u_col)),
                  pl.BlockSpec((rows, D_D), lambda i: (i, u_col + 1)),
                  pl.BlockSpec((1, D_D), lambda i: (0, 0)),
                  pl.BlockSpec((1, D_D), lambda i: (0, 0)),
                  pl.BlockSpec((G_D, SGU_LEN, SGU_LEN), lambda i: (0, 0, 0)),
                  pl.BlockSpec((G_D, SGU_LEN, 1), lambda i: (0, 0, 0))],
        out_specs=pl.BlockSpec((rows, D_D), lambda i: (i, 0)),
        out_shape=jax.ShapeDtypeStruct((t, D_D), BF16),
        compiler_params=_params("parallel"),
        name="spatial_gate",
    )(p, p, ln_g.reshape(1, D_D), ln_b.reshape(1, D_D), w_s, b_s.reshape(G_D, SGU_LEN, 1))


def kernel(x, norm_mix_g, norm_ffn_g, final_norm_g, rel_bias, ab_w_in, ab_w_out, diff_lambda,
           diff_subln_g, cd_w_in, cd_w_out, ret_norm_g, sgu_ln_g, sgu_ln_b, sgu_w, sgu_b,
           ffn_w_up, ffn_conv_w, ffn_conv_b, ffn_w_down):
    batch, seq, d = x.shape
    t = batch * seq
    depth = norm_mix_g.shape[0]
    xt = x.reshape(t, d)
    def rows(w):
        return w.reshape(-1, w.shape[-1])

    ab_w_in = ab_w_in.astype(BF16)
    for layer in range(depth):
        j = layer // 2
        if layer % 2 == 0:
            lam_init = 0.8 - 0.6 * math.exp(-0.3 * layer)
            p = _norm_matmul(xt, norm_mix_g[layer], ab_w_in, j, BF16, AB_PROJ_TILE)
            p3 = p.reshape(batch, seq, p.shape[1])
            if layer == 0:
                o_a, (up_bf,) = _stick_breaking(p3, batch, seq, (rows(ffn_w_up),))
                o_b, (down_bf, abo_bf, cdi_bf, cdo_bf) = _diff_attention(
                    p3, rel_bias, diff_lambda[j], diff_subln_g[j], lam_init, batch, seq,
                    (rows(ffn_w_down), rows(ab_w_out), rows(cd_w_in), rows(cd_w_out)))
                ffn_w_up, ffn_w_down, ab_w_out, cd_w_in, cd_w_out = (
                    b.reshape(w.shape) for b, w in ((up_bf, ffn_w_up), (down_bf, ffn_w_down),
                                                    (abo_bf, ab_w_out), (cdi_bf, cd_w_in),
                                                    (cdo_bf, cd_w_out)))
            else:
                o_a, _ = _stick_breaking(p3, batch, seq)
                o_b, _ = _diff_attention(p3, rel_bias, diff_lambda[j], diff_subln_g[j], lam_init,
                                         batch, seq)
            xt = _out_proj(o_a.reshape(t, -1), o_b.reshape(t, -1), ab_w_out, j, xt)
        else:
            p = _norm_matmul(xt, norm_mix_g[layer], cd_w_in, j, F32, CD_PROJ_TILE)
            o_c = _retention(p.reshape(batch, seq, p.shape[1]), ret_norm_g[j], batch, seq)
            o_d = _sgu(p, sgu_ln_g[j], sgu_ln_b[j], sgu_w[j], sgu_b[j])
            xt = _out_proj(o_c.reshape(t, -1), o_d, cd_w_out, j, xt)
        xt = _ffn(xt, norm_ffn_g, ffn_w_up, ffn_conv_w, ffn_conv_b, ffn_w_down, final_norm_g,
                  layer, seq=seq, final_norm=(layer == depth - 1))
    return xt.reshape(batch, seq, d)
```

```python
import functools
import math

import numpy as np
import jax
import jax.numpy as jnp
from jax import lax
from jax.experimental import pallas as pl
from jax.experimental.pallas import tpu as pltpu

F32 = jnp.float32
BF16 = jnp.bfloat16

EPS = 1e-6
CHUNK = 64
H_A = 8
DH_A = 128
H_B = 4
DK_B = 128
DV_B = 2 * DK_B
NUM_BUCKETS = 32
MAX_DISTANCE = 128
H_C = 4
DK_C = 128
DV_C = 2 * DK_C
ROPE_BASE = 10000.0
D_D = 1024
G_D = 4
SGU_LEN = 128
CONV_WIDTH = 3

SUBLANES = 8
BF16_SUBLANES = 16
ATT_BLOCK = 128
ATT_TILE = 512
STICK_HEADS = 2
RET_BLOCK = 256
VMEM_LIMIT = 62 * 1024 * 1024
PROJ_ROWS = 1024
AB_PROJ_TILE = 2048
CD_PROJ_TILE = 1280
FFN_ROWS = 1024
FFN_TILE = 512
MIXER_ROWS = 512
MASKED_LOGIT = -1e4
LOG2E = math.log2(math.e)
ZERO_WEIGHT_LOG2 = -150.0


def _params(*semantics):
    return pltpu.CompilerParams(dimension_semantics=semantics,
                                vmem_limit_bytes=VMEM_LIMIT)


def _dot(a, b):
    return jnp.dot(a, b, preferred_element_type=F32)


def _dot_nt(a, b):
    return lax.dot_general(a, b, (((1,), (1,)), ((), ())), preferred_element_type=F32)


def _rms_rows(x, g):
    return x * lax.rsqrt(jnp.mean(x * x, axis=-1, keepdims=True) + EPS) * g


def _with_riders(body, n_in, n_riders):
    def kern(*refs):
        ins = refs[:n_in]
        rider_in = refs[n_in:n_in + n_riders]
        out = refs[n_in + n_riders]
        rider_out = refs[n_in + n_riders + 1:n_in + 2 * n_riders + 1]
        scratch = refs[n_in + 2 * n_riders + 1:]
        for src, dst in zip(rider_in, rider_out):
            dst[...] = src[...].astype(BF16)
        body(*ins, out, *scratch)
    return kern


def _rider_specs(riders, grid):
    steps = math.prod(grid)

    def linear_step(*g):
        n = g[0]
        for size, idx in zip(grid[1:], g[1:len(grid)]):
            n = n * size + idx
        return n

    specs, shapes = [], []
    for w in riders:
        rows, cols = w.shape
        block_rows, rem = divmod(rows, steps)
        assert rem == 0 and block_rows % BF16_SUBLANES == 0, (w.shape, steps)
        specs.append(pl.BlockSpec((block_rows, cols), lambda *g: (linear_step(*g), 0)))
        shapes.append(jax.ShapeDtypeStruct((rows, cols), BF16))
    return specs, shapes


def _norm_matmul_kernel(x_ref, g_ref, w_ref, o_ref, h_ref):
    @pl.when(pl.program_id(1) == 0)
    def _():
        h_ref[...] = _rms_rows(x_ref[...], g_ref[...]).astype(BF16)

    o_ref[...] = _dot(h_ref[...], w_ref[...]).astype(o_ref.dtype)


def _norm_matmul(x, g, w, layer, out_dtype, tn, tm=PROJ_ROWS):
    t, d = x.shape
    n = w.shape[2]
    return pl.pallas_call(
        _norm_matmul_kernel,
        grid=(t // tm, n // tn),
        in_specs=[pl.BlockSpec((tm, d), lambda i, j: (i, 0)),
                  pl.BlockSpec((1, d), lambda i, j: (0, 0)),
                  pl.BlockSpec((None, d, tn), lambda i, j: (layer, 0, j))],
        out_specs=pl.BlockSpec((tm, tn), lambda i, j: (i, j)),
        out_shape=jax.ShapeDtypeStruct((t, n), out_dtype),
        scratch_shapes=[pltpu.VMEM((tm, d), BF16)],
        compiler_params=_params("parallel", "arbitrary"),
        name="norm_matmul",
    )(x, g.reshape(1, d), w)


def _out_proj_kernel(a1_ref, a2_ref, w_ref, x_ref, o_ref):
    half = a1_ref.shape[1]
    acc = _dot(a1_ref[...], w_ref[0:half, :])
    acc += _dot(a2_ref[...], w_ref[half:2 * half, :])
    o_ref[...] = x_ref[...] + acc


def _out_proj(a1, a2, w, layer, x, tm=MIXER_ROWS):
    t, half = a1.shape
    d = w.shape[2]
    return pl.pallas_call(
        _out_proj_kernel,
        grid=(t // tm,),
        in_specs=[pl.BlockSpec((tm, half), lambda i: (i, 0)),
                  pl.BlockSpec((tm, half), lambda i: (i, 0)),
                  pl.BlockSpec((None, 2 * half, d), lambda i: (layer, 0, 0)),
                  pl.BlockSpec((tm, d), lambda i: (i, 0))],
        out_specs=pl.BlockSpec((tm, d), lambda i: (i, 0)),
        out_shape=jax.ShapeDtypeStruct((t, d), F32),
        compiler_params=_params("parallel"),
        name="out_proj",
    )(a1, a2, w, x)


def _ffn_kernel(x_ref, g_ref, wa_ref, wg_ref, cwa_ref, cwg_ref, cba_ref, cbg_ref,
                wd_ref, wd_last_ref, fg_ref, o_ref, h_ref, act0_ref, act1_ref, carry_ref,
                *, seq_tiles, final_norm, n_hidden_tiles):
    i = pl.program_id(0)
    j = pl.program_id(1)
    nj = n_hidden_tiles
    tm = x_ref.shape[0]
    acts = (act0_ref, act1_ref)

    def up_phase(out_ref):
        h = h_ref[...]

        def gate(cg, ca):
            return (cg * (1.0 / (1.0 + jnp.exp(-cg))) * ca).astype(BF16)

        def conv(w_ref, cw_ref, cb_ref, branch):
            up = _dot(h, w_ref[...])
            w0 = cw_ref[0:1, :]
            w1 = cw_ref[1:2, :]
            w2 = cw_ref[2:3, :]
            b = cb_ref[...]
            c = b + w0 * pltpu.roll(up, 2, 0) + w1 * pltpu.roll(up, 1, 0) + w2 * up
            both = jnp.concatenate([carry_ref[j, branch], up[0:SUBLANES, :]], axis=0)
            head = slice(SUBLANES, 2 * SUBLANES)
            top = (b + w0 * pltpu.roll(both, 2, 0)[head, :]
                   + w1 * pltpu.roll(both, 1, 0)[head, :] + w2 * both[head, :])
            carry_ref[j, branch] = up[tm - SUBLANES:tm, :]
            return c, top

        ca, ta = conv(wa_ref, cwa_ref, cba_ref, 0)
        cg, tg = conv(wg_ref, cwg_ref, cbg_ref, 1)
        out_ref[...] = gate(cg, ca)
        out_ref[0:SUBLANES, :] = gate(tg, ta)

    def down_phase(in_ref):
        o_ref[...] += _dot(in_ref[...], wd_ref[...])

    @pl.when((i % seq_tiles) == 0)
    def _():
        carry_ref[j] = jnp.zeros(carry_ref.shape[1:], F32)

    @pl.when(j == 0)
    def _():
        x = x_ref[...]
        h_ref[...] = _rms_rows(x, g_ref[...]).astype(BF16)
        o_ref[...] = x
        up_phase(act0_ref)

    for parity in range(2):
        @pl.when((j > 0) & (j < nj - 1) & (j % 2 == parity))
        def _():
            up_phase(acts[parity])
            down_phase(acts[1 - parity])

    @pl.when(j == nj - 1)
    def _():
        parity = (n_hidden_tiles - 1) % 2
        up_phase(acts[parity])
        down_phase(acts[1 - parity])
        y = o_ref[...] + _dot(acts[parity][...], wd_last_ref[...])
        if final_norm:
            y = _rms_rows(y, fg_ref[...])
        o_ref[...] = y


def _ffn(x, g, w_up, conv_w, conv_b, w_down, final_g, layer, *, seq, final_norm,
         tm=FFN_ROWS, tf=FFN_TILE):
    t, d = x.shape
    f = w_down.shape[1]
    nj = f // tf
    assert nj >= 2
    kern = functools.partial(_ffn_kernel, seq_tiles=seq // tm, final_norm=final_norm,
                             n_hidden_tiles=nj)

    def down_row(j):
        return jnp.maximum(j - 1, 0)

    return pl.pallas_call(
        kern,
        grid=(t // tm, nj),
        in_specs=[pl.BlockSpec((tm, d), lambda i, j: (i, 0)),
                  pl.BlockSpec((None, 1, d), lambda i, j: (layer, 0, 0)),
                  pl.BlockSpec((None, d, tf), lambda i, j: (layer, 0, j)),
                  pl.BlockSpec((None, d, tf), lambda i, j: (layer, 0, j + nj)),
                  pl.BlockSpec((None, CONV_WIDTH, tf), lambda i, j: (layer, 0, j)),
                  pl.BlockSpec((None, CONV_WIDTH, tf), lambda i, j: (layer, 0, j + nj)),
                  pl.BlockSpec((None, 1, tf), lambda i, j: (layer, 0, j)),
                  pl.BlockSpec((None, 1, tf), lambda i, j: (layer, 0, j + nj)),
                  pl.BlockSpec((None, tf, d), lambda i, j: (layer, down_row(j), 0)),
                  pl.BlockSpec((None, tf, d), lambda i, j: (layer, nj - 1, 0)),
                  pl.BlockSpec((1, d), lambda i, j: (0, 0))],
        out_specs=pl.BlockSpec((tm, d), lambda i, j: (i, 0)),
        out_shape=jax.ShapeDtypeStruct((t, d), F32),
        scratch_shapes=[pltpu.VMEM((tm, d), BF16),
                        pltpu.VMEM((tm, tf), BF16),
                        pltpu.VMEM((tm, tf), BF16),
                        pltpu.VMEM((nj, 2, SUBLANES, tf), F32)],
        compiler_params=_params("arbitrary", "arbitrary"),
        name="conv_ffn",
    )(x, g[:, None, :], w_up, w_up, conv_w, conv_w, conv_b[:, None, :], conv_b[:, None, :],
      w_down, w_down, final_g.reshape(1, d))


def _suffix_sum_matrix():
    blk = ATT_BLOCK
    j = np.arange(blk)[:, None]
    s = np.arange(blk)[None, :]
    m = np.concatenate([(j > s).astype(np.float32), np.ones((blk, blk), np.float32)], axis=1)
    return jnp.asarray(np.concatenate([m, m], axis=0), dtype=BF16)


def _stick_kernel(q_ref, k_ref, v_ref, m_ref, o_ref, acc_ref, carry_ref):
    blk = ATT_BLOCK
    tile = q_ref.shape[1]
    half = tile // 2
    i = pl.program_id(2)
    scale2 = DH_A ** -0.5 * LOG2E
    suffix = m_ref[...]
    heads = range(STICK_HEADS)

    acc_ref[...] = jnp.zeros_like(acc_ref)
    carry_ref[...] = jnp.zeros_like(carry_ref)

    def sweep(hd, r0, r1, key_start, n_sub, diagonal=False):
        rows = slice(r0, r1)
        cols = slice(hd * DH_A, (hd + 1) * DH_A)
        k = k_ref[0, pl.ds(key_start, n_sub * blk), cols]
        v = v_ref[0, pl.ds(key_start, n_sub * blk), cols]
        z_all = _dot_nt(q_ref[0, rows, cols], k) * scale2
        carry = carry_ref[hd, rows, :]
        ws = [None] * n_sub
        for c in reversed(range(n_sub)):
            z = z_all[:, c * blk:(c + 1) * blk]
            if diagonal:
                row = lax.broadcasted_iota(jnp.int32, z.shape, 0) + r0
                col = lax.broadcasted_iota(jnp.int32, z.shape, 1) + c * blk
                z = jnp.where(col < row, z, MASKED_LOGIT)
            log_sig = jnp.minimum(z, 0.0) - jnp.log2(1.0 + jnp.exp2(-jnp.abs(z)))
            log_1m = log_sig - z
            hi = log_1m.astype(BF16)
            lo = (log_1m - hi.astype(F32)).astype(BF16)
            sums = _dot(jnp.concatenate([hi, lo], axis=1), suffix)
            ws[c] = jnp.exp2(log_sig + sums[:, :blk] + carry).astype(BF16)
            carry = carry + sums[:, blk:]
        carry_ref[hd, rows, :] = carry
        acc_ref[hd, rows, :] += _dot(jnp.concatenate(ws, axis=1), v)

    def alive(hd, r0, r1):
        return jnp.max(carry_ref[hd, r0:r1, :]) > ZERO_WEIGHT_LOG2

    tile_start = pl.multiple_of(i * tile, tile)
    n_sub = tile // blk
    for hd in heads:
        sweep(hd, 0, tile, tile_start, n_sub, diagonal=True)

    @pl.when(i > 0)
    def _():
        prev = pl.multiple_of(tile_start - tile, tile)
        for hd in heads:
            sweep(hd, 0, half, prev + half, n_sub // 2)

        for hd in heads:
            @pl.when(alive(hd, half, tile))
            def _():
                sweep(hd, half, tile, prev + half, n_sub // 2)

            @pl.when(alive(hd, 0, tile))
            def _():
                sweep(hd, 0, tile, prev, n_sub // 2)

                def more(t):
                    return jnp.logical_and(t < i, alive(hd, 0, tile))

                def body(t):
                    sweep(hd, 0, tile, pl.multiple_of(tile_start - (t + 1) * tile, tile), n_sub)
                    return t + 1

                lax.while_loop(more, body, 1)

    o_ref[0] = jnp.concatenate([acc_ref[hd] for hd in heads], axis=1).astype(o_ref.dtype)


def _stick_breaking(p, batch, seq, riders=()):
    blk = ATT_BLOCK
    tile = ATT_TILE
    width = STICK_HEADS * DH_A
    groups = H_A // STICK_HEADS
    grid = (batch, groups, seq // tile)
    rider_specs, rider_shapes = _rider_specs(riders, grid)
    out = pl.pallas_call(
        _with_riders(_stick_kernel, 4, len(riders)),
        grid=grid,
        in_specs=[pl.BlockSpec((1, tile, width), lambda b, h, i: (b, i, h)),
                  pl.BlockSpec((1, seq, width), lambda b, h, i: (b, 0, groups + h)),
                  pl.BlockSpec((1, seq, width), lambda b, h, i: (b, 0, 2 * groups + h)),
                  pl.BlockSpec((2 * blk, 2 * blk), lambda b, h, i: (0, 0))] + rider_specs,
        out_specs=[pl.BlockSpec((1, tile, width), lambda b, h, i: (b, i, h))] + rider_specs,
        out_shape=[jax.ShapeDtypeStruct((batch, seq, H_A * DH_A), BF16)] + rider_shapes,
        scratch_shapes=[pltpu.VMEM((STICK_HEADS, tile, DH_A), F32),
                        pltpu.VMEM((STICK_HEADS, tile, blk), F32)],
        compiler_params=_params("arbitrary", "arbitrary", "arbitrary"),
        name="stick_breaking",
    )(p, p, p, _suffix_sum_matrix(), *riders)
    return out[0], out[1:]


def _rel_bucket(rel):
    nb = NUM_BUCKETS // 2
    max_exact = nb // 2
    ret = jnp.where(rel > 0, nb, 0)
    n = jnp.abs(rel)
    n_f = jnp.maximum(n, 1).astype(F32)
    large = max_exact + (jnp.log(n_f / max_exact) / math.log(MAX_DISTANCE / max_exact)
                         * (nb - max_exact)).astype(jnp.int32)
    large = jnp.minimum(large, nb - 1)
    return ret + jnp.where(n < max_exact, n, large)


def _near_buckets():
    blk = ATT_BLOCK
    qpos = blk + jnp.arange(blk)
    kpos = jnp.arange(2 * blk)
    near = _rel_bucket(kpos[None, :] - qpos[:, None]).astype(jnp.int32)
    far = _rel_bucket(jnp.full((1,), -2 * blk, jnp.int32)).astype(jnp.int32)
    return near, far


def _diff_kernel(far_ref, relb_ref, q_ref, k_ref, v_ref, bucket_ref, lam_ref, g_ref, o_ref,
                 bias_ref, m_ref, l_ref, acc_ref, s0_ref, s1_ref, *, lam_init):
    blk = ATT_BLOCK
    tile = q_ref.shape[1]
    nsub = tile // blk
    h = pl.program_id(1)
    i = pl.program_id(2)
    scale2 = DK_B ** -0.5 * LOG2E

    @pl.when(i == 0)
    def _():
        bucket = bucket_ref[...]
        far = relb_ref[far_ref[0], h]
        near = jnp.zeros((blk, 2 * blk), F32)
        for b in range(NUM_BUCKETS):
            near = jnp.where(bucket == b, relb_ref[b, h], near)
        near = (near - far) * LOG2E
        qpos = blk + lax.broadcasted_iota(jnp.int32, (blk, 2 * blk), 0)
        kpos = lax.broadcasted_iota(jnp.int32, (blk, 2 * blk), 1)
        near = jnp.where((kpos // CHUNK) <= (qpos // CHUNK), near, -jnp.inf)
        bias_ref[...] = jnp.zeros_like(bias_ref)
        for a in range(nsub):
            rows = slice(a * blk, (a + 1) * blk)
            bias_ref[0, rows, a * blk:(a + 1) * blk] = near[:, blk:]
            if a >= 1:
                bias_ref[0, rows, (a - 1) * blk:a * blk] = near[:, :blk]
            if a + 1 < nsub:
                bias_ref[0, rows, (a + 1) * blk:] = jnp.full((blk, tile - (a + 1) * blk),
                                                             -jnp.inf, F32)
        bias_ref[1, 0:blk, (nsub - 1) * blk:] = near[:, :blk]

    m_ref[...] = jnp.full_like(m_ref, -jnp.inf)
    l_ref[...] = jnp.zeros_like(l_ref)
    acc_ref[...] = jnp.zeros_like(acc_ref)

    def key_tile(idx):
        return jnp.where(idx == 0, i, jnp.where(idx == 1, i - 1, idx - 2))

    def logits_into(s_ref, idx):
        j = key_tile(jnp.minimum(idx, i))
        kk = k_ref[0, pl.ds(pl.multiple_of(j * tile, tile), tile), :]
        for c in range(2):
            s_ref[c] = _dot_nt(q_ref[0, :, c * DK_B:(c + 1) * DK_B],
                               kk[:, c * DK_B:(c + 1) * DK_B])

    def consume(s_ref, idx):
        j = key_tile(idx)
        v = v_ref[0, pl.ds(pl.multiple_of(j * tile, tile), tile), :]
        bias = bias_ref[jnp.minimum(idx, 2)]
        probs = []
        for c in range(2):
            s = s_ref[c] * scale2 + bias
            subs = [s[:, u * blk:(u + 1) * blk] for u in range(nsub)]
            m_prev = m_ref[c]
            m_cur = functools.reduce(jnp.maximum, subs)
            m_new = jnp.maximum(m_prev, jnp.max(m_cur, axis=-1, keepdims=True))
            alpha = jnp.exp2(m_prev - m_new)
            ps = [jnp.exp2(u - m_new) for u in subs]
            l_ref[c] = alpha * l_ref[c] + functools.reduce(jnp.add, ps)
            m_ref[c] = m_new
            probs.append((alpha, jnp.concatenate([u.astype(BF16) for u in ps], axis=1)))
        for c in range(2):
            alpha, p = probs[c]
            acc_ref[c] = jnp.concatenate([alpha] * (DV_B // blk), axis=1) * acc_ref[c] + _dot(p, v)

    logits_into(s0_ref, 0)

    def pair(t, _):
        idx = 2 * t
        logits_into(s1_ref, idx + 1)
        consume(s0_ref, idx)
        logits_into(s0_ref, idx + 2)
        consume(s1_ref, idx + 1)
        return 0

    lax.fori_loop(0, (i + 1) // 2, pair, 0)

    @pl.when(i % 2 == 0)
    def _():
        consume(s0_ref, i)

    lv = lam_ref[...]
    lam = (jnp.exp(jnp.sum(lv[0:1] * lv[1:2], axis=-1, keepdims=True))
           - jnp.exp(jnp.sum(lv[2:3] * lv[3:4], axis=-1, keepdims=True)) + lam_init)
    l1 = jnp.sum(l_ref[0], axis=-1, keepdims=True)
    l2 = jnp.sum(l_ref[1], axis=-1, keepdims=True)
    o = acc_ref[0] / l1 - lam * (acc_ref[1] / l2)
    o_ref[0] = (_rms_rows(o, g_ref[...]) * (1.0 - lam_init)).astype(o_ref.dtype)


def _diff_attention(p, rel_bias, lam_vecs, subln_g, lam_init, batch, seq, riders=()):
    blk = ATT_BLOCK
    tile = ATT_TILE
    q_col = 3 * H_A * DH_A // DV_B
    k_col = q_col + H_B
    v_col = k_col + H_B
    near, far = _near_buckets()
    grid = (batch, H_B, seq // tile)
    rider_specs, rider_shapes = _rider_specs(riders, grid)
    kern = _with_riders(functools.partial(_diff_kernel, lam_init=lam_init), 8, len(riders))
    grid_spec = pltpu.PrefetchScalarGridSpec(
        num_scalar_prefetch=2,
        grid=grid,
        in_specs=[pl.BlockSpec((1, tile, DV_B), lambda b, h, i, *_: (b, i, q_col + h)),
                  pl.BlockSpec((1, seq, DV_B), lambda b, h, i, *_: (b, 0, k_col + h)),
                  pl.BlockSpec((1, seq, DV_B), lambda b, h, i, *_: (b, 0, v_col + h)),
                  pl.BlockSpec((blk, 2 * blk), lambda b, h, i, *_: (0, 0)),
                  pl.BlockSpec((4, DK_B), lambda b, h, i, *_: (0, 0)),
                  pl.BlockSpec((1, DV_B), lambda b, h, i, *_: (0, 0))] + rider_specs,
        out_specs=[pl.BlockSpec((1, tile, DV_B), lambda b, h, i, *_: (b, i, h))] + rider_specs,
        scratch_shapes=[pltpu.VMEM((3, tile, tile), F32),
                        pltpu.VMEM((2, tile, blk), F32),
                        pltpu.VMEM((2, tile, blk), F32),
                        pltpu.VMEM((2, tile, DV_B), F32),
                        pltpu.VMEM((2, tile, tile), F32),
                        pltpu.VMEM((2, tile, tile), F32)])
    out = pl.pallas_call(
        kern,
        grid_spec=grid_spec,
        out_shape=[jax.ShapeDtypeStruct((batch, seq, H_B * DV_B), BF16)] + rider_shapes,
        compiler_params=_params("arbitrary", "arbitrary", "arbitrary"),
        name="diff_attention",
    )(far, rel_bias.astype(F32), p, p, p, near, lam_vecs.astype(F32), subln_g.reshape(1, DV_B),
      *riders)
    return out[0], out[1:]


def _rotary_tables(seq):
    inv_freq = ROPE_BASE ** (-jnp.arange(0, DK_C, 2, dtype=F32) / DK_C)
    ang = jnp.arange(seq, dtype=F32)[:, None] * inv_freq[None, :]
    cos = jnp.concatenate([jnp.cos(ang), jnp.cos(ang)], axis=-1)
    sin = jnp.concatenate([-jnp.sin(ang), jnp.sin(ang)], axis=-1)
    return cos, sin


def _decay_tables():
    log_g = jnp.log(1.0 - 2.0 ** (-5.0 - jnp.arange(H_C, dtype=F32)))
    idx = jnp.arange(RET_BLOCK, dtype=F32)
    chunk = jnp.arange(RET_BLOCK) // CHUNK
    visible = chunk[None, :] <= chunk[:, None]
    intra = jnp.where(visible[None],
                      jnp.exp(log_g[:, None, None] * jnp.abs(idx[:, None] - idx[None, :])), 0.0)
    q_decay = jnp.exp(log_g[:, None] * (idx + 1.0))
    k_decay = jnp.exp(log_g[:, None] * (RET_BLOCK - 1.0 - idx))
    block_decay = jnp.exp(log_g * RET_BLOCK)
    return (intra,
            jnp.broadcast_to(q_decay[:, :, None], (H_C, RET_BLOCK, DV_C)),
            jnp.broadcast_to(k_decay[:, :, None], (H_C, RET_BLOCK, DK_C)),
            jnp.broadcast_to(block_decay[:, None, None], (H_C, 1, DV_C)))


def _retention_kernel(q_ref, k_ref, v_ref, gate_ref, cos_ref, sin_ref, intra_ref, qd_ref,
                      kd_ref, bd_ref, g_ref, o_ref, state_ref):
    rows = q_ref.shape[1]

    @pl.when(pl.program_id(1) == 0)
    def _():
        state_ref[...] = jnp.zeros_like(state_ref)

    cos = cos_ref[...]
    sin = sin_ref[...]

    def rotary(t):
        return t * cos + pltpu.roll(t, DK_C // 2, 1) * sin

    for h in range(H_C):
        qk_cols = slice(h * DK_C, (h + 1) * DK_C)
        v_cols = slice(h * DV_C, (h + 1) * DV_C)
        q = (rotary(q_ref[0, :, qk_cols]) * (DK_C ** -0.5)).astype(BF16)
        k = rotary(k_ref[0, :, qk_cols])
        state = state_ref[h]
        for r in range(0, rows, RET_BLOCK):
            sl = slice(r, r + RET_BLOCK)
            qc = q[sl]
            kc = k[sl]
            vc = v_ref[0, sl, v_cols].astype(BF16)
            scores = _dot_nt(qc, kc.astype(BF16)) * intra_ref[h]
            y = _dot(scores.astype(BF16), vc) + _dot(qc, state.astype(BF16)) * qd_ref[h]
            state = bd_ref[h] * state + _dot((kc * kd_ref[h]).T.astype(BF16), vc)
            gate = gate_ref[0, sl, v_cols]
            y = _rms_rows(y, g_ref[...])
            o_ref[0, sl, v_cols] = (gate * (1.0 / (1.0 + jnp.exp(-gate))) * y).astype(o_ref.dtype)
        state_ref[h] = state


def _retention(p, ret_norm_g, batch, seq, rows=MIXER_ROWS):
    qk_width = H_C * DK_C
    v_width = H_C * DV_C
    cos, sin = _rotary_tables(seq)
    intra, q_decay, k_decay, block_decay = _decay_tables()

    def whole(a):
        return pl.BlockSpec(a.shape, lambda b, i: (0,) * a.ndim)

    return pl.pallas_call(
        _retention_kernel,
        grid=(batch, seq // rows),
        in_specs=[pl.BlockSpec((1, rows, qk_width), lambda b, i: (b, i, 0)),
                  pl.BlockSpec((1, rows, qk_width), lambda b, i: (b, i, 1)),
                  pl.BlockSpec((1, rows, v_width), lambda b, i: (b, i, 2 * qk_width // v_width)),
                  pl.BlockSpec((1, rows, v_width), lambda b, i: (b, i, 2 * qk_width // v_width + 1)),
                  pl.BlockSpec((rows, DK_C), lambda b, i: (i, 0)),
                  pl.BlockSpec((rows, DK_C), lambda b, i: (i, 0)),
                  whole(intra), whole(q_decay), whole(k_decay), whole(block_decay),
                  pl.BlockSpec((1, DV_C), lambda b, i: (0, 0))],
        out_specs=pl.BlockSpec((1, rows, v_width), lambda b, i: (b, i, 0)),
        out_shape=jax.ShapeDtypeStruct((batch, seq, v_width), BF16),
        scratch_shapes=[pltpu.VMEM((H_C, DK_C, DV_C), F32)],
        compiler_params=_params("arbitrary", "arbitrary"),
        name="retention",
    )(p, p, p, p, cos, sin, intra, q_decay, k_decay, block_decay, ret_norm_g.reshape(1, DV_C))


def _gelu_tanh(x):
    return 0.5 * x * (1.0 + jnp.tanh(math.sqrt(2.0 / math.pi) * (x + 0.044715 * (x * x * x))))


def _sgu_kernel(zu_ref, zv_ref, lg_ref, lb_ref, w_ref, b_ref, o_ref):
    rows = zu_ref.shape[0]
    width = D_D // G_D
    v = _gelu_tanh(zv_ref[...])
    mu = jnp.mean(v, axis=-1, keepdims=True)
    var = jnp.mean(jnp.square(v - mu), axis=-1, keepdims=True)
    vn = ((v - mu) * lax.rsqrt(var + EPS) * lg_ref[...] + lb_ref[...]).astype(BF16)
    pos_i = lax.broadcasted_iota(jnp.int32, (SGU_LEN, SGU_LEN), 0)
    pos_j = lax.broadcasted_iota(jnp.int32, (SGU_LEN, SGU_LEN), 1)
    mask = (pos_j // CHUNK) <= (pos_i // CHUNK)
    for g in range(G_D):
        w = jnp.where(mask, w_ref[g], 0.0).astype(BF16)
        bias = b_ref[g]
        cols = slice(g * width, (g + 1) * width)
        for r in range(rows // SGU_LEN):
            sl = slice(r * SGU_LEN, (r + 1) * SGU_LEN)
            gate = _dot(w, vn[sl, cols]) + bias
            o_ref[sl, cols] = (_gelu_tanh(zu_ref[sl, cols]) * gate).astype(o_ref.dtype)


def _sgu(p, ln_g, ln_b, w_s, b_s, rows=MIXER_ROWS):
    t = p.shape[0]
    u_col = (2 * H_C * DK_C + 2 * H_C * DV_C) // D_D
    return pl.pallas_call(
        _sgu_kernel,
        grid=(t // rows,),
        in_specs=[pl.BlockSpec((rows, D_D), lambda i: (i, u_col)),
                  pl.BlockSpec((rows, D_D), lambda i: (i, u_col + 1)),
                  pl.BlockSpec((1, D_D), lambda i: (0, 0)),
                  pl.BlockSpec((1, D_D), lambda i: (0, 0)),
                  pl.BlockSpec((G_D, SGU_LEN, SGU_LEN), lambda i: (0, 0, 0)),
                  pl.BlockSpec((G_D, SGU_LEN, 1), lambda i: (0, 0, 0))],
        out_specs=pl.BlockSpec((rows, D_D), lambda i: (i, 0)),
        out_shape=jax.ShapeDtypeStruct((t, D_D), BF16),
        compiler_params=_params("parallel"),
        name="spatial_gate",
    )(p, p, ln_g.reshape(1, D_D), ln_b.reshape(1, D_D), w_s, b_s.reshape(G_D, SGU_LEN, 1))


def kernel(x, norm_mix_g, norm_ffn_g, final_norm_g, rel_bias, ab_w_in, ab_w_out, diff_lambda,
           diff_subln_g, cd_w_in, cd_w_out, ret_norm_g, sgu_ln_g, sgu_ln_b, sgu_w, sgu_b,
           ffn_w_up, ffn_conv_w, ffn_conv_b, ffn_w_down):
    batch, seq, d = x.shape
    t = batch * seq
    depth = norm_mix_g.shape[0]
    xt = x.reshape(t, d)
    def rows(w):
        return w.reshape(-1, w.shape[-1])

    ab_w_in = ab_w_in.astype(BF16)
    for layer in range(depth):
        j = layer // 2
        if layer % 2 == 0:
            lam_init = 0.8 - 0.6 * math.exp(-0.3 * layer)
            p = _norm_matmul(xt, norm_mix_g[layer], ab_w_in, j, BF16, AB_PROJ_TILE)
            p3 = p.reshape(batch, seq, p.shape[1])
            if layer == 0:
                o_a, (up_bf,) = _stick_breaking(p3, batch, seq, (rows(ffn_w_up),))
                o_b, (down_bf, abo_bf, cdi_bf, cdo_bf) = _diff_attention(
                    p3, rel_bias, diff_lambda[j], diff_subln_g[j], lam_init, batch, seq,
                    (rows(ffn_w_down), rows(ab_w_out), rows(cd_w_in), rows(cd_w_out)))
                ffn_w_up, ffn_w_down, ab_w_out, cd_w_in, cd_w_out = (
                    b.reshape(w.shape) for b, w in ((up_bf, ffn_w_up), (down_bf, ffn_w_down),
                                                    (abo_bf, ab_w_out), (cdi_bf, cd_w_in),
                                                    (cdo_bf, cd_w_out)))
            else:
                o_a, _ = _stick_breaking(p3, batch, seq)
                o_b, _ = _diff_attention(p3, rel_bias, diff_lambda[j], diff_subln_g[j], lam_init,
                                         batch, seq)
            xt = _out_proj(o_a.reshape(t, -1), o_b.reshape(t, -1), ab_w_out, j, xt)
        else:
            p = _norm_matmul(xt, norm_mix_g[layer], cd_w_in, j, F32, CD_PROJ_TILE)
            o_c = _retention(p.reshape(batch, seq, p.shape[1]), ret_norm_g[j], batch, seq)
            o_d = _sgu(p, sgu_ln_g[j], sgu_ln_b[j], sgu_w[j], sgu_b[j])
            xt = _out_proj(o_c.reshape(t, -1), o_d, cd_w_out, j, xt)
        xt = _ffn(xt, norm_ffn_g, ffn_w_up, ffn_conv_w, ffn_conv_b, ffn_w_down, final_norm_g,
                  layer, seq=seq, final_norm=(layer == depth - 1))
    return xt.reshape(batch, seq, d)
```

```python
import functools
import math

import numpy as np
import jax
import jax.numpy as jnp
from jax import lax
from jax.experimental import pallas as pl
from jax.experimental.pallas import tpu as pltpu

F32 = jnp.float32
BF16 = jnp.bfloat16

EPS = 1e-6
CHUNK = 64
H_A = 8
DH_A = 128
H_B = 4
DK_B = 128
DV_B = 2 * DK_B
NUM_BUCKETS = 32
MAX_DISTANCE = 128
H_C = 4
DK_C = 128
DV_C = 2 * DK_C
ROPE_BASE = 10000.0
D_D = 1024
G_D = 4
SGU_LEN = 128
CONV_WIDTH = 3

SUBLANES = 8
BF16_SUBLANES = 16
ATT_BLOCK = 128
ATT_TILE = 512
STICK_HEADS = 4
RET_BLOCK = 256
VMEM_LIMIT = 62 * 1024 * 1024
PROJ_ROWS = 1024
AB_PROJ_TILE = 2048
CD_PROJ_TILE = 1280
FFN_ROWS = 1024
FFN_TILE = 512
MIXER_ROWS = 512
MASKED_LOGIT = -1e4
LOG2E = math.log2(math.e)
ZERO_WEIGHT_LOG2 = -150.0


def _params(*semantics):
    return pltpu.CompilerParams(dimension_semantics=semantics,
                                vmem_limit_bytes=VMEM_LIMIT)


def _dot(a, b):
    return jnp.dot(a, b, preferred_element_type=F32)


def _dot_nt(a, b):
    return lax.dot_general(a, b, (((1,), (1,)), ((), ())), preferred_element_type=F32)


def _rms_rows(x, g):
    return x * lax.rsqrt(jnp.mean(x * x, axis=-1, keepdims=True) + EPS) * g


def _with_riders(body, n_in, n_riders):
    def kern(*refs):
        ins = refs[:n_in]
        rider_in = refs[n_in:n_in + n_riders]
        out = refs[n_in + n_riders]
        rider_out = refs[n_in + n_riders + 1:n_in + 2 * n_riders + 1]
        scratch = refs[n_in + 2 * n_riders + 1:]
        for src, dst in zip(rider_in, rider_out):
            dst[...] = src[...].astype(BF16)
        body(*ins, out, *scratch)
    return kern


def _rider_specs(riders, grid):
    steps = math.prod(grid)

    def linear_step(*g):
        n = g[0]
        for size, idx in zip(grid[1:], g[1:len(grid)]):
            n = n * size + idx
        return n

    specs, shapes = [], []
    for w in riders:
        rows, cols = w.shape
        block_rows, rem = divmod(rows, steps)
        assert rem == 0 and block_rows % BF16_SUBLANES == 0, (w.shape, steps)
        specs.append(pl.BlockSpec((block_rows, cols), lambda *g: (linear_step(*g), 0)))
        shapes.append(jax.ShapeDtypeStruct((rows, cols), BF16))
    return specs, shapes


def _norm_matmul_kernel(x_ref, g_ref, w_ref, o_ref, h_ref):
    @pl.when(pl.program_id(1) == 0)
    def _():
        h_ref[...] = _rms_rows(x_ref[...], g_ref[...]).astype(BF16)

    o_ref[...] = _dot(h_ref[...], w_ref[...]).astype(o_ref.dtype)


def _norm_matmul(x, g, w, layer, out_dtype, tn, tm=PROJ_ROWS):
    t, d = x.shape
    n = w.shape[2]
    return pl.pallas_call(
        _norm_matmul_kernel,
        grid=(t // tm, n // tn),
        in_specs=[pl.BlockSpec((tm, d), lambda i, j: (i, 0)),
                  pl.BlockSpec((1, d), lambda i, j: (0, 0)),
                  pl.BlockSpec((None, d, tn), lambda i, j: (layer, 0, j))],
        out_specs=pl.BlockSpec((tm, tn), lambda i, j: (i, j)),
        out_shape=jax.ShapeDtypeStruct((t, n), out_dtype),
        scratch_shapes=[pltpu.VMEM((tm, d), BF16)],
        compiler_params=_params("parallel", "arbitrary"),
        name="norm_matmul",
    )(x, g.reshape(1, d), w)


def _out_proj_kernel(a1_ref, a2_ref, w_ref, x_ref, o_ref):
    half = a1_ref.shape[1]
    acc = _dot(a1_ref[...], w_ref[0:half, :])
    acc += _dot(a2_ref[...], w_ref[half:2 * half, :])
    o_ref[...] = x_ref[...] + acc


def _out_proj(a1, a2, w, layer, x, tm=MIXER_ROWS):
    t, half = a1.shape
    d = w.shape[2]
    return pl.pallas_call(
        _out_proj_kernel,
        grid=(t // tm,),
        in_specs=[pl.BlockSpec((tm, half), lambda i: (i, 0)),
                  pl.BlockSpec((tm, half), lambda i: (i, 0)),
                  pl.BlockSpec((None, 2 * half, d), lambda i: (layer, 0, 0)),
                  pl.BlockSpec((tm, d), lambda i: (i, 0))],
        out_specs=pl.BlockSpec((tm, d), lambda i: (i, 0)),
        out_shape=jax.ShapeDtypeStruct((t, d), F32),
        compiler_params=_params("parallel"),
        name="out_proj",
    )(a1, a2, w, x)


def _ffn_kernel(x_ref, g_ref, wa_ref, wg_ref, cwa_ref, cwg_ref, cba_ref, cbg_ref,
                wd_ref, wd_last_ref, fg_ref, o_ref, h_ref, act0_ref, act1_ref, carry_ref,
                *, seq_tiles, final_norm, n_hidden_tiles):
    i = pl.program_id(0)
    j = pl.program_id(1)
    nj = n_hidden_tiles
    tm = x_ref.shape[0]
    acts = (act0_ref, act1_ref)

    def up_phase(out_ref):
        h = h_ref[...]

        def gate(cg, ca):
            return (cg * (1.0 / (1.0 + jnp.exp(-cg))) * ca).astype(BF16)

        def conv(w_ref, cw_ref, cb_ref, branch):
            up = _dot(h, w_ref[...])
            w0 = cw_ref[0:1, :]
            w1 = cw_ref[1:2, :]
            w2 = cw_ref[2:3, :]
            b = cb_ref[...]
            c = b + w0 * pltpu.roll(up, 2, 0) + w1 * pltpu.roll(up, 1, 0) + w2 * up
            both = jnp.concatenate([carry_ref[j, branch], up[0:SUBLANES, :]], axis=0)
            head = slice(SUBLANES, 2 * SUBLANES)
            top = (b + w0 * pltpu.roll(both, 2, 0)[head, :]
                   + w1 * pltpu.roll(both, 1, 0)[head, :] + w2 * both[head, :])
            carry_ref[j, branch] = up[tm - SUBLANES:tm, :]
            return c, top

        ca, ta = conv(wa_ref, cwa_ref, cba_ref, 0)
        cg, tg = conv(wg_ref, cwg_ref, cbg_ref, 1)
        out_ref[...] = gate(cg, ca)
        out_ref[0:SUBLANES, :] = gate(tg, ta)

    def down_phase(in_ref):
        o_ref[...] += _dot(in_ref[...], wd_ref[...])

    @pl.when((i % seq_tiles) == 0)
    def _():
        carry_ref[j] = jnp.zeros(carry_ref.shape[1:], F32)

    @pl.when(j == 0)
    def _():
        x = x_ref[...]
        h_ref[...] = _rms_rows(x, g_ref[...]).astype(BF16)
        o_ref[...] = x
        up_phase(act0_ref)

    for parity in range(2):
        @pl.when((j > 0) & (j < nj - 1) & (j % 2 == parity))
        def _():
            up_phase(acts[parity])
            down_phase(acts[1 - parity])

    @pl.when(j == nj - 1)
    def _():
        parity = (n_hidden_tiles - 1) % 2
        up_phase(acts[parity])
        down_phase(acts[1 - parity])
        y = o_ref[...] + _dot(acts[parity][...], wd_last_ref[...])
        if final_norm:
            y = _rms_rows(y, fg_ref[...])
        o_ref[...] = y


def _ffn(x, g, w_up, conv_w, conv_b, w_down, final_g, layer, *, seq, final_norm,
         tm=FFN_ROWS, tf=FFN_TILE):
    t, d = x.shape
    f = w_down.shape[1]
    nj = f // tf
    assert nj >= 2
    kern = functools.partial(_ffn_kernel, seq_tiles=seq // tm, final_norm=final_norm,
                             n_hidden_tiles=nj)

    def down_row(j):
        return jnp.maximum(j - 1, 0)

    return pl.pallas_call(
        kern,
        grid=(t // tm, nj),
        in_specs=[pl.BlockSpec((tm, d), lambda i, j: (i, 0)),
                  pl.BlockSpec((None, 1, d), lambda i, j: (layer, 0, 0)),
                  pl.BlockSpec((None, d, tf), lambda i, j: (layer, 0, j)),
                  pl.BlockSpec((None, d, tf), lambda i, j: (layer, 0, j + nj)),
                  pl.BlockSpec((None, CONV_WIDTH, tf), lambda i, j: (layer, 0, j)),
                  pl.BlockSpec((None, CONV_WIDTH, tf), lambda i, j: (layer, 0, j + nj)),
                  pl.BlockSpec((None, 1, tf), lambda i, j: (layer, 0, j)),
                  pl.BlockSpec((None, 1, tf), lambda i, j: (layer, 0, j + nj)),
                  pl.BlockSpec((None, tf, d), lambda i, j: (layer, down_row(j), 0)),
                  pl.BlockSpec((None, tf, d), lambda i, j: (layer, nj - 1, 0)),
                  pl.BlockSpec((1, d), lambda i, j: (0, 0))],
        out_specs=pl.BlockSpec((tm, d), lambda i, j: (i, 0)),
        out_shape=jax.ShapeDtypeStruct((t, d), F32),
        scratch_shapes=[pltpu.VMEM((tm, d), BF16),
                        pltpu.VMEM((tm, tf), BF16),
                        pltpu.VMEM((tm, tf), BF16),
                        pltpu.VMEM((nj, 2, SUBLANES, tf), F32)],
        compiler_params=_params("arbitrary", "arbitrary"),
        name="conv_ffn",
    )(x, g[:, None, :], w_up, w_up, conv_w, conv_w, conv_b[:, None, :], conv_b[:, None, :],
      w_down, w_down, final_g.reshape(1, d))


def _suffix_sum_matrix():
    blk = ATT_BLOCK
    j = np.arange(blk)[:, None]
    s = np.arange(blk)[None, :]
    m = np.concatenate([(j > s).astype(np.float32), np.ones((blk, blk), np.float32)], axis=1)
    return jnp.asarray(np.concatenate([m, m], axis=0), dtype=BF16)


def _stick_kernel(q_ref, k_ref, v_ref, m_ref, o_ref, acc_ref, carry_ref):
    blk = ATT_BLOCK
    tile = q_ref.shape[1]
    half = tile // 2
    i = pl.program_id(2)
    scale2 = DH_A ** -0.5 * LOG2E
    suffix = m_ref[...]
    heads = range(STICK_HEADS)

    acc_ref[...] = jnp.zeros_like(acc_ref)
    carry_ref[...] = jnp.zeros_like(carry_ref)

    def sweep(hd, r0, r1, key_start, n_sub, diagonal=False):
        rows = slice(r0, r1)
        cols = slice(hd * DH_A, (hd + 1) * DH_A)
        k = k_ref[0, pl.ds(key_start, n_sub * blk), cols]
        v = v_ref[0, pl.ds(key_start, n_sub * blk), cols]
        z_all = _dot_nt(q_ref[0, rows, cols], k) * scale2
        carry = carry_ref[hd, rows, :]
        ws = [None] * n_sub
        for c in reversed(range(n_sub)):
            z = z_all[:, c * blk:(c + 1) * blk]
            if diagonal:
                row = lax.broadcasted_iota(jnp.int32, z.shape, 0) + r0
                col = lax.broadcasted_iota(jnp.int32, z.shape, 1) + c * blk
                z = jnp.where(col < row, z, MASKED_LOGIT)
            log_sig = jnp.minimum(z, 0.0) - jnp.log2(1.0 + jnp.exp2(-jnp.abs(z)))
            log_1m = log_sig - z
            hi = log_1m.astype(BF16)
            lo = (log_1m - hi.astype(F32)).astype(BF16)
            sums = _dot(jnp.concatenate([hi, lo], axis=1), suffix)
            ws[c] = jnp.exp2(log_sig + sums[:, :blk] + carry).astype(BF16)
            carry = carry + sums[:, blk:]
        carry_ref[hd, rows, :] = carry
        acc_ref[hd, rows, :] += _dot(jnp.concatenate(ws, axis=1), v)

    def alive(hd, r0, r1):
        return jnp.max(carry_ref[hd, r0:r1, :]) > ZERO_WEIGHT_LOG2

    tile_start = pl.multiple_of(i * tile, tile)
    n_sub = tile // blk
    for hd in heads:
        sweep(hd, 0, tile, tile_start, n_sub, diagonal=True)

    @pl.when(i > 0)
    def _():
        prev = pl.multiple_of(tile_start - tile, tile)
        for hd in heads:
            sweep(hd, 0, half, prev + half, n_sub // 2)

        for hd in heads:
            @pl.when(alive(hd, half, tile))
            def _():
                sweep(hd, half, tile, prev + half, n_sub // 2)

            @pl.when(alive(hd, 0, tile))
            def _():
                sweep(hd, 0, tile, prev, n_sub // 2)

                def more(t):
                    return jnp.logical_and(t < i, alive(hd, 0, tile))

                def body(t):
                    sweep(hd, 0, tile, pl.multiple_of(tile_start - (t + 1) * tile, tile), n_sub)
                    return t + 1

                lax.while_loop(more, body, 1)

    o_ref[0] = jnp.concatenate([acc_ref[hd] for hd in heads], axis=1).astype(o_ref.dtype)


def _stick_breaking(p, batch, seq, riders=()):
    blk = ATT_BLOCK
    tile = ATT_TILE
    width = STICK_HEADS * DH_A
    groups = H_A // STICK_HEADS
    grid = (batch, groups, seq // tile)
    rider_specs, rider_shapes = _rider_specs(riders, grid)
    out = pl.pallas_call(
        _with_riders(_stick_kernel, 4, len(riders)),
        grid=grid,
        in_specs=[pl.BlockSpec((1, tile, width), lambda b, h, i: (b, i, h)),
                  pl.BlockSpec((1, seq, width), lambda b, h, i: (b, 0, groups + h)),
                  pl.BlockSpec((1, seq, width), lambda b, h, i: (b, 0, 2 * groups + h)),
                  pl.BlockSpec((2 * blk, 2 * blk), lambda b, h, i: (0, 0))] + rider_specs,
        out_specs=[pl.BlockSpec((1, tile, width), lambda b, h, i: (b, i, h))] + rider_specs,
        out_shape=[jax.ShapeDtypeStruct((batch, seq, H_A * DH_A), BF16)] + rider_shapes,
        scratch_shapes=[pltpu.VMEM((STICK_HEADS, tile, DH_A), F32),
                        pltpu.VMEM((STICK_HEADS, tile, blk), F32)],
        compiler_params=_params("arbitrary", "arbitrary", "arbitrary"),
        name="stick_breaking",
    )(p, p, p, _suffix_sum_matrix(), *riders)
    return out[0], out[1:]


def _rel_bucket(rel):
    nb = NUM_BUCKETS // 2
    max_exact = nb // 2
    ret = jnp.where(rel > 0, nb, 0)
    n = jnp.abs(rel)
    n_f = jnp.maximum(n, 1).astype(F32)
    large = max_exact + (jnp.log(n_f / max_exact) / math.log(MAX_DISTANCE / max_exact)
                         * (nb - max_exact)).astype(jnp.int32)
    large = jnp.minimum(large, nb - 1)
    return ret + jnp.where(n < max_exact, n, large)


def _near_buckets():
    blk = ATT_BLOCK
    qpos = blk + jnp.arange(blk)
    kpos = jnp.arange(2 * blk)
    near = _rel_bucket(kpos[None, :] - qpos[:, None]).astype(jnp.int32)
    far = _rel_bucket(jnp.full((1,), -2 * blk, jnp.int32)).astype(jnp.int32)
    return near, far


def _diff_kernel(far_ref, relb_ref, q_ref, k_ref, v_ref, bucket_ref, lam_ref, g_ref, o_ref,
                 bias_ref, m_ref, l_ref, acc_ref, s0_ref, s1_ref, *, lam_init):
    blk = ATT_BLOCK
    tile = q_ref.shape[1]
    nsub = tile // blk
    h = pl.program_id(1)
    i = pl.program_id(2)
    scale2 = DK_B ** -0.5 * LOG2E

    @pl.when(i == 0)
    def _():
        bucket = bucket_ref[...]
        far = relb_ref[far_ref[0], h]
        near = jnp.zeros((blk, 2 * blk), F32)
        for b in range(NUM_BUCKETS):
            near = jnp.where(bucket == b, relb_ref[b, h], near)
        near = (near - far) * LOG2E
        qpos = blk + lax.broadcasted_iota(jnp.int32, (blk, 2 * blk), 0)
        kpos = lax.broadcasted_iota(jnp.int32, (blk, 2 * blk), 1)
        near = jnp.where((kpos // CHUNK) <= (qpos // CHUNK), near, -jnp.inf)
        bias_ref[...] = jnp.zeros_like(bias_ref)
        for a in range(nsub):
            rows = slice(a * blk, (a + 1) * blk)
            bias_ref[0, rows, a * blk:(a + 1) * blk] = near[:, blk:]
            if a >= 1:
                bias_ref[0, rows, (a - 1) * blk:a * blk] = near[:, :blk]
            if a + 1 < nsub:
                bias_ref[0, rows, (a + 1) * blk:] = jnp.full((blk, tile - (a + 1) * blk),
                                                             -jnp.inf, F32)
        bias_ref[1, 0:blk, (nsub - 1) * blk:] = near[:, :blk]

    m_ref[...] = jnp.full_like(m_ref, -jnp.inf)
    l_ref[...] = jnp.zeros_like(l_ref)
    acc_ref[...] = jnp.zeros_like(acc_ref)

    def key_tile(idx):
        return jnp.where(idx == 0, i, jnp.where(idx == 1, i - 1, idx - 2))

    def logits_into(s_ref, idx):
        j = key_tile(jnp.minimum(idx, i))
        kk = k_ref[0, pl.ds(pl.multiple_of(j * tile, tile), tile), :]
        for c in range(2):
            s_ref[c] = _dot_nt(q_ref[0, :, c * DK_B:(c + 1) * DK_B],
                               kk[:, c * DK_B:(c + 1) * DK_B])

    def consume(s_ref, idx):
        j = key_tile(idx)
        v = v_ref[0, pl.ds(pl.multiple_of(j * tile, tile), tile), :]
        bias = bias_ref[jnp.minimum(idx, 2)]
        probs = []
        for c in range(2):
            s = s_ref[c] * scale2 + bias
            subs = [s[:, u * blk:(u + 1) * blk] for u in range(nsub)]
            m_prev = m_ref[c]
            m_cur = functools.reduce(jnp.maximum, subs)
            m_new = jnp.maximum(m_prev, jnp.max(m_cur, axis=-1, keepdims=True))
            alpha = jnp.exp2(m_prev - m_new)
            ps = [jnp.exp2(u - m_new) for u in subs]
            l_ref[c] = alpha * l_ref[c] + functools.reduce(jnp.add, ps)
            m_ref[c] = m_new
            probs.append((alpha, jnp.concatenate([u.astype(BF16) for u in ps], axis=1)))
        for c in range(2):
            alpha, p = probs[c]
            acc_ref[c] = jnp.concatenate([alpha] * (DV_B // blk), axis=1) * acc_ref[c] + _dot(p, v)

    logits_into(s0_ref, 0)

    def pair(t, _):
        idx = 2 * t
        logits_into(s1_ref, idx + 1)
        consume(s0_ref, idx)
        logits_into(s0_ref, idx + 2)
        consume(s1_ref, idx + 1)
        return 0

    lax.fori_loop(0, (i + 1) // 2, pair, 0)

    @pl.when(i % 2 == 0)
    def _():
        consume(s0_ref, i)

    lv = lam_ref[...]
    lam = (jnp.exp(jnp.sum(lv[0:1] * lv[1:2], axis=-1, keepdims=True))
           - jnp.exp(jnp.sum(lv[2:3] * lv[3:4], axis=-1, keepdims=True)) + lam_init)
    l1 = jnp.sum(l_ref[0], axis=-1, keepdims=True)
    l2 = jnp.sum(l_ref[1], axis=-1, keepdims=True)
    o = acc_ref[0] / l1 - lam * (acc_ref[1] / l2)
    o_ref[0] = (_rms_rows(o, g_ref[...]) * (1.0 - lam_init)).astype(o_ref.dtype)


def _diff_attention(p, rel_bias, lam_vecs, subln_g, lam_init, batch, seq, riders=()):
    blk = ATT_BLOCK
    tile = ATT_TILE
    q_col = 3 * H_A * DH_A // DV_B
    k_col = q_col + H_B
    v_col = k_col + H_B
    near, far = _near_buckets()
    grid = (batch, H_B, seq // tile)
    rider_specs, rider_shapes = _rider_specs(riders, grid)
    kern = _with_riders(functools.partial(_diff_kernel, lam_init=lam_init), 8, len(riders))
    grid_spec = pltpu.PrefetchScalarGridSpec(
        num_scalar_prefetch=2,
        grid=grid,
        in_specs=[pl.BlockSpec((1, tile, DV_B), lambda b, h, i, *_: (b, i, q_col + h)),
                  pl.BlockSpec((1, seq, DV_B), lambda b, h, i, *_: (b, 0, k_col + h)),
                  pl.BlockSpec((1, seq, DV_B), lambda b, h, i, *_: (b, 0, v_col + h)),
                  pl.BlockSpec((blk, 2 * blk), lambda b, h, i, *_: (0, 0)),
                  pl.BlockSpec((4, DK_B), lambda b, h, i, *_: (0, 0)),
                  pl.BlockSpec((1, DV_B), lambda b, h, i, *_: (0, 0))] + rider_specs,
        out_specs=[pl.BlockSpec((1, tile, DV_B), lambda b, h, i, *_: (b, i, h))] + rider_specs,
        scratch_shapes=[pltpu.VMEM((3, tile, tile), F32),
                        pltpu.VMEM((2, tile, blk), F32),
                        pltpu.VMEM((2, tile, blk), F32),
                        pltpu.VMEM((2, tile, DV_B), F32),
                        pltpu.VMEM((2, tile, tile), F32),
                        pltpu.VMEM((2, tile, tile), F32)])
    out = pl.pallas_call(
        kern,
        grid_spec=grid_spec,
        out_shape=[jax.ShapeDtypeStruct((batch, seq, H_B * DV_B), BF16)] + rider_shapes,
        compiler_params=_params("arbitrary", "arbitrary", "arbitrary"),
        name="diff_attention",
    )(far, rel_bias.astype(F32), p, p, p, near, lam_vecs.astype(F32), subln_g.reshape(1, DV_B),
      *riders)
    return out[0], out[1:]


def _rotary_tables(seq):
    inv_freq = ROPE_BASE ** (-jnp.arange(0, DK_C, 2, dtype=F32) / DK_C)
    ang = jnp.arange(seq, dtype=F32)[:, None] * inv_freq[None, :]
    cos = jnp.concatenate([jnp.cos(ang), jnp.cos(ang)], axis=-1)
    sin = jnp.concatenate([-jnp.sin(ang), jnp.sin(ang)], axis=-1)
    return cos, sin


def _decay_tables():
    log_g = jnp.log(1.0 - 2.0 ** (-5.0 - jnp.arange(H_C, dtype=F32)))
    idx = jnp.arange(RET_BLOCK, dtype=F32)
    chunk = jnp.arange(RET_BLOCK) // CHUNK
    visible = chunk[None, :] <= chunk[:, None]
    intra = jnp.where(visible[None],
                      jnp.exp(log_g[:, None, None] * jnp.abs(idx[:, None] - idx[None, :])), 0.0)
    q_decay = jnp.exp(log_g[:, None] * (idx + 1.0))
    k_decay = jnp.exp(log_g[:, None] * (RET_BLOCK - 1.0 - idx))
    block_decay = jnp.exp(log_g * RET_BLOCK)
    return (intra,
            jnp.broadcast_to(q_decay[:, :, None], (H_C, RET_BLOCK, DV_C)),
            jnp.broadcast_to(k_decay[:, :, None], (H_C, RET_BLOCK, DK_C)),
            jnp.broadcast_to(block_decay[:, None, None], (H_C, 1, DV_C)))


def _retention_kernel(q_ref, k_ref, v_ref, gate_ref, cos_ref, sin_ref, intra_ref, qd_ref,
                      kd_ref, bd_ref, g_ref, o_ref, state_ref):
    rows = q_ref.shape[1]

    @pl.when(pl.program_id(1) == 0)
    def _():
        state_ref[...] = jnp.zeros_like(state_ref)

    cos = cos_ref[...]
    sin = sin_ref[...]

    def rotary(t):
        return t * cos + pltpu.roll(t, DK_C // 2, 1) * sin

    for h in range(H_C):
        qk_cols = slice(h * DK_C, (h + 1) * DK_C)
        v_cols = slice(h * DV_C, (h + 1) * DV_C)
        q = (rotary(q_ref[0, :, qk_cols]) * (DK_C ** -0.5)).astype(BF16)
        k = rotary(k_ref[0, :, qk_cols])
        state = state_ref[h]
        for r in range(0, rows, RET_BLOCK):
            sl = slice(r, r + RET_BLOCK)
            qc = q[sl]
            kc = k[sl]
            vc = v_ref[0, sl, v_cols].astype(BF16)
            scores = _dot_nt(qc, kc.astype(BF16)) * intra_ref[h]
            y = _dot(scores.astype(BF16), vc) + _dot(qc, state.astype(BF16)) * qd_ref[h]
            state = bd_ref[h] * state + _dot((kc * kd_ref[h]).T.astype(BF16), vc)
            gate = gate_ref[0, sl, v_cols]
            y = _rms_rows(y, g_ref[...])
            o_ref[0, sl, v_cols] = (gate * (1.0 / (1.0 + jnp.exp(-gate))) * y).astype(o_ref.dtype)
        state_ref[h] = state


def _retention(p, ret_norm_g, batch, seq, rows=MIXER_ROWS):
    qk_width = H_C * DK_C
    v_width = H_C * DV_C
    cos, sin = _rotary_tables(seq)
    intra, q_decay, k_decay, block_decay = _decay_tables()

    def whole(a):
        return pl.BlockSpec(a.shape, lambda b, i: (0,) * a.ndim)

    return pl.pallas_call(
        _retention_kernel,
        grid=(batch, seq // rows),
        in_specs=[pl.BlockSpec((1, rows, qk_width), lambda b, i: (b, i, 0)),
                  pl.BlockSpec((1, rows, qk_width), lambda b, i: (b, i, 1)),
                  pl.BlockSpec((1, rows, v_width), lambda b, i: (b, i, 2 * qk_width // v_width)),
                  pl.BlockSpec((1, rows, v_width), lambda b, i: (b, i, 2 * qk_width // v_width + 1)),
                  pl.BlockSpec((rows, DK_C), lambda b, i: (i, 0)),
                  pl.BlockSpec((rows, DK_C), lambda b, i: (i, 0)),
                  whole(intra), whole(q_decay), whole(k_decay), whole(block_decay),
                  pl.BlockSpec((1, DV_C), lambda b, i: (0, 0))],
        out_specs=pl.BlockSpec((1, rows, v_width), lambda b, i: (b, i, 0)),
        out_shape=jax.ShapeDtypeStruct((batch, seq, v_width), BF16),
        scratch_shapes=[pltpu.VMEM((H_C, DK_C, DV_C), F32)],
        compiler_params=_params("arbitrary", "arbitrary"),
        name="retention",
    )(p, p, p, p, cos, sin, intra, q_decay, k_decay, block_decay, ret_norm_g.reshape(1, DV_C))


def _gelu_tanh(x):
    return 0.5 * x * (1.0 + jnp.tanh(math.sqrt(2.0 / math.pi) * (x + 0.044715 * (x * x * x))))


def _sgu_kernel(zu_ref, zv_ref, lg_ref, lb_ref, w_ref, b_ref, o_ref):
    rows = zu_ref.shape[0]
    width = D_D // G_D
    v = _gelu_tanh(zv_ref[...])
    mu = jnp.mean(v, axis=-1, keepdims=True)
    var = jnp.mean(jnp.square(v - mu), axis=-1, keepdims=True)
    vn = ((v - mu) * lax.rsqrt(var + EPS) * lg_ref[...] + lb_ref[...]).astype(BF16)
    pos_i = lax.broadcasted_iota(jnp.int32, (SGU_LEN, SGU_LEN), 0)
    pos_j = lax.broadcasted_iota(jnp.int32, (SGU_LEN, SGU_LEN), 1)
    mask = (pos_j // CHUNK) <= (pos_i // CHUNK)
    for g in range(G_D):
        w = jnp.where(mask, w_ref[g], 0.0).astype(BF16)
        bias = b_ref[g]
        cols = slice(g * width, (g + 1) * width)
        for r in range(rows // SGU_LEN):
            sl = slice(r * SGU_LEN, (r + 1) * SGU_LEN)
            gate = _dot(w, vn[sl, cols]) + bias
            o_ref[sl, cols] = (_gelu_tanh(zu_ref[sl, cols]) * gate).astype(o_ref.dtype)


def _sgu(p, ln_g, ln_b, w_s, b_s, rows=MIXER_ROWS):
    t = p.shape[0]
    u_col = (2 * H_C * DK_C + 2 * H_C * DV_C) // D_D
    return pl.pallas_call(
        _sgu_kernel,
        grid=(t // rows,),
        in_specs=[pl.BlockSpec((rows, D_D), lambda i: (i, u_col)),
                  pl.BlockSpec((rows, D_D), lambda i: (i, u_col + 1)),
                  pl.BlockSpec((1, D_D), lambda i: (0, 0)),
                  pl.BlockSpec((1, D_D), lambda i: (0, 0)),
                  pl.BlockSpec((G_D, SGU_LEN, SGU_LEN), lambda i: (0, 0, 0)),
                  pl.BlockSpec((G_D, SGU_LEN, 1), lambda i: (0, 0, 0))],
        out_specs=pl.BlockSpec((rows, D_D), lambda i: (i, 0)),
        out_shape=jax.ShapeDtypeStruct((t, D_D), BF16),
        compiler_params=_params("parallel"),
        name="spatial_gate",
    )(p, p, ln_g.reshape(1, D_D), ln_b.reshape(1, D_D), w_s, b_s.reshape(G_D, SGU_LEN, 1))


def kernel(x, norm_mix_g, norm_ffn_g, final_norm_g, rel_bias, ab_w_in, ab_w_out, diff_lambda,
           diff_subln_g, cd_w_in, cd_w_out, ret_norm_g, sgu_ln_g, sgu_ln_b, sgu_w, sgu_b,
           ffn_w_up, ffn_conv_w, ffn_conv_b, ffn_w_down):
    batch, seq, d = x.shape
    t = batch * seq
    depth = norm_mix_g.shape[0]
    xt = x.reshape(t, d)
    def rows(w):
        return w.reshape(-1, w.shape[-1])

    ab_w_in = ab_w_in.astype(BF16)
    for layer in range(depth):
        j = layer // 2
        if layer % 2 == 0:
            lam_init = 0.8 - 0.6 * math.exp(-0.3 * layer)
            p = _norm_matmul(xt, norm_mix_g[layer], ab_w_in, j, BF16, AB_PROJ_TILE)
            p3 = p.reshape(batch, seq, p.shape[1])
            if layer == 0:
                o_a, (up_bf,) = _stick_breaking(p3, batch, seq, (rows(ffn_w_up),))
                o_b, (down_bf, abo_bf, cdi_bf, cdo_bf) = _diff_attention(
                    p3, rel_bias, diff_lambda[j], diff_subln_g[j], lam_init, batch, seq,
                    (rows(ffn_w_down), rows(ab_w_out), rows(cd_w_in), rows(cd_w_out)))
                ffn_w_up, ffn_w_down, ab_w_out, cd_w_in, cd_w_out = (
                    b.reshape(w.shape) for b, w in ((up_bf, ffn_w_up), (down_bf, ffn_w_down),
                                                    (abo_bf, ab_w_out), (cdi_bf, cd_w_in),
                                                    (cdo_bf, cd_w_out)))
            else:
                o_a, _ = _stick_breaking(p3, batch, seq)
                o_b, _ = _diff_attention(p3, rel_bias, diff_lambda[j], diff_subln_g[j], lam_init,
                                         batch, seq)
            xt = _out_proj(o_a.reshape(t, -1), o_b.reshape(t, -1), ab_w_out, j, xt)
        else:
            p = _norm_matmul(xt, norm_mix_g[layer], cd_w_in, j, F32, CD_PROJ_TILE)
            o_c = _retention(p.reshape(batch, seq, p.shape[1]), ret_norm_g[j], batch, seq)
            o_d = _sgu(p, sgu_ln_g[j], sgu_ln_b[j], sgu_w[j], sgu_b[j])
            xt = _out_proj(o_c.reshape(t, -1), o_d, cd_w_out, j, xt)
        xt = _ffn(xt, norm_ffn_g, ffn_w_up, ffn_conv_w, ffn_conv_b, ffn_w_down, final_norm_g,
                  layer, seq=seq, final_norm=(layer == depth - 1))
    return xt.reshape(batch, seq, d)
```

```python
import functools
import math

import numpy as np
import jax
import jax.numpy as jnp
from jax import lax
from jax.experimental import pallas as pl
from jax.experimental.pallas import tpu as pltpu

F32 = jnp.float32
BF16 = jnp.bfloat16

EPS = 1e-6
CHUNK = 64
H_A = 8
DH_A = 128
H_B = 4
DK_B = 128
DV_B = 2 * DK_B
NUM_BUCKETS = 32
MAX_DISTANCE = 128
H_C = 4
DK_C = 128
DV_C = 2 * DK_C
ROPE_BASE = 10000.0
D_D = 1024
G_D = 4
SGU_LEN = 128
CONV_WIDTH = 3

SUBLANES = 8
BF16_SUBLANES = 16
ATT_BLOCK = 128
ATT_TILE = 512
STICK_HEADS = 4
RET_BLOCK = 256
VMEM_LIMIT = 62 * 1024 * 1024
PROJ_ROWS = 1024
AB_PROJ_TILE = 2048
CD_PROJ_TILE = 1280
FFN_ROWS = 1024
FFN_TILE = 512
GATE_ROWS = 64
MIXER_ROWS = 512
MASKED_LOGIT = -1e4
LOG2E = math.log2(math.e)
ZERO_WEIGHT_LOG2 = -150.0


def _params(*semantics):
    return pltpu.CompilerParams(dimension_semantics=semantics,
                                vmem_limit_bytes=VMEM_LIMIT)


def _dot(a, b):
    return jnp.dot(a, b, preferred_element_type=F32)


def _dot_nt(a, b):
    return lax.dot_general(a, b, (((1,), (1,)), ((), ())), preferred_element_type=F32)


def _rms_rows(x, g):
    return x * lax.rsqrt(jnp.mean(x * x, axis=-1, keepdims=True) + EPS) * g


def _with_riders(body, n_in, n_riders):
    def kern(*refs):
        ins = refs[:n_in]
        rider_in = refs[n_in:n_in + n_riders]
        out = refs[n_in + n_riders]
        rider_out = refs[n_in + n_riders + 1:n_in + 2 * n_riders + 1]
        scratch = refs[n_in + 2 * n_riders + 1:]
        for src, dst in zip(rider_in, rider_out):
            dst[...] = src[...].astype(BF16)
        body(*ins, out, *scratch)
    return kern


def _rider_specs(riders, grid):
    steps = math.prod(grid)

    def linear_step(*g):
        n = g[0]
        for size, idx in zip(grid[1:], g[1:len(grid)]):
            n = n * size + idx
        return n

    specs, shapes = [], []
    for w in riders:
        rows, cols = w.shape
        block_rows, rem = divmod(rows, steps)
        assert rem == 0 and block_rows % BF16_SUBLANES == 0, (w.shape, steps)
        specs.append(pl.BlockSpec((block_rows, cols), lambda *g: (linear_step(*g), 0)))
        shapes.append(jax.ShapeDtypeStruct((rows, cols), BF16))
    return specs, shapes


def _norm_matmul_kernel(x_ref, g_ref, w_ref, o_ref, h_ref):
    @pl.when(pl.program_id(1) == 0)
    def _():
        h_ref[...] = _rms_rows(x_ref[...], g_ref[...]).astype(BF16)

    o_ref[...] = _dot(h_ref[...], w_ref[...]).astype(o_ref.dtype)


def _norm_matmul(x, g, w, layer, out_dtype, tn, tm=PROJ_ROWS):
    t, d = x.shape
    n = w.shape[2]
    return pl.pallas_call(
        _norm_matmul_kernel,
        grid=(t // tm, n // tn),
        in_specs=[pl.BlockSpec((tm, d), lambda i, j: (i, 0)),
                  pl.BlockSpec((1, d), lambda i, j: (0, 0)),
                  pl.BlockSpec((None, d, tn), lambda i, j: (layer, 0, j))],
        out_specs=pl.BlockSpec((tm, tn), lambda i, j: (i, j)),
        out_shape=jax.ShapeDtypeStruct((t, n), out_dtype),
        scratch_shapes=[pltpu.VMEM((tm, d), BF16)],
        compiler_params=_params("parallel", "arbitrary"),
        name="norm_matmul",
    )(x, g.reshape(1, d), w)


def _out_proj_kernel(a1_ref, a2_ref, w_ref, x_ref, o_ref):
    half = a1_ref.shape[1]
    acc = _dot(a1_ref[...], w_ref[0:half, :])
    acc += _dot(a2_ref[...], w_ref[half:2 * half, :])
    o_ref[...] = x_ref[...] + acc


def _out_proj(a1, a2, w, layer, x, tm=MIXER_ROWS):
    t, half = a1.shape
    d = w.shape[2]
    return pl.pallas_call(
        _out_proj_kernel,
        grid=(t // tm,),
        in_specs=[pl.BlockSpec((tm, half), lambda i: (i, 0)),
                  pl.BlockSpec((tm, half), lambda i: (i, 0)),
                  pl.BlockSpec((None, 2 * half, d), lambda i: (layer, 0, 0)),
                  pl.BlockSpec((tm, d), lambda i: (i, 0))],
        out_specs=pl.BlockSpec((tm, d), lambda i: (i, 0)),
        out_shape=jax.ShapeDtypeStruct((t, d), F32),
        compiler_params=_params("parallel"),
        name="out_proj",
    )(a1, a2, w, x)


def _ffn_kernel(x_ref, g_ref, wa_ref, wg_ref, cwa_ref, cwg_ref, cba_ref, cbg_ref,
                wd_ref, fg_ref, o_ref, h_ref, act0_ref, act1_ref, carry_ref, up0_ref, up1_ref,
                *, seq_tiles, final_norm, n_hidden_tiles):
    i = pl.program_id(0)
    j = pl.program_id(1)
    nj = n_hidden_tiles
    tm = x_ref.shape[0]
    ups = (up0_ref, up1_ref)
    acts = (act0_ref, act1_ref)

    def up_phase(parity):
        h = h_ref[...]
        for branch, w_ref in ((0, wa_ref), (1, wg_ref)):
            u_ref = ups[parity].at[branch]
            u_ref[0:SUBLANES, :] = carry_ref[j, branch]
            u_ref[SUBLANES:SUBLANES + tm, :] = _dot(h, w_ref[...])
            carry_ref[j, branch] = u_ref[tm:tm + SUBLANES, :]

    def gate_phase(parity):
        def conv(branch, cw_ref, cb_ref, r):
            u = ups[parity][branch, r:r + GATE_ROWS + SUBLANES, :]
            c = (cb_ref[...] + cw_ref[0:1, :] * pltpu.roll(u, 2, 0)
                 + cw_ref[1:2, :] * pltpu.roll(u, 1, 0) + cw_ref[2:3, :] * u)
            return c[SUBLANES:, :]

        for r in range(0, tm, GATE_ROWS):
            ca = conv(0, cwa_ref, cba_ref, r)
            cg = conv(1, cwg_ref, cbg_ref, r)
            acts[parity][r:r + GATE_ROWS, :] = (cg * (1.0 / (1.0 + jnp.exp(-cg))) * ca).astype(BF16)

    def down_phase(parity):
        o_ref[...] += _dot(acts[parity][...], wd_ref[...])

    @pl.when(((i % seq_tiles) == 0) & (j < nj))
    def _():
        carry_ref[j] = jnp.zeros(carry_ref.shape[1:], F32)

    @pl.when(j == 0)
    def _():
        x = x_ref[...]
        h_ref[...] = _rms_rows(x, g_ref[...]).astype(BF16)
        o_ref[...] = x
        up_phase(0)

    @pl.when(j == 1)
    def _():
        gate_phase(0)
        up_phase(1)

    for parity in range(2):
        @pl.when((j >= 2) & (j < nj) & (j % 2 == parity))
        def _():
            gate_phase(1 - parity)
            up_phase(parity)
            down_phase(parity)

    @pl.when(j == nj)
    def _():
        gate_phase((nj - 1) % 2)
        down_phase(nj % 2)

    @pl.when(j == nj + 1)
    def _():
        y = o_ref[...] + _dot(acts[(nj - 1) % 2][...], wd_ref[...])
        if final_norm:
            y = _rms_rows(y, fg_ref[...])
        o_ref[...] = y


def _ffn(x, g, w_up, conv_w, conv_b, w_down, final_g, layer, *, seq, final_norm,
         tm=FFN_ROWS, tf=FFN_TILE):
    t, d = x.shape
    f = w_down.shape[1]
    nj = f // tf
    assert nj >= 2
    kern = functools.partial(_ffn_kernel, seq_tiles=seq // tm, final_norm=final_norm,
                             n_hidden_tiles=nj)

    def tile_of(j, lag):
        return jnp.clip(j - lag, 0, nj - 1)

    return pl.pallas_call(
        kern,
        grid=(t // tm, nj + 2),
        in_specs=[pl.BlockSpec((tm, d), lambda i, j: (i, 0), pipeline_mode=pl.Buffered(1)),
                  pl.BlockSpec((None, 1, d), lambda i, j: (layer, 0, 0)),
                  pl.BlockSpec((None, d, tf), lambda i, j: (layer, 0, tile_of(j, 0))),
                  pl.BlockSpec((None, d, tf), lambda i, j: (layer, 0, tile_of(j, 0) + nj)),
                  pl.BlockSpec((None, CONV_WIDTH, tf), lambda i, j: (layer, 0, tile_of(j, 1))),
                  pl.BlockSpec((None, CONV_WIDTH, tf), lambda i, j: (layer, 0, tile_of(j, 1) + nj)),
                  pl.BlockSpec((None, 1, tf), lambda i, j: (layer, 0, tile_of(j, 1))),
                  pl.BlockSpec((None, 1, tf), lambda i, j: (layer, 0, tile_of(j, 1) + nj)),
                  pl.BlockSpec((None, tf, d), lambda i, j: (layer, tile_of(j, 2), 0)),
                  pl.BlockSpec((1, d), lambda i, j: (0, 0))],
        out_specs=pl.BlockSpec((tm, d), lambda i, j: (i, 0)),
        out_shape=jax.ShapeDtypeStruct((t, d), F32),
        scratch_shapes=[pltpu.VMEM((tm, d), BF16),
                        pltpu.VMEM((tm, tf), BF16),
                        pltpu.VMEM((tm, tf), BF16),
                        pltpu.VMEM((nj, 2, SUBLANES, tf), F32),
                        pltpu.VMEM((2, tm + SUBLANES, tf), F32),
                        pltpu.VMEM((2, tm + SUBLANES, tf), F32)],
        compiler_params=_params("arbitrary", "arbitrary"),
        name="conv_ffn",
    )(x, g[:, None, :], w_up, w_up, conv_w, conv_w, conv_b[:, None, :], conv_b[:, None, :],
      w_down, final_g.reshape(1, d))


def _suffix_sum_matrix():
    blk = ATT_BLOCK
    j = np.arange(blk)[:, None]
    s = np.arange(blk)[None, :]
    m = np.concatenate([(j > s).astype(np.float32), np.ones((blk, blk), np.float32)], axis=1)
    return jnp.asarray(np.concatenate([m, m], axis=0), dtype=BF16)


def _stick_kernel(q_ref, k_ref, v_ref, m_ref, o_ref, acc_ref, carry_ref):
    blk = ATT_BLOCK
    tile = q_ref.shape[1]
    half = tile // 2
    i = pl.program_id(2)
    scale2 = DH_A ** -0.5 * LOG2E
    suffix = m_ref[...]
    heads = range(STICK_HEADS)

    acc_ref[...] = jnp.zeros_like(acc_ref)
    carry_ref[...] = jnp.zeros_like(carry_ref)

    def sweep(hd, r0, r1, key_start, n_sub, diagonal=False):
        rows = slice(r0, r1)
        cols = slice(hd * DH_A, (hd + 1) * DH_A)
        k = k_ref[0, pl.ds(key_start, n_sub * blk), cols]
        v = v_ref[0, pl.ds(key_start, n_sub * blk), cols]
        z_all = _dot_nt(q_ref[0, rows, cols], k) * scale2
        carry = carry_ref[hd, rows, :]
        ws = [None] * n_sub
        for c in reversed(range(n_sub)):
            z = z_all[:, c * blk:(c + 1) * blk]
            if diagonal:
                row = lax.broadcasted_iota(jnp.int32, z.shape, 0) + r0
                col = lax.broadcasted_iota(jnp.int32, z.shape, 1) + c * blk
                z = jnp.where(col < row, z, MASKED_LOGIT)
            log_sig = jnp.minimum(z, 0.0) - jnp.log2(1.0 + jnp.exp2(-jnp.abs(z)))
            log_1m = log_sig - z
            hi = log_1m.astype(BF16)
            lo = (log_1m - hi.astype(F32)).astype(BF16)
            sums = _dot(jnp.concatenate([hi, lo], axis=1), suffix)
            ws[c] = jnp.exp2(log_sig + sums[:, :blk] + carry).astype(BF16)
            carry = carry + sums[:, blk:]
        carry_ref[hd, rows, :] = carry
        acc_ref[hd, rows, :] += _dot(jnp.concatenate(ws, axis=1), v)

    def alive(hd, r0, r1):
        return jnp.max(carry_ref[hd, r0:r1, :]) > ZERO_WEIGHT_LOG2

    tile_start = pl.multiple_of(i * tile, tile)
    n_sub = tile // blk
    for hd in heads:
        sweep(hd, 0, tile, tile_start, n_sub, diagonal=True)

    @pl.when(i > 0)
    def _():
        prev = pl.multiple_of(tile_start - tile, tile)
        for hd in heads:
            sweep(hd, 0, half, prev + half, n_sub // 2)

        for hd in heads:
            @pl.when(alive(hd, half, tile))
            def _():
                sweep(hd, half, tile, prev + half, n_sub // 2)

            @pl.when(alive(hd, 0, tile))
            def _():
                sweep(hd, 0, tile, prev, n_sub // 2)

                def more(t):
                    return jnp.logical_and(t < i, alive(hd, 0, tile))

                def body(t):
                    sweep(hd, 0, tile, pl.multiple_of(tile_start - (t + 1) * tile, tile), n_sub)
                    return t + 1

                lax.while_loop(more, body, 1)

    o_ref[0] = jnp.concatenate([acc_ref[hd] for hd in heads], axis=1).astype(o_ref.dtype)


def _stick_breaking(p, batch, seq, riders=()):
    blk = ATT_BLOCK
    tile = ATT_TILE
    width = STICK_HEADS * DH_A
    groups = H_A // STICK_HEADS
    grid = (batch, groups, seq // tile)
    rider_specs, rider_shapes = _rider_specs(riders, grid)
    out = pl.pallas_call(
        _with_riders(_stick_kernel, 4, len(riders)),
        grid=grid,
        in_specs=[pl.BlockSpec((1, tile, width), lambda b, h, i: (b, i, h)),
                  pl.BlockSpec((1, seq, width), lambda b, h, i: (b, 0, groups + h)),
                  pl.BlockSpec((1, seq, width), lambda b, h, i: (b, 0, 2 * groups + h)),
                  pl.BlockSpec((2 * blk, 2 * blk), lambda b, h, i: (0, 0))] + rider_specs,
        out_specs=[pl.BlockSpec((1, tile, width), lambda b, h, i: (b, i, h))] + rider_specs,
        out_shape=[jax.ShapeDtypeStruct((batch, seq, H_A * DH_A), BF16)] + rider_shapes,
        scratch_shapes=[pltpu.VMEM((STICK_HEADS, tile, DH_A), F32),
                        pltpu.VMEM((STICK_HEADS, tile, blk), F32)],
        compiler_params=_params("arbitrary", "arbitrary", "arbitrary"),
        name="stick_breaking",
    )(p, p, p, _suffix_sum_matrix(), *riders)
    return out[0], out[1:]


def _rel_bucket(rel):
    nb = NUM_BUCKETS // 2
    max_exact = nb // 2
    ret = jnp.where(rel > 0, nb, 0)
    n = jnp.abs(rel)
    n_f = jnp.maximum(n, 1).astype(F32)
    large = max_exact + (jnp.log(n_f / max_exact) / math.log(MAX_DISTANCE / max_exact)
                         * (nb - max_exact)).astype(jnp.int32)
    large = jnp.minimum(large, nb - 1)
    return ret + jnp.where(n < max_exact, n, large)


def _near_buckets():
    blk = ATT_BLOCK
    qpos = blk + jnp.arange(blk)
    kpos = jnp.arange(2 * blk)
    near = _rel_bucket(kpos[None, :] - qpos[:, None]).astype(jnp.int32)
    far = _rel_bucket(jnp.full((1,), -2 * blk, jnp.int32)).astype(jnp.int32)
    return near, far


def _diff_kernel(far_ref, relb_ref, q_ref, k_ref, v_ref, bucket_ref, lam_ref, g_ref, o_ref,
                 bias_ref, m_ref, l_ref, acc_ref, s0_ref, s1_ref, *, lam_init):
    blk = ATT_BLOCK
    tile = q_ref.shape[1]
    nsub = tile // blk
    h = pl.program_id(1)
    i = pl.program_id(2)
    scale2 = DK_B ** -0.5 * LOG2E

    @pl.when(i == 0)
    def _():
        bucket = bucket_ref[...]
        far = relb_ref[far_ref[0], h]
        near = jnp.zeros((blk, 2 * blk), F32)
        for b in range(NUM_BUCKETS):
            near = jnp.where(bucket == b, relb_ref[b, h], near)
        near = (near - far) * LOG2E
        qpos = blk + lax.broadcasted_iota(jnp.int32, (blk, 2 * blk), 0)
        kpos = lax.broadcasted_iota(jnp.int32, (blk, 2 * blk), 1)
        near = jnp.where((kpos // CHUNK) <= (qpos // CHUNK), near, -jnp.inf)
        bias_ref[...] = jnp.zeros_like(bias_ref)
        for a in range(nsub):
            rows = slice(a * blk, (a + 1) * blk)
            bias_ref[0, rows, a * blk:(a + 1) * blk] = near[:, blk:]
            if a >= 1:
                bias_ref[0, rows, (a - 1) * blk:a * blk] = near[:, :blk]
            if a + 1 < nsub:
                bias_ref[0, rows, (a + 1) * blk:] = jnp.full((blk, tile - (a + 1) * blk),
                                                             -jnp.inf, F32)
        bias_ref[1, 0:blk, (nsub - 1) * blk:] = near[:, :blk]

    m_ref[...] = jnp.full_like(m_ref, -jnp.inf)
    l_ref[...] = jnp.zeros_like(l_ref)
    acc_ref[...] = jnp.zeros_like(acc_ref)

    def key_tile(idx):
        return jnp.where(idx == 0, i, jnp.where(idx == 1, i - 1, idx - 2))

    def logits_into(s_ref, idx):
        j = key_tile(jnp.minimum(idx, i))
        kk = k_ref[0, pl.ds(pl.multiple_of(j * tile, tile), tile), :]
        for c in range(2):
            s_ref[c] = _dot_nt(q_ref[0, :, c * DK_B:(c + 1) * DK_B],
                               kk[:, c * DK_B:(c + 1) * DK_B])

    def consume(s_ref, idx):
        j = key_tile(idx)
        v = v_ref[0, pl.ds(pl.multiple_of(j * tile, tile), tile), :]
        bias = bias_ref[jnp.minimum(idx, 2)]
        probs = []
        for c in range(2):
            s = s_ref[c] * scale2 + bias
            subs = [s[:, u * blk:(u + 1) * blk] for u in range(nsub)]
            m_prev = m_ref[c]
            m_cur = functools.reduce(jnp.maximum, subs)
            m_new = jnp.maximum(m_prev, jnp.max(m_cur, axis=-1, keepdims=True))
            alpha = jnp.exp2(m_prev - m_new)
            ps = [jnp.exp2(u - m_new) for u in subs]
            l_ref[c] = alpha * l_ref[c] + functools.reduce(jnp.add, ps)
            m_ref[c] = m_new
            probs.append((alpha, jnp.concatenate([u.astype(BF16) for u in ps], axis=1)))
        for c in range(2):
            alpha, p = probs[c]
            acc_ref[c] = jnp.concatenate([alpha] * (DV_B // blk), axis=1) * acc_ref[c] + _dot(p, v)

    logits_into(s0_ref, 0)

    def pair(t, _):
        idx = 2 * t
        logits_into(s1_ref, idx + 1)
        consume(s0_ref, idx)
        logits_into(s0_ref, idx + 2)
        consume(s1_ref, idx + 1)
        return 0

    lax.fori_loop(0, (i + 1) // 2, pair, 0)

    @pl.when(i % 2 == 0)
    def _():
        consume(s0_ref, i)

    lv = lam_ref[...]
    lam = (jnp.exp(jnp.sum(lv[0:1] * lv[1:2], axis=-1, keepdims=True))
           - jnp.exp(jnp.sum(lv[2:3] * lv[3:4], axis=-1, keepdims=True)) + lam_init)
    l1 = jnp.sum(l_ref[0], axis=-1, keepdims=True)
    l2 = jnp.sum(l_ref[1], axis=-1, keepdims=True)
    o = acc_ref[0] / l1 - lam * (acc_ref[1] / l2)
    o_ref[0] = (_rms_rows(o, g_ref[...]) * (1.0 - lam_init)).astype(o_ref.dtype)


def _diff_attention(p, rel_bias, lam_vecs, subln_g, lam_init, batch, seq, riders=()):
    blk = ATT_BLOCK
    tile = ATT_TILE
    q_col = 3 * H_A * DH_A // DV_B
    k_col = q_col + H_B
    v_col = k_col + H_B
    near, far = _near_buckets()
    grid = (batch, H_B, seq // tile)
    rider_specs, rider_shapes = _rider_specs(riders, grid)
    kern = _with_riders(functools.partial(_diff_kernel, lam_init=lam_init), 8, len(riders))
    grid_spec = pltpu.PrefetchScalarGridSpec(
        num_scalar_prefetch=2,
        grid=grid,
        in_specs=[pl.BlockSpec((1, tile, DV_B), lambda b, h, i, *_: (b, i, q_col + h)),
                  pl.BlockSpec((1, seq, DV_B), lambda b, h, i, *_: (b, 0, k_col + h)),
                  pl.BlockSpec((1, seq, DV_B), lambda b, h, i, *_: (b, 0, v_col + h)),
                  pl.BlockSpec((blk, 2 * blk), lambda b, h, i, *_: (0, 0)),
                  pl.BlockSpec((4, DK_B), lambda b, h, i, *_: (0, 0)),
                  pl.BlockSpec((1, DV_B), lambda b, h, i, *_: (0, 0))] + rider_specs,
        out_specs=[pl.BlockSpec((1, tile, DV_B), lambda b, h, i, *_: (b, i, h))] + rider_specs,
        scratch_shapes=[pltpu.VMEM((3, tile, tile), F32),
                        pltpu.VMEM((2, tile, blk), F32),
                        pltpu.VMEM((2, tile, blk), F32),
                        pltpu.VMEM((2, tile, DV_B), F32),
                        pltpu.VMEM((2, tile, tile), F32),
                        pltpu.VMEM((2, tile, tile), F32)])
    out = pl.pallas_call(
        kern,
        grid_spec=grid_spec,
        out_shape=[jax.ShapeDtypeStruct((batch, seq, H_B * DV_B), BF16)] + rider_shapes,
        compiler_params=_params("arbitrary", "arbitrary", "arbitrary"),
        name="diff_attention",
    )(far, rel_bias.astype(F32), p, p, p, near, lam_vecs.astype(F32), subln_g.reshape(1, DV_B),
      *riders)
    return out[0], out[1:]


def _rotary_tables(seq):
    inv_freq = ROPE_BASE ** (-jnp.arange(0, DK_C, 2, dtype=F32) / DK_C)
    ang = jnp.arange(seq, dtype=F32)[:, None] * inv_freq[None, :]
    cos = jnp.concatenate([jnp.cos(ang), jnp.cos(ang)], axis=-1)
    sin = jnp.concatenate([-jnp.sin(ang), jnp.sin(ang)], axis=-1)
    return cos, sin


def _decay_tables():
    log_g = jnp.log(1.0 - 2.0 ** (-5.0 - jnp.arange(H_C, dtype=F32)))
    idx = jnp.arange(RET_BLOCK, dtype=F32)
    chunk = jnp.arange(RET_BLOCK) // CHUNK
    visible = chunk[None, :] <= chunk[:, None]
    intra = jnp.where(visible[None],
                      jnp.exp(log_g[:, None, None] * jnp.abs(idx[:, None] - idx[None, :])), 0.0)
    q_decay = jnp.exp(log_g[:, None] * (idx + 1.0))
    k_decay = jnp.exp(log_g[:, None] * (RET_BLOCK - 1.0 - idx))
    block_decay = jnp.exp(log_g * RET_BLOCK)
    return (intra,
            jnp.broadcast_to(q_decay[:, :, None], (H_C, RET_BLOCK, DV_C)),
            jnp.broadcast_to(k_decay[:, :, None], (H_C, RET_BLOCK, DK_C)),
            jnp.broadcast_to(block_decay[:, None, None], (H_C, 1, DV_C)))


def _retention_kernel(q_ref, k_ref, v_ref, gate_ref, cos_ref, sin_ref, intra_ref, qd_ref,
                      kd_ref, bd_ref, g_ref, o_ref, state_ref):
    rows = q_ref.shape[1]

    @pl.when(pl.program_id(1) == 0)
    def _():
        state_ref[...] = jnp.zeros_like(state_ref)

    cos = cos_ref[...]
    sin = sin_ref[...]

    def rotary(t):
        return t * cos + pltpu.roll(t, DK_C // 2, 1) * sin

    for h in range(H_C):
        qk_cols = slice(h * DK_C, (h + 1) * DK_C)
        v_cols = slice(h * DV_C, (h + 1) * DV_C)
        q = (rotary(q_ref[0, :, qk_cols]) * (DK_C ** -0.5)).astype(BF16)
        k = rotary(k_ref[0, :, qk_cols])
        state = state_ref[h]
        for r in range(0, rows, RET_BLOCK):
            sl = slice(r, r + RET_BLOCK)
            qc = q[sl]
            kc = k[sl]
            vc = v_ref[0, sl, v_cols].astype(BF16)
            scores = _dot_nt(qc, kc.astype(BF16)) * intra_ref[h]
            y = _dot(scores.astype(BF16), vc) + _dot(qc, state.astype(BF16)) * qd_ref[h]
            state = bd_ref[h] * state + _dot((kc * kd_ref[h]).T.astype(BF16), vc)
            gate = gate_ref[0, sl, v_cols]
            y = _rms_rows(y, g_ref[...])
            o_ref[0, sl, v_cols] = (gate * (1.0 / (1.0 + jnp.exp(-gate))) * y).astype(o_ref.dtype)
        state_ref[h] = state


def _retention(p, ret_norm_g, batch, seq, rows=MIXER_ROWS):
    qk_width = H_C * DK_C
    v_width = H_C * DV_C
    cos, sin = _rotary_tables(seq)
    intra, q_decay, k_decay, block_decay = _decay_tables()

    def whole(a):
        return pl.BlockSpec(a.shape, lambda b, i: (0,) * a.ndim)

    return pl.pallas_call(
        _retention_kernel,
        grid=(batch, seq // rows),
        in_specs=[pl.BlockSpec((1, rows, qk_width), lambda b, i: (b, i, 0)),
                  pl.BlockSpec((1, rows, qk_width), lambda b, i: (b, i, 1)),
                  pl.BlockSpec((1, rows, v_width), lambda b, i: (b, i, 2 * qk_width // v_width)),
                  pl.BlockSpec((1, rows, v_width), lambda b, i: (b, i, 2 * qk_width // v_width + 1)),
                  pl.BlockSpec((rows, DK_C), lambda b, i: (i, 0)),
                  pl.BlockSpec((rows, DK_C), lambda b, i: (i, 0)),
                  whole(intra), whole(q_decay), whole(k_decay), whole(block_decay),
                  pl.BlockSpec((1, DV_C), lambda b, i: (0, 0))],
        out_specs=pl.BlockSpec((1, rows, v_width), lambda b, i: (b, i, 0)),
        out_shape=jax.ShapeDtypeStruct((batch, seq, v_width), BF16),
        scratch_shapes=[pltpu.VMEM((H_C, DK_C, DV_C), F32)],
        compiler_params=_params("arbitrary", "arbitrary"),
        name="retention",
    )(p, p, p, p, cos, sin, intra, q_decay, k_decay, block_decay, ret_norm_g.reshape(1, DV_C))


def _gelu_tanh(x):
    return 0.5 * x * (1.0 + jnp.tanh(math.sqrt(2.0 / math.pi) * (x + 0.044715 * (x * x * x))))


def _sgu_kernel(zu_ref, zv_ref, lg_ref, lb_ref, w_ref, b_ref, o_ref):
    rows = zu_ref.shape[0]
    width = D_D // G_D
    v = _gelu_tanh(zv_ref[...])
    mu = jnp.mean(v, axis=-1, keepdims=True)
    var = jnp.mean(jnp.square(v - mu), axis=-1, keepdims=True)
    vn = ((v - mu) * lax.rsqrt(var + EPS) * lg_ref[...] + lb_ref[...]).astype(BF16)
    pos_i = lax.broadcasted_iota(jnp.int32, (SGU_LEN, SGU_LEN), 0)
    pos_j = lax.broadcasted_iota(jnp.int32, (SGU_LEN, SGU_LEN), 1)
    mask = (pos_j // CHUNK) <= (pos_i // CHUNK)
    for g in range(G_D):
        w = jnp.where(mask, w_ref[g], 0.0).astype(BF16)
        bias = b_ref[g]
        cols = slice(g * width, (g + 1) * width)
        for r in range(rows // SGU_LEN):
            sl = slice(r * SGU_LEN, (r + 1) * SGU_LEN)
            gate = _dot(w, vn[sl, cols]) + bias
            o_ref[sl, cols] = (_gelu_tanh(zu_ref[sl, cols]) * gate).astype(o_ref.dtype)


def _sgu(p, ln_g, ln_b, w_s, b_s, rows=MIXER_ROWS):
    t = p.shape[0]
    u_col = (2 * H_C * DK_C + 2 * H_C * DV_C) // D_D
    return pl.pallas_call(
        _sgu_kernel,
        grid=(t // rows,),
        in_specs=[pl.BlockSpec((rows, D_D), lambda i: (i, u_col)),
                  pl.BlockSpec((rows, D_D), lambda i: (i, u_col + 1)),
                  pl.BlockSpec((1, D_D), lambda i: (0, 0)),
                  pl.BlockSpec((1, D_D), lambda i: (0, 0)),
                  pl.BlockSpec((G_D, SGU_LEN, SGU_LEN), lambda i: (0, 0, 0)),
                  pl.BlockSpec((G_D, SGU_LEN, 1), lambda i: (0, 0, 0))],
        out_specs=pl.BlockSpec((rows, D_D), lambda i: (i, 0)),
        out_shape=jax.ShapeDtypeStruct((t, D_D), BF16),
        compiler_params=_params("parallel"),
        name="spatial_gate",
    )(p, p, ln_g.reshape(1, D_D), ln_b.reshape(1, D_D), w_s, b_s.reshape(G_D, SGU_LEN, 1))


def kernel(x, norm_mix_g, norm_ffn_g, final_norm_g, rel_bias, ab_w_in, ab_w_out, diff_lambda,
           diff_subln_g, cd_w_in, cd_w_out, ret_norm_g, sgu_ln_g, sgu_ln_b, sgu_w, sgu_b,
           ffn_w_up, ffn_conv_w, ffn_conv_b, ffn_w_down):
    batch, seq, d = x.shape
    t = batch * seq
    depth = norm_mix_g.shape[0]
    xt = x.reshape(t, d)
    def rows(w):
        return w.reshape(-1, w.shape[-1])

    ab_w_in = ab_w_in.astype(BF16)
    for layer in range(depth):
        j = layer // 2
        if layer % 2 == 0:
            lam_init = 0.8 - 0.6 * math.exp(-0.3 * layer)
            p = _norm_matmul(xt, norm_mix_g[layer], ab_w_in, j, BF16, AB_PROJ_TILE)
            p3 = p.reshape(batch, seq, p.shape[1])
            if layer == 0:
                o_a, (up_bf,) = _stick_breaking(p3, batch, seq, (rows(ffn_w_up),))
                o_b, (down_bf, abo_bf, cdi_bf, cdo_bf) = _diff_attention(
                    p3, rel_bias, diff_lambda[j], diff_subln_g[j], lam_init, batch, seq,
                    (rows(ffn_w_down), rows(ab_w_out), rows(cd_w_in), rows(cd_w_out)))
                ffn_w_up, ffn_w_down, ab_w_out, cd_w_in, cd_w_out = (
                    b.reshape(w.shape) for b, w in ((up_bf, ffn_w_up), (down_bf, ffn_w_down),
                                                    (abo_bf, ab_w_out), (cdi_bf, cd_w_in),
                                                    (cdo_bf, cd_w_out)))
            else:
                o_a, _ = _stick_breaking(p3, batch, seq)
                o_b, _ = _diff_attention(p3, rel_bias, diff_lambda[j], diff_subln_g[j], lam_init,
                                         batch, seq)
            xt = _out_proj(o_a.reshape(t, -1), o_b.reshape(t, -1), ab_w_out, j, xt)
        else:
            p = _norm_matmul(xt, norm_mix_g[layer], cd_w_in, j, F32, CD_PROJ_TILE)
            o_c = _retention(p.reshape(batch, seq, p.shape[1]), ret_norm_g[j], batch, seq)
            o_d = _sgu(p, sgu_ln_g[j], sgu_ln_b[j], sgu_w[j], sgu_b[j])
            xt = _out_proj(o_c.reshape(t, -1), o_d, cd_w_out, j, xt)
        xt = _ffn(xt, norm_ffn_g, ffn_w_up, ffn_conv_w, ffn_conv_b, ffn_w_down, final_norm_g,
                  layer, seq=seq, final_norm=(layer == depth - 1))
    return xt.reshape(batch, seq, d)
```

```python
import functools
import math

import numpy as np
import jax
import jax.numpy as jnp
from jax import lax
from jax.experimental import pallas as pl
from jax.experimental.pallas import tpu as pltpu

F32 = jnp.float32
BF16 = jnp.bfloat16

EPS = 1e-6
CHUNK = 64
H_A = 8
DH_A = 128
H_B = 4
DK_B = 128
DV_B = 2 * DK_B
NUM_BUCKETS = 32
MAX_DISTANCE = 128
H_C = 4
DK_C = 128
DV_C = 2 * DK_C
ROPE_BASE = 10000.0
D_D = 1024
G_D = 4
SGU_LEN = 128
CONV_WIDTH = 3

SUBLANES = 8
BF16_SUBLANES = 16
ATT_BLOCK = 128
ATT_TILE = 512
STICK_HEADS = 4
RET_BLOCK = 256
VMEM_LIMIT = 62 * 1024 * 1024
PROJ_ROWS = 1024
AB_PROJ_TILE = 2048
CD_PROJ_TILE = 1280
FFN_ROWS = 1024
FFN_TILE = 512
MIXER_ROWS = 512
MASKED_LOGIT = -1e4
LOG2E = math.log2(math.e)
ZERO_WEIGHT_LOG2 = -150.0


def _params(*semantics):
    return pltpu.CompilerParams(dimension_semantics=semantics,
                                vmem_limit_bytes=VMEM_LIMIT)


def _dot(a, b):
    return jnp.dot(a, b, preferred_element_type=F32)


def _dot_nt(a, b):
    return lax.dot_general(a, b, (((1,), (1,)), ((), ())), preferred_element_type=F32)


def _rms_rows(x, g):
    return x * lax.rsqrt(jnp.mean(x * x, axis=-1, keepdims=True) + EPS) * g


def _with_riders(body, n_in, n_riders):
    def kern(*refs):
        ins = refs[:n_in]
        rider_in = refs[n_in:n_in + n_riders]
        out = refs[n_in + n_riders]
        rider_out = refs[n_in + n_riders + 1:n_in + 2 * n_riders + 1]
        scratch = refs[n_in + 2 * n_riders + 1:]
        for src, dst in zip(rider_in, rider_out):
            dst[...] = src[...].astype(BF16)
        body(*ins, out, *scratch)
    return kern


def _rider_specs(riders, grid):
    steps = math.prod(grid)

    def linear_step(*g):
        n = g[0]
        for size, idx in zip(grid[1:], g[1:len(grid)]):
            n = n * size + idx
        return n

    specs, shapes = [], []
    for w in riders:
        rows, cols = w.shape
        block_rows, rem = divmod(rows, steps)
        assert rem == 0 and block_rows % BF16_SUBLANES == 0, (w.shape, steps)
        specs.append(pl.BlockSpec((block_rows, cols), lambda *g: (linear_step(*g), 0)))
        shapes.append(jax.ShapeDtypeStruct((rows, cols), BF16))
    return specs, shapes


def _norm_matmul_kernel(x_ref, g_ref, w_ref, o_ref, h_ref):
    @pl.when(pl.program_id(1) == 0)
    def _():
        h_ref[...] = _rms_rows(x_ref[...], g_ref[...]).astype(BF16)

    o_ref[...] = _dot(h_ref[...], w_ref[...]).astype(o_ref.dtype)


def _norm_matmul(x, g, w, layer, out_dtype, tn, tm=PROJ_ROWS):
    t, d = x.shape
    n = w.shape[2]
    return pl.pallas_call(
        _norm_matmul_kernel,
        grid=(t // tm, n // tn),
        in_specs=[pl.BlockSpec((tm, d), lambda i, j: (i, 0)),
                  pl.BlockSpec((1, d), lambda i, j: (0, 0)),
                  pl.BlockSpec((None, d, tn), lambda i, j: (layer, 0, j))],
        out_specs=pl.BlockSpec((tm, tn), lambda i, j: (i, j)),
        out_shape=jax.ShapeDtypeStruct((t, n), out_dtype),
        scratch_shapes=[pltpu.VMEM((tm, d), BF16)],
        compiler_params=_params("parallel", "arbitrary"),
        name="norm_matmul",
    )(x, g.reshape(1, d), w)


def _out_proj_kernel(a1_ref, a2_ref, w_ref, x_ref, o_ref):
    half = a1_ref.shape[1]
    acc = _dot(a1_ref[...], w_ref[0:half, :])
    acc += _dot(a2_ref[...], w_ref[half:2 * half, :])
    o_ref[...] = x_ref[...] + acc


def _out_proj(a1, a2, w, layer, x, tm=MIXER_ROWS):
    t, half = a1.shape
    d = w.shape[2]
    return pl.pallas_call(
        _out_proj_kernel,
        grid=(t // tm,),
        in_specs=[pl.BlockSpec((tm, half), lambda i: (i, 0)),
                  pl.BlockSpec((tm, half), lambda i: (i, 0)),
                  pl.BlockSpec((None, 2 * half, d), lambda i: (layer, 0, 0)),
                  pl.BlockSpec((tm, d), lambda i: (i, 0))],
        out_specs=pl.BlockSpec((tm, d), lambda i: (i, 0)),
        out_shape=jax.ShapeDtypeStruct((t, d), F32),
        compiler_params=_params("parallel"),
        name="out_proj",
    )(a1, a2, w, x)


def _ffn_kernel(x_ref, g_ref, wa_ref, wg_ref, cwa_ref, cwg_ref, cba_ref, cbg_ref,
                wd_ref, wd_last_ref, fg_ref, o_ref, h_ref, act0_ref, act1_ref, carry_ref,
                *, seq_tiles, final_norm, n_hidden_tiles):
    i = pl.program_id(0)
    j = pl.program_id(1)
    nj = n_hidden_tiles
    tm = x_ref.shape[0]
    acts = (act0_ref, act1_ref)

    def up_phase(out_ref):
        h = h_ref[...]

        def gate(cg, ca):
            return (cg * (1.0 / (1.0 + jnp.exp(-cg))) * ca).astype(BF16)

        def conv(w_ref, cw_ref, cb_ref, branch):
            up = _dot(h, w_ref[...])
            w0 = cw_ref[0:1, :]
            w1 = cw_ref[1:2, :]
            w2 = cw_ref[2:3, :]
            b = cb_ref[...]
            c = b + w0 * pltpu.roll(up, 2, 0) + w1 * pltpu.roll(up, 1, 0) + w2 * up
            both = jnp.concatenate([carry_ref[j, branch], up[0:SUBLANES, :]], axis=0)
            head = slice(SUBLANES, 2 * SUBLANES)
            top = (b + w0 * pltpu.roll(both, 2, 0)[head, :]
                   + w1 * pltpu.roll(both, 1, 0)[head, :] + w2 * both[head, :])
            carry_ref[j, branch] = up[tm - SUBLANES:tm, :]
            return c, top

        ca, ta = conv(wa_ref, cwa_ref, cba_ref, 0)
        cg, tg = conv(wg_ref, cwg_ref, cbg_ref, 1)
        out_ref[...] = gate(cg, ca)
        out_ref[0:SUBLANES, :] = gate(tg, ta)

    def down_phase(in_ref):
        o_ref[...] += _dot(in_ref[...], wd_ref[...])

    @pl.when((i % seq_tiles) == 0)
    def _():
        carry_ref[j] = jnp.zeros(carry_ref.shape[1:], F32)

    @pl.when(j == 0)
    def _():
        x = x_ref[...]
        h_ref[...] = _rms_rows(x, g_ref[...]).astype(BF16)
        o_ref[...] = x
        up_phase(act0_ref)

    for parity in range(2):
        @pl.when((j > 0) & (j < nj - 1) & (j % 2 == parity))
        def _():
            up_phase(acts[parity])
            down_phase(acts[1 - parity])

    @pl.when(j == nj - 1)
    def _():
        parity = (n_hidden_tiles - 1) % 2
        up_phase(acts[parity])
        down_phase(acts[1 - parity])
        y = o_ref[...] + _dot(acts[parity][...], wd_last_ref[...])
        if final_norm:
            y = _rms_rows(y, fg_ref[...])
        o_ref[...] = y


def _ffn(x, g, w_up, conv_w, conv_b, w_down, final_g, layer, *, seq, final_norm,
         tm=FFN_ROWS, tf=FFN_TILE):
    t, d = x.shape
    f = w_down.shape[1]
    nj = f // tf
    assert nj >= 2
    kern = functools.partial(_ffn_kernel, seq_tiles=seq // tm, final_norm=final_norm,
                             n_hidden_tiles=nj)

    def down_row(j):
        return jnp.maximum(j - 1, 0)

    return pl.pallas_call(
        kern,
        grid=(t // tm, nj),
        in_specs=[pl.BlockSpec((tm, d), lambda i, j: (i, 0)),
                  pl.BlockSpec((None, 1, d), lambda i, j: (layer, 0, 0)),
                  pl.BlockSpec((None, d, tf), lambda i, j: (layer, 0, j)),
                  pl.BlockSpec((None, d, tf), lambda i, j: (layer, 0, j + nj)),
                  pl.BlockSpec((None, CONV_WIDTH, tf), lambda i, j: (layer, 0, j)),
                  pl.BlockSpec((None, CONV_WIDTH, tf), lambda i, j: (layer, 0, j + nj)),
                  pl.BlockSpec((None, 1, tf), lambda i, j: (layer, 0, j)),
                  pl.BlockSpec((None, 1, tf), lambda i, j: (layer, 0, j + nj)),
                  pl.BlockSpec((None, tf, d), lambda i, j: (layer, down_row(j), 0)),
                  pl.BlockSpec((None, tf, d), lambda i, j: (layer, nj - 1, 0)),
                  pl.BlockSpec((1, d), lambda i, j: (0, 0))],
        out_specs=pl.BlockSpec((tm, d), lambda i, j: (i, 0)),
        out_shape=jax.ShapeDtypeStruct((t, d), F32),
        scratch_shapes=[pltpu.VMEM((tm, d), BF16),
                        pltpu.VMEM((tm, tf), BF16),
                        pltpu.VMEM((tm, tf), BF16),
                        pltpu.VMEM((nj, 2, SUBLANES, tf), F32)],
        compiler_params=_params("arbitrary", "arbitrary"),
        name="conv_ffn",
    )(x, g[:, None, :], w_up, w_up, conv_w, conv_w, conv_b[:, None, :], conv_b[:, None, :],
      w_down, w_down, final_g.reshape(1, d))


def _suffix_sum_matrix():
    blk = ATT_BLOCK
    j = np.arange(blk)[:, None]
    s = np.arange(blk)[None, :]
    m = np.concatenate([(j > s).astype(np.float32), np.ones((blk, blk), np.float32)], axis=1)
    return jnp.asarray(np.concatenate([m, m], axis=0), dtype=BF16)


def _stick_kernel(q_ref, k_ref, v_ref, m_ref, o_ref, acc_ref, carry_ref):
    blk = ATT_BLOCK
    tile = q_ref.shape[1]
    half = tile // 2
    i = pl.program_id(2)
    scale2 = DH_A ** -0.5 * LOG2E
    suffix = m_ref[...]
    heads = range(STICK_HEADS)

    acc_ref[...] = jnp.zeros_like(acc_ref)
    carry_ref[...] = jnp.zeros_like(carry_ref)

    def sweep(hd, r0, r1, key_start, n_sub, diagonal=False):
        rows = slice(r0, r1)
        cols = slice(hd * DH_A, (hd + 1) * DH_A)
        k = k_ref[0, pl.ds(key_start, n_sub * blk), cols]
        v = v_ref[0, pl.ds(key_start, n_sub * blk), cols]
        z_all = _dot_nt(q_ref[0, rows, cols], k) * scale2
        carry = carry_ref[hd, rows, :]
        ws = [None] * n_sub
        for c in reversed(range(n_sub)):
            z = z_all[:, c * blk:(c + 1) * blk]
            if diagonal:
                row = lax.broadcasted_iota(jnp.int32, z.shape, 0) + r0
                col = lax.broadcasted_iota(jnp.int32, z.shape, 1) + c * blk
                z = jnp.where(col < row, z, MASKED_LOGIT)
            log_sig = jnp.minimum(z, 0.0) - jnp.log2(1.0 + jnp.exp2(-jnp.abs(z)))
            log_1m = log_sig - z
            hi = log_1m.astype(BF16)
            lo = (log_1m - hi.astype(F32)).astype(BF16)
            sums = _dot(jnp.concatenate([hi, lo], axis=1), suffix)
            ws[c] = jnp.exp2(log_sig + sums[:, :blk] + carry).astype(BF16)
            carry = carry + sums[:, blk:]
        carry_ref[hd, rows, :] = carry
        acc_ref[hd, rows, :] += _dot(jnp.concatenate(ws, axis=1), v)

    def alive(hd, r0, r1):
        return jnp.max(carry_ref[hd, r0:r1, :]) > ZERO_WEIGHT_LOG2

    tile_start = pl.multiple_of(i * tile, tile)
    n_sub = tile // blk
    for hd in heads:
        sweep(hd, 0, tile, tile_start, n_sub, diagonal=True)

    @pl.when(i > 0)
    def _():
        prev = pl.multiple_of(tile_start - tile, tile)
        for hd in heads:
            sweep(hd, 0, half, prev + half, n_sub // 2)

        top_alive = [alive(hd, 0, half) for hd in heads]
        bottom_alive = [alive(hd, half, tile) for hd in heads]
        for hd in heads:
            @pl.when(bottom_alive[hd])
            def _():
                sweep(hd, half, tile, prev + half, n_sub // 2)

            @pl.when(jnp.logical_or(top_alive[hd], bottom_alive[hd]))
            def _():
                sweep(hd, 0, tile, prev, n_sub // 2)

                def more(t):
                    return jnp.logical_and(t < i, alive(hd, 0, tile))

                def body(t):
                    sweep(hd, 0, tile, pl.multiple_of(tile_start - (t + 1) * tile, tile), n_sub)
                    return t + 1

                lax.while_loop(more, body, 1)

    o_ref[0] = jnp.concatenate([acc_ref[hd] for hd in heads], axis=1).astype(o_ref.dtype)


def _stick_breaking(p, batch, seq, riders=()):
    blk = ATT_BLOCK
    tile = ATT_TILE
    width = STICK_HEADS * DH_A
    groups = H_A // STICK_HEADS
    grid = (batch, groups, seq // tile)
    rider_specs, rider_shapes = _rider_specs(riders, grid)
    out = pl.pallas_call(
        _with_riders(_stick_kernel, 4, len(riders)),
        grid=grid,
        in_specs=[pl.BlockSpec((1, tile, width), lambda b, h, i: (b, i, h)),
                  pl.BlockSpec((1, seq, width), lambda b, h, i: (b, 0, groups + h)),
                  pl.BlockSpec((1, seq, width), lambda b, h, i: (b, 0, 2 * groups + h)),
                  pl.BlockSpec((2 * blk, 2 * blk), lambda b, h, i: (0, 0))] + rider_specs,
        out_specs=[pl.BlockSpec((1, tile, width), lambda b, h, i: (b, i, h))] + rider_specs,
        out_shape=[jax.ShapeDtypeStruct((batch, seq, H_A * DH_A), BF16)] + rider_shapes,
        scratch_shapes=[pltpu.VMEM((STICK_HEADS, tile, DH_A), F32),
                        pltpu.VMEM((STICK_HEADS, tile, blk), F32)],
        compiler_params=_params("arbitrary", "arbitrary", "arbitrary"),
        name="stick_breaking",
    )(p, p, p, _suffix_sum_matrix(), *riders)
    return out[0], out[1:]


def _rel_bucket(rel):
    nb = NUM_BUCKETS // 2
    max_exact = nb // 2
    ret = jnp.where(rel > 0, nb, 0)
    n = jnp.abs(rel)
    n_f = jnp.maximum(n, 1).astype(F32)
    large = max_exact + (jnp.log(n_f / max_exact) / math.log(MAX_DISTANCE / max_exact)
                         * (nb - max_exact)).astype(jnp.int32)
    large = jnp.minimum(large, nb - 1)
    return ret + jnp.where(n < max_exact, n, large)


def _near_buckets():
    blk = ATT_BLOCK
    qpos = blk + jnp.arange(blk)
    kpos = jnp.arange(2 * blk)
    near = _rel_bucket(kpos[None, :] - qpos[:, None]).astype(jnp.int32)
    far = _rel_bucket(jnp.full((1,), -2 * blk, jnp.int32)).astype(jnp.int32)
    return near, far


def _diff_kernel(far_ref, relb_ref, q_ref, k_ref, v_ref, bucket_ref, lam_ref, g_ref, o_ref,
                 bias_ref, m_ref, l_ref, acc_ref, s0_ref, s1_ref, *, lam_init):
    blk = ATT_BLOCK
    tile = q_ref.shape[1]
    nsub = tile // blk
    h = pl.program_id(1)
    i = pl.program_id(2)
    scale2 = DK_B ** -0.5 * LOG2E

    @pl.when(i == 0)
    def _():
        bucket = bucket_ref[...]
        far = relb_ref[far_ref[0], h]
        near = jnp.zeros((blk, 2 * blk), F32)
        for b in range(NUM_BUCKETS):
            near = jnp.where(bucket == b, relb_ref[b, h], near)
        near = (near - far) * LOG2E
        qpos = blk + lax.broadcasted_iota(jnp.int32, (blk, 2 * blk), 0)
        kpos = lax.broadcasted_iota(jnp.int32, (blk, 2 * blk), 1)
        near = jnp.where((kpos // CHUNK) <= (qpos // CHUNK), near, -jnp.inf)
        bias_ref[...] = jnp.zeros_like(bias_ref)
        for a in range(nsub):
            rows = slice(a * blk, (a + 1) * blk)
            bias_ref[0, rows, a * blk:(a + 1) * blk] = near[:, blk:]
            if a >= 1:
                bias_ref[0, rows, (a - 1) * blk:a * blk] = near[:, :blk]
            if a + 1 < nsub:
                bias_ref[0, rows, (a + 1) * blk:] = jnp.full((blk, tile - (a + 1) * blk),
                                                             -jnp.inf, F32)
        bias_ref[1, 0:blk, (nsub - 1) * blk:] = near[:, :blk]

    m_ref[...] = jnp.full_like(m_ref, -jnp.inf)
    l_ref[...] = jnp.zeros_like(l_ref)
    acc_ref[...] = jnp.zeros_like(acc_ref)

    def key_tile(idx):
        return jnp.where(idx == 0, i, jnp.where(idx == 1, i - 1, idx - 2))

    def logits_into(s_ref, idx):
        j = key_tile(jnp.minimum(idx, i))
        kk = k_ref[0, pl.ds(pl.multiple_of(j * tile, tile), tile), :]
        for c in range(2):
            s_ref[c] = _dot_nt(q_ref[0, :, c * DK_B:(c + 1) * DK_B],
                               kk[:, c * DK_B:(c + 1) * DK_B])

    def consume(s_ref, idx):
        j = key_tile(idx)
        v = v_ref[0, pl.ds(pl.multiple_of(j * tile, tile), tile), :]
        bias = bias_ref[jnp.minimum(idx, 2)]
        probs = []
        for c in range(2):
            s = s_ref[c] * scale2 + bias
            subs = [s[:, u * blk:(u + 1) * blk] for u in range(nsub)]
            m_prev = m_ref[c]
            m_cur = functools.reduce(jnp.maximum, subs)
            m_new = jnp.maximum(m_prev, jnp.max(m_cur, axis=-1, keepdims=True))
            alpha = jnp.exp2(m_prev - m_new)
            ps = [jnp.exp2(u - m_new) for u in subs]
            l_ref[c] = alpha * l_ref[c] + functools.reduce(jnp.add, ps)
            m_ref[c] = m_new
            probs.append((alpha, jnp.concatenate([u.astype(BF16) for u in ps], axis=1)))
        for c in range(2):
            alpha, p = probs[c]
            acc_ref[c] = jnp.concatenate([alpha] * (DV_B // blk), axis=1) * acc_ref[c] + _dot(p, v)

    logits_into(s0_ref, 0)

    def pair(t, _):
        idx = 2 * t
        logits_into(s1_ref, idx + 1)
        consume(s0_ref, idx)
        logits_into(s0_ref, idx + 2)
        consume(s1_ref, idx + 1)
        return 0

    lax.fori_loop(0, (i + 1) // 2, pair, 0)

    @pl.when(i % 2 == 0)
    def _():
        consume(s0_ref, i)

    lv = lam_ref[...]
    lam = (jnp.exp(jnp.sum(lv[0:1] * lv[1:2], axis=-1, keepdims=True))
           - jnp.exp(jnp.sum(lv[2:3] * lv[3:4], axis=-1, keepdims=True)) + lam_init)
    l1 = jnp.sum(l_ref[0], axis=-1, keepdims=True)
    l2 = jnp.sum(l_ref[1], axis=-1, keepdims=True)
    o = acc_ref[0] / l1 - lam * (acc_ref[1] / l2)
    o_ref[0] = (_rms_rows(o, g_ref[...]) * (1.0 - lam_init)).astype(o_ref.dtype)


def _diff_attention(p, rel_bias, lam_vecs, subln_g, lam_init, batch, seq, riders=()):
    blk = ATT_BLOCK
    tile = ATT_TILE
    q_col = 3 * H_A * DH_A // DV_B
    k_col = q_col + H_B
    v_col = k_col + H_B
    near, far = _near_buckets()
    grid = (batch, H_B, seq // tile)
    rider_specs, rider_shapes = _rider_specs(riders, grid)
    kern = _with_riders(functools.partial(_diff_kernel, lam_init=lam_init), 8, len(riders))
    grid_spec = pltpu.PrefetchScalarGridSpec(
        num_scalar_prefetch=2,
        grid=grid,
        in_specs=[pl.BlockSpec((1, tile, DV_B), lambda b, h, i, *_: (b, i, q_col + h)),
                  pl.BlockSpec((1, seq, DV_B), lambda b, h, i, *_: (b, 0, k_col + h)),
                  pl.BlockSpec((1, seq, DV_B), lambda b, h, i, *_: (b, 0, v_col + h)),
                  pl.BlockSpec((blk, 2 * blk), lambda b, h, i, *_: (0, 0)),
                  pl.BlockSpec((4, DK_B), lambda b, h, i, *_: (0, 0)),
                  pl.BlockSpec((1, DV_B), lambda b, h, i, *_: (0, 0))] + rider_specs,
        out_specs=[pl.BlockSpec((1, tile, DV_B), lambda b, h, i, *_: (b, i, h))] + rider_specs,
        scratch_shapes=[pltpu.VMEM((3, tile, tile), F32),
                        pltpu.VMEM((2, tile, blk), F32),
                        pltpu.VMEM((2, tile, blk), F32),
                        pltpu.VMEM((2, tile, DV_B), F32),
                        pltpu.VMEM((2, tile, tile), F32),
                        pltpu.VMEM((2, tile, tile), F32)])
    out = pl.pallas_call(
        kern,
        grid_spec=grid_spec,
        out_shape=[jax.ShapeDtypeStruct((batch, seq, H_B * DV_B), BF16)] + rider_shapes,
        compiler_params=_params("arbitrary", "arbitrary", "arbitrary"),
        name="diff_attention",
    )(far, rel_bias.astype(F32), p, p, p, near, lam_vecs.astype(F32), subln_g.reshape(1, DV_B),
      *riders)
    return out[0], out[1:]


def _rotary_tables(seq):
    inv_freq = ROPE_BASE ** (-jnp.arange(0, DK_C, 2, dtype=F32) / DK_C)
    ang = jnp.arange(seq, dtype=F32)[:, None] * inv_freq[None, :]
    cos = jnp.concatenate([jnp.cos(ang), jnp.cos(ang)], axis=-1)
    sin = jnp.concatenate([-jnp.sin(ang), jnp.sin(ang)], axis=-1)
    return cos, sin


def _decay_tables():
    log_g = jnp.log(1.0 - 2.0 ** (-5.0 - jnp.arange(H_C, dtype=F32)))
    idx = jnp.arange(RET_BLOCK, dtype=F32)
    chunk = jnp.arange(RET_BLOCK) // CHUNK
    visible = chunk[None, :] <= chunk[:, None]
    intra = jnp.where(visible[None],
                      jnp.exp(log_g[:, None, None] * jnp.abs(idx[:, None] - idx[None, :])), 0.0)
    q_decay = jnp.exp(log_g[:, None] * (idx + 1.0))
    k_decay = jnp.exp(log_g[:, None] * (RET_BLOCK - 1.0 - idx))
    block_decay = jnp.exp(log_g * RET_BLOCK)
    return (intra,
            jnp.broadcast_to(q_decay[:, :, None], (H_C, RET_BLOCK, DV_C)),
            jnp.broadcast_to(k_decay[:, :, None], (H_C, RET_BLOCK, DK_C)),
            jnp.broadcast_to(block_decay[:, None, None], (H_C, 1, DV_C)))


def _retention_kernel(q_ref, k_ref, v_ref, gate_ref, cos_ref, sin_ref, intra_ref, qd_ref,
                      kd_ref, bd_ref, g_ref, o_ref, state_ref):
    rows = q_ref.shape[1]

    @pl.when(pl.program_id(1) == 0)
    def _():
        state_ref[...] = jnp.zeros_like(state_ref)

    cos = cos_ref[...]
    sin = sin_ref[...]

    def rotary(t):
        return t * cos + pltpu.roll(t, DK_C // 2, 1) * sin

    for h in range(H_C):
        qk_cols = slice(h * DK_C, (h + 1) * DK_C)
        v_cols = slice(h * DV_C, (h + 1) * DV_C)
        q = (rotary(q_ref[0, :, qk_cols]) * (DK_C ** -0.5)).astype(BF16)
        k = rotary(k_ref[0, :, qk_cols])
        state = state_ref[h]
        for r in range(0, rows, RET_BLOCK):
            sl = slice(r, r + RET_BLOCK)
            qc = q[sl]
            kc = k[sl]
            vc = v_ref[0, sl, v_cols].astype(BF16)
            scores = _dot_nt(qc, kc.astype(BF16)) * intra_ref[h]
            y = _dot(scores.astype(BF16), vc) + _dot(qc, state.astype(BF16)) * qd_ref[h]
            state = bd_ref[h] * state + _dot((kc * kd_ref[h]).T.astype(BF16), vc)
            gate = gate_ref[0, sl, v_cols]
            y = _rms_rows(y, g_ref[...])
            o_ref[0, sl, v_cols] = (gate * (1.0 / (1.0 + jnp.exp(-gate))) * y).astype(o_ref.dtype)
        state_ref[h] = state


def _retention(p, ret_norm_g, batch, seq, rows=MIXER_ROWS):
    qk_width = H_C * DK_C
    v_width = H_C * DV_C
    cos, sin = _rotary_tables(seq)
    intra, q_decay, k_decay, block_decay = _decay_tables()

    def whole(a):
        return pl.BlockSpec(a.shape, lambda b, i: (0,) * a.ndim)

    return pl.pallas_call(
        _retention_kernel,
        grid=(batch, seq // rows),
        in_specs=[pl.BlockSpec((1, rows, qk_width), lambda b, i: (b, i, 0)),
                  pl.BlockSpec((1, rows, qk_width), lambda b, i: (b, i, 1)),
                  pl.BlockSpec((1, rows, v_width), lambda b, i: (b, i, 2 * qk_width // v_width)),
                  pl.BlockSpec((1, rows, v_width), lambda b, i: (b, i, 2 * qk_width // v_width + 1)),
                  pl.BlockSpec((rows, DK_C), lambda b, i: (i, 0)),
                  pl.BlockSpec((rows, DK_C), lambda b, i: (i, 0)),
                  whole(intra), whole(q_decay), whole(k_decay), whole(block_decay),
                  pl.BlockSpec((1, DV_C), lambda b, i: (0, 0))],
        out_specs=pl.BlockSpec((1, rows, v_width), lambda b, i: (b, i, 0)),
        out_shape=jax.ShapeDtypeStruct((batch, seq, v_width), BF16),
        scratch_shapes=[pltpu.VMEM((H_C, DK_C, DV_C), F32)],
        compiler_params=_params("arbitrary", "arbitrary"),
        name="retention",
    )(p, p, p, p, cos, sin, intra, q_decay, k_decay, block_decay, ret_norm_g.reshape(1, DV_C))


def _gelu_tanh(x):
    return 0.5 * x * (1.0 + jnp.tanh(math.sqrt(2.0 / math.pi) * (x + 0.044715 * (x * x * x))))


def _sgu_kernel(zu_ref, zv_ref, lg_ref, lb_ref, w_ref, b_ref, o_ref):
    rows = zu_ref.shape[0]
    width = D_D // G_D
    v = _gelu_tanh(zv_ref[...])
    mu = jnp.mean(v, axis=-1, keepdims=True)
    var = jnp.mean(jnp.square(v - mu), axis=-1, keepdims=True)
    vn = ((v - mu) * lax.rsqrt(var + EPS) * lg_ref[...] + lb_ref[...]).astype(BF16)
    pos_i = lax.broadcasted_iota(jnp.int32, (SGU_LEN, SGU_LEN), 0)
    pos_j = lax.broadcasted_iota(jnp.int32, (SGU_LEN, SGU_LEN), 1)
    mask = (pos_j // CHUNK) <= (pos_i // CHUNK)
    for g in range(G_D):
        w = jnp.where(mask, w_ref[g], 0.0).astype(BF16)
        bias = b_ref[g]
        cols = slice(g * width, (g + 1) * width)
        for r in range(rows // SGU_LEN):
            sl = slice(r * SGU_LEN, (r + 1) * SGU_LEN)
            gate = _dot(w, vn[sl, cols]) + bias
            o_ref[sl, cols] = (_gelu_tanh(zu_ref[sl, cols]) * gate).astype(o_ref.dtype)


def _sgu(p, ln_g, ln_b, w_s, b_s, rows=MIXER_ROWS):
    t = p.shape[0]
    u_col = (2 * H_C * DK_C + 2 * H_C * DV_C) // D_D
    return pl.pallas_call(
        _sgu_kernel,
        grid=(t // rows,),
        in_specs=[pl.BlockSpec((rows, D_D), lambda i: (i, u_col)),
                  pl.BlockSpec((rows, D_D), lambda i: (i, u_col + 1)),
                  pl.BlockSpec((1, D_D), lambda i: (0, 0)),
                  pl.BlockSpec((1, D_D), lambda i: (0, 0)),
                  pl.BlockSpec((G_D, SGU_LEN, SGU_LEN), lambda i: (0, 0, 0)),
                  pl.BlockSpec((G_D, SGU_LEN, 1), lambda i: (0, 0, 0))],
        out_specs=pl.BlockSpec((rows, D_D), lambda i: (i, 0)),
        out_shape=jax.ShapeDtypeStruct((t, D_D), BF16),
        compiler_params=_params("parallel"),
        name="spatial_gate",
    )(p, p, ln_g.reshape(1, D_D), ln_b.reshape(1, D_D), w_s, b_s.reshape(G_D, SGU_LEN, 1))


def kernel(x, norm_mix_g, norm_ffn_g, final_norm_g, rel_bias, ab_w_in, ab_w_out, diff_lambda,
           diff_subln_g, cd_w_in, cd_w_out, ret_norm_g, sgu_ln_g, sgu_ln_b, sgu_w, sgu_b,
           ffn_w_up, ffn_conv_w, ffn_conv_b, ffn_w_down):
    batch, seq, d = x.shape
    t = batch * seq
    depth = norm_mix_g.shape[0]
    xt = x.reshape(t, d)
    def rows(w):
        return w.reshape(-1, w.shape[-1])

    ab_w_in = ab_w_in.astype(BF16)
    for layer in range(depth):
        j = layer // 2
        if layer % 2 == 0:
            lam_init = 0.8 - 0.6 * math.exp(-0.3 * layer)
            p = _norm_matmul(xt, norm_mix_g[layer], ab_w_in, j, BF16, AB_PROJ_TILE)
            p3 = p.reshape(batch, seq, p.shape[1])
            if layer == 0:
                o_a, (up_bf,) = _stick_breaking(p3, batch, seq, (rows(ffn_w_up),))
                o_b, (down_bf, abo_bf, cdi_bf, cdo_bf) = _diff_attention(
                    p3, rel_bias, diff_lambda[j], diff_subln_g[j], lam_init, batch, seq,
                    (rows(ffn_w_down), rows(ab_w_out), rows(cd_w_in), rows(cd_w_out)))
                ffn_w_up, ffn_w_down, ab_w_out, cd_w_in, cd_w_out = (
                    b.reshape(w.shape) for b, w in ((up_bf, ffn_w_up), (down_bf, ffn_w_down),
                                                    (abo_bf, ab_w_out), (cdi_bf, cd_w_in),
                                                    (cdo_bf, cd_w_out)))
            else:
                o_a, _ = _stick_breaking(p3, batch, seq)
                o_b, _ = _diff_attention(p3, rel_bias, diff_lambda[j], diff_subln_g[j], lam_init,
                                         batch, seq)
            xt = _out_proj(o_a.reshape(t, -1), o_b.reshape(t, -1), ab_w_out, j, xt)
        else:
            p = _norm_matmul(xt, norm_mix_g[layer], cd_w_in, j, F32, CD_PROJ_TILE)
            o_c = _retention(p.reshape(batch, seq, p.shape[1]), ret_norm_g[j], batch, seq)
            o_d = _sgu(p, sgu_ln_g[j], sgu_ln_b[j], sgu_w[j], sgu_b[j])
            xt = _out_proj(o_c.reshape(t, -1), o_d, cd_w_out, j, xt)
        xt = _ffn(xt, norm_ffn_g, ffn_w_up, ffn_conv_w, ffn_conv_b, ffn_w_down, final_norm_g,
                  layer, seq=seq, final_norm=(layer == depth - 1))
    return xt.reshape(batch, seq, d)
```

```python
import functools
import math

import numpy as np
import jax
import jax.numpy as jnp
from jax import lax
from jax.experimental import pallas as pl
from jax.experimental.pallas import tpu as pltpu

F32 = jnp.float32
BF16 = jnp.bfloat16

EPS = 1e-6
CHUNK = 64
H_A = 8
DH_A = 128
H_B = 4
DK_B = 128
DV_B = 2 * DK_B
NUM_BUCKETS = 32
MAX_DISTANCE = 128
H_C = 4
DK_C = 128
DV_C = 2 * DK_C
ROPE_BASE = 10000.0
D_D = 1024
G_D = 4
SGU_LEN = 128
CONV_WIDTH = 3

SUBLANES = 8
BF16_SUBLANES = 16
ATT_BLOCK = 128
ATT_TILE = 512
STICK_HEADS = 4
RET_BLOCK = 256
VMEM_LIMIT = 62 * 1024 * 1024
PROJ_ROWS = 1024
AB_PROJ_TILE = 3072
CD_PROJ_TILE = 1280
FFN_ROWS = 1024
FFN_TILE = 512
OUT_PROJ_ROWS = 1024
MIXER_ROWS = 512
MASKED_LOGIT = -1e4
LOG2E = math.log2(math.e)
ZERO_WEIGHT_LOG2 = -150.0


def _params(*semantics):
    return pltpu.CompilerParams(dimension_semantics=semantics,
                                vmem_limit_bytes=VMEM_LIMIT)


def _dot(a, b):
    return jnp.dot(a, b, preferred_element_type=F32)


def _dot_nt(a, b):
    return lax.dot_general(a, b, (((1,), (1,)), ((), ())), preferred_element_type=F32)


def _rms_rows(x, g):
    return x * lax.rsqrt(jnp.mean(x * x, axis=-1, keepdims=True) + EPS) * g


def _with_riders(body, n_in, n_riders):
    def kern(*refs):
        ins = refs[:n_in]
        rider_in = refs[n_in:n_in + n_riders]
        out = refs[n_in + n_riders]
        rider_out = refs[n_in + n_riders + 1:n_in + 2 * n_riders + 1]
        scratch = refs[n_in + 2 * n_riders + 1:]
        for src, dst in zip(rider_in, rider_out):
            dst[...] = src[...].astype(BF16)
        body(*ins, out, *scratch)
    return kern


def _rider_specs(riders, grid):
    steps = math.prod(grid)

    def linear_step(*g):
        n = g[0]
        for size, idx in zip(grid[1:], g[1:len(grid)]):
            n = n * size + idx
        return n

    specs, shapes = [], []
    for w in riders:
        rows, cols = w.shape
        block_rows, rem = divmod(rows, steps)
        assert rem == 0 and block_rows % BF16_SUBLANES == 0, (w.shape, steps)
        specs.append(pl.BlockSpec((block_rows, cols), lambda *g: (linear_step(*g), 0)))
        shapes.append(jax.ShapeDtypeStruct((rows, cols), BF16))
    return specs, shapes


def _norm_matmul_kernel(x_ref, g_ref, w_ref, o_ref, h_ref):
    @pl.when(pl.program_id(1) == 0)
    def _():
        h_ref[...] = _rms_rows(x_ref[...], g_ref[...]).astype(BF16)

    o_ref[...] = _dot(h_ref[...], w_ref[...]).astype(o_ref.dtype)


def _norm_matmul(x, g, w, layer, out_dtype, tn, tm=PROJ_ROWS):
    t, d = x.shape
    n = w.shape[2]
    return pl.pallas_call(
        _norm_matmul_kernel,
        grid=(t // tm, n // tn),
        in_specs=[pl.BlockSpec((tm, d), lambda i, j: (i, 0)),
                  pl.BlockSpec((1, d), lambda i, j: (0, 0)),
                  pl.BlockSpec((None, d, tn), lambda i, j: (layer, 0, j))],
        out_specs=pl.BlockSpec((tm, tn), lambda i, j: (i, j)),
        out_shape=jax.ShapeDtypeStruct((t, n), out_dtype),
        scratch_shapes=[pltpu.VMEM((tm, d), BF16)],
        compiler_params=_params("parallel", "arbitrary"),
        name="norm_matmul",
    )(x, g.reshape(1, d), w)


def _out_proj_kernel(a1_ref, a2_ref, w_ref, x_ref, o_ref):
    half = a1_ref.shape[1]
    acc = _dot(a1_ref[...], w_ref[0:half, :])
    acc += _dot(a2_ref[...], w_ref[half:2 * half, :])
    o_ref[...] = x_ref[...] + acc


def _out_proj(a1, a2, w, layer, x, tm=OUT_PROJ_ROWS):
    t, half = a1.shape
    d = w.shape[2]
    return pl.pallas_call(
        _out_proj_kernel,
        grid=(t // tm,),
        in_specs=[pl.BlockSpec((tm, half), lambda i: (i, 0)),
                  pl.BlockSpec((tm, half), lambda i: (i, 0)),
                  pl.BlockSpec((None, 2 * half, d), lambda i: (layer, 0, 0)),
                  pl.BlockSpec((tm, d), lambda i: (i, 0))],
        out_specs=pl.BlockSpec((tm, d), lambda i: (i, 0)),
        out_shape=jax.ShapeDtypeStruct((t, d), F32),
        compiler_params=_params("parallel"),
        name="out_proj",
    )(a1, a2, w, x)


def _ffn_kernel(x_ref, g_ref, wa_ref, wg_ref, cwa_ref, cwg_ref, cba_ref, cbg_ref,
                wd_ref, wd_last_ref, fg_ref, o_ref, h_ref, act0_ref, act1_ref, carry_ref,
                *, seq_tiles, final_norm, n_hidden_tiles):
    i = pl.program_id(0)
    j = pl.program_id(1)
    nj = n_hidden_tiles
    tm = x_ref.shape[0]
    acts = (act0_ref, act1_ref)

    def up_phase(out_ref):
        h = h_ref[...]

        def gate(cg, ca):
            return (cg * (1.0 / (1.0 + jnp.exp(-cg))) * ca).astype(BF16)

        def conv(w_ref, cw_ref, cb_ref, branch):
            up = _dot(h, w_ref[...])
            w0 = cw_ref[0:1, :]
            w1 = cw_ref[1:2, :]
            w2 = cw_ref[2:3, :]
            b = cb_ref[...]
            c = b + w0 * pltpu.roll(up, 2, 0) + w1 * pltpu.roll(up, 1, 0) + w2 * up
            both = jnp.concatenate([carry_ref[j, branch], up[0:SUBLANES, :]], axis=0)
            head = slice(SUBLANES, 2 * SUBLANES)
            top = (b + w0 * pltpu.roll(both, 2, 0)[head, :]
                   + w1 * pltpu.roll(both, 1, 0)[head, :] + w2 * both[head, :])
            carry_ref[j, branch] = up[tm - SUBLANES:tm, :]
            return c, top

        ca, ta = conv(wa_ref, cwa_ref, cba_ref, 0)
        cg, tg = conv(wg_ref, cwg_ref, cbg_ref, 1)
        out_ref[...] = gate(cg, ca)
        out_ref[0:SUBLANES, :] = gate(tg, ta)

    def down_phase(in_ref):
        o_ref[...] += _dot(in_ref[...], wd_ref[...])

    @pl.when((i % seq_tiles) == 0)
    def _():
        carry_ref[j] = jnp.zeros(carry_ref.shape[1:], F32)

    @pl.when(j == 0)
    def _():
        x = x_ref[...]
        h_ref[...] = _rms_rows(x, g_ref[...]).astype(BF16)
        o_ref[...] = x
        up_phase(act0_ref)

    for parity in range(2):
        @pl.when((j > 0) & (j < nj - 1) & (j % 2 == parity))
        def _():
            up_phase(acts[parity])
            down_phase(acts[1 - parity])

    @pl.when(j == nj - 1)
    def _():
        parity = (n_hidden_tiles - 1) % 2
        up_phase(acts[parity])
        down_phase(acts[1 - parity])
        y = o_ref[...] + _dot(acts[parity][...], wd_last_ref[...])
        if final_norm:
            y = _rms_rows(y, fg_ref[...])
        o_ref[...] = y


def _ffn(x, g, w_up, conv_w, conv_b, w_down, final_g, layer, *, seq, final_norm,
         tm=FFN_ROWS, tf=FFN_TILE):
    t, d = x.shape
    f = w_down.shape[1]
    nj = f // tf
    assert nj >= 2
    kern = functools.partial(_ffn_kernel, seq_tiles=seq // tm, final_norm=final_norm,
                             n_hidden_tiles=nj)

    def down_row(j):
        return jnp.maximum(j - 1, 0)

    return pl.pallas_call(
        kern,
        grid=(t // tm, nj),
        in_specs=[pl.BlockSpec((tm, d), lambda i, j: (i, 0)),
                  pl.BlockSpec((None, 1, d), lambda i, j: (layer, 0, 0)),
                  pl.BlockSpec((None, d, tf), lambda i, j: (layer, 0, j)),
                  pl.BlockSpec((None, d, tf), lambda i, j: (layer, 0, j + nj)),
                  pl.BlockSpec((None, CONV_WIDTH, tf), lambda i, j: (layer, 0, j)),
                  pl.BlockSpec((None, CONV_WIDTH, tf), lambda i, j: (layer, 0, j + nj)),
                  pl.BlockSpec((None, 1, tf), lambda i, j: (layer, 0, j)),
                  pl.BlockSpec((None, 1, tf), lambda i, j: (layer, 0, j + nj)),
                  pl.BlockSpec((None, tf, d), lambda i, j: (layer, down_row(j), 0)),
                  pl.BlockSpec((None, tf, d), lambda i, j: (layer, nj - 1, 0)),
                  pl.BlockSpec((1, d), lambda i, j: (0, 0))],
        out_specs=pl.BlockSpec((tm, d), lambda i, j: (i, 0)),
        out_shape=jax.ShapeDtypeStruct((t, d), F32),
        scratch_shapes=[pltpu.VMEM((tm, d), BF16),
                        pltpu.VMEM((tm, tf), BF16),
                        pltpu.VMEM((tm, tf), BF16),
                        pltpu.VMEM((nj, 2, SUBLANES, tf), F32)],
        compiler_params=_params("arbitrary", "arbitrary"),
        name="conv_ffn",
    )(x, g[:, None, :], w_up, w_up, conv_w, conv_w, conv_b[:, None, :], conv_b[:, None, :],
      w_down, w_down, final_g.reshape(1, d))


def _suffix_sum_matrix():
    blk = ATT_BLOCK
    j = np.arange(blk)[:, None]
    s = np.arange(blk)[None, :]
    m = np.concatenate([(j > s).astype(np.float32), np.ones((blk, blk), np.float32)], axis=1)
    return jnp.asarray(np.concatenate([m, m], axis=0), dtype=BF16)


def _stick_kernel(q_ref, k_ref, v_ref, m_ref, o_ref, acc_ref, carry_ref):
    blk = ATT_BLOCK
    tile = q_ref.shape[1]
    half = tile // 2
    i = pl.program_id(2)
    scale2 = DH_A ** -0.5 * LOG2E
    suffix = m_ref[...]
    heads = range(STICK_HEADS)

    acc_ref[...] = jnp.zeros_like(acc_ref)
    carry_ref[...] = jnp.zeros_like(carry_ref)

    def sweep(hd, r0, r1, key_start, n_sub, diagonal=False):
        rows = slice(r0, r1)
        cols = slice(hd * DH_A, (hd + 1) * DH_A)
        k = k_ref[0, pl.ds(key_start, n_sub * blk), cols]
        v = v_ref[0, pl.ds(key_start, n_sub * blk), cols]
        z_all = _dot_nt(q_ref[0, rows, cols], k) * scale2
        carry = carry_ref[hd, rows, :]
        ws = [None] * n_sub
        for c in reversed(range(n_sub)):
            z = z_all[:, c * blk:(c + 1) * blk]
            if diagonal:
                row = lax.broadcasted_iota(jnp.int32, z.shape, 0) + r0
                col = lax.broadcasted_iota(jnp.int32, z.shape, 1) + c * blk
                z = jnp.where(col < row, z, MASKED_LOGIT)
            log_sig = jnp.minimum(z, 0.0) - jnp.log2(1.0 + jnp.exp2(-jnp.abs(z)))
            log_1m = log_sig - z
            hi = log_1m.astype(BF16)
            lo = (log_1m - hi.astype(F32)).astype(BF16)
            sums = _dot(jnp.concatenate([hi, lo], axis=1), suffix)
            ws[c] = jnp.exp2(log_sig + sums[:, :blk] + carry).astype(BF16)
            carry = carry + sums[:, blk:]
        carry_ref[hd, rows, :] = carry
        acc_ref[hd, rows, :] += _dot(jnp.concatenate(ws, axis=1), v)

    def alive(hd, r0, r1):
        return jnp.max(carry_ref[hd, r0:r1, :]) > ZERO_WEIGHT_LOG2

    tile_start = pl.multiple_of(i * tile, tile)
    n_sub = tile // blk
    for hd in heads:
        sweep(hd, 0, tile, tile_start, n_sub, diagonal=True)

    @pl.when(i > 0)
    def _():
        prev = pl.multiple_of(tile_start - tile, tile)
        for hd in heads:
            sweep(hd, 0, half, prev + half, n_sub // 2)

        top_alive = [alive(hd, 0, half) for hd in heads]
        bottom_alive = [alive(hd, half, tile) for hd in heads]
        for hd in heads:
            @pl.when(bottom_alive[hd])
            def _():
                sweep(hd, half, tile, prev + half, n_sub // 2)

            @pl.when(jnp.logical_or(top_alive[hd], bottom_alive[hd]))
            def _():
                sweep(hd, 0, tile, prev, n_sub // 2)

                def more(t):
                    return jnp.logical_and(t < i, alive(hd, 0, tile))

                def body(t):
                    sweep(hd, 0, tile, pl.multiple_of(tile_start - (t + 1) * tile, tile), n_sub)
                    return t + 1

                lax.while_loop(more, body, 1)

    o_ref[0] = jnp.concatenate([acc_ref[hd] for hd in heads], axis=1).astype(o_ref.dtype)


def _stick_breaking(p, batch, seq, riders=()):
    blk = ATT_BLOCK
    tile = ATT_TILE
    width = STICK_HEADS * DH_A
    groups = H_A // STICK_HEADS
    grid = (batch, groups, seq // tile)
    rider_specs, rider_shapes = _rider_specs(riders, grid)
    out = pl.pallas_call(
        _with_riders(_stick_kernel, 4, len(riders)),
        grid=grid,
        in_specs=[pl.BlockSpec((1, tile, width), lambda b, h, i: (b, i, h)),
                  pl.BlockSpec((1, seq, width), lambda b, h, i: (b, 0, groups + h)),
                  pl.BlockSpec((1, seq, width), lambda b, h, i: (b, 0, 2 * groups + h)),
                  pl.BlockSpec((2 * blk, 2 * blk), lambda b, h, i: (0, 0))] + rider_specs,
        out_specs=[pl.BlockSpec((1, tile, width), lambda b, h, i: (b, i, h))] + rider_specs,
        out_shape=[jax.ShapeDtypeStruct((batch, seq, H_A * DH_A), BF16)] + rider_shapes,
        scratch_shapes=[pltpu.VMEM((STICK_HEADS, tile, DH_A), F32),
                        pltpu.VMEM((STICK_HEADS, tile, blk), F32)],
        compiler_params=_params("arbitrary", "arbitrary", "arbitrary"),
        name="stick_breaking",
    )(p, p, p, _suffix_sum_matrix(), *riders)
    return out[0], out[1:]


def _rel_bucket(rel):
    nb = NUM_BUCKETS // 2
    max_exact = nb // 2
    ret = jnp.where(rel > 0, nb, 0)
    n = jnp.abs(rel)
    n_f = jnp.maximum(n, 1).astype(F32)
    large = max_exact + (jnp.log(n_f / max_exact) / math.log(MAX_DISTANCE / max_exact)
                         * (nb - max_exact)).astype(jnp.int32)
    large = jnp.minimum(large, nb - 1)
    return ret + jnp.where(n < max_exact, n, large)


def _near_buckets():
    blk = ATT_BLOCK
    qpos = blk + jnp.arange(blk)
    kpos = jnp.arange(2 * blk)
    near = _rel_bucket(kpos[None, :] - qpos[:, None]).astype(jnp.int32)
    far = _rel_bucket(jnp.full((1,), -2 * blk, jnp.int32)).astype(jnp.int32)
    return near, far


def _diff_kernel(far_ref, relb_ref, q_ref, k_ref, v_ref, bucket_ref, lam_ref, g_ref, o_ref,
                 bias_ref, m_ref, l_ref, acc_ref, s0_ref, s1_ref, *, lam_init):
    blk = ATT_BLOCK
    tile = q_ref.shape[1]
    nsub = tile // blk
    h = pl.program_id(1)
    i = pl.program_id(2)
    scale2 = DK_B ** -0.5 * LOG2E

    @pl.when(i == 0)
    def _():
        bucket = bucket_ref[...]
        far = relb_ref[far_ref[0], h]
        near = jnp.zeros((blk, 2 * blk), F32)
        for b in range(NUM_BUCKETS):
            near = jnp.where(bucket == b, relb_ref[b, h], near)
        near = (near - far) * LOG2E
        qpos = blk + lax.broadcasted_iota(jnp.int32, (blk, 2 * blk), 0)
        kpos = lax.broadcasted_iota(jnp.int32, (blk, 2 * blk), 1)
        near = jnp.where((kpos // CHUNK) <= (qpos // CHUNK), near, -jnp.inf)
        bias_ref[...] = jnp.zeros_like(bias_ref)
        for a in range(nsub):
            rows = slice(a * blk, (a + 1) * blk)
            bias_ref[0, rows, a * blk:(a + 1) * blk] = near[:, blk:]
            if a >= 1:
                bias_ref[0, rows, (a - 1) * blk:a * blk] = near[:, :blk]
            if a + 1 < nsub:
                bias_ref[0, rows, (a + 1) * blk:] = jnp.full((blk, tile - (a + 1) * blk),
                                                             -jnp.inf, F32)
        bias_ref[1, 0:blk, (nsub - 1) * blk:] = near[:, :blk]

    m_ref[...] = jnp.full_like(m_ref, -jnp.inf)
    l_ref[...] = jnp.zeros_like(l_ref)
    acc_ref[...] = jnp.zeros_like(acc_ref)

    def key_tile(idx):
        return jnp.where(idx == 0, i, jnp.where(idx == 1, i - 1, idx - 2))

    def logits_into(s_ref, idx):
        j = key_tile(jnp.minimum(idx, i))
        kk = k_ref[0, pl.ds(pl.multiple_of(j * tile, tile), tile), :]
        for c in range(2):
            s_ref[c] = _dot_nt(q_ref[0, :, c * DK_B:(c + 1) * DK_B],
                               kk[:, c * DK_B:(c + 1) * DK_B])

    def consume(s_ref, idx):
        j = key_tile(idx)
        v = v_ref[0, pl.ds(pl.multiple_of(j * tile, tile), tile), :]
        bias = bias_ref[jnp.minimum(idx, 2)]
        probs = []
        for c in range(2):
            s = s_ref[c] * scale2 + bias
            subs = [s[:, u * blk:(u + 1) * blk] for u in range(nsub)]
            m_prev = m_ref[c]
            m_cur = functools.reduce(jnp.maximum, subs)
            m_new = jnp.maximum(m_prev, jnp.max(m_cur, axis=-1, keepdims=True))
            alpha = jnp.exp2(m_prev - m_new)
            ps = [jnp.exp2(u - m_new) for u in subs]
            l_ref[c] = alpha * l_ref[c] + functools.reduce(jnp.add, ps)
            m_ref[c] = m_new
            probs.append((alpha, jnp.concatenate([u.astype(BF16) for u in ps], axis=1)))
        for c in range(2):
            alpha, p = probs[c]
            acc_ref[c] = jnp.concatenate([alpha] * (DV_B // blk), axis=1) * acc_ref[c] + _dot(p, v)

    logits_into(s0_ref, 0)

    def pair(t, _):
        idx = 2 * t
        logits_into(s1_ref, idx + 1)
        consume(s0_ref, idx)
        logits_into(s0_ref, idx + 2)
        consume(s1_ref, idx + 1)
        return 0

    lax.fori_loop(0, (i + 1) // 2, pair, 0)

    @pl.when(i % 2 == 0)
    def _():
        consume(s0_ref, i)

    lv = lam_ref[...]
    lam = (jnp.exp(jnp.sum(lv[0:1] * lv[1:2], axis=-1, keepdims=True))
           - jnp.exp(jnp.sum(lv[2:3] * lv[3:4], axis=-1, keepdims=True)) + lam_init)
    l1 = jnp.sum(l_ref[0], axis=-1, keepdims=True)
    l2 = jnp.sum(l_ref[1], axis=-1, keepdims=True)
    o = acc_ref[0] / l1 - lam * (acc_ref[1] / l2)
    o_ref[0] = (_rms_rows(o, g_ref[...]) * (1.0 - lam_init)).astype(o_ref.dtype)


def _diff_attention(p, rel_bias, lam_vecs, subln_g, lam_init, batch, seq, riders=()):
    blk = ATT_BLOCK
    tile = ATT_TILE
    q_col = 3 * H_A * DH_A // DV_B
    k_col = q_col + H_B
    v_col = k_col + H_B
    near, far = _near_buckets()
    grid = (batch, H_B, seq // tile)
    rider_specs, rider_shapes = _rider_specs(riders, grid)
    kern = _with_riders(functools.partial(_diff_kernel, lam_init=lam_init), 8, len(riders))
    grid_spec = pltpu.PrefetchScalarGridSpec(
        num_scalar_prefetch=2,
        grid=grid,
        in_specs=[pl.BlockSpec((1, tile, DV_B), lambda b, h, i, *_: (b, i, q_col + h)),
                  pl.BlockSpec((1, seq, DV_B), lambda b, h, i, *_: (b, 0, k_col + h)),
                  pl.BlockSpec((1, seq, DV_B), lambda b, h, i, *_: (b, 0, v_col + h)),
                  pl.BlockSpec((blk, 2 * blk), lambda b, h, i, *_: (0, 0)),
                  pl.BlockSpec((4, DK_B), lambda b, h, i, *_: (0, 0)),
                  pl.BlockSpec((1, DV_B), lambda b, h, i, *_: (0, 0))] + rider_specs,
        out_specs=[pl.BlockSpec((1, tile, DV_B), lambda b, h, i, *_: (b, i, h))] + rider_specs,
        scratch_shapes=[pltpu.VMEM((3, tile, tile), F32),
                        pltpu.VMEM((2, tile, blk), F32),
                        pltpu.VMEM((2, tile, blk), F32),
                        pltpu.VMEM((2, tile, DV_B), F32),
                        pltpu.VMEM((2, tile, tile), F32),
                        pltpu.VMEM((2, tile, tile), F32)])
    out = pl.pallas_call(
        kern,
        grid_spec=grid_spec,
        out_shape=[jax.ShapeDtypeStruct((batch, seq, H_B * DV_B), BF16)] + rider_shapes,
        compiler_params=_params("arbitrary", "arbitrary", "arbitrary"),
        name="diff_attention",
    )(far, rel_bias.astype(F32), p, p, p, near, lam_vecs.astype(F32), subln_g.reshape(1, DV_B),
      *riders)
    return out[0], out[1:]


def _rotary_tables(seq):
    inv_freq = ROPE_BASE ** (-jnp.arange(0, DK_C, 2, dtype=F32) / DK_C)
    ang = jnp.arange(seq, dtype=F32)[:, None] * inv_freq[None, :]
    cos = jnp.concatenate([jnp.cos(ang), jnp.cos(ang)], axis=-1)
    sin = jnp.concatenate([-jnp.sin(ang), jnp.sin(ang)], axis=-1)
    return cos, sin


def _decay_tables():
    log_g = jnp.log(1.0 - 2.0 ** (-5.0 - jnp.arange(H_C, dtype=F32)))
    idx = jnp.arange(RET_BLOCK, dtype=F32)
    chunk = jnp.arange(RET_BLOCK) // CHUNK
    visible = chunk[None, :] <= chunk[:, None]
    intra = jnp.where(visible[None],
                      jnp.exp(log_g[:, None, None] * jnp.abs(idx[:, None] - idx[None, :])), 0.0)
    q_decay = jnp.exp(log_g[:, None] * (idx + 1.0))
    k_decay = jnp.exp(log_g[:, None] * (RET_BLOCK - 1.0 - idx))
    block_decay = jnp.exp(log_g * RET_BLOCK)
    return (intra,
            jnp.broadcast_to(q_decay[:, :, None], (H_C, RET_BLOCK, DV_C)),
            jnp.broadcast_to(k_decay[:, :, None], (H_C, RET_BLOCK, DK_C)),
            jnp.broadcast_to(block_decay[:, None, None], (H_C, 1, DV_C)))


def _retention_kernel(q_ref, k_ref, v_ref, gate_ref, cos_ref, sin_ref, intra_ref, qd_ref,
                      kd_ref, bd_ref, g_ref, o_ref, state_ref):
    rows = q_ref.shape[1]

    @pl.when(pl.program_id(1) == 0)
    def _():
        state_ref[...] = jnp.zeros_like(state_ref)

    cos = cos_ref[...]
    sin = sin_ref[...]

    def rotary(t):
        return t * cos + pltpu.roll(t, DK_C // 2, 1) * sin

    for h in range(H_C):
        qk_cols = slice(h * DK_C, (h + 1) * DK_C)
        v_cols = slice(h * DV_C, (h + 1) * DV_C)
        q = (rotary(q_ref[0, :, qk_cols]) * (DK_C ** -0.5)).astype(BF16)
        k = rotary(k_ref[0, :, qk_cols])
        state = state_ref[h]
        for r in range(0, rows, RET_BLOCK):
            sl = slice(r, r + RET_BLOCK)
            qc = q[sl]
            kc = k[sl]
            vc = v_ref[0, sl, v_cols].astype(BF16)
            scores = _dot_nt(qc, kc.astype(BF16)) * intra_ref[h]
            y = _dot(scores.astype(BF16), vc) + _dot(qc, state.astype(BF16)) * qd_ref[h]
            state = bd_ref[h] * state + _dot((kc * kd_ref[h]).T.astype(BF16), vc)
            gate = gate_ref[0, sl, v_cols]
            y = _rms_rows(y, g_ref[...])
            o_ref[0, sl, v_cols] = (gate * (1.0 / (1.0 + jnp.exp(-gate))) * y).astype(o_ref.dtype)
        state_ref[h] = state


def _retention(p, ret_norm_g, batch, seq, rows=MIXER_ROWS):
    qk_width = H_C * DK_C
    v_width = H_C * DV_C
    cos, sin = _rotary_tables(seq)
    intra, q_decay, k_decay, block_decay = _decay_tables()

    def whole(a):
        return pl.BlockSpec(a.shape, lambda b, i: (0,) * a.ndim)

    return pl.pallas_call(
        _retention_kernel,
        grid=(batch, seq // rows),
        in_specs=[pl.BlockSpec((1, rows, qk_width), lambda b, i: (b, i, 0)),
                  pl.BlockSpec((1, rows, qk_width), lambda b, i: (b, i, 1)),
                  pl.BlockSpec((1, rows, v_width), lambda b, i: (b, i, 2 * qk_width // v_width)),
                  pl.BlockSpec((1, rows, v_width), lambda b, i: (b, i, 2 * qk_width // v_width + 1)),
                  pl.BlockSpec((rows, DK_C), lambda b, i: (i, 0)),
                  pl.BlockSpec((rows, DK_C), lambda b, i: (i, 0)),
                  whole(intra), whole(q_decay), whole(k_decay), whole(block_decay),
                  pl.BlockSpec((1, DV_C), lambda b, i: (0, 0))],
        out_specs=pl.BlockSpec((1, rows, v_width), lambda b, i: (b, i, 0)),
        out_shape=jax.ShapeDtypeStruct((batch, seq, v_width), BF16),
        scratch_shapes=[pltpu.VMEM((H_C, DK_C, DV_C), F32)],
        compiler_params=_params("arbitrary", "arbitrary"),
        name="retention",
    )(p, p, p, p, cos, sin, intra, q_decay, k_decay, block_decay, ret_norm_g.reshape(1, DV_C))


def _gelu_tanh(x):
    return 0.5 * x * (1.0 + jnp.tanh(math.sqrt(2.0 / math.pi) * (x + 0.044715 * (x * x * x))))


def _sgu_kernel(zu_ref, zv_ref, lg_ref, lb_ref, w_ref, b_ref, o_ref):
    rows = zu_ref.shape[0]
    width = D_D // G_D
    v = _gelu_tanh(zv_ref[...])
    mu = jnp.mean(v, axis=-1, keepdims=True)
    var = jnp.mean(jnp.square(v - mu), axis=-1, keepdims=True)
    vn = ((v - mu) * lax.rsqrt(var + EPS) * lg_ref[...] + lb_ref[...]).astype(BF16)
    pos_i = lax.broadcasted_iota(jnp.int32, (SGU_LEN, SGU_LEN), 0)
    pos_j = lax.broadcasted_iota(jnp.int32, (SGU_LEN, SGU_LEN), 1)
    mask = (pos_j // CHUNK) <= (pos_i // CHUNK)
    for g in range(G_D):
        w = jnp.where(mask, w_ref[g], 0.0).astype(BF16)
        bias = b_ref[g]
        cols = slice(g * width, (g + 1) * width)
        for r in range(rows // SGU_LEN):
            sl = slice(r * SGU_LEN, (r + 1) * SGU_LEN)
            gate = _dot(w, vn[sl, cols]) + bias
            o_ref[sl, cols] = (_gelu_tanh(zu_ref[sl, cols]) * gate).astype(o_ref.dtype)


def _sgu(p, ln_g, ln_b, w_s, b_s, rows=MIXER_ROWS):
    t = p.shape[0]
    u_col = (2 * H_C * DK_C + 2 * H_C * DV_C) // D_D
    return pl.pallas_call(
        _sgu_kernel,
        grid=(t // rows,),
        in_specs=[pl.BlockSpec((rows, D_D), lambda i: (i, u_col)),
                  pl.BlockSpec((rows, D_D), lambda i: (i, u_col + 1)),
                  pl.BlockSpec((1, D_D), lambda i: (0, 0)),
                  pl.BlockSpec((1, D_D), lambda i: (0, 0)),
                  pl.BlockSpec((G_D, SGU_LEN, SGU_LEN), lambda i: (0, 0, 0)),
                  pl.BlockSpec((G_D, SGU_LEN, 1), lambda i: (0, 0, 0))],
        out_specs=pl.BlockSpec((rows, D_D), lambda i: (i, 0)),
        out_shape=jax.ShapeDtypeStruct((t, D_D), BF16),
        compiler_params=_params("parallel"),
        name="spatial_gate",
    )(p, p, ln_g.reshape(1, D_D), ln_b.reshape(1, D_D), w_s, b_s.reshape(G_D, SGU_LEN, 1))


def kernel(x, norm_mix_g, norm_ffn_g, final_norm_g, rel_bias, ab_w_in, ab_w_out, diff_lambda,
           diff_subln_g, cd_w_in, cd_w_out, ret_norm_g, sgu_ln_g, sgu_ln_b, sgu_w, sgu_b,
           ffn_w_up, ffn_conv_w, ffn_conv_b, ffn_w_down):
    batch, seq, d = x.shape
    t = batch * seq
    depth = norm_mix_g.shape[0]
    xt = x.reshape(t, d)
    def rows(w):
        return w.reshape(-1, w.shape[-1])

    ab_w_in = ab_w_in.astype(BF16)
    for layer in range(depth):
        j = layer // 2
        if layer % 2 == 0:
            lam_init = 0.8 - 0.6 * math.exp(-0.3 * layer)
            p = _norm_matmul(xt, norm_mix_g[layer], ab_w_in, j, BF16, AB_PROJ_TILE)
            p3 = p.reshape(batch, seq, p.shape[1])
            if layer == 0:
                o_a, (up_bf,) = _stick_breaking(p3, batch, seq, (rows(ffn_w_up),))
                o_b, (down_bf, abo_bf, cdi_bf, cdo_bf) = _diff_attention(
                    p3, rel_bias, diff_lambda[j], diff_subln_g[j], lam_init, batch, seq,
                    (rows(ffn_w_down), rows(ab_w_out), rows(cd_w_in), rows(cd_w_out)))
                ffn_w_up, ffn_w_down, ab_w_out, cd_w_in, cd_w_out = (
                    b.reshape(w.shape) for b, w in ((up_bf, ffn_w_up), (down_bf, ffn_w_down),
                                                    (abo_bf, ab_w_out), (cdi_bf, cd_w_in),
                                                    (cdo_bf, cd_w_out)))
            else:
                o_a, _ = _stick_breaking(p3, batch, seq)
                o_b, _ = _diff_attention(p3, rel_bias, diff_lambda[j], diff_subln_g[j], lam_init,
                                         batch, seq)
            xt = _out_proj(o_a.reshape(t, -1), o_b.reshape(t, -1), ab_w_out, j, xt)
        else:
            p = _norm_matmul(xt, norm_mix_g[layer], cd_w_in, j, F32, CD_PROJ_TILE)
            o_c = _retention(p.reshape(batch, seq, p.shape[1]), ret_norm_g[j], batch, seq)
            o_d = _sgu(p, sgu_ln_g[j], sgu_ln_b[j], sgu_w[j], sgu_b[j])
            xt = _out_proj(o_c.reshape(t, -1), o_d, cd_w_out, j, xt)
        xt = _ffn(xt, norm_ffn_g, ffn_w_up, ffn_conv_w, ffn_conv_b, ffn_w_down, final_norm_g,
                  layer, seq=seq, final_norm=(layer == depth - 1))
    return xt.reshape(batch, seq, d)
```

```python
import functools
import math

import numpy as np
import jax
import jax.numpy as jnp
from jax import lax
from jax.experimental import pallas as pl
from jax.experimental.pallas import tpu as pltpu

F32 = jnp.float32
BF16 = jnp.bfloat16

EPS = 1e-6
CHUNK = 64
H_A = 8
DH_A = 128
H_B = 4
DK_B = 128
DV_B = 2 * DK_B
NUM_BUCKETS = 32
MAX_DISTANCE = 128
H_C = 4
DK_C = 128
DV_C = 2 * DK_C
ROPE_BASE = 10000.0
D_D = 1024
G_D = 4
SGU_LEN = 128
CONV_WIDTH = 3

SUBLANES = 8
BF16_SUBLANES = 16
ATT_BLOCK = 128
ATT_TILE = 512
STICK_HEADS = 4
RET_BLOCK = 256
VMEM_LIMIT = 62 * 1024 * 1024
PROJ_ROWS = 1024
AB_PROJ_TILE = 2048
CD_PROJ_TILE = 1280
FFN_ROWS = 1024
FFN_TILE = 512
MIXER_ROWS = 512
MASKED_LOGIT = -1e4
LOG2E = math.log2(math.e)
ZERO_WEIGHT_LOG2 = -150.0


def _params(*semantics):
    return pltpu.CompilerParams(dimension_semantics=semantics,
                                vmem_limit_bytes=VMEM_LIMIT)


def _dot(a, b):
    return jnp.dot(a, b, preferred_element_type=F32)


def _dot_nt(a, b):
    return lax.dot_general(a, b, (((1,), (1,)), ((), ())), preferred_element_type=F32)


def _rms_rows(x, g):
    return x * lax.rsqrt(jnp.mean(x * x, axis=-1, keepdims=True) + EPS) * g


def _with_riders(body, n_in, n_riders):
    def kern(*refs):
        ins = refs[:n_in]
        rider_in = refs[n_in:n_in + n_riders]
        out = refs[n_in + n_riders]
        rider_out = refs[n_in + n_riders + 1:n_in + 2 * n_riders + 1]
        scratch = refs[n_in + 2 * n_riders + 1:]
        for src, dst in zip(rider_in, rider_out):
            dst[...] = src[...].astype(BF16)
        body(*ins, out, *scratch)
    return kern


def _rider_specs(riders, grid):
    steps = math.prod(grid)

    def linear_step(*g):
        n = g[0]
        for size, idx in zip(grid[1:], g[1:len(grid)]):
            n = n * size + idx
        return n

    specs, shapes = [], []
    for w in riders:
        rows, cols = w.shape
        block_rows, rem = divmod(rows, steps)
        assert rem == 0 and block_rows % BF16_SUBLANES == 0, (w.shape, steps)
        specs.append(pl.BlockSpec((block_rows, cols), lambda *g: (linear_step(*g), 0)))
        shapes.append(jax.ShapeDtypeStruct((rows, cols), BF16))
    return specs, shapes


def _norm_matmul_kernel(x_ref, g_ref, w_ref, o_ref, h_ref):
    @pl.when(pl.program_id(1) == 0)
    def _():
        h_ref[...] = _rms_rows(x_ref[...], g_ref[...]).astype(BF16)

    o_ref[...] = _dot(h_ref[...], w_ref[...]).astype(o_ref.dtype)


def _norm_matmul(x, g, w, layer, out_dtype, tn, tm=PROJ_ROWS):
    t, d = x.shape
    n = w.shape[2]
    return pl.pallas_call(
        _norm_matmul_kernel,
        grid=(t // tm, n // tn),
        in_specs=[pl.BlockSpec((tm, d), lambda i, j: (i, 0)),
                  pl.BlockSpec((1, d), lambda i, j: (0, 0)),
                  pl.BlockSpec((None, d, tn), lambda i, j: (layer, 0, j))],
        out_specs=pl.BlockSpec((tm, tn), lambda i, j: (i, j)),
        out_shape=jax.ShapeDtypeStruct((t, n), out_dtype),
        scratch_shapes=[pltpu.VMEM((tm, d), BF16)],
        compiler_params=_params("parallel", "arbitrary"),
        name="norm_matmul",
    )(x, g.reshape(1, d), w)


def _out_proj_kernel(a1_ref, a2_ref, w_ref, x_ref, o_ref):
    half = a1_ref.shape[1]
    acc = _dot(a1_ref[...], w_ref[0:half, :])
    acc += _dot(a2_ref[...], w_ref[half:2 * half, :])
    o_ref[...] = x_ref[...] + acc


def _out_proj(a1, a2, w, layer, x, tm=MIXER_ROWS):
    t, half = a1.shape
    d = w.shape[2]
    return pl.pallas_call(
        _out_proj_kernel,
        grid=(t // tm,),
        in_specs=[pl.BlockSpec((tm, half), lambda i: (i, 0)),
                  pl.BlockSpec((tm, half), lambda i: (i, 0)),
                  pl.BlockSpec((None, 2 * half, d), lambda i: (layer, 0, 0)),
                  pl.BlockSpec((tm, d), lambda i: (i, 0))],
        out_specs=pl.BlockSpec((tm, d), lambda i: (i, 0)),
        out_shape=jax.ShapeDtypeStruct((t, d), F32),
        compiler_params=_params("parallel"),
        name="out_proj",
    )(a1, a2, w, x)


def _ffn_kernel(x_ref, g_ref, wa_ref, wg_ref, cwa_ref, cwg_ref, cba_ref, cbg_ref,
                wd_ref, wd_last_ref, fg_ref, o_ref, h_ref, act0_ref, act1_ref, carry_ref,
                *, seq_tiles, final_norm, n_hidden_tiles):
    i = pl.program_id(0)
    j = pl.program_id(1)
    nj = n_hidden_tiles
    tm = x_ref.shape[0]
    acts = (act0_ref, act1_ref)

    def up_phase(out_ref):
        h = h_ref[...]

        def gate(cg, ca):
            return (cg * (1.0 / (1.0 + jnp.exp(-cg))) * ca).astype(BF16)

        def conv(w_ref, cw_ref, cb_ref, branch):
            up = _dot(h, w_ref[...])
            w0 = cw_ref[0:1, :]
            w1 = cw_ref[1:2, :]
            w2 = cw_ref[2:3, :]
            b = cb_ref[...]
            c = b + w0 * pltpu.roll(up, 2, 0) + w1 * pltpu.roll(up, 1, 0) + w2 * up
            both = jnp.concatenate([carry_ref[j, branch], up[0:SUBLANES, :]], axis=0)
            head = slice(SUBLANES, 2 * SUBLANES)
            top = (b + w0 * pltpu.roll(both, 2, 0)[head, :]
                   + w1 * pltpu.roll(both, 1, 0)[head, :] + w2 * both[head, :])
            carry_ref[j, branch] = up[tm - SUBLANES:tm, :]
            return c, top

        ca, ta = conv(wa_ref, cwa_ref, cba_ref, 0)
        cg, tg = conv(wg_ref, cwg_ref, cbg_ref, 1)
        out_ref[...] = gate(cg, ca)
        out_ref[0:SUBLANES, :] = gate(tg, ta)

    def down_phase(in_ref):
        o_ref[...] += _dot(in_ref[...], wd_ref[...])

    @pl.when((i % seq_tiles) == 0)
    def _():
        carry_ref[j] = jnp.zeros(carry_ref.shape[1:], F32)

    @pl.when(j == 0)
    def _():
        x = x_ref[...]
        h_ref[...] = _rms_rows(x, g_ref[...]).astype(BF16)
        o_ref[...] = x
        up_phase(act0_ref)

    for parity in range(2):
        @pl.when((j > 0) & (j < nj - 1) & (j % 2 == parity))
        def _():
            up_phase(acts[parity])
            down_phase(acts[1 - parity])

    @pl.when(j == nj - 1)
    def _():
        parity = (n_hidden_tiles - 1) % 2
        up_phase(acts[parity])
        down_phase(acts[1 - parity])
        y = o_ref[...] + _dot(acts[parity][...], wd_last_ref[...])
        if final_norm:
            y = _rms_rows(y, fg_ref[...])
        o_ref[...] = y


def _ffn(x, g, w_up, conv_w, conv_b, w_down, final_g, layer, *, seq, final_norm,
         tm=FFN_ROWS, tf=FFN_TILE):
    t, d = x.shape
    f = w_down.shape[1]
    nj = f // tf
    assert nj >= 2
    kern = functools.partial(_ffn_kernel, seq_tiles=seq // tm, final_norm=final_norm,
                             n_hidden_tiles=nj)

    def down_row(j):
        return jnp.maximum(j - 1, 0)

    return pl.pallas_call(
        kern,
        grid=(t // tm, nj),
        in_specs=[pl.BlockSpec((tm, d), lambda i, j: (i, 0)),
                  pl.BlockSpec((None, 1, d), lambda i, j: (layer, 0, 0)),
                  pl.BlockSpec((None, d, tf), lambda i, j: (layer, 0, j)),
                  pl.BlockSpec((None, d, tf), lambda i, j: (layer, 0, j + nj)),
                  pl.BlockSpec((None, CONV_WIDTH, tf), lambda i, j: (layer, 0, j)),
                  pl.BlockSpec((None, CONV_WIDTH, tf), lambda i, j: (layer, 0, j + nj)),
                  pl.BlockSpec((None, 1, tf), lambda i, j: (layer, 0, j)),
                  pl.BlockSpec((None, 1, tf), lambda i, j: (layer, 0, j + nj)),
                  pl.BlockSpec((None, tf, d), lambda i, j: (layer, down_row(j), 0)),
                  pl.BlockSpec((None, tf, d), lambda i, j: (layer, nj - 1, 0)),
                  pl.BlockSpec((1, d), lambda i, j: (0, 0))],
        out_specs=pl.BlockSpec((tm, d), lambda i, j: (i, 0)),
        out_shape=jax.ShapeDtypeStruct((t, d), F32),
        scratch_shapes=[pltpu.VMEM((tm, d), BF16),
                        pltpu.VMEM((tm, tf), BF16),
                        pltpu.VMEM((tm, tf), BF16),
                        pltpu.VMEM((nj, 2, SUBLANES, tf), F32)],
        compiler_params=_params("arbitrary", "arbitrary"),
        name="conv_ffn",
    )(x, g[:, None, :], w_up, w_up, conv_w, conv_w, conv_b[:, None, :], conv_b[:, None, :],
      w_down, w_down, final_g.reshape(1, d))


def _suffix_sum_matrix():
    blk = ATT_BLOCK
    j = np.arange(blk)[:, None]
    s = np.arange(blk)[None, :]
    m = np.concatenate([(j > s).astype(np.float32), np.ones((blk, blk), np.float32)], axis=1)
    return jnp.asarray(np.concatenate([m, m], axis=0), dtype=BF16)


def _stick_kernel(q_ref, k_ref, v_ref, m_ref, o_ref, acc_ref, carry_ref):
    blk = ATT_BLOCK
    tile = q_ref.shape[1]
    half = tile // 2
    i = pl.program_id(2)
    scale2 = DH_A ** -0.5 * LOG2E
    suffix = m_ref[...]
    heads = range(STICK_HEADS)

    acc_ref[...] = jnp.zeros_like(acc_ref)
    carry_ref[...] = jnp.zeros_like(carry_ref)

    def sweep(hd, r0, r1, key_start, n_sub, diagonal=False):
        rows = slice(r0, r1)
        cols = slice(hd * DH_A, (hd + 1) * DH_A)
        k = k_ref[0, pl.ds(key_start, n_sub * blk), cols]
        v = v_ref[0, pl.ds(key_start, n_sub * blk), cols]
        z_all = _dot_nt(q_ref[0, rows, cols], k) * scale2
        carry = carry_ref[hd, rows, :]
        ws = [None] * n_sub
        for c in reversed(range(n_sub)):
            z = z_all[:, c * blk:(c + 1) * blk]
            if diagonal:
                row = lax.broadcasted_iota(jnp.int32, z.shape, 0) + r0
                col = lax.broadcasted_iota(jnp.int32, z.shape, 1) + c * blk
                z = jnp.where(col < row, z, MASKED_LOGIT)
            log_sig = jnp.minimum(z, 0.0) - jnp.log2(1.0 + jnp.exp2(-jnp.abs(z)))
            log_1m = log_sig - z
            hi = log_1m.astype(BF16)
            lo = (log_1m - hi.astype(F32)).astype(BF16)
            sums = _dot(jnp.concatenate([hi, lo], axis=1), suffix)
            ws[c] = jnp.exp2(log_sig + sums[:, :blk] + carry).astype(BF16)
            carry = carry + sums[:, blk:]
        carry_ref[hd, rows, :] = carry
        acc_ref[hd, rows, :] += _dot(jnp.concatenate(ws, axis=1), v)

    def alive(hd, r0, r1):
        return jnp.max(carry_ref[hd, r0:r1, :]) > ZERO_WEIGHT_LOG2

    tile_start = pl.multiple_of(i * tile, tile)
    n_sub = tile // blk
    def own_tile():
        for hd in heads:
            sweep(hd, 0, tile, tile_start, n_sub, diagonal=True)

    pl.when(i == 0)(own_tile)

    @pl.when(i > 0)
    def _():
        prev = pl.multiple_of(tile_start - tile, tile)
        own_tile()
        for hd in heads:
            sweep(hd, 0, half, prev + half, n_sub // 2)

        top_alive = [alive(hd, 0, half) for hd in heads]
        bottom_alive = [alive(hd, half, tile) for hd in heads]
        for hd in heads:
            @pl.when(bottom_alive[hd])
            def _():
                sweep(hd, half, tile, prev + half, n_sub // 2)

            @pl.when(jnp.logical_or(top_alive[hd], bottom_alive[hd]))
            def _():
                sweep(hd, 0, tile, prev, n_sub // 2)

                def more(t):
                    return jnp.logical_and(t < i, alive(hd, 0, tile))

                def body(t):
                    sweep(hd, 0, tile, pl.multiple_of(tile_start - (t + 1) * tile, tile), n_sub)
                    return t + 1

                lax.while_loop(more, body, 1)

    o_ref[0] = jnp.concatenate([acc_ref[hd] for hd in heads], axis=1).astype(o_ref.dtype)


def _stick_breaking(p, batch, seq, riders=()):
    blk = ATT_BLOCK
    tile = ATT_TILE
    width = STICK_HEADS * DH_A
    groups = H_A // STICK_HEADS
    grid = (batch, groups, seq // tile)
    rider_specs, rider_shapes = _rider_specs(riders, grid)
    out = pl.pallas_call(
        _with_riders(_stick_kernel, 4, len(riders)),
        grid=grid,
        in_specs=[pl.BlockSpec((1, tile, width), lambda b, h, i: (b, i, h)),
                  pl.BlockSpec((1, seq, width), lambda b, h, i: (b, 0, groups + h)),
                  pl.BlockSpec((1, seq, width), lambda b, h, i: (b, 0, 2 * groups + h)),
                  pl.BlockSpec((2 * blk, 2 * blk), lambda b, h, i: (0, 0))] + rider_specs,
        out_specs=[pl.BlockSpec((1, tile, width), lambda b, h, i: (b, i, h))] + rider_specs,
        out_shape=[jax.ShapeDtypeStruct((batch, seq, H_A * DH_A), BF16)] + rider_shapes,
        scratch_shapes=[pltpu.VMEM((STICK_HEADS, tile, DH_A), F32),
                        pltpu.VMEM((STICK_HEADS, tile, blk), F32)],
        compiler_params=_params("arbitrary", "arbitrary", "arbitrary"),
        name="stick_breaking",
    )(p, p, p, _suffix_sum_matrix(), *riders)
    return out[0], out[1:]


def _rel_bucket(rel):
    nb = NUM_BUCKETS // 2
    max_exact = nb // 2
    ret = jnp.where(rel > 0, nb, 0)
    n = jnp.abs(rel)
    n_f = jnp.maximum(n, 1).astype(F32)
    large = max_exact + (jnp.log(n_f / max_exact) / math.log(MAX_DISTANCE / max_exact)
                         * (nb - max_exact)).astype(jnp.int32)
    large = jnp.minimum(large, nb - 1)
    return ret + jnp.where(n < max_exact, n, large)


def _near_buckets():
    blk = ATT_BLOCK
    qpos = blk + jnp.arange(blk)
    kpos = jnp.arange(2 * blk)
    near = _rel_bucket(kpos[None, :] - qpos[:, None]).astype(jnp.int32)
    far = _rel_bucket(jnp.full((1,), -2 * blk, jnp.int32)).astype(jnp.int32)
    return near, far


def _diff_kernel(far_ref, relb_ref, q_ref, k_ref, v_ref, bucket_ref, lam_ref, g_ref, o_ref,
                 bias_ref, m_ref, l_ref, acc_ref, s0_ref, s1_ref, *, lam_init):
    blk = ATT_BLOCK
    tile = q_ref.shape[1]
    nsub = tile // blk
    h = pl.program_id(1)
    i = pl.program_id(2)
    scale2 = DK_B ** -0.5 * LOG2E

    @pl.when(i == 0)
    def _():
        bucket = bucket_ref[...]
        far = relb_ref[far_ref[0], h]
        near = jnp.zeros((blk, 2 * blk), F32)
        for b in range(NUM_BUCKETS):
            near = jnp.where(bucket == b, relb_ref[b, h], near)
        near = (near - far) * LOG2E
        qpos = blk + lax.broadcasted_iota(jnp.int32, (blk, 2 * blk), 0)
        kpos = lax.broadcasted_iota(jnp.int32, (blk, 2 * blk), 1)
        near = jnp.where((kpos // CHUNK) <= (qpos // CHUNK), near, -jnp.inf)
        bias_ref[...] = jnp.zeros_like(bias_ref)
        for a in range(nsub):
            rows = slice(a * blk, (a + 1) * blk)
            bias_ref[0, rows, a * blk:(a + 1) * blk] = near[:, blk:]
            if a >= 1:
                bias_ref[0, rows, (a - 1) * blk:a * blk] = near[:, :blk]
            if a + 1 < nsub:
                bias_ref[0, rows, (a + 1) * blk:] = jnp.full((blk, tile - (a + 1) * blk),
                                                             -jnp.inf, F32)
        bias_ref[1, 0:blk, (nsub - 1) * blk:] = near[:, :blk]

    m_ref[...] = jnp.full_like(m_ref, -jnp.inf)
    l_ref[...] = jnp.zeros_like(l_ref)
    acc_ref[...] = jnp.zeros_like(acc_ref)

    def key_tile(idx):
        return jnp.where(idx == 0, i, jnp.where(idx == 1, i - 1, idx - 2))

    def logits_into(s_ref, idx):
        j = key_tile(jnp.minimum(idx, i))
        kk = k_ref[0, pl.ds(pl.multiple_of(j * tile, tile), tile), :]
        for c in range(2):
            s_ref[c] = _dot_nt(q_ref[0, :, c * DK_B:(c + 1) * DK_B],
                               kk[:, c * DK_B:(c + 1) * DK_B])

    def consume(s_ref, idx):
        j = key_tile(idx)
        v = v_ref[0, pl.ds(pl.multiple_of(j * tile, tile), tile), :]
        bias = bias_ref[jnp.minimum(idx, 2)]
        probs = []
        for c in range(2):
            s = s_ref[c] * scale2 + bias
            subs = [s[:, u * blk:(u + 1) * blk] for u in range(nsub)]
            m_prev = m_ref[c]
            m_cur = functools.reduce(jnp.maximum, subs)
            m_new = jnp.maximum(m_prev, jnp.max(m_cur, axis=-1, keepdims=True))
            alpha = jnp.exp2(m_prev - m_new)
            ps = [jnp.exp2(u - m_new) for u in subs]
            l_ref[c] = alpha * l_ref[c] + functools.reduce(jnp.add, ps)
            m_ref[c] = m_new
            probs.append((alpha, jnp.concatenate([u.astype(BF16) for u in ps], axis=1)))
        for c in range(2):
            alpha, p = probs[c]
            acc_ref[c] = jnp.concatenate([alpha] * (DV_B // blk), axis=1) * acc_ref[c] + _dot(p, v)

    logits_into(s0_ref, 0)

    def pair(t, _):
        idx = 2 * t
        logits_into(s1_ref, idx + 1)
        consume(s0_ref, idx)
        logits_into(s0_ref, idx + 2)
        consume(s1_ref, idx + 1)
        return 0

    lax.fori_loop(0, (i + 1) // 2, pair, 0)

    @pl.when(i % 2 == 0)
    def _():
        consume(s0_ref, i)

    lv = lam_ref[...]
    lam = (jnp.exp(jnp.sum(lv[0:1] * lv[1:2], axis=-1, keepdims=True))
           - jnp.exp(jnp.sum(lv[2:3] * lv[3:4], axis=-1, keepdims=True)) + lam_init)
    l1 = jnp.sum(l_ref[0], axis=-1, keepdims=True)
    l2 = jnp.sum(l_ref[1], axis=-1, keepdims=True)
    o = acc_ref[0] / l1 - lam * (acc_ref[1] / l2)
    o_ref[0] = (_rms_rows(o, g_ref[...]) * (1.0 - lam_init)).astype(o_ref.dtype)


def _diff_attention(p, rel_bias, lam_vecs, subln_g, lam_init, batch, seq, riders=()):
    blk = ATT_BLOCK
    tile = ATT_TILE
    q_col = 3 * H_A * DH_A // DV_B
    k_col = q_col + H_B
    v_col = k_col + H_B
    near, far = _near_buckets()
    grid = (batch, H_B, seq // tile)
    rider_specs, rider_shapes = _rider_specs(riders, grid)
    kern = _with_riders(functools.partial(_diff_kernel, lam_init=lam_init), 8, len(riders))
    grid_spec = pltpu.PrefetchScalarGridSpec(
        num_scalar_prefetch=2,
        grid=grid,
        in_specs=[pl.BlockSpec((1, tile, DV_B), lambda b, h, i, *_: (b, i, q_col + h)),
                  pl.BlockSpec((1, seq, DV_B), lambda b, h, i, *_: (b, 0, k_col + h)),
                  pl.BlockSpec((1, seq, DV_B), lambda b, h, i, *_: (b, 0, v_col + h)),
                  pl.BlockSpec((blk, 2 * blk), lambda b, h, i, *_: (0, 0)),
                  pl.BlockSpec((4, DK_B), lambda b, h, i, *_: (0, 0)),
                  pl.BlockSpec((1, DV_B), lambda b, h, i, *_: (0, 0))] + rider_specs,
        out_specs=[pl.BlockSpec((1, tile, DV_B), lambda b, h, i, *_: (b, i, h))] + rider_specs,
        scratch_shapes=[pltpu.VMEM((3, tile, tile), F32),
                        pltpu.VMEM((2, tile, blk), F32),
                        pltpu.VMEM((2, tile, blk), F32),
                        pltpu.VMEM((2, tile, DV_B), F32),
                        pltpu.VMEM((2, tile, tile), F32),
                        pltpu.VMEM((2, tile, tile), F32)])
    out = pl.pallas_call(
        kern,
        grid_spec=grid_spec,
        out_shape=[jax.ShapeDtypeStruct((batch, seq, H_B * DV_B), BF16)] + rider_shapes,
        compiler_params=_params("arbitrary", "arbitrary", "arbitrary"),
        name="diff_attention",
    )(far, rel_bias.astype(F32), p, p, p, near, lam_vecs.astype(F32), subln_g.reshape(1, DV_B),
      *riders)
    return out[0], out[1:]


def _rotary_tables(seq):
    inv_freq = ROPE_BASE ** (-jnp.arange(0, DK_C, 2, dtype=F32) / DK_C)
    ang = jnp.arange(seq, dtype=F32)[:, None] * inv_freq[None, :]
    cos = jnp.concatenate([jnp.cos(ang), jnp.cos(ang)], axis=-1)
    sin = jnp.concatenate([-jnp.sin(ang), jnp.sin(ang)], axis=-1)
    return cos, sin


def _decay_tables():
    log_g = jnp.log(1.0 - 2.0 ** (-5.0 - jnp.arange(H_C, dtype=F32)))
    idx = jnp.arange(RET_BLOCK, dtype=F32)
    chunk = jnp.arange(RET_BLOCK) // CHUNK
    visible = chunk[None, :] <= chunk[:, None]
    intra = jnp.where(visible[None],
                      jnp.exp(log_g[:, None, None] * jnp.abs(idx[:, None] - idx[None, :])), 0.0)
    q_decay = jnp.exp(log_g[:, None] * (idx + 1.0))
    k_decay = jnp.exp(log_g[:, None] * (RET_BLOCK - 1.0 - idx))
    block_decay = jnp.exp(log_g * RET_BLOCK)
    return (intra,
            jnp.broadcast_to(q_decay[:, :, None], (H_C, RET_BLOCK, DV_C)),
            jnp.broadcast_to(k_decay[:, :, None], (H_C, RET_BLOCK, DK_C)),
            jnp.broadcast_to(block_decay[:, None, None], (H_C, 1, DV_C)))


def _retention_kernel(q_ref, k_ref, v_ref, gate_ref, cos_ref, sin_ref, intra_ref, qd_ref,
                      kd_ref, bd_ref, g_ref, o_ref, state_ref):
    rows = q_ref.shape[1]

    @pl.when(pl.program_id(1) == 0)
    def _():
        state_ref[...] = jnp.zeros_like(state_ref)

    cos = cos_ref[...]
    sin = sin_ref[...]

    def rotary(t):
        return t * cos + pltpu.roll(t, DK_C // 2, 1) * sin

    for h in range(H_C):
        qk_cols = slice(h * DK_C, (h + 1) * DK_C)
        v_cols = slice(h * DV_C, (h + 1) * DV_C)
        q = (rotary(q_ref[0, :, qk_cols]) * (DK_C ** -0.5)).astype(BF16)
        k = rotary(k_ref[0, :, qk_cols])
        state = state_ref[h]
        for r in range(0, rows, RET_BLOCK):
            sl = slice(r, r + RET_BLOCK)
            qc = q[sl]
            kc = k[sl]
            vc = v_ref[0, sl, v_cols].astype(BF16)
            scores = _dot_nt(qc, kc.astype(BF16)) * intra_ref[h]
            y = _dot(scores.astype(BF16), vc) + _dot(qc, state.astype(BF16)) * qd_ref[h]
            state = bd_ref[h] * state + _dot((kc * kd_ref[h]).T.astype(BF16), vc)
            gate = gate_ref[0, sl, v_cols]
            y = _rms_rows(y, g_ref[...])
            o_ref[0, sl, v_cols] = (gate * (1.0 / (1.0 + jnp.exp(-gate))) * y).astype(o_ref.dtype)
        state_ref[h] = state


def _retention(p, ret_norm_g, batch, seq, rows=MIXER_ROWS):
    qk_width = H_C * DK_C
    v_width = H_C * DV_C
    cos, sin = _rotary_tables(seq)
    intra, q_decay, k_decay, block_decay = _decay_tables()

    def whole(a):
        return pl.BlockSpec(a.shape, lambda b, i: (0,) * a.ndim)

    return pl.pallas_call(
        _retention_kernel,
        grid=(batch, seq // rows),
        in_specs=[pl.BlockSpec((1, rows, qk_width), lambda b, i: (b, i, 0)),
                  pl.BlockSpec((1, rows, qk_width), lambda b, i: (b, i, 1)),
                  pl.BlockSpec((1, rows, v_width), lambda b, i: (b, i, 2 * qk_width // v_width)),
                  pl.BlockSpec((1, rows, v_width), lambda b, i: (b, i, 2 * qk_width // v_width + 1)),
                  pl.BlockSpec((rows, DK_C), lambda b, i: (i, 0)),
                  pl.BlockSpec((rows, DK_C), lambda b, i: (i, 0)),
                  whole(intra), whole(q_decay), whole(k_decay), whole(block_decay),
                  pl.BlockSpec((1, DV_C), lambda b, i: (0, 0))],
        out_specs=pl.BlockSpec((1, rows, v_width), lambda b, i: (b, i, 0)),
        out_shape=jax.ShapeDtypeStruct((batch, seq, v_width), BF16),
        scratch_shapes=[pltpu.VMEM((H_C, DK_C, DV_C), F32)],
        compiler_params=_params("arbitrary", "arbitrary"),
        name="retention",
    )(p, p, p, p, cos, sin, intra, q_decay, k_decay, block_decay, ret_norm_g.reshape(1, DV_C))


def _gelu_tanh(x):
    return 0.5 * x * (1.0 + jnp.tanh(math.sqrt(2.0 / math.pi) * (x + 0.044715 * (x * x * x))))


def _sgu_kernel(zu_ref, zv_ref, lg_ref, lb_ref, w_ref, b_ref, o_ref):
    rows = zu_ref.shape[0]
    width = D_D // G_D
    v = _gelu_tanh(zv_ref[...])
    mu = jnp.mean(v, axis=-1, keepdims=True)
    var = jnp.mean(jnp.square(v - mu), axis=-1, keepdims=True)
    vn = ((v - mu) * lax.rsqrt(var + EPS) * lg_ref[...] + lb_ref[...]).astype(BF16)
    pos_i = lax.broadcasted_iota(jnp.int32, (SGU_LEN, SGU_LEN), 0)
    pos_j = lax.broadcasted_iota(jnp.int32, (SGU_LEN, SGU_LEN), 1)
    mask = (pos_j // CHUNK) <= (pos_i // CHUNK)
    for g in range(G_D):
        w = jnp.where(mask, w_ref[g], 0.0).astype(BF16)
        bias = b_ref[g]
        cols = slice(g * width, (g + 1) * width)
        for r in range(rows // SGU_LEN):
            sl = slice(r * SGU_LEN, (r + 1) * SGU_LEN)
            gate = _dot(w, vn[sl, cols]) + bias
            o_ref[sl, cols] = (_gelu_tanh(zu_ref[sl, cols]) * gate).astype(o_ref.dtype)


def _sgu(p, ln_g, ln_b, w_s, b_s, rows=MIXER_ROWS):
    t = p.shape[0]
    u_col = (2 * H_C * DK_C + 2 * H_C * DV_C) // D_D
    return pl.pallas_call(
        _sgu_kernel,
        grid=(t // rows,),
        in_specs=[pl.BlockSpec((rows, D_D), lambda i: (i, u_col)),
                  pl.BlockSpec((rows, D_D), lambda i: (i, u_col + 1)),
                  pl.BlockSpec((1, D_D), lambda i: (0, 0)),
                  pl.BlockSpec((1, D_D), lambda i: (0, 0)),
                  pl.BlockSpec((G_D, SGU_LEN, SGU_LEN), lambda i: (0, 0, 0)),
                  pl.BlockSpec((G_D, SGU_LEN, 1), lambda i: (0, 0, 0))],
        out_specs=pl.BlockSpec((rows, D_D), lambda i: (i, 0)),
        out_shape=jax.ShapeDtypeStruct((t, D_D), BF16),
        compiler_params=_params("parallel"),
        name="spatial_gate",
    )(p, p, ln_g.reshape(1, D_D), ln_b.reshape(1, D_D), w_s, b_s.reshape(G_D, SGU_LEN, 1))


def kernel(x, norm_mix_g, norm_ffn_g, final_norm_g, rel_bias, ab_w_in, ab_w_out, diff_lambda,
           diff_subln_g, cd_w_in, cd_w_out, ret_norm_g, sgu_ln_g, sgu_ln_b, sgu_w, sgu_b,
           ffn_w_up, ffn_conv_w, ffn_conv_b, ffn_w_down):
    batch, seq, d = x.shape
    t = batch * seq
    depth = norm_mix_g.shape[0]
    xt = x.reshape(t, d)
    def rows(w):
        return w.reshape(-1, w.shape[-1])

    ab_w_in = ab_w_in.astype(BF16)
    for layer in range(depth):
        j = layer // 2
        if layer % 2 == 0:
            lam_init = 0.8 - 0.6 * math.exp(-0.3 * layer)
            p = _norm_matmul(xt, norm_mix_g[layer], ab_w_in, j, BF16, AB_PROJ_TILE)
            p3 = p.reshape(batch, seq, p.shape[1])
            if layer == 0:
                o_a, (up_bf,) = _stick_breaking(p3, batch, seq, (rows(ffn_w_up),))
                o_b, (down_bf, abo_bf, cdi_bf, cdo_bf) = _diff_attention(
                    p3, rel_bias, diff_lambda[j], diff_subln_g[j], lam_init, batch, seq,
                    (rows(ffn_w_down), rows(ab_w_out), rows(cd_w_in), rows(cd_w_out)))
                ffn_w_up, ffn_w_down, ab_w_out, cd_w_in, cd_w_out = (
                    b.reshape(w.shape) for b, w in ((up_bf, ffn_w_up), (down_bf, ffn_w_down),
                                                    (abo_bf, ab_w_out), (cdi_bf, cd_w_in),
                                                    (cdo_bf, cd_w_out)))
            else:
                o_a, _ = _stick_breaking(p3, batch, seq)
                o_b, _ = _diff_attention(p3, rel_bias, diff_lambda[j], diff_subln_g[j], lam_init,
                                         batch, seq)
            xt = _out_proj(o_a.reshape(t, -1), o_b.reshape(t, -1), ab_w_out, j, xt)
        else:
            p = _norm_matmul(xt, norm_mix_g[layer], cd_w_in, j, F32, CD_PROJ_TILE)
            o_c = _retention(p.reshape(batch, seq, p.shape[1]), ret_norm_g[j], batch, seq)
            o_d = _sgu(p, sgu_ln_g[j], sgu_ln_b[j], sgu_w[j], sgu_b[j])
            xt = _out_proj(o_c.reshape(t, -1), o_d, cd_w_out, j, xt)
        xt = _ffn(xt, norm_ffn_g, ffn_w_up, ffn_conv_w, ffn_conv_b, ffn_w_down, final_norm_g,
                  layer, seq=seq, final_norm=(layer == depth - 1))
    return xt.reshape(batch, seq, d)
```

```python
import functools
import math

import numpy as np
import jax
import jax.numpy as jnp
from jax import lax
from jax.experimental import pallas as pl
from jax.experimental.pallas import tpu as pltpu

F32 = jnp.float32
BF16 = jnp.bfloat16

EPS = 1e-6
CHUNK = 64
H_A = 8
DH_A = 128
H_B = 4
DK_B = 128
DV_B = 2 * DK_B
NUM_BUCKETS = 32
MAX_DISTANCE = 128
H_C = 4
DK_C = 128
DV_C = 2 * DK_C
ROPE_BASE = 10000.0
D_D = 1024
G_D = 4
SGU_LEN = 128
CONV_WIDTH = 3

SUBLANES = 8
BF16_SUBLANES = 16
ATT_BLOCK = 128
ATT_TILE = 512
STICK_HEADS = 4
DIFF_HEADS = 2
RET_BLOCK = 256
VMEM_LIMIT = 62 * 1024 * 1024
PROJ_ROWS = 1024
AB_PROJ_TILE = 1024
CD_PROJ_TILE = 1280
FFN_ROWS = 1024
FFN_TILE = 512
MIXER_ROWS = 512
MASKED_LOGIT = -1e4
LOG2E = math.log2(math.e)
ZERO_WEIGHT_LOG2 = -150.0


def _params(*semantics):
    return pltpu.CompilerParams(dimension_semantics=semantics,
                                vmem_limit_bytes=VMEM_LIMIT)


def _dot(a, b):
    return jnp.dot(a, b, preferred_element_type=F32)


def _dot_nt(a, b):
    return lax.dot_general(a, b, (((1,), (1,)), ((), ())), preferred_element_type=F32)


def _rms_rows(x, g):
    return x * lax.rsqrt(jnp.mean(x * x, axis=-1, keepdims=True) + EPS) * g


def _with_riders(body, n_in, n_riders):
    def kern(*refs):
        ins = refs[:n_in]
        rider_in = refs[n_in:n_in + n_riders]
        out = refs[n_in + n_riders]
        rider_out = refs[n_in + n_riders + 1:n_in + 2 * n_riders + 1]
        scratch = refs[n_in + 2 * n_riders + 1:]
        for src, dst in zip(rider_in, rider_out):
            dst[...] = src[...].astype(BF16)
        body(*ins, out, *scratch)
    return kern


def _rider_specs(riders, grid):
    steps = math.prod(grid)

    def linear_step(*g):
        n = g[0]
        for size, idx in zip(grid[1:], g[1:len(grid)]):
            n = n * size + idx
        return n

    specs, shapes = [], []
    for w in riders:
        rows, cols = w.shape
        block_rows, rem = divmod(rows, steps)
        assert rem == 0 and block_rows % BF16_SUBLANES == 0, (w.shape, steps)
        specs.append(pl.BlockSpec((block_rows, cols), lambda *g: (linear_step(*g), 0)))
        shapes.append(jax.ShapeDtypeStruct((rows, cols), BF16))
    return specs, shapes


def _norm_matmul_kernel(x_ref, g_ref, w_ref, o_ref, h_ref):
    @pl.when(pl.program_id(1) == 0)
    def _():
        h_ref[...] = _rms_rows(x_ref[...], g_ref[...]).astype(BF16)

    w = w_ref[...]
    if w.dtype != BF16:
        w = w.astype(BF16)
    o_ref[...] = _dot(h_ref[...], w).astype(o_ref.dtype)


def _norm_matmul(x, g, w, layer, out_dtype, tn, tm=PROJ_ROWS):
    t, d = x.shape
    n = w.shape[2]
    return pl.pallas_call(
        _norm_matmul_kernel,
        grid=(t // tm, n // tn),
        in_specs=[pl.BlockSpec((tm, d), lambda i, j: (i, 0)),
                  pl.BlockSpec((1, d), lambda i, j: (0, 0)),
                  pl.BlockSpec((None, d, tn), lambda i, j: (layer, 0, j))],
        out_specs=pl.BlockSpec((tm, tn), lambda i, j: (i, j)),
        out_shape=jax.ShapeDtypeStruct((t, n), out_dtype),
        scratch_shapes=[pltpu.VMEM((tm, d), BF16)],
        compiler_params=_params("parallel", "arbitrary"),
        name="norm_matmul",
    )(x, g.reshape(1, d), w)


def _out_proj_kernel(a1_ref, a2_ref, w_ref, x_ref, o_ref):
    half = a1_ref.shape[1]
    acc = _dot(a1_ref[...], w_ref[0:half, :])
    acc += _dot(a2_ref[...], w_ref[half:2 * half, :])
    o_ref[...] = x_ref[...] + acc


def _out_proj(a1, a2, w, layer, x, tm=MIXER_ROWS):
    t, half = a1.shape
    d = w.shape[2]
    return pl.pallas_call(
        _out_proj_kernel,
        grid=(t // tm,),
        in_specs=[pl.BlockSpec((tm, half), lambda i: (i, 0)),
                  pl.BlockSpec((tm, half), lambda i: (i, 0)),
                  pl.BlockSpec((None, 2 * half, d), lambda i: (layer, 0, 0)),
                  pl.BlockSpec((tm, d), lambda i: (i, 0))],
        out_specs=pl.BlockSpec((tm, d), lambda i: (i, 0)),
        out_shape=jax.ShapeDtypeStruct((t, d), F32),
        compiler_params=_params("parallel"),
        name="out_proj",
    )(a1, a2, w, x)


def _ffn_kernel(x_ref, g_ref, wa_ref, wg_ref, cwa_ref, cwg_ref, cba_ref, cbg_ref,
                wd_ref, wd_last_ref, fg_ref, o_ref, h_ref, act0_ref, act1_ref, carry_ref,
                *, seq_tiles, final_norm, n_hidden_tiles):
    i = pl.program_id(0)
    j = pl.program_id(1)
    nj = n_hidden_tiles
    tm = x_ref.shape[0]
    acts = (act0_ref, act1_ref)

    def up_phase(out_ref):
        h = h_ref[...]

        def gate(cg, ca):
            return (cg * (1.0 / (1.0 + jnp.exp(-cg))) * ca).astype(BF16)

        def conv(w_ref, cw_ref, cb_ref, branch):
            up = _dot(h, w_ref[...])
            w0 = cw_ref[0:1, :]
            w1 = cw_ref[1:2, :]
            w2 = cw_ref[2:3, :]
            b = cb_ref[...]
            c = b + w0 * pltpu.roll(up, 2, 0) + w1 * pltpu.roll(up, 1, 0) + w2 * up
            both = jnp.concatenate([carry_ref[j, branch], up[0:SUBLANES, :]], axis=0)
            head = slice(SUBLANES, 2 * SUBLANES)
            top = (b + w0 * pltpu.roll(both, 2, 0)[head, :]
                   + w1 * pltpu.roll(both, 1, 0)[head, :] + w2 * both[head, :])
            carry_ref[j, branch] = up[tm - SUBLANES:tm, :]
            return c, top

        ca, ta = conv(wa_ref, cwa_ref, cba_ref, 0)
        cg, tg = conv(wg_ref, cwg_ref, cbg_ref, 1)
        out_ref[...] = gate(cg, ca)
        out_ref[0:SUBLANES, :] = gate(tg, ta)

    def down_phase(in_ref):
        o_ref[...] += _dot(in_ref[...], wd_ref[...])

    @pl.when((i % seq_tiles) == 0)
    def _():
        carry_ref[j] = jnp.zeros(carry_ref.shape[1:], F32)

    @pl.when(j == 0)
    def _():
        x = x_ref[...]
        h_ref[...] = _rms_rows(x, g_ref[...]).astype(BF16)
        o_ref[...] = x
        up_phase(act0_ref)

    for parity in range(2):
        @pl.when((j > 0) & (j < nj - 1) & (j % 2 == parity))
        def _():
            up_phase(acts[parity])
            down_phase(acts[1 - parity])

    @pl.when(j == nj - 1)
    def _():
        parity = (n_hidden_tiles - 1) % 2
        up_phase(acts[parity])
        down_phase(acts[1 - parity])
        y = o_ref[...] + _dot(acts[parity][...], wd_last_ref[...])
        if final_norm:
            y = _rms_rows(y, fg_ref[...])
        o_ref[...] = y


def _ffn(x, g, w_up, conv_w, conv_b, w_down, final_g, layer, *, seq, final_norm,
         tm=FFN_ROWS, tf=FFN_TILE):
    t, d = x.shape
    f = w_down.shape[1]
    nj = f // tf
    assert nj >= 2
    kern = functools.partial(_ffn_kernel, seq_tiles=seq // tm, final_norm=final_norm,
                             n_hidden_tiles=nj)

    def down_row(j):
        return jnp.maximum(j - 1, 0)

    return pl.pallas_call(
        kern,
        grid=(t // tm, nj),
        in_specs=[pl.BlockSpec((tm, d), lambda i, j: (i, 0)),
                  pl.BlockSpec((None, 1, d), lambda i, j: (layer, 0, 0)),
                  pl.BlockSpec((None, d, tf), lambda i, j: (layer, 0, j)),
                  pl.BlockSpec((None, d, tf), lambda i, j: (layer, 0, j + nj)),
                  pl.BlockSpec((None, CONV_WIDTH, tf), lambda i, j: (layer, 0, j)),
                  pl.BlockSpec((None, CONV_WIDTH, tf), lambda i, j: (layer, 0, j + nj)),
                  pl.BlockSpec((None, 1, tf), lambda i, j: (layer, 0, j)),
                  pl.BlockSpec((None, 1, tf), lambda i, j: (layer, 0, j + nj)),
                  pl.BlockSpec((None, tf, d), lambda i, j: (layer, down_row(j), 0)),
                  pl.BlockSpec((None, tf, d), lambda i, j: (layer, nj - 1, 0)),
                  pl.BlockSpec((1, d), lambda i, j: (0, 0))],
        out_specs=pl.BlockSpec((tm, d), lambda i, j: (i, 0)),
        out_shape=jax.ShapeDtypeStruct((t, d), F32),
        scratch_shapes=[pltpu.VMEM((tm, d), BF16),
                        pltpu.VMEM((tm, tf), BF16),
                        pltpu.VMEM((tm, tf), BF16),
                        pltpu.VMEM((nj, 2, SUBLANES, tf), F32)],
        compiler_params=_params("arbitrary", "arbitrary"),
        name="conv_ffn",
    )(x, g[:, None, :], w_up, w_up, conv_w, conv_w, conv_b[:, None, :], conv_b[:, None, :],
      w_down, w_down, final_g.reshape(1, d))


def _suffix_sum_matrix():
    blk = ATT_BLOCK
    j = np.arange(blk)[:, None]
    s = np.arange(blk)[None, :]
    m = np.concatenate([(j > s).astype(np.float32), np.ones((blk, blk), np.float32)], axis=1)
    return jnp.asarray(np.concatenate([m, m], axis=0), dtype=BF16)


def _stick_kernel(q_ref, k_ref, v_ref, m_ref, o_ref, acc_ref, carry_ref):
    blk = ATT_BLOCK
    tile = q_ref.shape[1]
    half = tile // 2
    i = pl.program_id(2)
    scale2 = DH_A ** -0.5 * LOG2E
    suffix = m_ref[...]
    heads = range(STICK_HEADS)

    acc_ref[...] = jnp.zeros_like(acc_ref)
    carry_ref[...] = jnp.zeros_like(carry_ref)

    def sweep(hd, r0, r1, key_start, n_sub, diagonal=False):
        rows = slice(r0, r1)
        cols = slice(hd * DH_A, (hd + 1) * DH_A)
        k = k_ref[0, pl.ds(key_start, n_sub * blk), cols]
        v = v_ref[0, pl.ds(key_start, n_sub * blk), cols]
        z_all = _dot_nt(q_ref[0, rows, cols], k) * scale2
        carry = carry_ref[hd, rows, :]
        ws = [None] * n_sub
        for c in reversed(range(n_sub)):
            z = z_all[:, c * blk:(c + 1) * blk]
            if diagonal:
                row = lax.broadcasted_iota(jnp.int32, z.shape, 0) + r0
                col = lax.broadcasted_iota(jnp.int32, z.shape, 1) + c * blk
                z = jnp.where(col < row, z, MASKED_LOGIT)
            log_sig = jnp.minimum(z, 0.0) - jnp.log2(1.0 + jnp.exp2(-jnp.abs(z)))
            log_1m = log_sig - z
            hi = log_1m.astype(BF16)
            lo = (log_1m - hi.astype(F32)).astype(BF16)
            sums = _dot(jnp.concatenate([hi, lo], axis=1), suffix)
            ws[c] = jnp.exp2(log_sig + sums[:, :blk] + carry).astype(BF16)
            carry = carry + sums[:, blk:]
        carry_ref[hd, rows, :] = carry
        acc_ref[hd, rows, :] += _dot(jnp.concatenate(ws, axis=1), v)

    def alive(hd, r0, r1):
        return jnp.max(carry_ref[hd, r0:r1, :]) > ZERO_WEIGHT_LOG2

    tile_start = pl.multiple_of(i * tile, tile)
    n_sub = tile // blk
    for hd in heads:
        sweep(hd, 0, tile, tile_start, n_sub, diagonal=True)

    @pl.when(i > 0)
    def _():
        prev = pl.multiple_of(tile_start - tile, tile)
        for hd in heads:
            sweep(hd, 0, half, prev + half, n_sub // 2)

        top_alive = [alive(hd, 0, half) for hd in heads]
        bottom_alive = [alive(hd, half, tile) for hd in heads]
        for hd in heads:
            @pl.when(bottom_alive[hd])
            def _():
                sweep(hd, half, tile, prev + half, n_sub // 2)

            @pl.when(jnp.logical_or(top_alive[hd], bottom_alive[hd]))
            def _():
                sweep(hd, 0, tile, prev, n_sub // 2)

                def more(t):
                    return jnp.logical_and(t < i, alive(hd, 0, tile))

                def body(t):
                    sweep(hd, 0, tile, pl.multiple_of(tile_start - (t + 1) * tile, tile), n_sub)
                    return t + 1

                lax.while_loop(more, body, 1)

    o_ref[0] = jnp.concatenate([acc_ref[hd] for hd in heads], axis=1).astype(o_ref.dtype)


def _stick_breaking(p, batch, seq, riders=()):
    blk = ATT_BLOCK
    tile = ATT_TILE
    width = STICK_HEADS * DH_A
    groups = H_A // STICK_HEADS
    grid = (batch, groups, seq // tile)
    rider_specs, rider_shapes = _rider_specs(riders, grid)
    out = pl.pallas_call(
        _with_riders(_stick_kernel, 4, len(riders)),
        grid=grid,
        in_specs=[pl.BlockSpec((1, tile, width), lambda b, h, i: (b, i, h)),
                  pl.BlockSpec((1, seq, width), lambda b, h, i: (b, 0, groups + h)),
                  pl.BlockSpec((1, seq, width), lambda b, h, i: (b, 0, 2 * groups + h)),
                  pl.BlockSpec((2 * blk, 2 * blk), lambda b, h, i: (0, 0))] + rider_specs,
        out_specs=[pl.BlockSpec((1, tile, width), lambda b, h, i: (b, i, h))] + rider_specs,
        out_shape=[jax.ShapeDtypeStruct((batch, seq, H_A * DH_A), BF16)] + rider_shapes,
        scratch_shapes=[pltpu.VMEM((STICK_HEADS, tile, DH_A), F32),
                        pltpu.VMEM((STICK_HEADS, tile, blk), F32)],
        compiler_params=_params("arbitrary", "arbitrary", "arbitrary"),
        name="stick_breaking",
    )(p, p, p, _suffix_sum_matrix(), *riders)
    return out[0], out[1:]


def _rel_bucket(rel):
    nb = NUM_BUCKETS // 2
    max_exact = nb // 2
    ret = jnp.where(rel > 0, nb, 0)
    n = jnp.abs(rel)
    n_f = jnp.maximum(n, 1).astype(F32)
    large = max_exact + (jnp.log(n_f / max_exact) / math.log(MAX_DISTANCE / max_exact)
                         * (nb - max_exact)).astype(jnp.int32)
    large = jnp.minimum(large, nb - 1)
    return ret + jnp.where(n < max_exact, n, large)


def _near_buckets():
    blk = ATT_BLOCK
    qpos = blk + jnp.arange(blk)
    kpos = jnp.arange(2 * blk)
    near = _rel_bucket(kpos[None, :] - qpos[:, None]).astype(jnp.int32)
    far = _rel_bucket(jnp.full((1,), -2 * blk, jnp.int32)).astype(jnp.int32)
    return near, far


def _diff_kernel(far_ref, relb_ref, q_ref, k_ref, v_ref, bucket_ref, lam_ref, g_ref, o_ref,
                 bias_ref, m_ref, l_ref, acc_ref, s0_ref, s1_ref, *, lam_init):
    blk = ATT_BLOCK
    tile = q_ref.shape[1]
    nsub = tile // blk
    group = pl.program_id(1)
    i = pl.program_id(2)
    scale2 = DK_B ** -0.5 * LOG2E
    heads = range(DIFF_HEADS)

    @pl.when(i == 0)
    def _():
        bucket = bucket_ref[...]
        qpos = blk + lax.broadcasted_iota(jnp.int32, (blk, 2 * blk), 0)
        kpos = lax.broadcasted_iota(jnp.int32, (blk, 2 * blk), 1)
        allowed = (kpos // CHUNK) <= (qpos // CHUNK)
        bias_ref[...] = jnp.zeros_like(bias_ref)
        for hd in heads:
            h = group * DIFF_HEADS + hd
            far = relb_ref[far_ref[0], h]
            near = jnp.zeros((blk, 2 * blk), F32)
            for b in range(NUM_BUCKETS):
                near = jnp.where(bucket == b, relb_ref[b, h], near)
            near = jnp.where(allowed, (near - far) * LOG2E, -jnp.inf)
            for a in range(nsub):
                rows = slice(a * blk, (a + 1) * blk)
                bias_ref[hd, 0, rows, a * blk:(a + 1) * blk] = near[:, blk:]
                if a >= 1:
                    bias_ref[hd, 0, rows, (a - 1) * blk:a * blk] = near[:, :blk]
                if a + 1 < nsub:
                    bias_ref[hd, 0, rows, (a + 1) * blk:] = jnp.full(
                        (blk, tile - (a + 1) * blk), -jnp.inf, F32)
            bias_ref[hd, 1, 0:blk, (nsub - 1) * blk:] = near[:, :blk]

    m_ref[...] = jnp.full_like(m_ref, -jnp.inf)
    l_ref[...] = jnp.zeros_like(l_ref)
    acc_ref[...] = jnp.zeros_like(acc_ref)

    def key_tile(idx):
        return jnp.where(idx == 0, i, jnp.where(idx == 1, i - 1, idx - 2))

    def logits_into(hd, s_ref, idx):
        j = key_tile(jnp.minimum(idx, i))
        rows = pl.ds(pl.multiple_of(j * tile, tile), tile)
        for c in range(2):
            cols = slice(hd * DV_B + c * DK_B, hd * DV_B + (c + 1) * DK_B)
            s_ref[hd, c] = _dot_nt(q_ref[0, :, cols], k_ref[0, rows, cols])

    def consume(hd, s_ref, idx):
        j = key_tile(idx)
        v = v_ref[0, pl.ds(pl.multiple_of(j * tile, tile), tile), hd * DV_B:(hd + 1) * DV_B]
        bias = bias_ref[hd, jnp.minimum(idx, 2)]
        probs = []
        for c in range(2):
            s = s_ref[hd, c] * scale2 + bias
            subs = [s[:, u * blk:(u + 1) * blk] for u in range(nsub)]
            m_prev = m_ref[hd, c]
            m_cur = functools.reduce(jnp.maximum, subs)
            m_new = jnp.maximum(m_prev, jnp.max(m_cur, axis=-1, keepdims=True))
            alpha = jnp.exp2(m_prev - m_new)
            ps = [jnp.exp2(u - m_new) for u in subs]
            l_ref[hd, c] = alpha * l_ref[hd, c] + functools.reduce(jnp.add, ps)
            m_ref[hd, c] = m_new
            probs.append((alpha, jnp.concatenate([u.astype(BF16) for u in ps], axis=1)))
        for c in range(2):
            alpha, p = probs[c]
            acc_ref[hd, c] = (jnp.concatenate([alpha] * (DV_B // blk), axis=1) * acc_ref[hd, c]
                              + _dot(p, v))

    for hd in heads:
        logits_into(hd, s0_ref, 0)

    for hd in heads:
        def pair(t, _, hd=hd):
            idx = 2 * t
            logits_into(hd, s1_ref, idx + 1)
            consume(hd, s0_ref, idx)
            logits_into(hd, s0_ref, idx + 2)
            consume(hd, s1_ref, idx + 1)
            return 0

        lax.fori_loop(0, (i + 1) // 2, pair, 0)

    @pl.when(i % 2 == 0)
    def _():
        for hd in heads:
            consume(hd, s0_ref, i)

    lv = lam_ref[...]
    lam = (jnp.exp(jnp.sum(lv[0:1] * lv[1:2], axis=-1, keepdims=True))
           - jnp.exp(jnp.sum(lv[2:3] * lv[3:4], axis=-1, keepdims=True)) + lam_init)
    for hd in heads:
        l1 = jnp.sum(l_ref[hd, 0], axis=-1, keepdims=True)
        l2 = jnp.sum(l_ref[hd, 1], axis=-1, keepdims=True)
        o = acc_ref[hd, 0] / l1 - lam * (acc_ref[hd, 1] / l2)
        o_ref[0, :, hd * DV_B:(hd + 1) * DV_B] = (
            _rms_rows(o, g_ref[...]) * (1.0 - lam_init)).astype(o_ref.dtype)


def _diff_attention(p, rel_bias, lam_vecs, subln_g, lam_init, batch, seq, riders=()):
    blk = ATT_BLOCK
    tile = ATT_TILE
    width = DIFF_HEADS * DV_B
    groups = H_B // DIFF_HEADS
    q_col = 3 * H_A * DH_A // width
    k_col = q_col + groups
    v_col = k_col + groups
    near, far = _near_buckets()
    grid = (batch, groups, seq // tile)
    rider_specs, rider_shapes = _rider_specs(riders, grid)
    kern = _with_riders(functools.partial(_diff_kernel, lam_init=lam_init), 8, len(riders))
    grid_spec = pltpu.PrefetchScalarGridSpec(
        num_scalar_prefetch=2,
        grid=grid,
        in_specs=[pl.BlockSpec((1, tile, width), lambda b, h, i, *_: (b, i, q_col + h)),
                  pl.BlockSpec((1, seq, width), lambda b, h, i, *_: (b, 0, k_col + h)),
                  pl.BlockSpec((1, seq, width), lambda b, h, i, *_: (b, 0, v_col + h)),
                  pl.BlockSpec((blk, 2 * blk), lambda b, h, i, *_: (0, 0)),
                  pl.BlockSpec((4, DK_B), lambda b, h, i, *_: (0, 0)),
                  pl.BlockSpec((1, DV_B), lambda b, h, i, *_: (0, 0))] + rider_specs,
        out_specs=[pl.BlockSpec((1, tile, width), lambda b, h, i, *_: (b, i, h))] + rider_specs,
        scratch_shapes=[pltpu.VMEM((DIFF_HEADS, 3, tile, tile), F32),
                        pltpu.VMEM((DIFF_HEADS, 2, tile, blk), F32),
                        pltpu.VMEM((DIFF_HEADS, 2, tile, blk), F32),
                        pltpu.VMEM((DIFF_HEADS, 2, tile, DV_B), F32),
                        pltpu.VMEM((DIFF_HEADS, 2, tile, tile), F32),
                        pltpu.VMEM((DIFF_HEADS, 2, tile, tile), F32)])
    out = pl.pallas_call(
        kern,
        grid_spec=grid_spec,
        out_shape=[jax.ShapeDtypeStruct((batch, seq, H_B * DV_B), BF16)] + rider_shapes,
        compiler_params=_params("arbitrary", "arbitrary", "arbitrary"),
        name="diff_attention",
    )(far, rel_bias.astype(F32), p, p, p, near, lam_vecs.astype(F32), subln_g.reshape(1, DV_B),
      *riders)
    return out[0], out[1:]


def _rotary_tables(seq):
    inv_freq = ROPE_BASE ** (-jnp.arange(0, DK_C, 2, dtype=F32) / DK_C)
    ang = jnp.arange(seq, dtype=F32)[:, None] * inv_freq[None, :]
    cos = jnp.concatenate([jnp.cos(ang), jnp.cos(ang)], axis=-1)
    sin = jnp.concatenate([-jnp.sin(ang), jnp.sin(ang)], axis=-1)
    return cos, sin


def _decay_tables():
    log_g = jnp.log(1.0 - 2.0 ** (-5.0 - jnp.arange(H_C, dtype=F32)))
    idx = jnp.arange(RET_BLOCK, dtype=F32)
    chunk = jnp.arange(RET_BLOCK) // CHUNK
    visible = chunk[None, :] <= chunk[:, None]
    intra = jnp.where(visible[None],
                      jnp.exp(log_g[:, None, None] * jnp.abs(idx[:, None] - idx[None, :])), 0.0)
    q_decay = jnp.exp(log_g[:, None] * (idx + 1.0))
    k_decay = jnp.exp(log_g[:, None] * (RET_BLOCK - 1.0 - idx))
    block_decay = jnp.exp(log_g * RET_BLOCK)
    return (intra,
            jnp.broadcast_to(q_decay[:, :, None], (H_C, RET_BLOCK, DV_C)),
            jnp.broadcast_to(k_decay[:, :, None], (H_C, RET_BLOCK, DK_C)),
            jnp.broadcast_to(block_decay[:, None, None], (H_C, 1, DV_C)))


def _retention_kernel(q_ref, k_ref, v_ref, gate_ref, cos_ref, sin_ref, intra_ref, qd_ref,
                      kd_ref, bd_ref, g_ref, o_ref, state_ref):
    rows = q_ref.shape[1]

    @pl.when(pl.program_id(1) == 0)
    def _():
        state_ref[...] = jnp.zeros_like(state_ref)

    cos = cos_ref[...]
    sin = sin_ref[...]

    def rotary(t):
        return t * cos + pltpu.roll(t, DK_C // 2, 1) * sin

    for h in range(H_C):
        qk_cols = slice(h * DK_C, (h + 1) * DK_C)
        v_cols = slice(h * DV_C, (h + 1) * DV_C)
        q = (rotary(q_ref[0, :, qk_cols]) * (DK_C ** -0.5)).astype(BF16)
        k = rotary(k_ref[0, :, qk_cols])
        state = state_ref[h]
        for r in range(0, rows, RET_BLOCK):
            sl = slice(r, r + RET_BLOCK)
            qc = q[sl]
            kc = k[sl]
            vc = v_ref[0, sl, v_cols].astype(BF16)
            scores = _dot_nt(qc, kc.astype(BF16)) * intra_ref[h]
            y = _dot(scores.astype(BF16), vc) + _dot(qc, state.astype(BF16)) * qd_ref[h]
            state = bd_ref[h] * state + _dot((kc * kd_ref[h]).T.astype(BF16), vc)
            gate = gate_ref[0, sl, v_cols]
            y = _rms_rows(y, g_ref[...])
            o_ref[0, sl, v_cols] = (gate * (1.0 / (1.0 + jnp.exp(-gate))) * y).astype(o_ref.dtype)
        state_ref[h] = state


def _retention(p, ret_norm_g, batch, seq, rows=MIXER_ROWS):
    qk_width = H_C * DK_C
    v_width = H_C * DV_C
    cos, sin = _rotary_tables(seq)
    intra, q_decay, k_decay, block_decay = _decay_tables()

    def whole(a):
        return pl.BlockSpec(a.shape, lambda b, i: (0,) * a.ndim)

    return pl.pallas_call(
        _retention_kernel,
        grid=(batch, seq // rows),
        in_specs=[pl.BlockSpec((1, rows, qk_width), lambda b, i: (b, i, 0)),
                  pl.BlockSpec((1, rows, qk_width), lambda b, i: (b, i, 1)),
                  pl.BlockSpec((1, rows, v_width), lambda b, i: (b, i, 2 * qk_width // v_width)),
                  pl.BlockSpec((1, rows, v_width), lambda b, i: (b, i, 2 * qk_width // v_width + 1)),
                  pl.BlockSpec((rows, DK_C), lambda b, i: (i, 0)),
                  pl.BlockSpec((rows, DK_C), lambda b, i: (i, 0)),
                  whole(intra), whole(q_decay), whole(k_decay), whole(block_decay),
                  pl.BlockSpec((1, DV_C), lambda b, i: (0, 0))],
        out_specs=pl.BlockSpec((1, rows, v_width), lambda b, i: (b, i, 0)),
        out_shape=jax.ShapeDtypeStruct((batch, seq, v_width), BF16),
        scratch_shapes=[pltpu.VMEM((H_C, DK_C, DV_C), F32)],
        compiler_params=_params("arbitrary", "arbitrary"),
        name="retention",
    )(p, p, p, p, cos, sin, intra, q_decay, k_decay, block_decay, ret_norm_g.reshape(1, DV_C))


def _gelu_tanh(x):
    return 0.5 * x * (1.0 + jnp.tanh(math.sqrt(2.0 / math.pi) * (x + 0.044715 * (x * x * x))))


def _sgu_kernel(zu_ref, zv_ref, lg_ref, lb_ref, w_ref, b_ref, o_ref):
    rows = zu_ref.shape[0]
    width = D_D // G_D
    v = _gelu_tanh(zv_ref[...])
    mu = jnp.mean(v, axis=-1, keepdims=True)
    var = jnp.mean(jnp.square(v - mu), axis=-1, keepdims=True)
    vn = ((v - mu) * lax.rsqrt(var + EPS) * lg_ref[...] + lb_ref[...]).astype(BF16)
    pos_i = lax.broadcasted_iota(jnp.int32, (SGU_LEN, SGU_LEN), 0)
    pos_j = lax.broadcasted_iota(jnp.int32, (SGU_LEN, SGU_LEN), 1)
    mask = (pos_j // CHUNK) <= (pos_i // CHUNK)
    for g in range(G_D):
        w = jnp.where(mask, w_ref[g], 0.0).astype(BF16)
        bias = b_ref[g]
        cols = slice(g * width, (g + 1) * width)
        for r in range(rows // SGU_LEN):
            sl = slice(r * SGU_LEN, (r + 1) * SGU_LEN)
            gate = _dot(w, vn[sl, cols]) + bias
            o_ref[sl, cols] = (_gelu_tanh(zu_ref[sl, cols]) * gate).astype(o_ref.dtype)


def _sgu(p, ln_g, ln_b, w_s, b_s, rows=MIXER_ROWS):
    t = p.shape[0]
    u_col = (2 * H_C * DK_C + 2 * H_C * DV_C) // D_D
    return pl.pallas_call(
        _sgu_kernel,
        grid=(t // rows,),
        in_specs=[pl.BlockSpec((rows, D_D), lambda i: (i, u_col)),
                  pl.BlockSpec((rows, D_D), lambda i: (i, u_col + 1)),
                  pl.BlockSpec((1, D_D), lambda i: (0, 0)),
                  pl.BlockSpec((1, D_D), lambda i: (0, 0)),
                  pl.BlockSpec((G_D, SGU_LEN, SGU_LEN), lambda i: (0, 0, 0)),
                  pl.BlockSpec((G_D, SGU_LEN, 1), lambda i: (0, 0, 0))],
        out_specs=pl.BlockSpec((rows, D_D), lambda i: (i, 0)),
        out_shape=jax.ShapeDtypeStruct((t, D_D), BF16),
        compiler_params=_params("parallel"),
        name="spatial_gate",
    )(p, p, ln_g.reshape(1, D_D), ln_b.reshape(1, D_D), w_s, b_s.reshape(G_D, SGU_LEN, 1))


def kernel(x, norm_mix_g, norm_ffn_g, final_norm_g, rel_bias, ab_w_in, ab_w_out, diff_lambda,
           diff_subln_g, cd_w_in, cd_w_out, ret_norm_g, sgu_ln_g, sgu_ln_b, sgu_w, sgu_b,
           ffn_w_up, ffn_conv_w, ffn_conv_b, ffn_w_down):
    batch, seq, d = x.shape
    t = batch * seq
    depth = norm_mix_g.shape[0]
    xt = x.reshape(t, d)
    def rows(w):
        return w.reshape(-1, w.shape[-1])

    for layer in range(depth):
        j = layer // 2
        if layer % 2 == 0:
            lam_init = 0.8 - 0.6 * math.exp(-0.3 * layer)
            p = _norm_matmul(xt, norm_mix_g[layer], ab_w_in, j, BF16, AB_PROJ_TILE)
            p3 = p.reshape(batch, seq, p.shape[1])
            if layer == 0:
                o_a, (up_bf,) = _stick_breaking(p3, batch, seq, (rows(ffn_w_up),))
                o_b, (down_bf, abo_bf, cdi_bf, cdo_bf) = _diff_attention(
                    p3, rel_bias, diff_lambda[j], diff_subln_g[j], lam_init, batch, seq,
                    (rows(ffn_w_down), rows(ab_w_out), rows(cd_w_in), rows(cd_w_out)))
                ffn_w_up, ffn_w_down, ab_w_out, cd_w_in, cd_w_out = (
                    b.reshape(w.shape) for b, w in ((up_bf, ffn_w_up), (down_bf, ffn_w_down),
                                                    (abo_bf, ab_w_out), (cdi_bf, cd_w_in),
                                                    (cdo_bf, cd_w_out)))
            else:
                o_a, _ = _stick_breaking(p3, batch, seq)
                o_b, _ = _diff_attention(p3, rel_bias, diff_lambda[j], diff_subln_g[j], lam_init,
                                         batch, seq)
            xt = _out_proj(o_a.reshape(t, -1), o_b.reshape(t, -1), ab_w_out, j, xt)
        else:
            p = _norm_matmul(xt, norm_mix_g[layer], cd_w_in, j, F32, CD_PROJ_TILE)
            o_c = _retention(p.reshape(batch, seq, p.shape[1]), ret_norm_g[j], batch, seq)
            o_d = _sgu(p, sgu_ln_g[j], sgu_ln_b[j], sgu_w[j], sgu_b[j])
            xt = _out_proj(o_c.reshape(t, -1), o_d, cd_w_out, j, xt)
        xt = _ffn(xt, norm_ffn_g, ffn_w_up, ffn_conv_w, ffn_conv_b, ffn_w_down, final_norm_g,
                  layer, seq=seq, final_norm=(layer == depth - 1))
    return xt.reshape(batch, seq, d)
```

```python
import functools
import math

import numpy as np
import jax
import jax.numpy as jnp
from jax import lax
from jax.experimental import pallas as pl
from jax.experimental.pallas import tpu as pltpu

F32 = jnp.float32
BF16 = jnp.bfloat16

EPS = 1e-6
CHUNK = 64
H_A = 8
DH_A = 128
H_B = 4
DK_B = 128
DV_B = 2 * DK_B
NUM_BUCKETS = 32
MAX_DISTANCE = 128
H_C = 4
DK_C = 128
DV_C = 2 * DK_C
ROPE_BASE = 10000.0
D_D = 1024
G_D = 4
SGU_LEN = 128
CONV_WIDTH = 3

SUBLANES = 8
BF16_SUBLANES = 16
ATT_BLOCK = 128
ATT_TILE = 512
STICK_HEADS = 4
DIFF_HEADS = 2
RET_BLOCK = 256
VMEM_LIMIT = 62 * 1024 * 1024
PROJ_ROWS = 1024
AB_PROJ_TILE = 1536
CD_PROJ_TILE = 1280
FFN_ROWS = 1024
FFN_TILE = 512
MIXER_ROWS = 512
MASKED_LOGIT = -1e4
LOG2E = math.log2(math.e)
ZERO_WEIGHT_LOG2 = -150.0


def _params(*semantics):
    return pltpu.CompilerParams(dimension_semantics=semantics,
                                vmem_limit_bytes=VMEM_LIMIT)


def _dot(a, b):
    return jnp.dot(a, b, preferred_element_type=F32)


def _dot_nt(a, b):
    return lax.dot_general(a, b, (((1,), (1,)), ((), ())), preferred_element_type=F32)


def _rms_rows(x, g):
    return x * lax.rsqrt(jnp.mean(x * x, axis=-1, keepdims=True) + EPS) * g


def _with_riders(body, n_in, n_riders):
    def kern(*refs):
        ins = refs[:n_in]
        rider_in = refs[n_in:n_in + n_riders]
        out = refs[n_in + n_riders]
        rider_out = refs[n_in + n_riders + 1:n_in + 2 * n_riders + 1]
        scratch = refs[n_in + 2 * n_riders + 1:]
        for src, dst in zip(rider_in, rider_out):
            dst[...] = src[...].astype(BF16)
        body(*ins, out, *scratch)
    return kern


def _rider_specs(riders, grid):
    steps = math.prod(grid)

    def linear_step(*g):
        n = g[0]
        for size, idx in zip(grid[1:], g[1:len(grid)]):
            n = n * size + idx
        return n

    specs, shapes = [], []
    for w in riders:
        rows, cols = w.shape
        block_rows, rem = divmod(rows, steps)
        assert rem == 0 and block_rows % BF16_SUBLANES == 0, (w.shape, steps)
        specs.append(pl.BlockSpec((block_rows, cols), lambda *g: (linear_step(*g), 0)))
        shapes.append(jax.ShapeDtypeStruct((rows, cols), BF16))
    return specs, shapes


def _norm_matmul_kernel(x_ref, g_ref, w_ref, o_ref, h_ref):
    @pl.when(pl.program_id(1) == 0)
    def _():
        h_ref[...] = _rms_rows(x_ref[...], g_ref[...]).astype(BF16)

    w = w_ref[...]
    if w.dtype != BF16:
        w = w.astype(BF16)
    o_ref[...] = _dot(h_ref[...], w).astype(o_ref.dtype)


def _norm_matmul(x, g, w, layer, out_dtype, tn, tm=PROJ_ROWS):
    t, d = x.shape
    n = w.shape[2]
    return pl.pallas_call(
        _norm_matmul_kernel,
        grid=(t // tm, n // tn),
        in_specs=[pl.BlockSpec((tm, d), lambda i, j: (i, 0)),
                  pl.BlockSpec((1, d), lambda i, j: (0, 0)),
                  pl.BlockSpec((None, d, tn), lambda i, j: (layer, 0, j))],
        out_specs=pl.BlockSpec((tm, tn), lambda i, j: (i, j)),
        out_shape=jax.ShapeDtypeStruct((t, n), out_dtype),
        scratch_shapes=[pltpu.VMEM((tm, d), BF16)],
        compiler_params=_params("parallel", "arbitrary"),
        name="norm_matmul",
    )(x, g.reshape(1, d), w)


def _out_proj_kernel(a1_ref, a2_ref, w_ref, x_ref, o_ref):
    half = a1_ref.shape[1]
    acc = _dot(a1_ref[...], w_ref[0:half, :])
    acc += _dot(a2_ref[...], w_ref[half:2 * half, :])
    o_ref[...] = x_ref[...] + acc


def _out_proj(a1, a2, w, layer, x, tm=MIXER_ROWS):
    t, half = a1.shape
    d = w.shape[2]
    return pl.pallas_call(
        _out_proj_kernel,
        grid=(t // tm,),
        in_specs=[pl.BlockSpec((tm, half), lambda i: (i, 0)),
                  pl.BlockSpec((tm, half), lambda i: (i, 0)),
                  pl.BlockSpec((None, 2 * half, d), lambda i: (layer, 0, 0)),
                  pl.BlockSpec((tm, d), lambda i: (i, 0))],
        out_specs=pl.BlockSpec((tm, d), lambda i: (i, 0)),
        out_shape=jax.ShapeDtypeStruct((t, d), F32),
        compiler_params=_params("parallel"),
        name="out_proj",
    )(a1, a2, w, x)


def _ffn_kernel(x_ref, g_ref, wa_ref, wg_ref, cwa_ref, cwg_ref, cba_ref, cbg_ref,
                wd_ref, wd_last_ref, fg_ref, o_ref, h_ref, act0_ref, act1_ref, carry_ref,
                *, seq_tiles, final_norm, n_hidden_tiles):
    i = pl.program_id(0)
    j = pl.program_id(1)
    nj = n_hidden_tiles
    tm = x_ref.shape[0]
    acts = (act0_ref, act1_ref)

    def up_phase(out_ref):
        h = h_ref[...]

        def gate(cg, ca):
            return (cg * (1.0 / (1.0 + jnp.exp(-cg))) * ca).astype(BF16)

        def conv(w_ref, cw_ref, cb_ref, branch):
            up = _dot(h, w_ref[...])
            w0 = cw_ref[0:1, :]
            w1 = cw_ref[1:2, :]
            w2 = cw_ref[2:3, :]
            b = cb_ref[...]
            c = b + w0 * pltpu.roll(up, 2, 0) + w1 * pltpu.roll(up, 1, 0) + w2 * up
            both = jnp.concatenate([carry_ref[j, branch], up[0:SUBLANES, :]], axis=0)
            head = slice(SUBLANES, 2 * SUBLANES)
            top = (b + w0 * pltpu.roll(both, 2, 0)[head, :]
                   + w1 * pltpu.roll(both, 1, 0)[head, :] + w2 * both[head, :])
            carry_ref[j, branch] = up[tm - SUBLANES:tm, :]
            return c, top

        ca, ta = conv(wa_ref, cwa_ref, cba_ref, 0)
        cg, tg = conv(wg_ref, cwg_ref, cbg_ref, 1)
        out_ref[...] = gate(cg, ca)
        out_ref[0:SUBLANES, :] = gate(tg, ta)

    def down_phase(in_ref):
        o_ref[...] += _dot(in_ref[...], wd_ref[...])

    @pl.when((i % seq_tiles) == 0)
    def _():
        carry_ref[j] = jnp.zeros(carry_ref.shape[1:], F32)

    @pl.when(j == 0)
    def _():
        x = x_ref[...]
        h_ref[...] = _rms_rows(x, g_ref[...]).astype(BF16)
        o_ref[...] = x
        up_phase(act0_ref)

    for parity in range(2):
        @pl.when((j > 0) & (j < nj - 1) & (j % 2 == parity))
        def _():
            up_phase(acts[parity])
            down_phase(acts[1 - parity])

    @pl.when(j == nj - 1)
    def _():
        parity = (n_hidden_tiles - 1) % 2
        up_phase(acts[parity])
        down_phase(acts[1 - parity])
        y = o_ref[...] + _dot(acts[parity][...], wd_last_ref[...])
        if final_norm:
            y = _rms_rows(y, fg_ref[...])
        o_ref[...] = y


def _ffn(x, g, w_up, conv_w, conv_b, w_down, final_g, layer, *, seq, final_norm,
         tm=FFN_ROWS, tf=FFN_TILE):
    t, d = x.shape
    f = w_down.shape[1]
    nj = f // tf
    assert nj >= 2
    kern = functools.partial(_ffn_kernel, seq_tiles=seq // tm, final_norm=final_norm,
                             n_hidden_tiles=nj)

    def down_row(j):
        return jnp.maximum(j - 1, 0)

    return pl.pallas_call(
        kern,
        grid=(t // tm, nj),
        in_specs=[pl.BlockSpec((tm, d), lambda i, j: (i, 0)),
                  pl.BlockSpec((None, 1, d), lambda i, j: (layer, 0, 0)),
                  pl.BlockSpec((None, d, tf), lambda i, j: (layer, 0, j)),
                  pl.BlockSpec((None, d, tf), lambda i, j: (layer, 0, j + nj)),
                  pl.BlockSpec((None, CONV_WIDTH, tf), lambda i, j: (layer, 0, j)),
                  pl.BlockSpec((None, CONV_WIDTH, tf), lambda i, j: (layer, 0, j + nj)),
                  pl.BlockSpec((None, 1, tf), lambda i, j: (layer, 0, j)),
                  pl.BlockSpec((None, 1, tf), lambda i, j: (layer, 0, j + nj)),
                  pl.BlockSpec((None, tf, d), lambda i, j: (layer, down_row(j), 0)),
                  pl.BlockSpec((None, tf, d), lambda i, j: (layer, nj - 1, 0)),
                  pl.BlockSpec((1, d), lambda i, j: (0, 0))],
        out_specs=pl.BlockSpec((tm, d), lambda i, j: (i, 0)),
        out_shape=jax.ShapeDtypeStruct((t, d), F32),
        scratch_shapes=[pltpu.VMEM((tm, d), BF16),
                        pltpu.VMEM((tm, tf), BF16),
                        pltpu.VMEM((tm, tf), BF16),
                        pltpu.VMEM((nj, 2, SUBLANES, tf), F32)],
        compiler_params=_params("arbitrary", "arbitrary"),
        name="conv_ffn",
    )(x, g[:, None, :], w_up, w_up, conv_w, conv_w, conv_b[:, None, :], conv_b[:, None, :],
      w_down, w_down, final_g.reshape(1, d))


def _suffix_sum_matrix():
    blk = ATT_BLOCK
    j = np.arange(blk)[:, None]
    s = np.arange(blk)[None, :]
    m = np.concatenate([(j > s).astype(np.float32), np.ones((blk, blk), np.float32)], axis=1)
    return jnp.asarray(np.concatenate([m, m], axis=0), dtype=BF16)


def _stick_kernel(q_ref, k_ref, v_ref, m_ref, o_ref, acc_ref, carry_ref):
    blk = ATT_BLOCK
    tile = q_ref.shape[1]
    half = tile // 2
    i = pl.program_id(2)
    scale2 = DH_A ** -0.5 * LOG2E
    suffix = m_ref[...]
    heads = range(STICK_HEADS)

    acc_ref[...] = jnp.zeros_like(acc_ref)
    carry_ref[...] = jnp.zeros_like(carry_ref)

    def sweep(hd, r0, r1, key_start, n_sub, diagonal=False):
        rows = slice(r0, r1)
        cols = slice(hd * DH_A, (hd + 1) * DH_A)
        k = k_ref[0, pl.ds(key_start, n_sub * blk), cols]
        v = v_ref[0, pl.ds(key_start, n_sub * blk), cols]
        z_all = _dot_nt(q_ref[0, rows, cols], k) * scale2
        carry = carry_ref[hd, rows, :]
        ws = [None] * n_sub
        for c in reversed(range(n_sub)):
            z = z_all[:, c * blk:(c + 1) * blk]
            if diagonal:
                row = lax.broadcasted_iota(jnp.int32, z.shape, 0) + r0
                col = lax.broadcasted_iota(jnp.int32, z.shape, 1) + c * blk
                z = jnp.where(col < row, z, MASKED_LOGIT)
            log_sig = jnp.minimum(z, 0.0) - jnp.log2(1.0 + jnp.exp2(-jnp.abs(z)))
            log_1m = log_sig - z
            hi = log_1m.astype(BF16)
            lo = (log_1m - hi.astype(F32)).astype(BF16)
            sums = _dot(jnp.concatenate([hi, lo], axis=1), suffix)
            ws[c] = jnp.exp2(log_sig + sums[:, :blk] + carry).astype(BF16)
            carry = carry + sums[:, blk:]
        carry_ref[hd, rows, :] = carry
        acc_ref[hd, rows, :] += _dot(jnp.concatenate(ws, axis=1), v)

    def alive(hd, r0, r1):
        return jnp.max(carry_ref[hd, r0:r1, :]) > ZERO_WEIGHT_LOG2

    tile_start = pl.multiple_of(i * tile, tile)
    n_sub = tile // blk
    for hd in heads:
        sweep(hd, 0, tile, tile_start, n_sub, diagonal=True)

    @pl.when(i > 0)
    def _():
        prev = pl.multiple_of(tile_start - tile, tile)
        for hd in heads:
            sweep(hd, 0, half, prev + half, n_sub // 2)

        top_alive = [alive(hd, 0, half) for hd in heads]
        bottom_alive = [alive(hd, half, tile) for hd in heads]
        for hd in heads:
            @pl.when(bottom_alive[hd])
            def _():
                sweep(hd, half, tile, prev + half, n_sub // 2)

            @pl.when(jnp.logical_or(top_alive[hd], bottom_alive[hd]))
            def _():
                sweep(hd, 0, tile, prev, n_sub // 2)

                def more(t):
                    return jnp.logical_and(t < i, alive(hd, 0, tile))

                def body(t):
                    sweep(hd, 0, tile, pl.multiple_of(tile_start - (t + 1) * tile, tile), n_sub)
                    return t + 1

                lax.while_loop(more, body, 1)

    o_ref[0] = jnp.concatenate([acc_ref[hd] for hd in heads], axis=1).astype(o_ref.dtype)


def _stick_breaking(p, batch, seq, riders=()):
    blk = ATT_BLOCK
    tile = ATT_TILE
    width = STICK_HEADS * DH_A
    groups = H_A // STICK_HEADS
    grid = (batch, groups, seq // tile)
    rider_specs, rider_shapes = _rider_specs(riders, grid)
    out = pl.pallas_call(
        _with_riders(_stick_kernel, 4, len(riders)),
        grid=grid,
        in_specs=[pl.BlockSpec((1, tile, width), lambda b, h, i: (b, i, h)),
                  pl.BlockSpec((1, seq, width), lambda b, h, i: (b, 0, groups + h)),
                  pl.BlockSpec((1, seq, width), lambda b, h, i: (b, 0, 2 * groups + h)),
                  pl.BlockSpec((2 * blk, 2 * blk), lambda b, h, i: (0, 0))] + rider_specs,
        out_specs=[pl.BlockSpec((1, tile, width), lambda b, h, i: (b, i, h))] + rider_specs,
        out_shape=[jax.ShapeDtypeStruct((batch, seq, H_A * DH_A), BF16)] + rider_shapes,
        scratch_shapes=[pltpu.VMEM((STICK_HEADS, tile, DH_A), F32),
                        pltpu.VMEM((STICK_HEADS, tile, blk), F32)],
        compiler_params=_params("arbitrary", "arbitrary", "arbitrary"),
        name="stick_breaking",
    )(p, p, p, _suffix_sum_matrix(), *riders)
    return out[0], out[1:]


def _rel_bucket(rel):
    nb = NUM_BUCKETS // 2
    max_exact = nb // 2
    ret = jnp.where(rel > 0, nb, 0)
    n = jnp.abs(rel)
    n_f = jnp.maximum(n, 1).astype(F32)
    large = max_exact + (jnp.log(n_f / max_exact) / math.log(MAX_DISTANCE / max_exact)
                         * (nb - max_exact)).astype(jnp.int32)
    large = jnp.minimum(large, nb - 1)
    return ret + jnp.where(n < max_exact, n, large)


def _near_buckets():
    blk = ATT_BLOCK
    qpos = blk + jnp.arange(blk)
    kpos = jnp.arange(2 * blk)
    near = _rel_bucket(kpos[None, :] - qpos[:, None]).astype(jnp.int32)
    far = _rel_bucket(jnp.full((1,), -2 * blk, jnp.int32)).astype(jnp.int32)
    return near, far


def _diff_kernel(far_ref, relb_ref, q_ref, k_ref, v_ref, bucket_ref, lam_ref, g_ref, o_ref,
                 bias_ref, m_ref, l_ref, acc_ref, s0_ref, s1_ref, *, lam_init):
    blk = ATT_BLOCK
    tile = q_ref.shape[1]
    nsub = tile // blk
    group = pl.program_id(1)
    i = pl.program_id(2)
    scale2 = DK_B ** -0.5 * LOG2E
    heads = range(DIFF_HEADS)

    @pl.when(i == 0)
    def _():
        bucket = bucket_ref[...]
        qpos = blk + lax.broadcasted_iota(jnp.int32, (blk, 2 * blk), 0)
        kpos = lax.broadcasted_iota(jnp.int32, (blk, 2 * blk), 1)
        allowed = (kpos // CHUNK) <= (qpos // CHUNK)
        bias_ref[...] = jnp.zeros_like(bias_ref)
        for hd in heads:
            h = group * DIFF_HEADS + hd
            far = relb_ref[far_ref[0], h]
            near = jnp.zeros((blk, 2 * blk), F32)
            for b in range(NUM_BUCKETS):
                near = jnp.where(bucket == b, relb_ref[b, h], near)
            near = jnp.where(allowed, (near - far) * LOG2E, -jnp.inf)
            for a in range(nsub):
                rows = slice(a * blk, (a + 1) * blk)
                bias_ref[hd, 0, rows, a * blk:(a + 1) * blk] = near[:, blk:]
                if a >= 1:
                    bias_ref[hd, 0, rows, (a - 1) * blk:a * blk] = near[:, :blk]
                if a + 1 < nsub:
                    bias_ref[hd, 0, rows, (a + 1) * blk:] = jnp.full(
                        (blk, tile - (a + 1) * blk), -jnp.inf, F32)
            bias_ref[hd, 1, 0:blk, (nsub - 1) * blk:] = near[:, :blk]

    m_ref[...] = jnp.full_like(m_ref, -jnp.inf)
    l_ref[...] = jnp.zeros_like(l_ref)
    acc_ref[...] = jnp.zeros_like(acc_ref)

    def key_tile(idx):
        return jnp.where(idx == 0, i, jnp.where(idx == 1, i - 1, idx - 2))

    def logits_into(hd, s_ref, idx):
        j = key_tile(jnp.minimum(idx, i))
        rows = pl.ds(pl.multiple_of(j * tile, tile), tile)
        for c in range(2):
            cols = slice(hd * DV_B + c * DK_B, hd * DV_B + (c + 1) * DK_B)
            s_ref[hd, c] = _dot_nt(q_ref[0, :, cols], k_ref[0, rows, cols])

    def consume(hd, s_ref, idx):
        j = key_tile(idx)
        v = v_ref[0, pl.ds(pl.multiple_of(j * tile, tile), tile), hd * DV_B:(hd + 1) * DV_B]
        bias = bias_ref[hd, jnp.minimum(idx, 2)]
        probs = []
        for c in range(2):
            s = s_ref[hd, c] * scale2 + bias
            subs = [s[:, u * blk:(u + 1) * blk] for u in range(nsub)]
            m_prev = m_ref[hd, c]
            m_cur = functools.reduce(jnp.maximum, subs)
            m_new = jnp.maximum(m_prev, jnp.max(m_cur, axis=-1, keepdims=True))
            alpha = jnp.exp2(m_prev - m_new)
            ps = [jnp.exp2(u - m_new) for u in subs]
            l_ref[hd, c] = alpha * l_ref[hd, c] + functools.reduce(jnp.add, ps)
            m_ref[hd, c] = m_new
            probs.append((alpha, jnp.concatenate([u.astype(BF16) for u in ps], axis=1)))
        for c in range(2):
            alpha, p = probs[c]
            acc_ref[hd, c] = (jnp.concatenate([alpha] * (DV_B // blk), axis=1) * acc_ref[hd, c]
                              + _dot(p, v))

    for hd in heads:
        logits_into(hd, s0_ref, 0)

    for hd in heads:
        def pair(t, _, hd=hd):
            idx = 2 * t
            logits_into(hd, s1_ref, idx + 1)
            consume(hd, s0_ref, idx)
            logits_into(hd, s0_ref, idx + 2)
            consume(hd, s1_ref, idx + 1)
            return 0

        lax.fori_loop(0, (i + 1) // 2, pair, 0)

    @pl.when(i % 2 == 0)
    def _():
        for hd in heads:
            consume(hd, s0_ref, i)

    lv = lam_ref[...]
    lam = (jnp.exp(jnp.sum(lv[0:1] * lv[1:2], axis=-1, keepdims=True))
           - jnp.exp(jnp.sum(lv[2:3] * lv[3:4], axis=-1, keepdims=True)) + lam_init)
    for hd in heads:
        l1 = jnp.sum(l_ref[hd, 0], axis=-1, keepdims=True)
        l2 = jnp.sum(l_ref[hd, 1], axis=-1, keepdims=True)
        o = acc_ref[hd, 0] / l1 - lam * (acc_ref[hd, 1] / l2)
        o_ref[0, :, hd * DV_B:(hd + 1) * DV_B] = (
            _rms_rows(o, g_ref[...]) * (1.0 - lam_init)).astype(o_ref.dtype)


def _diff_attention(p, rel_bias, lam_vecs, subln_g, lam_init, batch, seq, riders=()):
    blk = ATT_BLOCK
    tile = ATT_TILE
    width = DIFF_HEADS * DV_B
    groups = H_B // DIFF_HEADS
    q_col = 3 * H_A * DH_A // width
    k_col = q_col + groups
    v_col = k_col + groups
    near, far = _near_buckets()
    grid = (batch, groups, seq // tile)
    rider_specs, rider_shapes = _rider_specs(riders, grid)
    kern = _with_riders(functools.partial(_diff_kernel, lam_init=lam_init), 8, len(riders))
    grid_spec = pltpu.PrefetchScalarGridSpec(
        num_scalar_prefetch=2,
        grid=grid,
        in_specs=[pl.BlockSpec((1, tile, width), lambda b, h, i, *_: (b, i, q_col + h)),
                  pl.BlockSpec((1, seq, width), lambda b, h, i, *_: (b, 0, k_col + h)),
                  pl.BlockSpec((1, seq, width), lambda b, h, i, *_: (b, 0, v_col + h)),
                  pl.BlockSpec((blk, 2 * blk), lambda b, h, i, *_: (0, 0)),
                  pl.BlockSpec((4, DK_B), lambda b, h, i, *_: (0, 0)),
                  pl.BlockSpec((1, DV_B), lambda b, h, i, *_: (0, 0))] + rider_specs,
        out_specs=[pl.BlockSpec((1, tile, width), lambda b, h, i, *_: (b, i, h))] + rider_specs,
        scratch_shapes=[pltpu.VMEM((DIFF_HEADS, 3, tile, tile), F32),
                        pltpu.VMEM((DIFF_HEADS, 2, tile, blk), F32),
                        pltpu.VMEM((DIFF_HEADS, 2, tile, blk), F32),
                        pltpu.VMEM((DIFF_HEADS, 2, tile, DV_B), F32),
                        pltpu.VMEM((DIFF_HEADS, 2, tile, tile), F32),
                        pltpu.VMEM((DIFF_HEADS, 2, tile, tile), F32)])
    out = pl.pallas_call(
        kern,
        grid_spec=grid_spec,
        out_shape=[jax.ShapeDtypeStruct((batch, seq, H_B * DV_B), BF16)] + rider_shapes,
        compiler_params=_params("arbitrary", "arbitrary", "arbitrary"),
        name="diff_attention",
    )(far, rel_bias.astype(F32), p, p, p, near, lam_vecs.astype(F32), subln_g.reshape(1, DV_B),
      *riders)
    return out[0], out[1:]


def _rotary_tables(seq):
    inv_freq = ROPE_BASE ** (-jnp.arange(0, DK_C, 2, dtype=F32) / DK_C)
    ang = jnp.arange(seq, dtype=F32)[:, None] * inv_freq[None, :]
    cos = jnp.concatenate([jnp.cos(ang), jnp.cos(ang)], axis=-1)
    sin = jnp.concatenate([-jnp.sin(ang), jnp.sin(ang)], axis=-1)
    return cos, sin


def _decay_tables():
    log_g = jnp.log(1.0 - 2.0 ** (-5.0 - jnp.arange(H_C, dtype=F32)))
    idx = jnp.arange(RET_BLOCK, dtype=F32)
    chunk = jnp.arange(RET_BLOCK) // CHUNK
    visible = chunk[None, :] <= chunk[:, None]
    intra = jnp.where(visible[None],
                      jnp.exp(log_g[:, None, None] * jnp.abs(idx[:, None] - idx[None, :])), 0.0)
    q_decay = jnp.exp(log_g[:, None] * (idx + 1.0))
    k_decay = jnp.exp(log_g[:, None] * (RET_BLOCK - 1.0 - idx))
    block_decay = jnp.exp(log_g * RET_BLOCK)
    return (intra,
            jnp.broadcast_to(q_decay[:, :, None], (H_C, RET_BLOCK, DV_C)),
            jnp.broadcast_to(k_decay[:, :, None], (H_C, RET_BLOCK, DK_C)),
            jnp.broadcast_to(block_decay[:, None, None], (H_C, 1, DV_C)))


def _retention_kernel(q_ref, k_ref, v_ref, gate_ref, cos_ref, sin_ref, intra_ref, qd_ref,
                      kd_ref, bd_ref, g_ref, o_ref, state_ref):
    rows = q_ref.shape[1]

    @pl.when(pl.program_id(1) == 0)
    def _():
        state_ref[...] = jnp.zeros_like(state_ref)

    cos = cos_ref[...]
    sin = sin_ref[...]

    def rotary(t):
        return t * cos + pltpu.roll(t, DK_C // 2, 1) * sin

    for h in range(H_C):
        qk_cols = slice(h * DK_C, (h + 1) * DK_C)
        v_cols = slice(h * DV_C, (h + 1) * DV_C)
        q = (rotary(q_ref[0, :, qk_cols]) * (DK_C ** -0.5)).astype(BF16)
        k = rotary(k_ref[0, :, qk_cols])
        state = state_ref[h]
        for r in range(0, rows, RET_BLOCK):
            sl = slice(r, r + RET_BLOCK)
            qc = q[sl]
            kc = k[sl]
            vc = v_ref[0, sl, v_cols].astype(BF16)
            scores = _dot_nt(qc, kc.astype(BF16)) * intra_ref[h]
            y = _dot(scores.astype(BF16), vc) + _dot(qc, state.astype(BF16)) * qd_ref[h]
            state = bd_ref[h] * state + _dot((kc * kd_ref[h]).T.astype(BF16), vc)
            gate = gate_ref[0, sl, v_cols]
            y = _rms_rows(y, g_ref[...])
            o_ref[0, sl, v_cols] = (gate * (1.0 / (1.0 + jnp.exp(-gate))) * y).astype(o_ref.dtype)
        state_ref[h] = state


def _retention(p, ret_norm_g, batch, seq, rows=MIXER_ROWS):
    qk_width = H_C * DK_C
    v_width = H_C * DV_C
    cos, sin = _rotary_tables(seq)
    intra, q_decay, k_decay, block_decay = _decay_tables()

    def whole(a):
        return pl.BlockSpec(a.shape, lambda b, i: (0,) * a.ndim)

    return pl.pallas_call(
        _retention_kernel,
        grid=(batch, seq // rows),
        in_specs=[pl.BlockSpec((1, rows, qk_width), lambda b, i: (b, i, 0)),
                  pl.BlockSpec((1, rows, qk_width), lambda b, i: (b, i, 1)),
                  pl.BlockSpec((1, rows, v_width), lambda b, i: (b, i, 2 * qk_width // v_width)),
                  pl.BlockSpec((1, rows, v_width), lambda b, i: (b, i, 2 * qk_width // v_width + 1)),
                  pl.BlockSpec((rows, DK_C), lambda b, i: (i, 0)),
                  pl.BlockSpec((rows, DK_C), lambda b, i: (i, 0)),
                  whole(intra), whole(q_decay), whole(k_decay), whole(block_decay),
                  pl.BlockSpec((1, DV_C), lambda b, i: (0, 0))],
        out_specs=pl.BlockSpec((1, rows, v_width), lambda b, i: (b, i, 0)),
        out_shape=jax.ShapeDtypeStruct((batch, seq, v_width), BF16),
        scratch_shapes=[pltpu.VMEM((H_C, DK_C, DV_C), F32)],
        compiler_params=_params("arbitrary", "arbitrary"),
        name="retention",
    )(p, p, p, p, cos, sin, intra, q_decay, k_decay, block_decay, ret_norm_g.reshape(1, DV_C))


def _gelu_tanh(x):
    return 0.5 * x * (1.0 + jnp.tanh(math.sqrt(2.0 / math.pi) * (x + 0.044715 * (x * x * x))))


def _sgu_kernel(zu_ref, zv_ref, lg_ref, lb_ref, w_ref, b_ref, o_ref):
    rows = zu_ref.shape[0]
    width = D_D // G_D
    v = _gelu_tanh(zv_ref[...])
    mu = jnp.mean(v, axis=-1, keepdims=True)
    var = jnp.mean(jnp.square(v - mu), axis=-1, keepdims=True)
    vn = ((v - mu) * lax.rsqrt(var + EPS) * lg_ref[...] + lb_ref[...]).astype(BF16)
    pos_i = lax.broadcasted_iota(jnp.int32, (SGU_LEN, SGU_LEN), 0)
    pos_j = lax.broadcasted_iota(jnp.int32, (SGU_LEN, SGU_LEN), 1)
    mask = (pos_j // CHUNK) <= (pos_i // CHUNK)
    for g in range(G_D):
        w = jnp.where(mask, w_ref[g], 0.0).astype(BF16)
        bias = b_ref[g]
        cols = slice(g * width, (g + 1) * width)
        for r in range(rows // SGU_LEN):
            sl = slice(r * SGU_LEN, (r + 1) * SGU_LEN)
            gate = _dot(w, vn[sl, cols]) + bias
            o_ref[sl, cols] = (_gelu_tanh(zu_ref[sl, cols]) * gate).astype(o_ref.dtype)


def _sgu(p, ln_g, ln_b, w_s, b_s, rows=MIXER_ROWS):
    t = p.shape[0]
    u_col = (2 * H_C * DK_C + 2 * H_C * DV_C) // D_D
    return pl.pallas_call(
        _sgu_kernel,
        grid=(t // rows,),
        in_specs=[pl.BlockSpec((rows, D_D), lambda i: (i, u_col)),
                  pl.BlockSpec((rows, D_D), lambda i: (i, u_col + 1)),
                  pl.BlockSpec((1, D_D), lambda i: (0, 0)),
                  pl.BlockSpec((1, D_D), lambda i: (0, 0)),
                  pl.BlockSpec((G_D, SGU_LEN, SGU_LEN), lambda i: (0, 0, 0)),
                  pl.BlockSpec((G_D, SGU_LEN, 1), lambda i: (0, 0, 0))],
        out_specs=pl.BlockSpec((rows, D_D), lambda i: (i, 0)),
        out_shape=jax.ShapeDtypeStruct((t, D_D), BF16),
        compiler_params=_params("parallel"),
        name="spatial_gate",
    )(p, p, ln_g.reshape(1, D_D), ln_b.reshape(1, D_D), w_s, b_s.reshape(G_D, SGU_LEN, 1))


def kernel(x, norm_mix_g, norm_ffn_g, final_norm_g, rel_bias, ab_w_in, ab_w_out, diff_lambda,
           diff_subln_g, cd_w_in, cd_w_out, ret_norm_g, sgu_ln_g, sgu_ln_b, sgu_w, sgu_b,
           ffn_w_up, ffn_conv_w, ffn_conv_b, ffn_w_down):
    batch, seq, d = x.shape
    t = batch * seq
    depth = norm_mix_g.shape[0]
    xt = x.reshape(t, d)
    def rows(w):
        return w.reshape(-1, w.shape[-1])

    for layer in range(depth):
        j = layer // 2
        if layer % 2 == 0:
            lam_init = 0.8 - 0.6 * math.exp(-0.3 * layer)
            p = _norm_matmul(xt, norm_mix_g[layer], ab_w_in, j, BF16, AB_PROJ_TILE)
            p3 = p.reshape(batch, seq, p.shape[1])
            if layer == 0:
                o_a, (up_bf,) = _stick_breaking(p3, batch, seq, (rows(ffn_w_up),))
                o_b, (down_bf, abo_bf, cdi_bf, cdo_bf) = _diff_attention(
                    p3, rel_bias, diff_lambda[j], diff_subln_g[j], lam_init, batch, seq,
                    (rows(ffn_w_down), rows(ab_w_out), rows(cd_w_in), rows(cd_w_out)))
                ffn_w_up, ffn_w_down, ab_w_out, cd_w_in, cd_w_out = (
                    b.reshape(w.shape) for b, w in ((up_bf, ffn_w_up), (down_bf, ffn_w_down),
                                                    (abo_bf, ab_w_out), (cdi_bf, cd_w_in),
                                                    (cdo_bf, cd_w_out)))
            else:
                o_a, _ = _stick_breaking(p3, batch, seq)
                o_b, _ = _diff_attention(p3, rel_bias, diff_lambda[j], diff_subln_g[j], lam_init,
                                         batch, seq)
            xt = _out_proj(o_a.reshape(t, -1), o_b.reshape(t, -1), ab_w_out, j, xt)
        else:
            p = _norm_matmul(xt, norm_mix_g[layer], cd_w_in, j, F32, CD_PROJ_TILE)
            o_c = _retention(p.reshape(batch, seq, p.shape[1]), ret_norm_g[j], batch, seq)
            o_d = _sgu(p, sgu_ln_g[j], sgu_ln_b[j], sgu_w[j], sgu_b[j])
            xt = _out_proj(o_c.reshape(t, -1), o_d, cd_w_out, j, xt)
        xt = _ffn(xt, norm_ffn_g, ffn_w_up, ffn_conv_w, ffn_conv_b, ffn_w_down, final_norm_g,
                  layer, seq=seq, final_norm=(layer == depth - 1))
    return xt.reshape(batch, seq, d)
```
